```python
import math
import jax, jax.numpy as jnp
from jax import lax
import numpy as np

D_MODEL = 4096
BATCH = 1
SEQ = 8192
DEPTH = 1
DEC_BATCH = 128
DEC_SEQ = 8
PAST_LEN = 8192
PAGE_SIZE = 128

HEAD_DIM = 128
ATT_HEADS = 16
ATT_KV_HEADS = 4
ATT_GROUP = ATT_HEADS // ATT_KV_HEADS
WINDOW = 128
ATT_BLOCK = 128
ATT_WIDTH = ATT_HEADS * HEAD_DIM
NUM_BUCKETS = 32
MAX_EXACT = NUM_BUCKETS // 2
MAX_DISTANCE = 128
MLSTM_HEADS = 8
MLSTM_DK = 128
MLSTM_DV = 256
MLSTM_CHUNK = 64
MLSTM_WIDTH = MLSTM_HEADS * MLSTM_DV
MIX_WIDTH = ATT_WIDTH + MLSTM_WIDTH
N_GROUPS = 4
EXPERTS_PER_GROUP = 8
N_EXPERTS = N_GROUPS * EXPERTS_PER_GROUP
TOP_K = 2
D_FF_EXPERT = 1024
MOE_BLOCK = 128
NORM_EPS = 1e-6

kernel_name = 'hymba_swa_mlstm_hiermoe_adaln_step'


def _in_split_points():
    sizes = [ATT_HEADS * HEAD_DIM, ATT_KV_HEADS * HEAD_DIM, ATT_KV_HEADS * HEAD_DIM,
             MLSTM_HEADS * MLSTM_DK, MLSTM_HEADS * MLSTM_DK, MLSTM_WIDTH, MLSTM_WIDTH,
             MLSTM_HEADS, MLSTM_HEADS]
    return sizes, [int(s) for s in np.cumsum(sizes)[:-1]]


def rms_norm(x, g):
    xf = x.astype(jnp.float32)
    y = xf * lax.rsqrt(jnp.mean(xf * xf, axis=-1, keepdims=True) + NORM_EPS)
    return (y * g.astype(jnp.float32)).astype(x.dtype)


def t5_bucket(dist):
    n = jnp.maximum(dist, 0)
    nf = jnp.maximum(n, 1).astype(jnp.float32)
    large = MAX_EXACT + (jnp.log(nf / MAX_EXACT) / math.log(MAX_DISTANCE / MAX_EXACT)
                         * (NUM_BUCKETS - MAX_EXACT)).astype(jnp.int32)
    large = jnp.minimum(large, NUM_BUCKETS - 1)
    return jnp.where(n < MAX_EXACT, n, large)


def sink_attention(q, k, v, dist, valid, sinks, rel_bias):
    T, K = dist.shape
    bias = jnp.transpose(rel_bias[t5_bucket(dist)], (2, 0, 1)).reshape(ATT_KV_HEADS, ATT_GROUP, T, K)
    logits = jnp.einsum('...tgrd,...kgd->...grtk', q.astype(jnp.float32), k.astype(jnp.float32)) * HEAD_DIM ** -0.5
    logits = logits + bias.astype(jnp.float32)
    logits = jnp.where(valid[..., None, None, :, :], logits, -jnp.inf)
    sink = sinks.astype(jnp.float32).reshape(ATT_KV_HEADS, ATT_GROUP, 1, 1)
    m = jnp.maximum(jnp.max(logits, axis=-1, keepdims=True), sink)
    p = jnp.exp(logits - m)
    denom = jnp.sum(p, axis=-1, keepdims=True) + jnp.exp(sink - m)
    return jnp.einsum('...grtk,...kgd->...tgrd', p / denom, v.astype(jnp.float32))


def swa_prompt(q, k, v, sinks, rel_bias):
    B, S = q.shape[:2]
    nb = S // ATT_BLOCK
    qb = q.reshape(B, nb, ATT_BLOCK, ATT_KV_HEADS, ATT_GROUP, HEAD_DIM)
    kb = k.reshape(B, nb, ATT_BLOCK, ATT_KV_HEADS, HEAD_DIM)
    vb = v.reshape(B, nb, ATT_BLOCK, ATT_KV_HEADS, HEAD_DIM)
    kk = jnp.concatenate([jnp.concatenate([jnp.zeros_like(kb[:, :1]), kb[:, :-1]], axis=1), kb], axis=2)
    vv = jnp.concatenate([jnp.concatenate([jnp.zeros_like(vb[:, :1]), vb[:, :-1]], axis=1), vb], axis=2)
    qi = jnp.arange(ATT_BLOCK)[:, None]
    kj = jnp.arange(2 * ATT_BLOCK)[None, :]
    dist = qi + ATT_BLOCK - kj
    key_pos = jnp.arange(nb)[:, None, None] * ATT_BLOCK - ATT_BLOCK + kj[None]
    valid = (dist >= 0) & (dist < WINDOW) & (key_pos >= 0)
    out = sink_attention(qb, kk, vv, dist, valid, sinks, rel_bias)
    return out.reshape(B, S, ATT_WIDTH)


def swa_sample(q, k, v, cache_k, cache_v, sinks, rel_bias):
    B, T = q.shape[:2]
    kk = jnp.concatenate([cache_k.astype(k.dtype), k], axis=1)
    vv = jnp.concatenate([cache_v.astype(v.dtype), v], axis=1)
    dist = jnp.arange(T)[:, None] + WINDOW - jnp.arange(WINDOW + T)[None, :]
    valid = (dist >= 0) & (dist < WINDOW)
    out = sink_attention(q, kk, vv, dist, valid, sinks, rel_bias)
    return out.reshape(B, T, ATT_WIDTH), kk[:, -WINDOW:], vv[:, -WINDOW:]


def mlstm_chunkwise(q, k, v, ig, lf, C0, n0, m0):
    B, T = q.shape[:2]
    L = math.gcd(T, MLSTM_CHUNK)
    nc = T // L

    def to_chunks(a):
        return jnp.moveaxis(a.astype(jnp.float32).reshape((B, nc, L) + a.shape[2:]), 1, 0)

    causal = jnp.tril(jnp.ones((L, L), dtype=bool))

    def step(carry, xs):
        C, n, m = carry
        qc, kc, vc, igc, lfc = xs
        b = jnp.cumsum(lfc, axis=1)
        log_d = b[:, :, None, :] - b[:, None, :, :] + igc[:, None, :, :]
        log_d = jnp.where(causal[None, :, :, None], log_d, -jnp.inf)
        log_inter = b + m[:, None, :]
        m_t = jnp.maximum(log_inter, jnp.max(log_d, axis=2))
        d = jnp.exp(log_d - m_t[:, :, None, :])
        w_inter = jnp.exp(log_inter - m_t)
        s = jnp.einsum('bthk,bshk->btsh', qc, kc) * d
        num = jnp.einsum('btsh,bshv->bthv', s, vc) + w_inter[..., None] * jnp.einsum('bhvk,bthk->bthv', C, qc)
        den = jnp.sum(s, axis=2) + w_inter * jnp.einsum('bhk,bthk->bth', n, qc)
        h = num / jnp.maximum(jnp.abs(den), jnp.exp(-m_t))[..., None]
        m_new = m_t[:, -1]
        w_s = jnp.exp(b[:, -1:, :] - b + igc - m_new[:, None, :])
        decay = jnp.exp(b[:, -1] + m - m_new)
        C_new = decay[..., None, None] * C + jnp.einsum('bsh,bshv,bshk->bhvk', w_s, vc, kc)
        n_new = decay[..., None] * n + jnp.einsum('bsh,bshk->bhk', w_s, kc)
        return (C_new, n_new, m_new), h

    carry0 = (C0.astype(jnp.float32), n0.astype(jnp.float32), m0.astype(jnp.float32))
    carry, hs = lax.scan(step, carry0, (to_chunks(q), to_chunks(k), to_chunks(v), to_chunks(ig), to_chunks(lf)))
    h = jnp.moveaxis(hs, 0, 1).reshape(B, T, MLSTM_HEADS, MLSTM_DV)
    return h, carry


def hier_moe(h, w_rg, b_rg, w_re, b_re, w_gate, w_up, w_down):
    N, D = h.shape
    hf = h.astype(jnp.float32)
    g_logits = hf @ w_rg.astype(jnp.float32) + b_rg.astype(jnp.float32)
    g_prob = jax.nn.softmax(g_logits, axis=-1)
    grp = jnp.argmax(g_logits, axis=-1)
    rows = jnp.arange(N)
    p_grp = g_prob[rows, grp]
    e_logits = (hf @ w_re.astype(jnp.float32) + b_re.astype(jnp.float32)).reshape(N, N_GROUPS, EXPERTS_PER_GROUP)
    top_v, top_i = lax.top_k(e_logits[rows, grp], TOP_K)
    gate = p_grp[:, None] * jax.nn.softmax(top_v, axis=-1)
    expert = grp[:, None] * EXPERTS_PER_GROUP + top_i
    A = N * TOP_K
    flat_e = expert.reshape(A)
    order = jnp.argsort(flat_e)
    se = flat_e[order]
    stok = order // TOP_K
    sw = gate.reshape(A)[order]
    counts = jnp.bincount(flat_e, length=N_EXPERTS)
    padded = (counts + MOE_BLOCK - 1) // MOE_BLOCK * MOE_BLOCK
    pad_end = jnp.cumsum(padded)
    pad_start = pad_end - padded
    start = jnp.cumsum(counts) - counts
    dest = pad_start[se] + jnp.arange(A) - start[se]
    n_blk = -(-A // MOE_BLOCK) + N_EXPERTS
    buf = jnp.zeros((n_blk * MOE_BLOCK, D), h.dtype).at[dest].set(h[stok])
    blk_expert = jnp.minimum(jnp.searchsorted(pad_end, jnp.arange(n_blk) * MOE_BLOCK, side='right'), N_EXPERTS - 1)

    def expert_mlp(args):
        xb, e = args
        return (jax.nn.silu(xb @ w_gate[e]) * (xb @ w_up[e])) @ w_down[e]

    out = lax.map(expert_mlp, (buf.reshape(n_blk, MOE_BLOCK, D), blk_expert))
    out = out.reshape(n_blk * MOE_BLOCK, D)[dest]
    return jax.ops.segment_sum(out * sw[:, None].astype(out.dtype), stok, num_segments=N)


def layer(x, c, kv_state, m_state, rel_bias, w_ada, b_ada, g_mix, g_ffn, w_in, sinks, b_igate, b_fgate,
          w_out, w_rg, b_rg, w_re, b_re, w_gate, w_up, w_down):
    B, T, D = x.shape
    mod = (jax.nn.silu(c) @ w_ada + b_ada)[:, None, :]
    sh1, sc1, gt1, sh2, sc2, gt2 = jnp.split(mod, 6, axis=-1)
    h = rms_norm(x, g_mix) * (1 + sc1) + sh1
    _, points = _in_split_points()
    qa, ka, va, qm, km, vm, om, im, fm = jnp.split(h @ w_in, points, axis=-1)
    qa = qa.reshape(B, T, ATT_KV_HEADS, ATT_GROUP, HEAD_DIM)
    ka = ka.reshape(B, T, ATT_KV_HEADS, HEAD_DIM)
    va = va.reshape(B, T, ATT_KV_HEADS, HEAD_DIM)
    qm = qm.reshape(B, T, MLSTM_HEADS, MLSTM_DK) * MLSTM_DK ** -0.5
    km = km.reshape(B, T, MLSTM_HEADS, MLSTM_DK)
    vm = vm.reshape(B, T, MLSTM_HEADS, MLSTM_DV)
    ig = im.astype(jnp.float32) + b_igate.astype(jnp.float32)
    lf = jax.nn.log_sigmoid(fm.astype(jnp.float32) + b_fgate.astype(jnp.float32))
    if kv_state is None:
        ya = swa_prompt(qa, ka, va, sinks, rel_bias)
        new_k, new_v = ka[:, -WINDOW:], va[:, -WINDOW:]
        C0 = jnp.zeros((B, MLSTM_HEADS, MLSTM_DV, MLSTM_DK), jnp.float32)
        n0 = jnp.zeros((B, MLSTM_HEADS, MLSTM_DK), jnp.float32)
        m0 = jnp.zeros((B, MLSTM_HEADS), jnp.float32)
    else:
        ya, new_k, new_v = swa_sample(qa, ka, va, kv_state[0], kv_state[1], sinks, rel_bias)
        C0, n0, m0 = m_state
    hm, (C, n, m) = mlstm_chunkwise(qm, km, vm, ig, lf, C0, n0, m0)
    ym = jax.nn.sigmoid(om) * hm.reshape(B, T, MLSTM_WIDTH).astype(x.dtype)
    mix = jnp.concatenate([ya.astype(x.dtype), ym], axis=-1)
    x = x + gt1 * (mix @ w_out)
    h2 = rms_norm(x, g_ffn) * (1 + sc2) + sh2
    y = hier_moe(h2.reshape(B * T, D), w_rg, b_rg, w_re, b_re, w_gate, w_up, w_down).reshape(B, T, D)
    x = x + gt2 * y
    sd = x.dtype
    return x, (new_k.astype(sd), new_v.astype(sd), C.astype(sd), n.astype(sd), m.astype(sd))


def setup_inputs(seed: int = 0) -> dict:
    key = jax.random.key(seed)
    ks = jax.random.split(key, 28)
    D = D_MODEL
    sizes, _ = _in_split_points()
    in_width = int(sum(sizes))

    def nrm(k, shape, scale):
        return scale * jax.random.normal(k, shape, jnp.float32)

    return {
        'x_prompt': nrm(ks[0], (BATCH, SEQ, D), 1.0),
        'x_sample': nrm(ks[1], (DEC_BATCH, DEC_SEQ, D), 1.0),
        'cache_k': nrm(ks[2], (DEPTH, DEC_BATCH, WINDOW, ATT_KV_HEADS, HEAD_DIM), 1.0),
        'cache_v': nrm(ks[3], (DEPTH, DEC_BATCH, WINDOW, ATT_KV_HEADS, HEAD_DIM), 1.0),
        'state_C': nrm(ks[4], (DEPTH, DEC_BATCH, MLSTM_HEADS, MLSTM_DV, MLSTM_DK), MLSTM_DK ** -0.5),
        'state_n': nrm(ks[5], (DEPTH, DEC_BATCH, MLSTM_HEADS, MLSTM_DK), 1.0),
        'state_m': nrm(ks[6], (DEPTH, DEC_BATCH, MLSTM_HEADS), 1.0),
        'c_prompt': nrm(ks[7], (BATCH, D), 1.0),
        'c_sample': nrm(ks[8], (DEC_BATCH, D), 1.0),
        'rel_bias': nrm(ks[9], (NUM_BUCKETS, ATT_HEADS), 0.5),
        'w_ada': nrm(ks[10], (DEPTH, D, 6 * D), 0.5 * D ** -0.5),
        'b_ada': nrm(ks[11], (DEPTH, 6 * D), 0.02),
        'g_mix': 1.0 + nrm(ks[12], (DEPTH, D), 0.05),
        'g_ffn': 1.0 + nrm(ks[13], (DEPTH, D), 0.05),
        'w_in': nrm(ks[14], (DEPTH, D, in_width), D ** -0.5),
        'sinks': nrm(ks[15], (DEPTH, ATT_HEADS), 0.5),
        'b_igate': -2.0 + nrm(ks[16], (DEPTH, MLSTM_HEADS), 0.1),
        'b_fgate': 3.0 + nrm(ks[17], (DEPTH, MLSTM_HEADS), 0.5),
        'w_out': nrm(ks[18], (DEPTH, MIX_WIDTH, D), MIX_WIDTH ** -0.5),
        'w_router_grp': nrm(ks[19], (DEPTH, D, N_GROUPS), D ** -0.5),
        'b_router_grp': nrm(ks[20], (DEPTH, N_GROUPS), 0.01),
        'w_router_exp': nrm(ks[21], (DEPTH, D, N_EXPERTS), D ** -0.5),
        'b_router_exp': nrm(ks[22], (DEPTH, N_EXPERTS), 0.01),
        'w_gate': nrm(ks[23], (DEPTH, N_EXPERTS, D, D_FF_EXPERT), D ** -0.5),
        'w_up': nrm(ks[24], (DEPTH, N_EXPERTS, D, D_FF_EXPERT), D ** -0.5),
        'w_down': nrm(ks[25], (DEPTH, N_EXPERTS, D_FF_EXPERT, D), D_FF_EXPERT ** -0.5),
        'g_final': 1.0 + nrm(ks[26], (D,), 0.05),
    }


def reference(x_prompt, x_sample, cache_k, cache_v, state_C, state_n, state_m, c_prompt, c_sample,
              rel_bias, w_ada, b_ada, g_mix, g_ffn, w_in, sinks, b_igate, b_fgate, w_out,
              w_router_grp, b_router_grp, w_router_exp, b_router_exp, w_gate, w_up, w_down, g_final):
    xp, xs = x_prompt, x_sample
    sp, ss = [], []
    for l in range(DEPTH):
        lw = (w_ada[l], b_ada[l], g_mix[l], g_ffn[l], w_in[l], sinks[l], b_igate[l], b_fgate[l], w_out[l],
              w_router_grp[l], b_router_grp[l], w_router_exp[l], b_router_exp[l], w_gate[l], w_up[l], w_down[l])
        xp, st_p = layer(xp, c_prompt, None, None, rel_bias, *lw)
        xs, st_s = layer(xs, c_sample, (cache_k[l], cache_v[l]), (state_C[l], state_n[l], state_m[l]), rel_bias, *lw)
        sp.append(st_p)
        ss.append(st_s)
    y_prompt = rms_norm(xp, g_final)
    y_sample = rms_norm(xs, g_final)

    def stk(sts, i):
        return jnp.stack([s[i] for s in sts], axis=0)

    return (y_prompt, y_sample,
            stk(sp, 0), stk(sp, 1), stk(sp, 2), stk(sp, 3), stk(sp, 4),
            stk(ss, 0), stk(ss, 1), stk(ss, 2), stk(ss, 3), stk(ss, 4))
```

```python
import functools
import math

import numpy as np
import jax
import jax.numpy as jnp
from jax import lax
from jax.experimental import pallas as pl
from jax.experimental.pallas import tpu as pltpu

F32 = jnp.float32
BF16 = jnp.bfloat16
NEG_INF = float("-inf")

D = 4096
N_PROMPT = 8192
N_BATCH = 128
T_DEC = 8
N_SAMPLE = N_BATCH * T_DEC
N_TOK = N_PROMPT + N_SAMPLE
HEAD_DIM = 128
ATT_HEADS = 16
KV_HEADS = 4
GROUP = ATT_HEADS // KV_HEADS
WINDOW = 128
ATT_WIDTH = ATT_HEADS * HEAD_DIM
KV_WIDTH = KV_HEADS * HEAD_DIM
NUM_BUCKETS = 32
MAX_EXACT = 16
MAX_DISTANCE = 128
M_HEADS = 8
DK = 128
DV = 256
M_WIDTH = M_HEADS * DV
Z_WIDTH = ATT_WIDTH + 2 * KV_WIDTH + 2 * M_HEADS * DK + 2 * M_WIDTH
N_GROUPS = 4
EXP_PER_GROUP = 8
N_EXPERTS = N_GROUPS * EXP_PER_GROUP
TOP_K = 2
D_FF = 1024
EPS = 1e-6
ATT_SCALE = HEAD_DIM ** -0.5
Q_SCALE = DK ** -0.5

LANES = 128
SUBLANES = 8
VMEM_LIMIT = 56 * 1024 * 1024

TM_SHARED = 512
TM_ROWMOD = 256
TN_IN = 1024
TN_OUT = 1024
TN_ADA = 512
C_ROWS = 136
ATT_BLOCK = 128
SAMPLE_BT = 8
ML_CHUNK = 256
TM_MOE = 256
TN_FF = 512
TN_DOWN = 1024
N_ASSIGN = N_TOK * TOP_K
MAX_TILES = N_ASSIGN // TM_MOE + N_EXPERTS
A_PAD = MAX_TILES * TM_MOE
TM_TOK = 256


def _params(sem):
    return pltpu.CompilerParams(dimension_semantics=sem, vmem_limit_bytes=VMEM_LIMIT)


def _iota(shape, dim):
    return lax.broadcasted_iota(jnp.int32, shape, dim)


def _dot(a, b):
    return jnp.dot(a, b, preferred_element_type=F32)


def _dot_nt(a, b):
    return lax.dot_general(a, b, (((1,), (1,)), ((), ())), preferred_element_type=F32)


def _dot_tn(a, b):
    return lax.dot_general(a, b, (((0,), (0,)), ((), ())), preferred_element_type=F32)


def _split3(x):
    x1 = x.astype(BF16)
    r1 = x - x1.astype(F32)
    x2 = r1.astype(BF16)
    r2 = r1 - x2.astype(F32)
    return x1, x2, r2.astype(BF16)


def _dot_exact_lhs01(a01, x):
    x1, x2, x3 = _split3(x)
    return _dot(a01, x1) + _dot(a01, x2) + _dot(a01, x3)


def _sigmoid(x):
    return 1.0 / (1.0 + jnp.exp(-x))


def _log_sigmoid(x):
    return jnp.minimum(x, 0.0) - jnp.log(1.0 + jnp.exp(-jnp.abs(x)))


def _ada_kernel(c_ref, w_ref, b_ref, o_ref):
    c = c_ref[...]
    s = (c * _sigmoid(c)).astype(BF16)
    o_ref[...] = _dot(s, w_ref[...].astype(BF16)) + b_ref[...]


def _ada(c_all, w_ada, b_ada):
    n = w_ada.shape[1]
    return pl.pallas_call(
        _ada_kernel,
        grid=(n // TN_ADA,),
        in_specs=[pl.BlockSpec((C_ROWS, D), lambda j: (0, 0)),
                  pl.BlockSpec((D, TN_ADA), lambda j: (0, j)),
                  pl.BlockSpec((1, TN_ADA), lambda j: (0, j))],
        out_specs=pl.BlockSpec((C_ROWS, TN_ADA), lambda j: (0, j)),
        out_shape=jax.ShapeDtypeStruct((C_ROWS, n), F32),
        compiler_params=_params(("arbitrary",)),
        name="ada",
    )(c_all, w_ada, b_ada)


def _inproj_kernel(x_ref, sh_ref, sc_ref, g_ref, w_ref, wg_ref, z_ref, zg_ref, h_scr):
    @pl.when(pl.program_id(1) == 0)
    def _():
        x = x_ref[...]
        y = x * lax.rsqrt(jnp.mean(x * x, axis=-1, keepdims=True) + EPS) * g_ref[...]
        hb = (y * (1.0 + sc_ref[...]) + sh_ref[...]).astype(BF16)
        h_scr[...] = hb
        zg_ref[...] = _dot(hb, wg_ref[...])

    z_ref[...] = _dot(h_scr[...], w_ref[...])


def _inproj(x, mod, per_row, g_mix, w_main, w_gate):
    n = x.shape[0]
    TM = TM_ROWMOD if per_row else TM_SHARED
    mrows = TM if per_row else 1
    mi = (lambda i: i) if per_row else (lambda i: 0)
    return pl.pallas_call(
        _inproj_kernel,
        grid=(n // TM, Z_WIDTH // TN_IN),
        in_specs=[pl.BlockSpec((TM, D), lambda i, j: (i, 0)),
                  pl.BlockSpec((mrows, D), lambda i, j: (mi(i), 0)),
                  pl.BlockSpec((mrows, D), lambda i, j: (mi(i), 1)),
                  pl.BlockSpec((1, D), lambda i, j: (0, 0)),
                  pl.BlockSpec((D, TN_IN), lambda i, j: (0, j)),
                  pl.BlockSpec((D, LANES), lambda i, j: (0, 0))],
        out_specs=[pl.BlockSpec((TM, TN_IN), lambda i, j: (i, j)),
                   pl.BlockSpec((TM, LANES), lambda i, j: (i, 0))],
        out_shape=[jax.ShapeDtypeStruct((n, Z_WIDTH), F32),
                   jax.ShapeDtypeStruct((n, LANES), F32)],
        scratch_shapes=[pltpu.VMEM((TM, D), BF16)],
        compiler_params=_params(("arbitrary", "arbitrary")),
        name="inproj",
    )(x, mod, mod, g_mix, w_main, w_gate)


def _t5_bucket_np(dist):
    n = np.maximum(dist, 0)
    nf = np.maximum(n, 1).astype(np.float32)
    large = MAX_EXACT + (np.log(nf / MAX_EXACT) / math.log(MAX_DISTANCE / MAX_EXACT)
                         * (NUM_BUCKETS - MAX_EXACT)).astype(np.int32)
    large = np.minimum(large, NUM_BUCKETS - 1)
    return np.where(n < MAX_EXACT, n, large).astype(np.int32)


def _bucket_table(n_q, n_keys_valid, n_keys_padded):
    t = np.arange(n_q)[:, None]
    j = np.arange(n_keys_padded)[None, :]
    dist = t + WINDOW - j
    valid = (dist >= 0) & (dist < WINDOW) & (j < n_keys_valid)
    return np.where(valid, _t5_bucket_np(dist), -1).astype(np.int32)


def _fill_bias(bucket_ref, rb_ref, bias_scr, rows):
    bk = bucket_ref[...]
    for h in range(ATT_HEADS):
        acc = jnp.full(bk.shape, NEG_INF, F32)
        for b in range(NUM_BUCKETS):
            acc = jnp.where(bk == b, rb_ref[b * ATT_HEADS + h], acc)
        bias_scr[h * rows:(h + 1) * rows, :] = acc


def _sink_softmax_av(lg, sink_col, v2):
    m = jnp.maximum(jnp.max(lg, axis=-1, keepdims=True), sink_col)
    p = jnp.exp(lg - m)
    den = jnp.sum(p, axis=-1, keepdims=True) + jnp.exp(sink_col - m)
    return _dot((p / den).astype(BF16), v2)


def _swa_prompt_kernel(rb_ref, sink_ref, bucket_ref, q_ref, kp_ref, kc_ref, vp_ref, vc_ref,
                       o_ref, bias_scr):
    i = pl.program_id(0)

    @pl.when(i == 0)
    def _():
        _fill_bias(bucket_ref, rb_ref, bias_scr, ATT_BLOCK)

    rows = GROUP * ATT_BLOCK
    first_prev = jnp.logical_and(i == 0, _iota((rows, 2 * ATT_BLOCK), 1) < ATT_BLOCK)
    for g in range(KV_HEADS):
        ks = slice(g * HEAD_DIM, (g + 1) * HEAD_DIM)
        k2 = jnp.concatenate([kp_ref[:, ks], kc_ref[:, ks]], axis=0).astype(BF16)
        v2 = jnp.concatenate([vp_ref[:, ks], vc_ref[:, ks]], axis=0).astype(BF16)
        q4 = jnp.concatenate(
            [q_ref[:, (g * GROUP + r) * HEAD_DIM:(g * GROUP + r + 1) * HEAD_DIM] for r in range(GROUP)],
            axis=0).astype(BF16)
        lg = _dot_nt(q4, k2) * ATT_SCALE + bias_scr[g * rows:(g + 1) * rows, :]
        lg = jnp.where(first_prev, NEG_INF, lg)
        sink_col = jnp.concatenate(
            [jnp.full((ATT_BLOCK, 1), sink_ref[g * GROUP + r], F32) for r in range(GROUP)], axis=0)
        o = _sink_softmax_av(lg, sink_col, v2)
        for r in range(GROUP):
            h = g * GROUP + r
            o_ref[:, h * HEAD_DIM:(h + 1) * HEAD_DIM] = o[r * ATT_BLOCK:(r + 1) * ATT_BLOCK].astype(BF16)


def _swa_prompt(z, rb_flat, sinks):
    nb = N_PROMPT // ATT_BLOCK
    bucket = jnp.asarray(_bucket_table(ATT_BLOCK, 2 * ATT_BLOCK, 2 * ATT_BLOCK))
    kcol = ATT_WIDTH // KV_WIDTH
    prev = lambda i: jnp.maximum(i - 1, 0)
    smem = pl.BlockSpec(memory_space=pltpu.SMEM)
    return pl.pallas_call(
        _swa_prompt_kernel,
        grid=(nb,),
        in_specs=[smem, smem,
                  pl.BlockSpec((ATT_BLOCK, 2 * ATT_BLOCK), lambda i: (0, 0)),
                  pl.BlockSpec((ATT_BLOCK, ATT_WIDTH), lambda i: (i, 0)),
                  pl.BlockSpec((ATT_BLOCK, KV_WIDTH), lambda i: (prev(i), kcol)),
                  pl.BlockSpec((ATT_BLOCK, KV_WIDTH), lambda i: (i, kcol)),
                  pl.BlockSpec((ATT_BLOCK, KV_WIDTH), lambda i: (prev(i), kcol + 1)),
                  pl.BlockSpec((ATT_BLOCK, KV_WIDTH), lambda i: (i, kcol + 1))],
        out_specs=pl.BlockSpec((ATT_BLOCK, ATT_WIDTH), lambda i: (i, 0)),
        out_shape=jax.ShapeDtypeStruct((N_PROMPT, ATT_WIDTH), BF16),
        scratch_shapes=[pltpu.VMEM((ATT_HEADS * ATT_BLOCK, 2 * ATT_BLOCK), F32)],
        compiler_params=_params(("arbitrary",)),
        name="swa_prompt",
    )(rb_flat, sinks, bucket, z, z, z, z, z)


S_KEYS = 2 * WINDOW


def _swa_sample_kernel(rb_ref, sink_ref, bucket_ref, q_ref, kn_ref, vn_ref, ck_ref, cv_ref,
                       o_ref, nk_ref, nv_ref, bias_scr):
    @pl.when(pl.program_id(0) == 0)
    def _():
        _fill_bias(bucket_ref, rb_ref, bias_scr, T_DEC)

    nk_ref[:, 0:WINDOW - T_DEC, :] = ck_ref[:, T_DEC:WINDOW, :]
    nk_ref[:, WINDOW - T_DEC:WINDOW, :] = kn_ref[...].reshape(SAMPLE_BT, T_DEC, KV_WIDTH)
    nv_ref[:, 0:WINDOW - T_DEC, :] = cv_ref[:, T_DEC:WINDOW, :]
    nv_ref[:, WINDOW - T_DEC:WINDOW, :] = vn_ref[...].reshape(SAMPLE_BT, T_DEC, KV_WIDTH)

    rows = GROUP * T_DEC
    pad = jnp.zeros((S_KEYS - WINDOW - T_DEC, HEAD_DIM), F32)
    for b in range(SAMPLE_BT):
        ts = slice(b * T_DEC, (b + 1) * T_DEC)
        for g in range(KV_HEADS):
            ks = slice(g * HEAD_DIM, (g + 1) * HEAD_DIM)
            k2 = jnp.concatenate([ck_ref[b, :, ks], kn_ref[ts, ks], pad], axis=0).astype(BF16)
            v2 = jnp.concatenate([cv_ref[b, :, ks], vn_ref[ts, ks], pad], axis=0).astype(BF16)
            q4 = jnp.concatenate(
                [q_ref[ts, (g * GROUP + r) * HEAD_DIM:(g * GROUP + r + 1) * HEAD_DIM] for r in range(GROUP)],
                axis=0).astype(BF16)
            lg = _dot_nt(q4, k2) * ATT_SCALE + bias_scr[g * rows:(g + 1) * rows, :]
            sink_col = jnp.concatenate(
                [jnp.full((T_DEC, 1), sink_ref[g * GROUP + r], F32) for r in range(GROUP)], axis=0)
            o = _sink_softmax_av(lg, sink_col, v2)
            for r in range(GROUP):
                h = g * GROUP + r
                o_ref[ts, h * HEAD_DIM:(h + 1) * HEAD_DIM] = o[r * T_DEC:(r + 1) * T_DEC].astype(BF16)


def _swa_sample(z, cache_k, cache_v, rb_flat, sinks):
    rows = SAMPLE_BT * T_DEC
    bucket = jnp.asarray(_bucket_table(T_DEC, WINDOW + T_DEC, S_KEYS))
    kcol = ATT_WIDTH // KV_WIDTH
    smem = pl.BlockSpec(memory_space=pltpu.SMEM)
    cache_spec = pl.BlockSpec((SAMPLE_BT, WINDOW, KV_WIDTH), lambda i: (i, 0, 0))
    return pl.pallas_call(
        _swa_sample_kernel,
        grid=(N_BATCH // SAMPLE_BT,),
        in_specs=[smem, smem,
                  pl.BlockSpec((T_DEC, S_KEYS), lambda i: (0, 0)),
                  pl.BlockSpec((rows, ATT_WIDTH), lambda i: (i, 0)),
                  pl.BlockSpec((rows, KV_WIDTH), lambda i: (i, kcol)),
                  pl.BlockSpec((rows, KV_WIDTH), lambda i: (i, kcol + 1)),
                  cache_spec, cache_spec],
        out_specs=[pl.BlockSpec((rows, ATT_WIDTH), lambda i: (i, 0)), cache_spec, cache_spec],
        out_shape=[jax.ShapeDtypeStruct((N_SAMPLE, ATT_WIDTH), BF16),
                   jax.ShapeDtypeStruct((N_BATCH, WINDOW, KV_WIDTH), F32),
                   jax.ShapeDtypeStruct((N_BATCH, WINDOW, KV_WIDTH), F32)],
        scratch_shapes=[pltpu.VMEM((ATT_HEADS * T_DEC, S_KEYS), F32)],
        compiler_params=_params(("arbitrary",)),
        name="swa_sample",
    )(rb_flat, sinks, bucket, z, z, z, cache_k, cache_v)


LANE_IG, LANE_LF, LANE_B = 0, M_HEADS, 2 * M_HEADS


def _gate_table(zg, gate_bias, seg_len):
    L = zg.shape[0]
    g = zg + gate_bias
    lane = _iota((L, LANES), 1)
    lf = _log_sigmoid(g)
    lf_only = jnp.where(jnp.logical_and(lane >= LANE_LF, lane < LANE_B), lf, 0.0)
    row = _iota((L, L), 0)
    col = _iota((L, L), 1)
    same_seg = (row // seg_len) == (col // seg_len)
    tril = jnp.where(jnp.logical_and(col <= row, same_seg), 1.0, 0.0).astype(BF16)
    cum = pltpu.roll(_dot_exact_lhs01(tril, lf_only), M_HEADS, axis=1)
    table = jnp.where(lane < LANE_LF, g, jnp.where(lane < LANE_B, lf, jnp.where(lane < LANE_B + M_HEADS, cum, 0.0)))
    return table, jnp.logical_and(col <= row, same_seg)


def _mlstm_intra(table, table_t, mask, m0_col, h, q, k, v):
    b_c = table[:, LANE_B + h:LANE_B + h + 1]
    b_r = table_t[LANE_B + h:LANE_B + h + 1, :]
    ig_r = table_t[LANE_IG + h:LANE_IG + h + 1, :]
    log_d = jnp.where(mask, b_c - b_r + ig_r, NEG_INF)
    log_inter = b_c + m0_col
    m_t = jnp.maximum(log_inter, jnp.max(log_d, axis=1, keepdims=True))
    d = jnp.exp(log_d - m_t)
    w_inter = jnp.exp(log_inter - m_t)
    s = _dot_nt(q.astype(BF16), k.astype(BF16)) * d
    num_intra = _dot(s.astype(BF16), v.astype(BF16))
    den_intra = jnp.sum(s, axis=1, keepdims=True)
    return b_c, m_t, w_inter, num_intra, den_intra


def _mlstm_prompt_kernel(gb_ref, zg_ref, q_ref, k_ref, va_ref, vb_ref, oa_ref, ob_ref,
                         y_ref, c_out, n_out, m_out, c_scr, n_scr, m_scr):
    step = pl.program_id(0)
    L = ML_CHUNK

    @pl.when(step == 0)
    def _():
        c_scr[...] = jnp.zeros_like(c_scr)
        n_scr[...] = jnp.zeros_like(n_scr)
        m_scr[...] = jnp.zeros_like(m_scr)

    table, mask = _gate_table(zg_ref[...], gb_ref[...], L)
    table_t = table.T
    half = M_HEADS // 2
    for h in range(M_HEADS):
        v_ref, o_ref = (va_ref, oa_ref) if h < half else (vb_ref, ob_ref)
        vs = slice((h % half) * DV, (h % half + 1) * DV)
        q = q_ref[:, h * DK:(h + 1) * DK] * Q_SCALE
        k = k_ref[:, h * DK:(h + 1) * DK]
        v = v_ref[:, vs]
        m0 = m_scr[h:h + 1, 0:1]
        b_c, m_t, w_inter, num_intra, den_intra = _mlstm_intra(table, table_t, mask, m0, h, q, k, v)
        c_old = c_scr[h]
        n_old = n_scr[h:h + 1, :]
        num = num_intra + w_inter * _dot_nt(q.astype(BF16), c_old.astype(BF16))
        den = den_intra + w_inter * jnp.sum(q * n_old, axis=1, keepdims=True)
        hh = num / jnp.maximum(jnp.abs(den), jnp.exp(-m_t))
        y_ref[:, h * DV:(h + 1) * DV] = (_sigmoid(o_ref[:, vs]) * hh).astype(BF16)

        ig_c = table[:, LANE_IG + h:LANE_IG + h + 1]
        m_new = m_t[L - 1:L, :]
        b_last = b_c[L - 1:L, :]
        w_s = jnp.exp(b_last - b_c + ig_c - m_new)
        decay = jnp.exp(b_last + m0 - m_new)
        c_scr[h] = decay * c_old + _dot_tn((w_s * v).astype(BF16), k.astype(BF16))
        n_scr[h:h + 1, :] = decay * n_old + jnp.sum(w_s * k, axis=0, keepdims=True)
        m_scr[h:h + 1, :] = jnp.broadcast_to(m_new, (1, LANES))

    @pl.when(step == pl.num_programs(0) - 1)
    def _():
        c_out[...] = c_scr[...]
        n_out[...] = n_scr[...]
        m_out[...] = m_scr[...]


def _mlstm_prompt(z, zg, gate_bias):
    L = ML_CHUNK
    blk = M_HEADS * DK
    col = lambda c: pl.BlockSpec((L, blk), lambda i: (i, c))
    const = lambda shape: pl.BlockSpec(shape, lambda i: tuple(0 for _ in shape))
    return pl.pallas_call(
        _mlstm_prompt_kernel,
        grid=(N_PROMPT // L,),
        in_specs=[const((1, LANES)),
                  pl.BlockSpec((L, LANES), lambda i: (i, 0)),
                  col(3), col(4), col(5), col(6), col(7), col(8)],
        out_specs=[pl.BlockSpec((L, M_WIDTH), lambda i: (i, 0)),
                   const((M_HEADS, DV, DK)), const((M_HEADS, DK)), const((M_HEADS, LANES))],
        out_shape=[jax.ShapeDtypeStruct((N_PROMPT, M_WIDTH), BF16),
                   jax.ShapeDtypeStruct((M_HEADS, DV, DK), F32),
                   jax.ShapeDtypeStruct((M_HEADS, DK), F32),
                   jax.ShapeDtypeStruct((M_HEADS, LANES), F32)],
        scratch_shapes=[pltpu.VMEM((M_HEADS, DV, DK), F32),
                        pltpu.VMEM((M_HEADS, DK), F32),
                        pltpu.VMEM((M_HEADS, LANES), F32)],
        compiler_params=_params(("arbitrary",)),
        name="mlstm_prompt",
    )(gate_bias, zg, z, z, z, z, z, z)


S_ROWS = SAMPLE_BT * T_DEC


def _mlstm_sample_kernel(gb_ref, zg_ref, m0_ref, n0t_ref, q_ref, k_ref, va_ref, vb_ref, oa_ref, ob_ref,
                         c0_ref, n0_ref, y_ref, c_out, n_out, m_out):
    L = S_ROWS
    table, mask = _gate_table(zg_ref[...], gb_ref[...], T_DEC)
    table_t = jnp.concatenate([table, jnp.zeros((LANES - L, LANES), F32)], axis=0).T[:, 0:L]
    m0_all = m0_ref[...]
    lane = _iota((L, LANES), 1)
    row_b = _iota((SAMPLE_BT, L), 1) // T_DEC
    seg_sum = jnp.where(row_b == _iota((SAMPLE_BT, L), 0), 1.0, 0.0)
    m_tok = jnp.zeros((L, LANES), F32)
    half = M_HEADS // 2
    for h in range(M_HEADS):
        v_ref, o_ref = (va_ref, oa_ref) if h < half else (vb_ref, ob_ref)
        vs = slice((h % half) * DV, (h % half + 1) * DV)
        q = q_ref[:, h * DK:(h + 1) * DK] * Q_SCALE
        k = k_ref[:, h * DK:(h + 1) * DK]
        v = v_ref[:, vs]
        m0 = m0_all[:, h:h + 1]
        b_c, m_t, w_inter, num_intra, den_intra = _mlstm_intra(table, table_t, mask, m0, h, q, k, v)
        qb = q.astype(BF16)
        num_inter = jnp.concatenate(
            [_dot_nt(qb[b * T_DEC:(b + 1) * T_DEC], c0_ref[b, h].astype(BF16)) for b in range(SAMPLE_BT)], axis=0)
        num = num_intra + w_inter * num_inter
        den = den_intra + w_inter * jnp.sum(q * n0t_ref[:, h * DK:(h + 1) * DK], axis=1, keepdims=True)
        hh = num / jnp.maximum(jnp.abs(den), jnp.exp(-m_t))
        y_ref[:, h * DV:(h + 1) * DV] = (_sigmoid(o_ref[:, vs]) * hh).astype(BF16)
        m_tok = jnp.where(lane == h, m_t, m_tok)

        def last_tok(x):
            x3 = x.reshape(SAMPLE_BT, T_DEC, 1)
            return jnp.broadcast_to(x3[:, T_DEC - 1:T_DEC, :], x3.shape).reshape(L, 1)

        ig_c = table[:, LANE_IG + h:LANE_IG + h + 1]
        m_new = last_tok(m_t)
        b_last = last_tok(b_c)
        w_s = jnp.exp(b_last - b_c + ig_c - m_new)
        decay = jnp.exp(b_last + m0 - m_new)
        wv = (w_s * v).astype(BF16)
        kb = k.astype(BF16)
        rowsel = _iota((L, 1), 0) // T_DEC
        for b in range(SAMPLE_BT):
            dec_b = decay[b * T_DEC + T_DEC - 1:(b + 1) * T_DEC, :]
            wv_b = jnp.where(rowsel == b, wv, jnp.zeros_like(wv))
            c_out[b, h] = dec_b * c0_ref[b, h] + _dot_tn(wv_b, kb)
        dec_rows = decay.reshape(SAMPLE_BT, T_DEC, 1)[:, T_DEC - 1, :]
        n_out[:, h * DK:(h + 1) * DK] = dec_rows * n0_ref[:, h * DK:(h + 1) * DK] + jnp.dot(
            seg_sum, w_s * k, preferred_element_type=F32, precision=lax.Precision.HIGHEST)
    m_out[...] = m_tok


def _mlstm_sample(z, zg, gate_bias, m0_tok, n0_tok, state_c, state_n):
    L = S_ROWS
    blk = M_HEADS * DK
    col = lambda c: pl.BlockSpec((L, blk), lambda i: (i, c))
    c_spec = pl.BlockSpec((SAMPLE_BT, M_HEADS, DV, DK), lambda i: (i, 0, 0, 0))
    n_spec = pl.BlockSpec((SAMPLE_BT, blk), lambda i: (i, 0))
    return pl.pallas_call(
        _mlstm_sample_kernel,
        grid=(N_BATCH // SAMPLE_BT,),
        in_specs=[pl.BlockSpec((1, LANES), lambda i: (0, 0)),
                  pl.BlockSpec((L, LANES), lambda i: (i, 0)),
                  pl.BlockSpec((L, LANES), lambda i: (i, 0)),
                  pl.BlockSpec((L, blk), lambda i: (i, 0)),
                  col(3), col(4), col(5), col(6), col(7), col(8),
                  c_spec, n_spec],
        out_specs=[pl.BlockSpec((L, M_WIDTH), lambda i: (i, 0)), c_spec, n_spec,
                   pl.BlockSpec((L, LANES), lambda i: (i, 0))],
        out_shape=[jax.ShapeDtypeStruct((N_SAMPLE, M_WIDTH), BF16),
                   jax.ShapeDtypeStruct((N_BATCH, M_HEADS, DV, DK), F32),
                   jax.ShapeDtypeStruct((N_BATCH, blk), F32),
                   jax.ShapeDtypeStruct((N_SAMPLE, LANES), F32)],
        compiler_params=_params(("arbitrary",)),
        name="mlstm_sample",
    )(gate_bias, zg, m0_tok, n0_tok, z, z, z, z, z, z, state_c, state_n)


def _outproj_kernel(ya_ref, ym_ref, wa_ref, wm_ref, x_ref, gt_ref, o_ref):
    mix = _dot(ya_ref[...], wa_ref[...]) + _dot(ym_ref[...], wm_ref[...])
    o_ref[...] = x_ref[...] + gt_ref[...] * mix


def _outproj(ya, ym, w_out_bf, x, mod, per_row):
    n = x.shape[0]
    TM = TM_ROWMOD if per_row else TM_SHARED
    mrows = TM if per_row else 1
    mi = (lambda i: i) if per_row else (lambda i: 0)
    gate1_col = 2 * (D // TN_OUT)
    return pl.pallas_call(
        _outproj_kernel,
        grid=(n // TM, D // TN_OUT),
        in_specs=[pl.BlockSpec((TM, ATT_WIDTH), lambda i, j: (i, 0)),
                  pl.BlockSpec((TM, M_WIDTH), lambda i, j: (i, 0)),
                  pl.BlockSpec((ATT_WIDTH, TN_OUT), lambda i, j: (0, j)),
                  pl.BlockSpec((M_WIDTH, TN_OUT), lambda i, j: (1, j)),
                  pl.BlockSpec((TM, TN_OUT), lambda i, j: (i, j)),
                  pl.BlockSpec((mrows, TN_OUT), lambda i, j: (mi(i), gate1_col + j))],
        out_specs=pl.BlockSpec((TM, TN_OUT), lambda i, j: (i, j)),
        out_shape=jax.ShapeDtypeStruct((n, D), F32),
        compiler_params=_params(("arbitrary", "arbitrary")),
        name="outproj",
    )(ya, ym, w_out_bf, w_out_bf, x, mod)


def _router_kernel(x_ref, sh_ref, sc_ref, g_ref, wr_ref, br_ref, h_ref, route_ref):
    x = x_ref[...]
    y = x * lax.rsqrt(jnp.mean(x * x, axis=-1, keepdims=True) + EPS) * g_ref[...]
    h2 = y * (1.0 + sc_ref[...]) + sh_ref[...]
    h_ref[...] = h2
    logits = jnp.dot(h2, wr_ref[...], preferred_element_type=F32, precision=lax.Precision.HIGHEST) + br_ref[...]
    lane = _iota(logits.shape, 1)

    def first_max(vals):
        vmax = jnp.max(vals, axis=1, keepdims=True)
        idx = jnp.min(jnp.where(vals == vmax, lane, LANES), axis=1, keepdims=True)
        return vmax, idx

    gl = jnp.where(lane < N_GROUPS, logits, NEG_INF)
    gmax, grp = first_max(gl)
    p_grp = 1.0 / jnp.sum(jnp.exp(gl - gmax), axis=1, keepdims=True)
    e_lane = lane - N_GROUPS
    in_grp = jnp.logical_and(e_lane >= 0, jnp.logical_and(e_lane < N_EXPERTS, e_lane // EXP_PER_GROUP == grp))
    el = jnp.where(in_grp, logits, NEG_INF)
    v1, i1 = first_max(el)
    v2, i2 = first_max(jnp.where(lane == i1, NEG_INF, el))
    e2w = jnp.exp(v2 - v1)
    w1 = 1.0 / (1.0 + e2w)
    w2 = e2w / (1.0 + e2w)
    route = jnp.where(lane == 0, (i1 - N_GROUPS).astype(F32),
                      jnp.where(lane == 1, (i2 - N_GROUPS).astype(F32),
                                jnp.where(lane == 2, p_grp * w1, jnp.where(lane == 3, p_grp * w2, 0.0))))
    route_ref[...] = route


def _router(x1, mod, per_row, g_ffn, w_route, b_route):
    n = x1.shape[0]
    mrows = TM_TOK if per_row else 1
    mi = (lambda i: i) if per_row else (lambda i: 0)
    return pl.pallas_call(
        _router_kernel,
        grid=(n // TM_TOK,),
        in_specs=[pl.BlockSpec((TM_TOK, D), lambda i: (i, 0)),
                  pl.BlockSpec((mrows, D), lambda i: (mi(i), 3)),
                  pl.BlockSpec((mrows, D), lambda i: (mi(i), 4)),
                  pl.BlockSpec((1, D), lambda i: (0, 0)),
                  pl.BlockSpec((D, LANES), lambda i: (0, 0)),
                  pl.BlockSpec((1, LANES), lambda i: (0, 0))],
        out_specs=[pl.BlockSpec((TM_TOK, D), lambda i: (i, 0)),
                   pl.BlockSpec((TM_TOK, LANES), lambda i: (i, 0))],
        out_shape=[jax.ShapeDtypeStruct((n, D), F32), jax.ShapeDtypeStruct((n, LANES), F32)],
        compiler_params=_params(("arbitrary",)),
        name="router",
    )(x1, mod, mod, g_ffn, w_route, b_route)


def _rank_kernel(route_ref, rank_ref, cnt_ref, carry):
    step = pl.program_id(0)

    @pl.when(step == 0)
    def _():
        carry[...] = jnp.zeros_like(carry)

    route = route_ref[...]
    n = route.shape[0]
    lane = _iota((n, LANES), 1).astype(F32)
    o1 = jnp.where(lane == route[:, 0:1], 1.0, 0.0)
    o2 = jnp.where(lane == route[:, 1:2], 1.0, 0.0)
    both = o1 + o2
    strict = jnp.where(_iota((n, n), 1) < _iota((n, n), 0), 1.0, 0.0).astype(BF16)
    prior = _dot(strict, both.astype(BF16)) + carry[0:1, :]
    r1 = jnp.sum(o1 * prior, axis=1, keepdims=True)
    r2 = jnp.sum(o2 * prior, axis=1, keepdims=True)
    lane_i = _iota((n, LANES), 1)
    rank_ref[...] = jnp.where(lane_i == 0, r1, jnp.where(lane_i == 1, r2, 0.0))
    carry[...] = carry[...] + jnp.sum(both, axis=0, keepdims=True)
    cnt_ref[...] = carry[...]


def _rank(route):
    n = route.shape[0]
    return pl.pallas_call(
        _rank_kernel,
        grid=(n // TM_TOK,),
        in_specs=[pl.BlockSpec((TM_TOK, LANES), lambda i: (i, 0))],
        out_specs=[pl.BlockSpec((TM_TOK, LANES), lambda i: (i, 0)),
                   pl.BlockSpec((SUBLANES, LANES), lambda i: (0, 0))],
        out_shape=[jax.ShapeDtypeStruct((n, LANES), F32), jax.ShapeDtypeStruct((SUBLANES, LANES), F32)],
        scratch_shapes=[pltpu.VMEM((SUBLANES, LANES), F32)],
        compiler_params=_params(("arbitrary",)),
        name="rank",
    )(route)


def _row_copy(src, dst, sem):
    return pltpu.make_async_copy(src, dst, sem)


def _dest_rows(tok, e1_ref, e2_ref, r1_ref, r2_ref, ps_ref):
    return ps_ref[e1_ref[tok]] + r1_ref[tok], ps_ref[e2_ref[tok]] + r2_ref[tok]


def _dispatch_kernel(e1_ref, e2_ref, r1_ref, r2_ref, ps_ref, h_ref, xs_in, xs_out, sem):
    del xs_in
    base = pl.program_id(0) * TM_TOK

    def issue(r, carry):
        d1, d2 = _dest_rows(base + r, e1_ref, e2_ref, r1_ref, r2_ref, ps_ref)
        _row_copy(h_ref.at[pl.ds(r, 1)], xs_out.at[pl.ds(d1, 1)], sem.at[0]).start()
        _row_copy(h_ref.at[pl.ds(r, 1)], xs_out.at[pl.ds(d2, 1)], sem.at[0]).start()
        return carry

    lax.fori_loop(0, TM_TOK, issue, 0)

    def drain(r, carry):
        _row_copy(h_ref.at[pl.ds(0, 1)], xs_out.at[pl.ds(0, 1)], sem.at[0]).wait()
        _row_copy(h_ref.at[pl.ds(0, 1)], xs_out.at[pl.ds(0, 1)], sem.at[0]).wait()
        return carry

    lax.fori_loop(0, TM_TOK, drain, 0)


def _dispatch(sched, h2, xs_zero):
    grid_spec = pltpu.PrefetchScalarGridSpec(
        num_scalar_prefetch=5,
        grid=(N_TOK // TM_TOK,),
        in_specs=[pl.BlockSpec((TM_TOK, D), lambda i, *_: (i, 0)),
                  pl.BlockSpec(memory_space=pl.ANY)],
        out_specs=pl.BlockSpec(memory_space=pl.ANY),
        scratch_shapes=[pltpu.SemaphoreType.DMA((1,))],
    )
    return pl.pallas_call(
        _dispatch_kernel,
        grid_spec=grid_spec,
        out_shape=jax.ShapeDtypeStruct((A_PAD, D), F32),
        input_output_aliases={6: 0},
        compiler_params=_params(("arbitrary",)),
        name="dispatch",
    )(*sched, h2, xs_zero)


def _expert_up_kernel(te_ref, tv_ref, nt_ref, x_ref, wg_ref, wu_ref, o_ref):
    t = pl.program_id(1)

    @pl.when(tv_ref[t] == 1)
    def _():
        x = x_ref[...].astype(BF16)
        g = _dot(x, wg_ref[...].astype(BF16))
        u = _dot(x, wu_ref[...].astype(BF16))
        o_ref[...] = (g * _sigmoid(g) * u).astype(BF16)

    @pl.when(tv_ref[t] == 0)
    def _():
        o_ref[...] = jnp.zeros_like(o_ref)


def _expert_up(tiles, xs, w_gate, w_up):
    last = lambda t, nt: jnp.minimum(t, nt[0] - 1)
    grid_spec = pltpu.PrefetchScalarGridSpec(
        num_scalar_prefetch=3,
        grid=(D_FF // TN_FF, MAX_TILES),
        in_specs=[pl.BlockSpec((TM_MOE, D), lambda n, t, te, tv, nt: (last(t, nt), 0)),
                  pl.BlockSpec((None, D, TN_FF), lambda n, t, te, tv, nt: (te[t], 0, n)),
                  pl.BlockSpec((None, D, TN_FF), lambda n, t, te, tv, nt: (te[t], 0, n))],
        out_specs=pl.BlockSpec((TM_MOE, TN_FF), lambda n, t, te, tv, nt: (t, n)),
    )
    return pl.pallas_call(
        _expert_up_kernel,
        grid_spec=grid_spec,
        out_shape=jax.ShapeDtypeStruct((A_PAD, D_FF), BF16),
        compiler_params=_params(("arbitrary", "arbitrary")),
        name="expert_up",
    )(*tiles, xs, w_gate, w_up)


def _expert_down_kernel(te_ref, tv_ref, nt_ref, h_ref, wd_ref, o_ref):
    t = pl.program_id(1)

    @pl.when(tv_ref[t] == 1)
    def _():
        o_ref[...] = _dot(h_ref[...], wd_ref[...].astype(BF16))

    @pl.when(tv_ref[t] == 0)
    def _():
        o_ref[...] = jnp.zeros_like(o_ref)


def _expert_down(tiles, h1, w_down):
    last = lambda t, nt: jnp.minimum(t, nt[0] - 1)
    grid_spec = pltpu.PrefetchScalarGridSpec(
        num_scalar_prefetch=3,
        grid=(D // TN_DOWN, MAX_TILES),
        in_specs=[pl.BlockSpec((TM_MOE, D_FF), lambda n, t, te, tv, nt: (last(t, nt), 0)),
                  pl.BlockSpec((None, D_FF, TN_DOWN), lambda n, t, te, tv, nt: (te[t], 0, n))],
        out_specs=pl.BlockSpec((TM_MOE, TN_DOWN), lambda n, t, te, tv, nt: (t, n)),
    )
    return pl.pallas_call(
        _expert_down_kernel,
        grid_spec=grid_spec,
        out_shape=jax.ShapeDtypeStruct((A_PAD, D), F32),
        compiler_params=_params(("arbitrary", "arbitrary")),
        name="expert_down",
    )(*tiles, h1, w_down)


def _combine_kernel(e1_ref, e2_ref, r1_ref, r2_ref, ps_ref, x_ref, route_ref, gt_ref, gf_ref, o_hbm,
                    y_ref, buf, sem, *, tok_offset):
    base = tok_offset + pl.program_id(0) * TM_TOK

    def issue(r, carry):
        d1, d2 = _dest_rows(base + r, e1_ref, e2_ref, r1_ref, r2_ref, ps_ref)
        _row_copy(o_hbm.at[pl.ds(d1, 1)], buf.at[0, pl.ds(r, 1)], sem.at[0]).start()
        _row_copy(o_hbm.at[pl.ds(d2, 1)], buf.at[1, pl.ds(r, 1)], sem.at[0]).start()
        return carry

    lax.fori_loop(0, TM_TOK, issue, 0)

    def drain(r, carry):
        _row_copy(o_hbm.at[pl.ds(0, 1)], buf.at[0, pl.ds(0, 1)], sem.at[0]).wait()
        _row_copy(o_hbm.at[pl.ds(0, 1)], buf.at[1, pl.ds(0, 1)], sem.at[0]).wait()
        return carry

    lax.fori_loop(0, TM_TOK, drain, 0)

    route = route_ref[...]
    moe = route[:, 2:3] * buf[0] + route[:, 3:4] * buf[1]
    x2 = x_ref[...] + gt_ref[...] * moe
    y_ref[...] = x2 * lax.rsqrt(jnp.mean(x2 * x2, axis=-1, keepdims=True) + EPS) * gf_ref[...]


def _combine(sched, x1, route, mod, per_row, g_final, o_rows, tok_offset):
    n = x1.shape[0]
    mrows = TM_TOK if per_row else 1
    mi = (lambda i: i) if per_row else (lambda i: 0)
    off = tok_offset // TM_TOK
    grid_spec = pltpu.PrefetchScalarGridSpec(
        num_scalar_prefetch=5,
        grid=(n // TM_TOK,),
        in_specs=[pl.BlockSpec((TM_TOK, D), lambda i, *_: (i, 0)),
                  pl.BlockSpec((TM_TOK, LANES), lambda i, *_: (i + off, 0)),
                  pl.BlockSpec((mrows, D), lambda i, *_: (mi(i), 5)),
                  pl.BlockSpec((1, D), lambda i, *_: (0, 0)),
                  pl.BlockSpec(memory_space=pl.ANY)],
        out_specs=pl.BlockSpec((TM_TOK, D), lambda i, *_: (i, 0)),
        scratch_shapes=[pltpu.VMEM((TOP_K, TM_TOK, D), F32), pltpu.SemaphoreType.DMA((1,))],
    )
    return pl.pallas_call(
        functools.partial(_combine_kernel, tok_offset=tok_offset),
        grid_spec=grid_spec,
        out_shape=jax.ShapeDtypeStruct((n, D), F32),
        compiler_params=_params(("arbitrary",)),
        name="combine",
    )(*sched, x1, route, mod, g_final, o_rows)


def kernel(x_prompt, x_sample, cache_k, cache_v, state_C, state_n, state_m, c_prompt, c_sample, rel_bias, w_ada, b_ada, g_mix, g_ffn, w_in, sinks, b_igate, b_fgate, w_out, w_router_grp, b_router_grp, w_router_exp, b_router_exp, w_gate, w_up, w_down, g_final):
    xp = x_prompt.reshape(N_PROMPT, D)
    xs = x_sample.reshape(N_SAMPLE, D)

    c_all = jnp.concatenate([c_prompt, c_sample, jnp.zeros((C_ROWS - 1 - N_BATCH, D), F32)], axis=0)
    mod = _ada(c_all, w_ada[0], b_ada)
    mod_p = mod[0:1]
    mod_s = jnp.repeat(mod[1:1 + N_BATCH], T_DEC, axis=0)

    w_in_bf = w_in[0].astype(BF16)
    w_main = w_in_bf[:, :Z_WIDTH]
    w_gates = jnp.pad(w_in_bf[:, Z_WIDTH:], ((0, 0), (0, LANES - 2 * M_HEADS)))
    zp, zgp = _inproj(xp, mod_p, False, g_mix, w_main, w_gates)
    zs, zgs = _inproj(xs, mod_s, True, g_mix, w_main, w_gates)

    rb_flat = rel_bias.reshape(NUM_BUCKETS * ATT_HEADS)
    sink_v = sinks[0]
    ya_p = _swa_prompt(zp, rb_flat, sink_v)
    ya_s, nk_s, nv_s = _swa_sample(zs, cache_k[0].reshape(N_BATCH, WINDOW, KV_WIDTH),
                                   cache_v[0].reshape(N_BATCH, WINDOW, KV_WIDTH), rb_flat, sink_v)

    gate_bias = jnp.concatenate([b_igate[0], b_fgate[0], jnp.zeros((LANES - 2 * M_HEADS,), F32)]).reshape(1, LANES)
    ym_p, c_p, n_p, m_p = _mlstm_prompt(zp, zgp, gate_bias)
    m0_tok = jnp.pad(jnp.repeat(state_m[0], T_DEC, axis=0), ((0, 0), (0, LANES - M_HEADS)))
    n0_flat = state_n[0].reshape(N_BATCH, M_HEADS * DK)
    n0_tok = jnp.repeat(n0_flat, T_DEC, axis=0)
    ym_s, c_s, n_s, m_s = _mlstm_sample(zs, zgs, gate_bias, m0_tok, n0_tok, state_C[0], n0_flat)

    w_out_bf = w_out[0].astype(BF16)
    x1_p = _outproj(ya_p, ym_p, w_out_bf, xp, mod_p, False)
    x1_s = _outproj(ya_s, ym_s, w_out_bf, xs, mod_s, True)

    w_route = jnp.pad(jnp.concatenate([w_router_grp[0], w_router_exp[0]], axis=1),
                      ((0, 0), (0, LANES - N_GROUPS - N_EXPERTS)))
    b_route = jnp.pad(jnp.concatenate([b_router_grp[0], b_router_exp[0]]),
                      (0, LANES - N_GROUPS - N_EXPERTS)).reshape(1, LANES)
    h2_p, route_p = _router(x1_p, mod_p, False, g_ffn, w_route, b_route)
    h2_s, route_s = _router(x1_s, mod_s, True, g_ffn, w_route, b_route)
    h2 = jnp.concatenate([h2_p, h2_s], axis=0)
    route = jnp.concatenate([route_p, route_s], axis=0)

    rank, counts = _rank(route)

    cnt = counts[0, :N_EXPERTS].astype(jnp.int32)
    tiles_per = (cnt + TM_MOE - 1) // TM_MOE
    tile_end = jnp.cumsum(tiles_per)
    pad_start = (tile_end - tiles_per) * TM_MOE
    n_tiles = tile_end[-1]
    tile_ids = jnp.arange(MAX_TILES, dtype=jnp.int32)
    tile_expert = jnp.minimum(jnp.searchsorted(tile_end, tile_ids, side="right"), N_EXPERTS - 1).astype(jnp.int32)
    last_expert = tile_expert[jnp.maximum(n_tiles - 1, 0)]
    tile_valid = (tile_ids < n_tiles).astype(jnp.int32)
    tile_expert = jnp.where(tile_valid == 1, tile_expert, last_expert)
    tiles = (tile_expert, tile_valid, n_tiles.reshape(1).astype(jnp.int32))
    sched = (route[:, 0].astype(jnp.int32), route[:, 1].astype(jnp.int32),
             rank[:, 0].astype(jnp.int32), rank[:, 1].astype(jnp.int32), pad_start.astype(jnp.int32))

    xs_rows = _dispatch(sched, h2, jnp.zeros((A_PAD, D), F32))
    h1 = _expert_up(tiles, xs_rows, w_gate[0], w_up[0])
    o_rows = _expert_down(tiles, h1, w_down[0])

    gf = g_final.reshape(1, D)
    y_p = _combine(sched, x1_p, route, mod_p, False, gf, o_rows, 0)
    y_s = _combine(sched, x1_s, route, mod_s, True, gf, o_rows, N_PROMPT)

    kv5 = lambda a: a.reshape(1, -1, WINDOW, KV_HEADS, HEAD_DIM)
    kcol = ATT_WIDTH
    nk_p = zp[N_PROMPT - WINDOW:, kcol:kcol + KV_WIDTH]
    nv_p = zp[N_PROMPT - WINDOW:, kcol + KV_WIDTH:kcol + 2 * KV_WIDTH]
    return (y_p.reshape(1, N_PROMPT, D), y_s.reshape(N_BATCH, T_DEC, D),
            kv5(nk_p), kv5(nv_p),
            c_p.reshape(1, 1, M_HEADS, DV, DK), n_p.reshape(1, 1, M_HEADS, DK), m_p[:, 0].reshape(1, 1, M_HEADS),
            kv5(nk_s), kv5(nv_s),
            c_s.reshape(1, N_BATCH, M_HEADS, DV, DK), n_s.reshape(1, N_BATCH, M_HEADS, DK),
            m_s.reshape(N_BATCH, T_DEC, LANES)[:, T_DEC - 1, :M_HEADS].reshape(1, N_BATCH, M_HEADS))
```

```python
import functools
import math

import numpy as np
import jax
import jax.numpy as jnp
from jax import lax
from jax.experimental import pallas as pl
from jax.experimental.pallas import tpu as pltpu

F32 = jnp.float32
BF16 = jnp.bfloat16
NEG_INF = float("-inf")

D = 4096
N_PROMPT = 8192
N_BATCH = 128
T_DEC = 8
N_SAMPLE = N_BATCH * T_DEC
N_TOK = N_PROMPT + N_SAMPLE
HEAD_DIM = 128
ATT_HEADS = 16
KV_HEADS = 4
GROUP = ATT_HEADS // KV_HEADS
WINDOW = 128
ATT_WIDTH = ATT_HEADS * HEAD_DIM
KV_WIDTH = KV_HEADS * HEAD_DIM
NUM_BUCKETS = 32
MAX_EXACT = 16
MAX_DISTANCE = 128
M_HEADS = 8
DK = 128
DV = 256
M_WIDTH = M_HEADS * DV
Z_WIDTH = ATT_WIDTH + 2 * KV_WIDTH + 2 * M_HEADS * DK + 2 * M_WIDTH
N_GROUPS = 4
EXP_PER_GROUP = 8
N_EXPERTS = N_GROUPS * EXP_PER_GROUP
TOP_K = 2
D_FF = 1024
EPS = 1e-6
ATT_SCALE = HEAD_DIM ** -0.5
Q_SCALE = DK ** -0.5

LANES = 128
SUBLANES = 8
VMEM_LIMIT = 56 * 1024 * 1024

TM_SHARED = 512
TM_ROWMOD = 256
TN_IN = 1024
TN_OUT = 1024
TN_ADA = 512
C_ROWS = 136
ATT_BLOCK = 128
SAMPLE_BT = 8
ML_CHUNK = 256
TM_MOE = 256
TN_FF = 512
N_ASSIGN = N_TOK * TOP_K
MAX_TILES = N_ASSIGN // TM_MOE + N_EXPERTS
A_PAD = MAX_TILES * TM_MOE
TM_TOK = 256


def _params(sem):
    return pltpu.CompilerParams(dimension_semantics=sem, vmem_limit_bytes=VMEM_LIMIT)


def _iota(shape, dim):
    return lax.broadcasted_iota(jnp.int32, shape, dim)


def _dot(a, b):
    return jnp.dot(a, b, preferred_element_type=F32)


def _dot_nt(a, b):
    return lax.dot_general(a, b, (((1,), (1,)), ((), ())), preferred_element_type=F32)


def _dot_tn(a, b):
    return lax.dot_general(a, b, (((0,), (0,)), ((), ())), preferred_element_type=F32)


def _split3(x):
    x1 = x.astype(BF16)
    r1 = x - x1.astype(F32)
    x2 = r1.astype(BF16)
    r2 = r1 - x2.astype(F32)
    return x1, x2, r2.astype(BF16)


def _dot_exact_lhs01(a01, x):
    x1, x2, x3 = _split3(x)
    return _dot(a01, x1) + _dot(a01, x2) + _dot(a01, x3)


def _sigmoid(x):
    return 1.0 / (1.0 + jnp.exp(-x))


def _log_sigmoid(x):
    return jnp.minimum(x, 0.0) - jnp.log(1.0 + jnp.exp(-jnp.abs(x)))


def _ada_kernel(c_ref, w_ref, b_ref, o_ref):
    c = c_ref[...]
    s = (c * _sigmoid(c)).astype(BF16)
    o_ref[...] = _dot(s, w_ref[...].astype(BF16)) + b_ref[...]


def _ada(c_all, w_ada, b_ada):
    n = w_ada.shape[1]
    return pl.pallas_call(
        _ada_kernel,
        grid=(n // TN_ADA,),
        in_specs=[pl.BlockSpec((C_ROWS, D), lambda j: (0, 0)),
                  pl.BlockSpec((D, TN_ADA), lambda j: (0, j)),
                  pl.BlockSpec((1, TN_ADA), lambda j: (0, j))],
        out_specs=pl.BlockSpec((C_ROWS, TN_ADA), lambda j: (0, j)),
        out_shape=jax.ShapeDtypeStruct((C_ROWS, n), F32),
        compiler_params=_params(("arbitrary",)),
        name="ada",
    )(c_all, w_ada, b_ada)


def _inproj_kernel(x_ref, sh_ref, sc_ref, g_ref, w_ref, wg_ref, z_ref, zg_ref, h_scr):
    @pl.when(pl.program_id(1) == 0)
    def _():
        x = x_ref[...]
        y = x * lax.rsqrt(jnp.mean(x * x, axis=-1, keepdims=True) + EPS) * g_ref[...]
        hb = (y * (1.0 + sc_ref[...]) + sh_ref[...]).astype(BF16)
        h_scr[...] = hb
        zg_ref[...] = _dot(hb, wg_ref[...])

    z_ref[...] = _dot(h_scr[...], w_ref[...])


def _inproj(x, mod, per_row, g_mix, w_main, w_gate):
    n = x.shape[0]
    TM = TM_ROWMOD if per_row else TM_SHARED
    mrows = TM if per_row else 1
    mi = (lambda i: i) if per_row else (lambda i: 0)
    return pl.pallas_call(
        _inproj_kernel,
        grid=(n // TM, Z_WIDTH // TN_IN),
        in_specs=[pl.BlockSpec((TM, D), lambda i, j: (i, 0)),
                  pl.BlockSpec((mrows, D), lambda i, j: (mi(i), 0)),
                  pl.BlockSpec((mrows, D), lambda i, j: (mi(i), 1)),
                  pl.BlockSpec((1, D), lambda i, j: (0, 0)),
                  pl.BlockSpec((D, TN_IN), lambda i, j: (0, j)),
                  pl.BlockSpec((D, LANES), lambda i, j: (0, 0))],
        out_specs=[pl.BlockSpec((TM, TN_IN), lambda i, j: (i, j)),
                   pl.BlockSpec((TM, LANES), lambda i, j: (i, 0))],
        out_shape=[jax.ShapeDtypeStruct((n, Z_WIDTH), F32),
                   jax.ShapeDtypeStruct((n, LANES), F32)],
        scratch_shapes=[pltpu.VMEM((TM, D), BF16)],
        compiler_params=_params(("arbitrary", "arbitrary")),
        name="inproj",
    )(x, mod, mod, g_mix, w_main, w_gate)


def _t5_bucket_np(dist):
    n = np.maximum(dist, 0)
    nf = np.maximum(n, 1).astype(np.float32)
    large = MAX_EXACT + (np.log(nf / MAX_EXACT) / math.log(MAX_DISTANCE / MAX_EXACT)
                         * (NUM_BUCKETS - MAX_EXACT)).astype(np.int32)
    large = np.minimum(large, NUM_BUCKETS - 1)
    return np.where(n < MAX_EXACT, n, large).astype(np.int32)


def _bucket_table(n_q, n_keys_valid, n_keys_padded):
    t = np.arange(n_q)[:, None]
    j = np.arange(n_keys_padded)[None, :]
    dist = t + WINDOW - j
    valid = (dist >= 0) & (dist < WINDOW) & (j < n_keys_valid)
    return np.where(valid, _t5_bucket_np(dist), -1).astype(np.int32)


def _fill_bias(bucket_ref, rb_ref, bias_scr, rows):
    bk = bucket_ref[...]
    for h in range(ATT_HEADS):
        acc = jnp.full(bk.shape, NEG_INF, F32)
        for b in range(NUM_BUCKETS):
            acc = jnp.where(bk == b, rb_ref[b * ATT_HEADS + h], acc)
        bias_scr[h * rows:(h + 1) * rows, :] = acc


def _sink_softmax_av(lg, sink_col, v2):
    m = jnp.maximum(jnp.max(lg, axis=-1, keepdims=True), sink_col)
    p = jnp.exp(lg - m)
    den = jnp.sum(p, axis=-1, keepdims=True) + jnp.exp(sink_col - m)
    return _dot((p / den).astype(BF16), v2)


def _swa_prompt_kernel(rb_ref, sink_ref, bucket_ref, q_ref, kp_ref, kc_ref, vp_ref, vc_ref,
                       o_ref, bias_scr):
    i = pl.program_id(0)

    @pl.when(i == 0)
    def _():
        _fill_bias(bucket_ref, rb_ref, bias_scr, ATT_BLOCK)

    rows = GROUP * ATT_BLOCK
    first_prev = jnp.logical_and(i == 0, _iota((rows, 2 * ATT_BLOCK), 1) < ATT_BLOCK)
    for g in range(KV_HEADS):
        ks = slice(g * HEAD_DIM, (g + 1) * HEAD_DIM)
        k2 = jnp.concatenate([kp_ref[:, ks], kc_ref[:, ks]], axis=0).astype(BF16)
        v2 = jnp.concatenate([vp_ref[:, ks], vc_ref[:, ks]], axis=0).astype(BF16)
        q4 = jnp.concatenate(
            [q_ref[:, (g * GROUP + r) * HEAD_DIM:(g * GROUP + r + 1) * HEAD_DIM] for r in range(GROUP)],
            axis=0).astype(BF16)
        lg = _dot_nt(q4, k2) * ATT_SCALE + bias_scr[g * rows:(g + 1) * rows, :]
        lg = jnp.where(first_prev, NEG_INF, lg)
        sink_col = jnp.concatenate(
            [jnp.full((ATT_BLOCK, 1), sink_ref[g * GROUP + r], F32) for r in range(GROUP)], axis=0)
        o = _sink_softmax_av(lg, sink_col, v2)
        for r in range(GROUP):
            h = g * GROUP + r
            o_ref[:, h * HEAD_DIM:(h + 1) * HEAD_DIM] = o[r * ATT_BLOCK:(r + 1) * ATT_BLOCK].astype(BF16)


def _swa_prompt(z, rb_flat, sinks):
    nb = N_PROMPT // ATT_BLOCK
    bucket = jnp.asarray(_bucket_table(ATT_BLOCK, 2 * ATT_BLOCK, 2 * ATT_BLOCK))
    kcol = ATT_WIDTH // KV_WIDTH
    prev = lambda i: jnp.maximum(i - 1, 0)
    smem = pl.BlockSpec(memory_space=pltpu.SMEM)
    return pl.pallas_call(
        _swa_prompt_kernel,
        grid=(nb,),
        in_specs=[smem, smem,
                  pl.BlockSpec((ATT_BLOCK, 2 * ATT_BLOCK), lambda i: (0, 0)),
                  pl.BlockSpec((ATT_BLOCK, ATT_WIDTH), lambda i: (i, 0)),
                  pl.BlockSpec((ATT_BLOCK, KV_WIDTH), lambda i: (prev(i), kcol)),
                  pl.BlockSpec((ATT_BLOCK, KV_WIDTH), lambda i: (i, kcol)),
                  pl.BlockSpec((ATT_BLOCK, KV_WIDTH), lambda i: (prev(i), kcol + 1)),
                  pl.BlockSpec((ATT_BLOCK, KV_WIDTH), lambda i: (i, kcol + 1))],
        out_specs=pl.BlockSpec((ATT_BLOCK, ATT_WIDTH), lambda i: (i, 0)),
        out_shape=jax.ShapeDtypeStruct((N_PROMPT, ATT_WIDTH), BF16),
        scratch_shapes=[pltpu.VMEM((ATT_HEADS * ATT_BLOCK, 2 * ATT_BLOCK), F32)],
        compiler_params=_params(("arbitrary",)),
        name="swa_prompt",
    )(rb_flat, sinks, bucket, z, z, z, z, z)


S_KEYS = 2 * WINDOW


def _swa_sample_kernel(rb_ref, sink_ref, bucket_ref, q_ref, kn_ref, vn_ref, ck_ref, cv_ref,
                       o_ref, nk_ref, nv_ref, bias_scr):
    @pl.when(pl.program_id(0) == 0)
    def _():
        _fill_bias(bucket_ref, rb_ref, bias_scr, T_DEC)

    nk_ref[:, 0:WINDOW - T_DEC, :] = ck_ref[:, T_DEC:WINDOW, :]
    nk_ref[:, WINDOW - T_DEC:WINDOW, :] = kn_ref[...].reshape(SAMPLE_BT, T_DEC, KV_WIDTH)
    nv_ref[:, 0:WINDOW - T_DEC, :] = cv_ref[:, T_DEC:WINDOW, :]
    nv_ref[:, WINDOW - T_DEC:WINDOW, :] = vn_ref[...].reshape(SAMPLE_BT, T_DEC, KV_WIDTH)

    rows = GROUP * T_DEC
    pad = jnp.zeros((S_KEYS - WINDOW - T_DEC, HEAD_DIM), F32)
    for b in range(SAMPLE_BT):
        ts = slice(b * T_DEC, (b + 1) * T_DEC)
        for g in range(KV_HEADS):
            ks = slice(g * HEAD_DIM, (g + 1) * HEAD_DIM)
            k2 = jnp.concatenate([ck_ref[b, :, ks], kn_ref[ts, ks], pad], axis=0).astype(BF16)
            v2 = jnp.concatenate([cv_ref[b, :, ks], vn_ref[ts, ks], pad], axis=0).astype(BF16)
            q4 = jnp.concatenate(
                [q_ref[ts, (g * GROUP + r) * HEAD_DIM:(g * GROUP + r + 1) * HEAD_DIM] for r in range(GROUP)],
                axis=0).astype(BF16)
            lg = _dot_nt(q4, k2) * ATT_SCALE + bias_scr[g * rows:(g + 1) * rows, :]
            sink_col = jnp.concatenate(
                [jnp.full((T_DEC, 1), sink_ref[g * GROUP + r], F32) for r in range(GROUP)], axis=0)
            o = _sink_softmax_av(lg, sink_col, v2)
            for r in range(GROUP):
                h = g * GROUP + r
                o_ref[ts, h * HEAD_DIM:(h + 1) * HEAD_DIM] = o[r * T_DEC:(r + 1) * T_DEC].astype(BF16)


def _swa_sample(z, cache_k, cache_v, rb_flat, sinks):
    rows = SAMPLE_BT * T_DEC
    bucket = jnp.asarray(_bucket_table(T_DEC, WINDOW + T_DEC, S_KEYS))
    kcol = ATT_WIDTH // KV_WIDTH
    smem = pl.BlockSpec(memory_space=pltpu.SMEM)
    cache_spec = pl.BlockSpec((SAMPLE_BT, WINDOW, KV_WIDTH), lambda i: (i, 0, 0))
    return pl.pallas_call(
        _swa_sample_kernel,
        grid=(N_BATCH // SAMPLE_BT,),
        in_specs=[smem, smem,
                  pl.BlockSpec((T_DEC, S_KEYS), lambda i: (0, 0)),
                  pl.BlockSpec((rows, ATT_WIDTH), lambda i: (i, 0)),
                  pl.BlockSpec((rows, KV_WIDTH), lambda i: (i, kcol)),
                  pl.BlockSpec((rows, KV_WIDTH), lambda i: (i, kcol + 1)),
                  cache_spec, cache_spec],
        out_specs=[pl.BlockSpec((rows, ATT_WIDTH), lambda i: (i, 0)), cache_spec, cache_spec],
        out_shape=[jax.ShapeDtypeStruct((N_SAMPLE, ATT_WIDTH), BF16),
                   jax.ShapeDtypeStruct((N_BATCH, WINDOW, KV_WIDTH), F32),
                   jax.ShapeDtypeStruct((N_BATCH, WINDOW, KV_WIDTH), F32)],
        scratch_shapes=[pltpu.VMEM((ATT_HEADS * T_DEC, S_KEYS), F32)],
        compiler_params=_params(("arbitrary",)),
        name="swa_sample",
    )(rb_flat, sinks, bucket, z, z, z, cache_k, cache_v)


LANE_IG, LANE_LF, LANE_B = 0, M_HEADS, 2 * M_HEADS


def _gate_table(zg, gate_bias, seg_len):
    L = zg.shape[0]
    g = zg + gate_bias
    lane = _iota((L, LANES), 1)
    lf = _log_sigmoid(g)
    lf_only = jnp.where(jnp.logical_and(lane >= LANE_LF, lane < LANE_B), lf, 0.0)
    row = _iota((L, L), 0)
    col = _iota((L, L), 1)
    same_seg = (row // seg_len) == (col // seg_len)
    tril = jnp.where(jnp.logical_and(col <= row, same_seg), 1.0, 0.0).astype(BF16)
    cum = pltpu.roll(_dot_exact_lhs01(tril, lf_only), M_HEADS, axis=1)
    table = jnp.where(lane < LANE_LF, g, jnp.where(lane < LANE_B, lf, jnp.where(lane < LANE_B + M_HEADS, cum, 0.0)))
    return table, jnp.logical_and(col <= row, same_seg)


def _mlstm_intra(table, table_t, mask, m0_col, h, q, k, v):
    b_c = table[:, LANE_B + h:LANE_B + h + 1]
    b_r = table_t[LANE_B + h:LANE_B + h + 1, :]
    ig_r = table_t[LANE_IG + h:LANE_IG + h + 1, :]
    log_d = jnp.where(mask, b_c - b_r + ig_r, NEG_INF)
    log_inter = b_c + m0_col
    m_t = jnp.maximum(log_inter, jnp.max(log_d, axis=1, keepdims=True))
    d = jnp.exp(log_d - m_t)
    w_inter = jnp.exp(log_inter - m_t)
    s = _dot_nt(q.astype(BF16), k.astype(BF16)) * d
    num_intra = _dot(s.astype(BF16), v.astype(BF16))
    den_intra = jnp.sum(s, axis=1, keepdims=True)
    return b_c, m_t, w_inter, num_intra, den_intra


def _mlstm_prompt_kernel(gb_ref, zg_ref, q_ref, k_ref, va_ref, vb_ref, oa_ref, ob_ref,
                         y_ref, c_out, n_out, m_out, c_scr, n_scr, m_scr):
    step = pl.program_id(0)
    L = ML_CHUNK

    @pl.when(step == 0)
    def _():
        c_scr[...] = jnp.zeros_like(c_scr)
        n_scr[...] = jnp.zeros_like(n_scr)
        m_scr[...] = jnp.zeros_like(m_scr)

    table, mask = _gate_table(zg_ref[...], gb_ref[...], L)
    table_t = table.T
    half = M_HEADS // 2
    for h in range(M_HEADS):
        v_ref, o_ref = (va_ref, oa_ref) if h < half else (vb_ref, ob_ref)
        vs = slice((h % half) * DV, (h % half + 1) * DV)
        q = q_ref[:, h * DK:(h + 1) * DK] * Q_SCALE
        k = k_ref[:, h * DK:(h + 1) * DK]
        v = v_ref[:, vs]
        m0 = m_scr[h:h + 1, 0:1]
        b_c, m_t, w_inter, num_intra, den_intra = _mlstm_intra(table, table_t, mask, m0, h, q, k, v)
        c_old = c_scr[h]
        n_old = n_scr[h:h + 1, :]
        num = num_intra + w_inter * _dot_nt(q.astype(BF16), c_old.astype(BF16))
        den = den_intra + w_inter * jnp.sum(q * n_old, axis=1, keepdims=True)
        hh = num / jnp.maximum(jnp.abs(den), jnp.exp(-m_t))
        y_ref[:, h * DV:(h + 1) * DV] = (_sigmoid(o_ref[:, vs]) * hh).astype(BF16)

        ig_c = table[:, LANE_IG + h:LANE_IG + h + 1]
        m_new = m_t[L - 1:L, :]
        b_last = b_c[L - 1:L, :]
        w_s = jnp.exp(b_last - b_c + ig_c - m_new)
        decay = jnp.exp(b_last + m0 - m_new)
        c_scr[h] = decay * c_old + _dot_tn((w_s * v).astype(BF16), k.astype(BF16))
        n_scr[h:h + 1, :] = decay * n_old + jnp.sum(w_s * k, axis=0, keepdims=True)
        m_scr[h:h + 1, :] = jnp.broadcast_to(m_new, (1, LANES))

    @pl.when(step == pl.num_programs(0) - 1)
    def _():
        c_out[...] = c_scr[...]
        n_out[...] = n_scr[...]
        m_out[...] = m_scr[...]


def _mlstm_prompt(z, zg, gate_bias):
    L = ML_CHUNK
    blk = M_HEADS * DK
    col = lambda c: pl.BlockSpec((L, blk), lambda i: (i, c))
    const = lambda shape: pl.BlockSpec(shape, lambda i: tuple(0 for _ in shape))
    return pl.pallas_call(
        _mlstm_prompt_kernel,
        grid=(N_PROMPT // L,),
        in_specs=[const((1, LANES)),
                  pl.BlockSpec((L, LANES), lambda i: (i, 0)),
                  col(3), col(4), col(5), col(6), col(7), col(8)],
        out_specs=[pl.BlockSpec((L, M_WIDTH), lambda i: (i, 0)),
                   const((M_HEADS, DV, DK)), const((M_HEADS, DK)), const((M_HEADS, LANES))],
        out_shape=[jax.ShapeDtypeStruct((N_PROMPT, M_WIDTH), BF16),
                   jax.ShapeDtypeStruct((M_HEADS, DV, DK), F32),
                   jax.ShapeDtypeStruct((M_HEADS, DK), F32),
                   jax.ShapeDtypeStruct((M_HEADS, LANES), F32)],
        scratch_shapes=[pltpu.VMEM((M_HEADS, DV, DK), F32),
                        pltpu.VMEM((M_HEADS, DK), F32),
                        pltpu.VMEM((M_HEADS, LANES), F32)],
        compiler_params=_params(("arbitrary",)),
        name="mlstm_prompt",
    )(gate_bias, zg, z, z, z, z, z, z)


S_ROWS = SAMPLE_BT * T_DEC


def _mlstm_sample_kernel(gb_ref, zg_ref, m0_ref, n0t_ref, q_ref, k_ref, va_ref, vb_ref, oa_ref, ob_ref,
                         c0_ref, n0_ref, y_ref, c_out, n_out, m_out):
    L = S_ROWS
    table, mask = _gate_table(zg_ref[...], gb_ref[...], T_DEC)
    table_t = jnp.concatenate([table, jnp.zeros((LANES - L, LANES), F32)], axis=0).T[:, 0:L]
    m0_all = m0_ref[...]
    lane = _iota((L, LANES), 1)
    row_b = _iota((SAMPLE_BT, L), 1) // T_DEC
    seg_sum = jnp.where(row_b == _iota((SAMPLE_BT, L), 0), 1.0, 0.0)
    m_tok = jnp.zeros((L, LANES), F32)
    half = M_HEADS // 2
    for h in range(M_HEADS):
        v_ref, o_ref = (va_ref, oa_ref) if h < half else (vb_ref, ob_ref)
        vs = slice((h % half) * DV, (h % half + 1) * DV)
        q = q_ref[:, h * DK:(h + 1) * DK] * Q_SCALE
        k = k_ref[:, h * DK:(h + 1) * DK]
        v = v_ref[:, vs]
        m0 = m0_all[:, h:h + 1]
        b_c, m_t, w_inter, num_intra, den_intra = _mlstm_intra(table, table_t, mask, m0, h, q, k, v)
        qb = q.astype(BF16)
        num_inter = jnp.concatenate(
            [_dot_nt(qb[b * T_DEC:(b + 1) * T_DEC], c0_ref[b, h].astype(BF16)) for b in range(SAMPLE_BT)], axis=0)
        num = num_intra + w_inter * num_inter
        den = den_intra + w_inter * jnp.sum(q * n0t_ref[:, h * DK:(h + 1) * DK], axis=1, keepdims=True)
        hh = num / jnp.maximum(jnp.abs(den), jnp.exp(-m_t))
        y_ref[:, h * DV:(h + 1) * DV] = (_sigmoid(o_ref[:, vs]) * hh).astype(BF16)
        m_tok = jnp.where(lane == h, m_t, m_tok)

        def last_tok(x):
            x3 = x.reshape(SAMPLE_BT, T_DEC, 1)
            return jnp.broadcast_to(x3[:, T_DEC - 1:T_DEC, :], x3.shape).reshape(L, 1)

        ig_c = table[:, LANE_IG + h:LANE_IG + h + 1]
        m_new = last_tok(m_t)
        b_last = last_tok(b_c)
        w_s = jnp.exp(b_last - b_c + ig_c - m_new)
        decay = jnp.exp(b_last + m0 - m_new)
        wv = (w_s * v).astype(BF16)
        kb = k.astype(BF16)
        rowsel = _iota((L, 1), 0) // T_DEC
        for b in range(SAMPLE_BT):
            dec_b = decay[b * T_DEC + T_DEC - 1:(b + 1) * T_DEC, :]
            wv_b = jnp.where(rowsel == b, wv, jnp.zeros_like(wv))
            c_out[b, h] = dec_b * c0_ref[b, h] + _dot_tn(wv_b, kb)
        dec_rows = decay.reshape(SAMPLE_BT, T_DEC, 1)[:, T_DEC - 1, :]
        n_out[:, h * DK:(h + 1) * DK] = dec_rows * n0_ref[:, h * DK:(h + 1) * DK] + jnp.dot(
            seg_sum, w_s * k, preferred_element_type=F32, precision=lax.Precision.HIGHEST)
    m_out[...] = m_tok


def _mlstm_sample(z, zg, gate_bias, m0_tok, n0_tok, state_c, state_n):
    L = S_ROWS
    blk = M_HEADS * DK
    col = lambda c: pl.BlockSpec((L, blk), lambda i: (i, c))
    c_spec = pl.BlockSpec((SAMPLE_BT, M_HEADS, DV, DK), lambda i: (i, 0, 0, 0))
    n_spec = pl.BlockSpec((SAMPLE_BT, blk), lambda i: (i, 0))
    return pl.pallas_call(
        _mlstm_sample_kernel,
        grid=(N_BATCH // SAMPLE_BT,),
        in_specs=[pl.BlockSpec((1, LANES), lambda i: (0, 0)),
                  pl.BlockSpec((L, LANES), lambda i: (i, 0)),
                  pl.BlockSpec((L, LANES), lambda i: (i, 0)),
                  pl.BlockSpec((L, blk), lambda i: (i, 0)),
                  col(3), col(4), col(5), col(6), col(7), col(8),
                  c_spec, n_spec],
        out_specs=[pl.BlockSpec((L, M_WIDTH), lambda i: (i, 0)), c_spec, n_spec,
                   pl.BlockSpec((L, LANES), lambda i: (i, 0))],
        out_shape=[jax.ShapeDtypeStruct((N_SAMPLE, M_WIDTH), BF16),
                   jax.ShapeDtypeStruct((N_BATCH, M_HEADS, DV, DK), F32),
                   jax.ShapeDtypeStruct((N_BATCH, blk), F32),
                   jax.ShapeDtypeStruct((N_SAMPLE, LANES), F32)],
        compiler_params=_params(("arbitrary",)),
        name="mlstm_sample",
    )(gate_bias, zg, m0_tok, n0_tok, z, z, z, z, z, z, state_c, state_n)


def _outproj_kernel(ya_ref, ym_ref, wa_ref, wm_ref, x_ref, gt_ref, o_ref):
    mix = _dot(ya_ref[...], wa_ref[...]) + _dot(ym_ref[...], wm_ref[...])
    o_ref[...] = x_ref[...] + gt_ref[...] * mix


def _outproj(ya, ym, w_out_bf, x, mod, per_row):
    n = x.shape[0]
    TM = TM_ROWMOD if per_row else TM_SHARED
    mrows = TM if per_row else 1
    mi = (lambda i: i) if per_row else (lambda i: 0)
    gate1_col = 2 * (D // TN_OUT)
    return pl.pallas_call(
        _outproj_kernel,
        grid=(n // TM, D // TN_OUT),
        in_specs=[pl.BlockSpec((TM, ATT_WIDTH), lambda i, j: (i, 0)),
                  pl.BlockSpec((TM, M_WIDTH), lambda i, j: (i, 0)),
                  pl.BlockSpec((ATT_WIDTH, TN_OUT), lambda i, j: (0, j)),
                  pl.BlockSpec((M_WIDTH, TN_OUT), lambda i, j: (1, j)),
                  pl.BlockSpec((TM, TN_OUT), lambda i, j: (i, j)),
                  pl.BlockSpec((mrows, TN_OUT), lambda i, j: (mi(i), gate1_col + j))],
        out_specs=pl.BlockSpec((TM, TN_OUT), lambda i, j: (i, j)),
        out_shape=jax.ShapeDtypeStruct((n, D), F32),
        compiler_params=_params(("arbitrary", "arbitrary")),
        name="outproj",
    )(ya, ym, w_out_bf, w_out_bf, x, mod)


def _router_kernel(x_ref, sh_ref, sc_ref, g_ref, wr_ref, br_ref, h_ref, route_ref):
    x = x_ref[...]
    y = x * lax.rsqrt(jnp.mean(x * x, axis=-1, keepdims=True) + EPS) * g_ref[...]
    h2 = y * (1.0 + sc_ref[...]) + sh_ref[...]
    h_ref[...] = h2
    logits = jnp.dot(h2, wr_ref[...], preferred_element_type=F32, precision=lax.Precision.HIGHEST) + br_ref[...]
    lane = _iota(logits.shape, 1)

    def first_max(vals):
        vmax = jnp.max(vals, axis=1, keepdims=True)
        idx = jnp.min(jnp.where(vals == vmax, lane, LANES), axis=1, keepdims=True)
        return vmax, idx

    gl = jnp.where(lane < N_GROUPS, logits, NEG_INF)
    gmax, grp = first_max(gl)
    p_grp = 1.0 / jnp.sum(jnp.exp(gl - gmax), axis=1, keepdims=True)
    e_lane = lane - N_GROUPS
    in_grp = jnp.logical_and(e_lane >= 0, jnp.logical_and(e_lane < N_EXPERTS, e_lane // EXP_PER_GROUP == grp))
    el = jnp.where(in_grp, logits, NEG_INF)
    v1, i1 = first_max(el)
    v2, i2 = first_max(jnp.where(lane == i1, NEG_INF, el))
    e2w = jnp.exp(v2 - v1)
    w1 = 1.0 / (1.0 + e2w)
    w2 = e2w / (1.0 + e2w)
    route = jnp.where(lane == 0, (i1 - N_GROUPS).astype(F32),
                      jnp.where(lane == 1, (i2 - N_GROUPS).astype(F32),
                                jnp.where(lane == 2, p_grp * w1, jnp.where(lane == 3, p_grp * w2, 0.0))))
    route_ref[...] = route


PROMPT_TOK_BLOCKS = N_PROMPT // TM_TOK


def _router_merged_kernel(xp_ref, xs_ref, shp_ref, scp_ref, shs_ref, scs_ref, g_ref, wr_ref, br_ref,
                          h_ref, route_ref):
    i = pl.program_id(0)

    @pl.when(i < PROMPT_TOK_BLOCKS)
    def _():
        _router_kernel(xp_ref, shp_ref, scp_ref, g_ref, wr_ref, br_ref, h_ref, route_ref)

    @pl.when(i >= PROMPT_TOK_BLOCKS)
    def _():
        _router_kernel(xs_ref, shs_ref, scs_ref, g_ref, wr_ref, br_ref, h_ref, route_ref)


def _router(x1_p, x1_s, mod_p, mod_s, g_ffn, w_route, b_route):
    pi = lambda i: jnp.minimum(i, PROMPT_TOK_BLOCKS - 1)
    si = lambda i: jnp.maximum(i - PROMPT_TOK_BLOCKS, 0)
    return pl.pallas_call(
        _router_merged_kernel,
        grid=(N_TOK // TM_TOK,),
        in_specs=[pl.BlockSpec((TM_TOK, D), lambda i: (pi(i), 0)),
                  pl.BlockSpec((TM_TOK, D), lambda i: (si(i), 0)),
                  pl.BlockSpec((1, D), lambda i: (0, 3)),
                  pl.BlockSpec((1, D), lambda i: (0, 4)),
                  pl.BlockSpec((TM_TOK, D), lambda i: (si(i), 3)),
                  pl.BlockSpec((TM_TOK, D), lambda i: (si(i), 4)),
                  pl.BlockSpec((1, D), lambda i: (0, 0)),
                  pl.BlockSpec((D, LANES), lambda i: (0, 0)),
                  pl.BlockSpec((1, LANES), lambda i: (0, 0))],
        out_specs=[pl.BlockSpec((TM_TOK, D), lambda i: (i, 0)),
                   pl.BlockSpec((TM_TOK, LANES), lambda i: (i, 0))],
        out_shape=[jax.ShapeDtypeStruct((N_TOK, D), F32), jax.ShapeDtypeStruct((N_TOK, LANES), F32)],
        compiler_params=_params(("arbitrary",)),
        name="router",
    )(x1_p, x1_s, mod_p, mod_p, mod_s, mod_s, g_ffn, w_route, b_route)


def _rank_kernel(route_ref, rank_ref, cnt_ref, carry):
    step = pl.program_id(0)

    @pl.when(step == 0)
    def _():
        carry[...] = jnp.zeros_like(carry)

    route = route_ref[...]
    n = route.shape[0]
    lane = _iota((n, LANES), 1).astype(F32)
    o1 = jnp.where(lane == route[:, 0:1], 1.0, 0.0)
    o2 = jnp.where(lane == route[:, 1:2], 1.0, 0.0)
    both = o1 + o2
    strict = jnp.where(_iota((n, n), 1) < _iota((n, n), 0), 1.0, 0.0).astype(BF16)
    prior = _dot(strict, both.astype(BF16)) + carry[0:1, :]
    r1 = jnp.sum(o1 * prior, axis=1, keepdims=True)
    r2 = jnp.sum(o2 * prior, axis=1, keepdims=True)
    lane_i = _iota((n, LANES), 1)
    rank_ref[...] = jnp.where(lane_i == 0, r1, jnp.where(lane_i == 1, r2, 0.0))
    carry[...] = carry[...] + jnp.sum(both, axis=0, keepdims=True)
    cnt_ref[...] = carry[...]


def _rank(route):
    n = route.shape[0]
    return pl.pallas_call(
        _rank_kernel,
        grid=(n // TM_TOK,),
        in_specs=[pl.BlockSpec((TM_TOK, LANES), lambda i: (i, 0))],
        out_specs=[pl.BlockSpec((TM_TOK, LANES), lambda i: (i, 0)),
                   pl.BlockSpec((SUBLANES, LANES), lambda i: (0, 0))],
        out_shape=[jax.ShapeDtypeStruct((n, LANES), F32), jax.ShapeDtypeStruct((SUBLANES, LANES), F32)],
        scratch_shapes=[pltpu.VMEM((SUBLANES, LANES), F32)],
        compiler_params=_params(("arbitrary",)),
        name="rank",
    )(route)


def _row_copy(src, dst, sem):
    return pltpu.make_async_copy(src, dst, sem)


def _dest_rows(tok, e1_ref, e2_ref, r1_ref, r2_ref, ps_ref):
    return ps_ref[e1_ref[tok]] + r1_ref[tok], ps_ref[e2_ref[tok]] + r2_ref[tok]


ZERO_FIRST, ZERO_ANYTIME = 1, 2


def _dispatch_kernel(e1_ref, e2_ref, r1_ref, r2_ref, ps_ref, zc_ref, h_ref, xs_out, zbuf, sem):
    step = pl.program_id(0)
    base = step * TM_TOK

    def zero_tile(t, s):
        rows = pl.ds(pl.multiple_of(t * TM_MOE, TM_MOE), TM_MOE)
        return pltpu.make_async_copy(zbuf, xs_out.at[rows], sem.at[s])

    def for_tiles(cls, s, act):
        def body(t, carry):
            @pl.when(zc_ref[t] == cls)
            def _():
                act(zero_tile(t, s))
            return carry
        lax.fori_loop(0, MAX_TILES, body, 0)

    @pl.when(step == 0)
    def _():
        zbuf[...] = jnp.zeros_like(zbuf)
        for_tiles(ZERO_FIRST, 1, lambda c: c.start())
        for_tiles(ZERO_ANYTIME, 2, lambda c: c.start())
        for_tiles(ZERO_FIRST, 1, lambda c: c.wait())

    @pl.when(step == pl.num_programs(0) - 1)
    def _():
        for_tiles(ZERO_ANYTIME, 2, lambda c: c.wait())

    def issue(r, carry):
        d1, d2 = _dest_rows(base + r, e1_ref, e2_ref, r1_ref, r2_ref, ps_ref)
        _row_copy(h_ref.at[pl.ds(r, 1)], xs_out.at[pl.ds(d1, 1)], sem.at[0]).start()
        _row_copy(h_ref.at[pl.ds(r, 1)], xs_out.at[pl.ds(d2, 1)], sem.at[0]).start()
        return carry

    lax.fori_loop(0, TM_TOK, issue, 0)

    def drain(r, carry):
        _row_copy(h_ref.at[pl.ds(0, 1)], xs_out.at[pl.ds(0, 1)], sem.at[0]).wait()
        _row_copy(h_ref.at[pl.ds(0, 1)], xs_out.at[pl.ds(0, 1)], sem.at[0]).wait()
        return carry

    lax.fori_loop(0, TM_TOK, drain, 0)


def _dispatch(sched, zero_class, h2):
    grid_spec = pltpu.PrefetchScalarGridSpec(
        num_scalar_prefetch=6,
        grid=(N_TOK // TM_TOK,),
        in_specs=[pl.BlockSpec((TM_TOK, D), lambda i, *_: (i, 0))],
        out_specs=pl.BlockSpec(memory_space=pl.ANY),
        scratch_shapes=[pltpu.VMEM((TM_MOE, D), F32), pltpu.SemaphoreType.DMA((3,))],
    )
    return pl.pallas_call(
        _dispatch_kernel,
        grid_spec=grid_spec,
        out_shape=jax.ShapeDtypeStruct((A_PAD, D), F32),
        compiler_params=_params(("arbitrary",)),
        name="dispatch",
    )(*sched, zero_class, h2)


N_UP_CHUNKS = D_FF // TN_FF


def _expert_up_kernel(te_ref, tv_ref, first_ref, nxt_ref, run_ref, meta_ref, x_ref, wg_hbm, wu_hbm,
                      o_ref, wbuf, sem):
    n = pl.program_id(0)
    t = pl.program_id(1)
    slot = lax.rem(n * meta_ref[1] + run_ref[t], 2)

    def fetch(e, chunk, s):
        cols = pl.ds(pl.multiple_of(chunk * TN_FF, TN_FF), TN_FF)
        return (pltpu.make_async_copy(wg_hbm.at[e, :, cols], wbuf.at[s, 0], sem.at[s, 0]),
                pltpu.make_async_copy(wu_hbm.at[e, :, cols], wbuf.at[s, 1], sem.at[s, 1]))

    @pl.when(first_ref[t] == 1)
    def _():
        @pl.when(jnp.logical_and(n == 0, t == 0))
        def _():
            for c in fetch(te_ref[0], 0, 0):
                c.start()

        for c in fetch(te_ref[t], n, slot):
            c.wait()

        @pl.when(nxt_ref[t] >= 0)
        def _():
            for c in fetch(nxt_ref[t], n, 1 - slot):
                c.start()

        @pl.when(jnp.logical_and(nxt_ref[t] < 0, n + 1 < N_UP_CHUNKS))
        def _():
            for c in fetch(te_ref[0], n + 1, 1 - slot):
                c.start()

    @pl.when(tv_ref[t] == 1)
    def _():
        x = x_ref[...].astype(BF16)
        g = _dot(x, wbuf[slot, 0].astype(BF16))
        u = _dot(x, wbuf[slot, 1].astype(BF16))
        o_ref[...] = (g * _sigmoid(g) * u).astype(BF16)

    @pl.when(tv_ref[t] == 0)
    def _():
        o_ref[...] = jnp.zeros_like(o_ref)


def _expert_up(tiles, xs, w_gate, w_up):
    last = lambda t, meta: jnp.minimum(t, meta[0] - 1)
    grid_spec = pltpu.PrefetchScalarGridSpec(
        num_scalar_prefetch=6,
        grid=(N_UP_CHUNKS, MAX_TILES),
        in_specs=[pl.BlockSpec((TM_MOE, D), lambda n, t, *s: (last(t, s[5]), 0)),
                  pl.BlockSpec(memory_space=pl.ANY),
                  pl.BlockSpec(memory_space=pl.ANY)],
        out_specs=pl.BlockSpec((TM_MOE, TN_FF), lambda n, t, *s: (t, n)),
        scratch_shapes=[pltpu.VMEM((2, 2, D, TN_FF), F32), pltpu.SemaphoreType.DMA((2, 2))],
    )
    return pl.pallas_call(
        _expert_up_kernel,
        grid_spec=grid_spec,
        out_shape=jax.ShapeDtypeStruct((A_PAD, D_FF), BF16),
        compiler_params=_params(("arbitrary", "arbitrary")),
        name="expert_up",
    )(*tiles, xs, w_gate, w_up)


def _expert_down_kernel(te_ref, tv_ref, first_ref, nxt_ref, run_ref, meta_ref, h_ref, wd_hbm,
                        o_ref, wbuf, sem):
    t = pl.program_id(0)
    slot = lax.rem(run_ref[t], 2)

    def fetch(e, s):
        return pltpu.make_async_copy(wd_hbm.at[e], wbuf.at[s], sem.at[s])

    @pl.when(first_ref[t] == 1)
    def _():
        @pl.when(t == 0)
        def _():
            fetch(te_ref[0], 0).start()

        fetch(te_ref[t], slot).wait()

        @pl.when(nxt_ref[t] >= 0)
        def _():
            fetch(nxt_ref[t], 1 - slot).start()

    @pl.when(tv_ref[t] == 1)
    def _():
        o_ref[...] = _dot(h_ref[...], wbuf[slot].astype(BF16))

    @pl.when(tv_ref[t] == 0)
    def _():
        o_ref[...] = jnp.zeros_like(o_ref)


def _expert_down(tiles, h1, w_down):
    last = lambda t, meta: jnp.minimum(t, meta[0] - 1)
    grid_spec = pltpu.PrefetchScalarGridSpec(
        num_scalar_prefetch=6,
        grid=(MAX_TILES,),
        in_specs=[pl.BlockSpec((TM_MOE, D_FF), lambda t, *s: (last(t, s[5]), 0)),
                  pl.BlockSpec(memory_space=pl.ANY)],
        out_specs=pl.BlockSpec((TM_MOE, D), lambda t, *s: (t, 0)),
        scratch_shapes=[pltpu.VMEM((2, D_FF, D), F32), pltpu.SemaphoreType.DMA((2,))],
    )
    return pl.pallas_call(
        _expert_down_kernel,
        grid_spec=grid_spec,
        out_shape=jax.ShapeDtypeStruct((A_PAD, D), F32),
        compiler_params=_params(("arbitrary",)),
        name="expert_down",
    )(*tiles, h1, w_down)


def _combine_kernel(e1_ref, e2_ref, r1_ref, r2_ref, ps_ref, x_ref, route_ref, gt_ref, gf_ref, o_hbm,
                    y_ref, buf, sem, *, tok_offset):
    base = tok_offset + pl.program_id(0) * TM_TOK

    def issue(r, carry):
        d1, d2 = _dest_rows(base + r, e1_ref, e2_ref, r1_ref, r2_ref, ps_ref)
        _row_copy(o_hbm.at[pl.ds(d1, 1)], buf.at[0, pl.ds(r, 1)], sem.at[0]).start()
        _row_copy(o_hbm.at[pl.ds(d2, 1)], buf.at[1, pl.ds(r, 1)], sem.at[0]).start()
        return carry

    lax.fori_loop(0, TM_TOK, issue, 0)

    def drain(r, carry):
        _row_copy(o_hbm.at[pl.ds(0, 1)], buf.at[0, pl.ds(0, 1)], sem.at[0]).wait()
        _row_copy(o_hbm.at[pl.ds(0, 1)], buf.at[1, pl.ds(0, 1)], sem.at[0]).wait()
        return carry

    lax.fori_loop(0, TM_TOK, drain, 0)

    route = route_ref[...]
    moe = route[:, 2:3] * buf[0] + route[:, 3:4] * buf[1]
    x2 = x_ref[...] + gt_ref[...] * moe
    y_ref[...] = x2 * lax.rsqrt(jnp.mean(x2 * x2, axis=-1, keepdims=True) + EPS) * gf_ref[...]


def _combine(sched, x1, route, mod, per_row, g_final, o_rows, tok_offset):
    n = x1.shape[0]
    mrows = TM_TOK if per_row else 1
    mi = (lambda i: i) if per_row else (lambda i: 0)
    off = tok_offset // TM_TOK
    grid_spec = pltpu.PrefetchScalarGridSpec(
        num_scalar_prefetch=5,
        grid=(n // TM_TOK,),
        in_specs=[pl.BlockSpec((TM_TOK, D), lambda i, *_: (i, 0)),
                  pl.BlockSpec((TM_TOK, LANES), lambda i, *_: (i + off, 0)),
                  pl.BlockSpec((mrows, D), lambda i, *_: (mi(i), 5)),
                  pl.BlockSpec((1, D), lambda i, *_: (0, 0)),
                  pl.BlockSpec(memory_space=pl.ANY)],
        out_specs=pl.BlockSpec((TM_TOK, D), lambda i, *_: (i, 0)),
        scratch_shapes=[pltpu.VMEM((TOP_K, TM_TOK, D), F32), pltpu.SemaphoreType.DMA((1,))],
    )
    return pl.pallas_call(
        functools.partial(_combine_kernel, tok_offset=tok_offset),
        grid_spec=grid_spec,
        out_shape=jax.ShapeDtypeStruct((n, D), F32),
        compiler_params=_params(("arbitrary",)),
        name="combine",
    )(*sched, x1, route, mod, g_final, o_rows)


def kernel(x_prompt, x_sample, cache_k, cache_v, state_C, state_n, state_m, c_prompt, c_sample, rel_bias, w_ada, b_ada, g_mix, g_ffn, w_in, sinks, b_igate, b_fgate, w_out, w_router_grp, b_router_grp, w_router_exp, b_router_exp, w_gate, w_up, w_down, g_final):
    xp = x_prompt.reshape(N_PROMPT, D)
    xs = x_sample.reshape(N_SAMPLE, D)

    c_all = jnp.concatenate([c_prompt, c_sample, jnp.zeros((C_ROWS - 1 - N_BATCH, D), F32)], axis=0)
    mod = _ada(c_all, w_ada[0], b_ada)
    mod_p = mod[0:1]
    mod_s = jnp.repeat(mod[1:1 + N_BATCH], T_DEC, axis=0)

    w_in_bf = w_in[0].astype(BF16)
    w_gates = jnp.pad(w_in[0, :, Z_WIDTH:].astype(BF16), ((0, 0), (0, LANES - 2 * M_HEADS)))
    zp, zgp = _inproj(xp, mod_p, False, g_mix, w_in_bf, w_gates)
    zs, zgs = _inproj(xs, mod_s, True, g_mix, w_in_bf, w_gates)

    rb_flat = rel_bias.reshape(NUM_BUCKETS * ATT_HEADS)
    sink_v = sinks[0]
    ya_p = _swa_prompt(zp, rb_flat, sink_v)
    ya_s, nk_s, nv_s = _swa_sample(zs, cache_k[0].reshape(N_BATCH, WINDOW, KV_WIDTH),
                                   cache_v[0].reshape(N_BATCH, WINDOW, KV_WIDTH), rb_flat, sink_v)

    gate_bias = jnp.concatenate([b_igate[0], b_fgate[0], jnp.zeros((LANES - 2 * M_HEADS,), F32)]).reshape(1, LANES)
    ym_p, c_p, n_p, m_p = _mlstm_prompt(zp, zgp, gate_bias)
    m0_tok = jnp.pad(jnp.repeat(state_m[0], T_DEC, axis=0), ((0, 0), (0, LANES - M_HEADS)))
    n0_flat = state_n[0].reshape(N_BATCH, M_HEADS * DK)
    n0_tok = jnp.repeat(n0_flat, T_DEC, axis=0)
    ym_s, c_s, n_s, m_s = _mlstm_sample(zs, zgs, gate_bias, m0_tok, n0_tok, state_C[0], n0_flat)

    w_out_bf = w_out[0].astype(BF16)
    x1_p = _outproj(ya_p, ym_p, w_out_bf, xp, mod_p, False)
    x1_s = _outproj(ya_s, ym_s, w_out_bf, xs, mod_s, True)

    w_route = jnp.pad(jnp.concatenate([w_router_grp[0], w_router_exp[0]], axis=1),
                      ((0, 0), (0, LANES - N_GROUPS - N_EXPERTS)))
    b_route = jnp.pad(jnp.concatenate([b_router_grp[0], b_router_exp[0]]),
                      (0, LANES - N_GROUPS - N_EXPERTS)).reshape(1, LANES)
    h2, route = _router(x1_p, x1_s, mod_p, mod_s, g_ffn, w_route, b_route)

    rank, counts = _rank(route)

    i32 = lambda a: a.astype(jnp.int32)
    cnt = i32(counts[0, :N_EXPERTS])
    tiles_per = (cnt + TM_MOE - 1) // TM_MOE
    tile_end = jnp.cumsum(tiles_per)
    pad_start = (tile_end - tiles_per) * TM_MOE
    n_tiles = tile_end[-1]
    tile_ids = jnp.arange(MAX_TILES, dtype=jnp.int32)
    tile_expert = i32(jnp.minimum(jnp.searchsorted(tile_end, tile_ids, side="right"), N_EXPERTS - 1))
    last_expert = tile_expert[jnp.maximum(n_tiles - 1, 0)]
    tile_valid = tile_ids < n_tiles
    tile_expert = jnp.where(tile_valid, tile_expert, last_expert)
    prev_expert = jnp.concatenate([jnp.full((1,), -1, jnp.int32), tile_expert[:-1]])
    run_first = jnp.logical_and(tile_valid, tile_expert != prev_expert)
    run_id = jnp.maximum(jnp.cumsum(i32(run_first)) - 1, 0)
    expert_ids = jnp.arange(N_EXPERTS, dtype=jnp.int32)
    used = jnp.where(tiles_per > 0, expert_ids, N_EXPERTS)
    next_used = jnp.concatenate([lax.cummin(used[::-1])[::-1][1:], jnp.full((1,), N_EXPERTS, jnp.int32)])
    next_used = jnp.where(next_used >= N_EXPERTS, -1, next_used)
    tiles = (tile_expert, i32(tile_valid), i32(run_first), next_used[tile_expert], i32(run_id),
             jnp.stack([n_tiles, jnp.sum(i32(run_first))]).astype(jnp.int32))
    next_expert = jnp.concatenate([tile_expert[1:], jnp.full((1,), -1, jnp.int32)])
    run_last = jnp.logical_or(tile_expert != next_expert, tile_ids == n_tiles - 1)
    zero_class = jnp.where(tile_valid, jnp.where(run_last, ZERO_FIRST, 0), ZERO_ANYTIME)
    sched = (i32(route[:, 0]), i32(route[:, 1]), i32(rank[:, 0]), i32(rank[:, 1]), i32(pad_start))

    xs_rows = _dispatch(sched, i32(zero_class), h2)
    h1 = _expert_up(tiles, xs_rows, w_gate[0], w_up[0])
    o_rows = _expert_down(tiles, h1, w_down[0])

    gf = g_final.reshape(1, D)
    y_p = _combine(sched, x1_p, route, mod_p, False, gf, o_rows, 0)
    y_s = _combine(sched, x1_s, route, mod_s, True, gf, o_rows, N_PROMPT)

    kv5 = lambda a: a.reshape(1, -1, WINDOW, KV_HEADS, HEAD_DIM)
    kcol = ATT_WIDTH
    nk_p = zp[N_PROMPT - WINDOW:, kcol:kcol + KV_WIDTH]
    nv_p = zp[N_PROMPT - WINDOW:, kcol + KV_WIDTH:kcol + 2 * KV_WIDTH]
    return (y_p.reshape(1, N_PROMPT, D), y_s.reshape(N_BATCH, T_DEC, D),
            kv5(nk_p), kv5(nv_p),
            c_p.reshape(1, 1, M_HEADS, DV, DK), n_p.reshape(1, 1, M_HEADS, DK), m_p[:, 0].reshape(1, 1, M_HEADS),
            kv5(nk_s), kv5(nv_s),
            c_s.reshape(1, N_BATCH, M_HEADS, DV, DK), n_s.reshape(1, N_BATCH, M_HEADS, DK),
            m_s.reshape(N_BATCH, T_DEC, LANES)[:, T_DEC - 1, :M_HEADS].reshape(1, N_BATCH, M_HEADS))
```

```python
import functools
import math

import numpy as np
import jax
import jax.numpy as jnp
from jax import lax
from jax.experimental import pallas as pl
from jax.experimental.pallas import tpu as pltpu

F32 = jnp.float32
BF16 = jnp.bfloat16
NEG_INF = float("-inf")

D = 4096
N_PROMPT = 8192
N_BATCH = 128
T_DEC = 8
N_SAMPLE = N_BATCH * T_DEC
N_TOK = N_PROMPT + N_SAMPLE
HEAD_DIM = 128
ATT_HEADS = 16
KV_HEADS = 4
GROUP = ATT_HEADS // KV_HEADS
WINDOW = 128
ATT_WIDTH = ATT_HEADS * HEAD_DIM
KV_WIDTH = KV_HEADS * HEAD_DIM
NUM_BUCKETS = 32
MAX_EXACT = 16
MAX_DISTANCE = 128
M_HEADS = 8
DK = 128
DV = 256
M_WIDTH = M_HEADS * DV
Z_WIDTH = ATT_WIDTH + 2 * KV_WIDTH + 2 * M_HEADS * DK + 2 * M_WIDTH
N_GROUPS = 4
EXP_PER_GROUP = 8
N_EXPERTS = N_GROUPS * EXP_PER_GROUP
TOP_K = 2
D_FF = 1024
EPS = 1e-6
ATT_SCALE = HEAD_DIM ** -0.5
Q_SCALE = DK ** -0.5

LANES = 128
SUBLANES = 8
VMEM_LIMIT = 56 * 1024 * 1024

TM_SHARED = 512
TM_ROWMOD = 256
TN_IN = 1024
TN_OUT = 1024
TN_ADA = 512
C_ROWS = 136
ATT_BLOCK = 128
SAMPLE_BT = 8
ML_CHUNK = 256
TM_MOE = 256
TN_FF = 512
N_ASSIGN = N_TOK * TOP_K
MAX_TILES = N_ASSIGN // TM_MOE + N_EXPERTS
A_PAD = MAX_TILES * TM_MOE
TM_TOK = 256


def _params(sem):
    return pltpu.CompilerParams(dimension_semantics=sem, vmem_limit_bytes=VMEM_LIMIT)


def _iota(shape, dim):
    return lax.broadcasted_iota(jnp.int32, shape, dim)


def _dot(a, b):
    return jnp.dot(a, b, preferred_element_type=F32)


def _dot_nt(a, b):
    return lax.dot_general(a, b, (((1,), (1,)), ((), ())), preferred_element_type=F32)


def _dot_tn(a, b):
    return lax.dot_general(a, b, (((0,), (0,)), ((), ())), preferred_element_type=F32)


def _split3(x):
    x1 = x.astype(BF16)
    r1 = x - x1.astype(F32)
    x2 = r1.astype(BF16)
    r2 = r1 - x2.astype(F32)
    return x1, x2, r2.astype(BF16)


def _dot_exact_lhs01(a01, x):
    x1, x2, x3 = _split3(x)
    return _dot(a01, x1) + _dot(a01, x2) + _dot(a01, x3)


def _sigmoid(x):
    return 1.0 / (1.0 + jnp.exp(-x))


def _log_sigmoid(x):
    return jnp.minimum(x, 0.0) - jnp.log(1.0 + jnp.exp(-jnp.abs(x)))


def _ada_kernel(c_ref, w_ref, b_ref, o_ref):
    c = c_ref[...]
    s = (c * _sigmoid(c)).astype(BF16)
    o_ref[...] = _dot(s, w_ref[...].astype(BF16)) + b_ref[...]


def _ada(c_all, w_ada, b_ada):
    n = w_ada.shape[1]
    return pl.pallas_call(
        _ada_kernel,
        grid=(n // TN_ADA,),
        in_specs=[pl.BlockSpec((C_ROWS, D), lambda j: (0, 0)),
                  pl.BlockSpec((D, TN_ADA), lambda j: (0, j)),
                  pl.BlockSpec((1, TN_ADA), lambda j: (0, j))],
        out_specs=pl.BlockSpec((C_ROWS, TN_ADA), lambda j: (0, j)),
        out_shape=jax.ShapeDtypeStruct((C_ROWS, n), F32),
        compiler_params=_params(("arbitrary",)),
        name="ada",
    )(c_all, w_ada, b_ada)


def _inproj_kernel(x_ref, sh_ref, sc_ref, g_ref, w_ref, wg_ref, z_ref, zg_ref, h_scr):
    @pl.when(pl.program_id(1) == 0)
    def _():
        x = x_ref[...]
        y = x * lax.rsqrt(jnp.mean(x * x, axis=-1, keepdims=True) + EPS) * g_ref[...]
        hb = (y * (1.0 + sc_ref[...]) + sh_ref[...]).astype(BF16)
        h_scr[...] = hb
        zg_ref[...] = _dot(hb, wg_ref[...])

    z_ref[...] = _dot(h_scr[...], w_ref[...])


def _inproj(x, mod, per_row, g_mix, w_main, w_gate):
    n = x.shape[0]
    TM = TM_ROWMOD if per_row else TM_SHARED
    mrows = TM if per_row else 1
    mi = (lambda i: i) if per_row else (lambda i: 0)
    return pl.pallas_call(
        _inproj_kernel,
        grid=(n // TM, Z_WIDTH // TN_IN),
        in_specs=[pl.BlockSpec((TM, D), lambda i, j: (i, 0)),
                  pl.BlockSpec((mrows, D), lambda i, j: (mi(i), 0)),
                  pl.BlockSpec((mrows, D), lambda i, j: (mi(i), 1)),
                  pl.BlockSpec((1, D), lambda i, j: (0, 0)),
                  pl.BlockSpec((D, TN_IN), lambda i, j: (0, j)),
                  pl.BlockSpec((D, LANES), lambda i, j: (0, 0))],
        out_specs=[pl.BlockSpec((TM, TN_IN), lambda i, j: (i, j)),
                   pl.BlockSpec((TM, LANES), lambda i, j: (i, 0))],
        out_shape=[jax.ShapeDtypeStruct((n, Z_WIDTH), F32),
                   jax.ShapeDtypeStruct((n, LANES), F32)],
        scratch_shapes=[pltpu.VMEM((TM, D), BF16)],
        compiler_params=_params(("arbitrary", "arbitrary")),
        name="inproj",
    )(x, mod, mod, g_mix, w_main, w_gate)


def _t5_bucket_np(dist):
    n = np.maximum(dist, 0)
    nf = np.maximum(n, 1).astype(np.float32)
    large = MAX_EXACT + (np.log(nf / MAX_EXACT) / math.log(MAX_DISTANCE / MAX_EXACT)
                         * (NUM_BUCKETS - MAX_EXACT)).astype(np.int32)
    large = np.minimum(large, NUM_BUCKETS - 1)
    return np.where(n < MAX_EXACT, n, large).astype(np.int32)


def _bucket_table(n_q, n_keys_valid, n_keys_padded):
    t = np.arange(n_q)[:, None]
    j = np.arange(n_keys_padded)[None, :]
    dist = t + WINDOW - j
    valid = (dist >= 0) & (dist < WINDOW) & (j < n_keys_valid)
    return np.where(valid, _t5_bucket_np(dist), -1).astype(np.int32)


def _fill_bias(bucket_ref, rb_ref, bias_scr, rows, per_head_table=False):
    for h in range(ATT_HEADS):
        bk = bucket_ref[h * rows:(h + 1) * rows, :] if per_head_table else bucket_ref[...]
        acc = jnp.full(bk.shape, NEG_INF, F32)
        for b in range(NUM_BUCKETS):
            acc = jnp.where(bk == b, rb_ref[b * ATT_HEADS + h], acc)
        bias_scr[h * rows:(h + 1) * rows, :] = acc


def _sink_softmax_av(lg, sink_col, v2):
    m = jnp.maximum(jnp.max(lg, axis=-1, keepdims=True), sink_col)
    p = jnp.exp(lg - m)
    den = jnp.sum(p, axis=-1, keepdims=True) + jnp.exp(sink_col - m)
    return _dot((p / den).astype(BF16), v2)


def _swa_prompt_kernel(rb_ref, sink_ref, bucket_ref, q_ref, kp_ref, kc_ref, vp_ref, vc_ref,
                       o_ref, bias_scr):
    i = pl.program_id(0)

    @pl.when(i == 0)
    def _():
        _fill_bias(bucket_ref, rb_ref, bias_scr, ATT_BLOCK)

    rows = GROUP * ATT_BLOCK
    first_prev = jnp.logical_and(i == 0, _iota((rows, 2 * ATT_BLOCK), 1) < ATT_BLOCK)
    for g in range(KV_HEADS):
        ks = slice(g * HEAD_DIM, (g + 1) * HEAD_DIM)
        k2 = jnp.concatenate([kp_ref[:, ks], kc_ref[:, ks]], axis=0).astype(BF16)
        v2 = jnp.concatenate([vp_ref[:, ks], vc_ref[:, ks]], axis=0).astype(BF16)
        q4 = jnp.concatenate(
            [q_ref[:, (g * GROUP + r) * HEAD_DIM:(g * GROUP + r + 1) * HEAD_DIM] for r in range(GROUP)],
            axis=0).astype(BF16)
        lg = _dot_nt(q4, k2) * ATT_SCALE + bias_scr[g * rows:(g + 1) * rows, :]
        lg = jnp.where(first_prev, NEG_INF, lg)
        sink_col = jnp.concatenate(
            [jnp.full((ATT_BLOCK, 1), sink_ref[g * GROUP + r], F32) for r in range(GROUP)], axis=0)
        o = _sink_softmax_av(lg, sink_col, v2)
        for r in range(GROUP):
            h = g * GROUP + r
            o_ref[:, h * HEAD_DIM:(h + 1) * HEAD_DIM] = o[r * ATT_BLOCK:(r + 1) * ATT_BLOCK].astype(BF16)


def _swa_prompt(z, rb_flat, sinks):
    nb = N_PROMPT // ATT_BLOCK
    bucket = jnp.asarray(_bucket_table(ATT_BLOCK, 2 * ATT_BLOCK, 2 * ATT_BLOCK))
    kcol = ATT_WIDTH // KV_WIDTH
    prev = lambda i: jnp.maximum(i - 1, 0)
    smem = pl.BlockSpec(memory_space=pltpu.SMEM)
    return pl.pallas_call(
        _swa_prompt_kernel,
        grid=(nb,),
        in_specs=[smem, smem,
                  pl.BlockSpec((ATT_BLOCK, 2 * ATT_BLOCK), lambda i: (0, 0)),
                  pl.BlockSpec((ATT_BLOCK, ATT_WIDTH), lambda i: (i, 0)),
                  pl.BlockSpec((ATT_BLOCK, KV_WIDTH), lambda i: (prev(i), kcol)),
                  pl.BlockSpec((ATT_BLOCK, KV_WIDTH), lambda i: (i, kcol)),
                  pl.BlockSpec((ATT_BLOCK, KV_WIDTH), lambda i: (prev(i), kcol + 1)),
                  pl.BlockSpec((ATT_BLOCK, KV_WIDTH), lambda i: (i, kcol + 1))],
        out_specs=pl.BlockSpec((ATT_BLOCK, ATT_WIDTH), lambda i: (i, 0)),
        out_shape=jax.ShapeDtypeStruct((N_PROMPT, ATT_WIDTH), BF16),
        scratch_shapes=[pltpu.VMEM((ATT_HEADS * ATT_BLOCK, 2 * ATT_BLOCK), F32)],
        compiler_params=_params(("arbitrary",)),
        name="swa_prompt",
    )(rb_flat, sinks, bucket, z, z, z, z, z)


CACHE_ROWS = WINDOW * KV_HEADS
NEW_ROWS = T_DEC * KV_HEADS
S_KEYS = 5 * LANES


def _sample_bucket_table():
    t = np.arange(T_DEC)[:, None]
    col = np.arange(S_KEYS)[None, :]
    in_cache = col < CACHE_ROWS
    in_new = (col >= CACHE_ROWS) & (col < CACHE_ROWS + NEW_ROWS)
    key_head = np.where(in_cache, col % KV_HEADS, (col - CACHE_ROWS) // T_DEC)
    key_pos = np.where(in_cache, col // KV_HEADS, WINDOW + (col - CACHE_ROWS) % T_DEC)
    dist = t + WINDOW - key_pos
    valid = (dist >= 0) & (dist < WINDOW) & (in_cache | in_new)
    per_query = np.where(valid, _t5_bucket_np(dist), -1)
    heads = np.arange(ATT_HEADS)[:, None, None] // GROUP
    table = np.where(heads == key_head[None], per_query[None], -1)
    return table.reshape(ATT_HEADS * T_DEC, S_KEYS).astype(np.int32)


def _swa_sample_kernel(rb_ref, sink_ref, bucket_ref, q_ref, kn_ref, vn_ref, ck_ref, cv_ref,
                       o_ref, nk_ref, nv_ref, bias_scr):
    @pl.when(pl.program_id(0) == 0)
    def _():
        _fill_bias(bucket_ref, rb_ref, bias_scr, T_DEC, per_head_table=True)

    nk_ref[:, 0:CACHE_ROWS - NEW_ROWS, :] = ck_ref[:, NEW_ROWS:CACHE_ROWS, :]
    nv_ref[:, 0:CACHE_ROWS - NEW_ROWS, :] = cv_ref[:, NEW_ROWS:CACHE_ROWS, :]

    pad = jnp.zeros((S_KEYS - CACHE_ROWS - NEW_ROWS, HEAD_DIM), F32)
    sink_col = jnp.concatenate([jnp.full((T_DEC, 1), sink_ref[h], F32) for h in range(ATT_HEADS)], axis=0)
    bias = bias_scr[...]
    for b in range(SAMPLE_BT):
        ts = slice(b * T_DEC, (b + 1) * T_DEC)
        head_cols = lambda ref, n: [ref[ts, h * HEAD_DIM:(h + 1) * HEAD_DIM] for h in range(n)]
        k_new, v_new = head_cols(kn_ref, KV_HEADS), head_cols(vn_ref, KV_HEADS)
        for g in range(KV_HEADS):
            new_rows = pl.ds(CACHE_ROWS - NEW_ROWS + g, T_DEC, stride=KV_HEADS)
            nk_ref[b, new_rows, :] = k_new[g]
            nv_ref[b, new_rows, :] = v_new[g]
        qa = jnp.concatenate(head_cols(q_ref, ATT_HEADS), axis=0).astype(BF16)
        k2 = jnp.concatenate([ck_ref[b]] + k_new + [pad], axis=0).astype(BF16)
        v2 = jnp.concatenate([cv_ref[b]] + v_new + [pad], axis=0).astype(BF16)
        o = _sink_softmax_av(_dot_nt(qa, k2) * ATT_SCALE + bias, sink_col, v2)
        for h in range(ATT_HEADS):
            o_ref[ts, h * HEAD_DIM:(h + 1) * HEAD_DIM] = o[h * T_DEC:(h + 1) * T_DEC].astype(BF16)


def _swa_sample(z, cache_k, cache_v, rb_flat, sinks):
    rows = SAMPLE_BT * T_DEC
    bucket = jnp.asarray(_sample_bucket_table())
    kcol = ATT_WIDTH // KV_WIDTH
    smem = pl.BlockSpec(memory_space=pltpu.SMEM)
    cache_spec = pl.BlockSpec((SAMPLE_BT, CACHE_ROWS, HEAD_DIM), lambda i: (i, 0, 0))
    cache_shape = jax.ShapeDtypeStruct((N_BATCH, CACHE_ROWS, HEAD_DIM), F32)
    return pl.pallas_call(
        _swa_sample_kernel,
        grid=(N_BATCH // SAMPLE_BT,),
        in_specs=[smem, smem,
                  pl.BlockSpec((ATT_HEADS * T_DEC, S_KEYS), lambda i: (0, 0)),
                  pl.BlockSpec((rows, ATT_WIDTH), lambda i: (i, 0)),
                  pl.BlockSpec((rows, KV_WIDTH), lambda i: (i, kcol)),
                  pl.BlockSpec((rows, KV_WIDTH), lambda i: (i, kcol + 1)),
                  cache_spec, cache_spec],
        out_specs=[pl.BlockSpec((rows, ATT_WIDTH), lambda i: (i, 0)), cache_spec, cache_spec],
        out_shape=[jax.ShapeDtypeStruct((N_SAMPLE, ATT_WIDTH), BF16), cache_shape, cache_shape],
        scratch_shapes=[pltpu.VMEM((ATT_HEADS * T_DEC, S_KEYS), F32)],
        compiler_params=_params(("arbitrary",)),
        name="swa_sample",
    )(rb_flat, sinks, bucket, z, z, z, cache_k, cache_v)


LANE_IG, LANE_LF, LANE_B = 0, M_HEADS, 2 * M_HEADS


def _gate_table(zg, gate_bias, seg_len):
    L = zg.shape[0]
    g = zg + gate_bias
    lane = _iota((L, LANES), 1)
    lf = _log_sigmoid(g)
    lf_only = jnp.where(jnp.logical_and(lane >= LANE_LF, lane < LANE_B), lf, 0.0)
    row = _iota((L, L), 0)
    col = _iota((L, L), 1)
    same_seg = (row // seg_len) == (col // seg_len)
    tril = jnp.where(jnp.logical_and(col <= row, same_seg), 1.0, 0.0).astype(BF16)
    cum = pltpu.roll(_dot_exact_lhs01(tril, lf_only), M_HEADS, axis=1)
    table = jnp.where(lane < LANE_LF, g, jnp.where(lane < LANE_B, lf, jnp.where(lane < LANE_B + M_HEADS, cum, 0.0)))
    return table, jnp.logical_and(col <= row, same_seg)


def _mlstm_intra(table, table_t, mask, m0_col, h, q, k, v):
    b_c = table[:, LANE_B + h:LANE_B + h + 1]
    b_r = table_t[LANE_B + h:LANE_B + h + 1, :]
    ig_r = table_t[LANE_IG + h:LANE_IG + h + 1, :]
    log_d = jnp.where(mask, b_c - b_r + ig_r, NEG_INF)
    log_inter = b_c + m0_col
    m_t = jnp.maximum(log_inter, jnp.max(log_d, axis=1, keepdims=True))
    d = jnp.exp(log_d - m_t)
    w_inter = jnp.exp(log_inter - m_t)
    s = _dot_nt(q.astype(BF16), k.astype(BF16)) * d
    num_intra = _dot(s.astype(BF16), v.astype(BF16))
    den_intra = jnp.sum(s, axis=1, keepdims=True)
    return b_c, m_t, w_inter, num_intra, den_intra


def _mlstm_prompt_kernel(gb_ref, zg_ref, q_ref, k_ref, va_ref, vb_ref, oa_ref, ob_ref,
                         y_ref, c_out, n_out, m_out, c_scr, n_scr, m_scr):
    step = pl.program_id(0)
    L = ML_CHUNK

    @pl.when(step == 0)
    def _():
        c_scr[...] = jnp.zeros_like(c_scr)
        n_scr[...] = jnp.zeros_like(n_scr)
        m_scr[...] = jnp.zeros_like(m_scr)

    table, mask = _gate_table(zg_ref[...], gb_ref[...], L)
    table_t = table.T
    half = M_HEADS // 2
    for h in range(M_HEADS):
        v_ref, o_ref = (va_ref, oa_ref) if h < half else (vb_ref, ob_ref)
        vs = slice((h % half) * DV, (h % half + 1) * DV)
        q = q_ref[:, h * DK:(h + 1) * DK] * Q_SCALE
        k = k_ref[:, h * DK:(h + 1) * DK]
        v = v_ref[:, vs]
        m0 = m_scr[h:h + 1, 0:1]
        b_c, m_t, w_inter, num_intra, den_intra = _mlstm_intra(table, table_t, mask, m0, h, q, k, v)
        c_old = c_scr[h]
        n_old = n_scr[h:h + 1, :]
        num = num_intra + w_inter * _dot_nt(q.astype(BF16), c_old.astype(BF16))
        den = den_intra + w_inter * jnp.sum(q * n_old, axis=1, keepdims=True)
        hh = num / jnp.maximum(jnp.abs(den), jnp.exp(-m_t))
        y_ref[:, h * DV:(h + 1) * DV] = (_sigmoid(o_ref[:, vs]) * hh).astype(BF16)

        ig_c = table[:, LANE_IG + h:LANE_IG + h + 1]
        m_new = m_t[L - 1:L, :]
        b_last = b_c[L - 1:L, :]
        w_s = jnp.exp(b_last - b_c + ig_c - m_new)
        decay = jnp.exp(b_last + m0 - m_new)
        c_scr[h] = decay * c_old + _dot_tn((w_s * v).astype(BF16), k.astype(BF16))
        n_scr[h:h + 1, :] = decay * n_old + jnp.sum(w_s * k, axis=0, keepdims=True)
        m_scr[h:h + 1, :] = jnp.broadcast_to(m_new, (1, LANES))

    @pl.when(step == pl.num_programs(0) - 1)
    def _():
        c_out[...] = c_scr[...]
        n_out[...] = n_scr[...]
        m_out[...] = m_scr[...]


def _mlstm_prompt(z, zg, gate_bias):
    L = ML_CHUNK
    blk = M_HEADS * DK
    col = lambda c: pl.BlockSpec((L, blk), lambda i: (i, c))
    const = lambda shape: pl.BlockSpec(shape, lambda i: tuple(0 for _ in shape))
    return pl.pallas_call(
        _mlstm_prompt_kernel,
        grid=(N_PROMPT // L,),
        in_specs=[const((1, LANES)),
                  pl.BlockSpec((L, LANES), lambda i: (i, 0)),
                  col(3), col(4), col(5), col(6), col(7), col(8)],
        out_specs=[pl.BlockSpec((L, M_WIDTH), lambda i: (i, 0)),
                   const((M_HEADS, DV, DK)), const((M_HEADS, DK)), const((M_HEADS, LANES))],
        out_shape=[jax.ShapeDtypeStruct((N_PROMPT, M_WIDTH), BF16),
                   jax.ShapeDtypeStruct((M_HEADS, DV, DK), F32),
                   jax.ShapeDtypeStruct((M_HEADS, DK), F32),
                   jax.ShapeDtypeStruct((M_HEADS, LANES), F32)],
        scratch_shapes=[pltpu.VMEM((M_HEADS, DV, DK), F32),
                        pltpu.VMEM((M_HEADS, DK), F32),
                        pltpu.VMEM((M_HEADS, LANES), F32)],
        compiler_params=_params(("arbitrary",)),
        name="mlstm_prompt",
    )(gate_bias, zg, z, z, z, z, z, z)


S_ROWS = SAMPLE_BT * T_DEC


def _mlstm_sample_kernel(gb_ref, zg_ref, m0_ref, n0t_ref, q_ref, k_ref, va_ref, vb_ref, oa_ref, ob_ref,
                         c0_ref, n0_ref, y_ref, c_out, n_out, m_out):
    L = S_ROWS
    table, mask = _gate_table(zg_ref[...], gb_ref[...], T_DEC)
    table_t = jnp.concatenate([table, jnp.zeros((LANES - L, LANES), F32)], axis=0).T[:, 0:L]
    m0_all = m0_ref[...]
    lane = _iota((L, LANES), 1)
    row_b = _iota((SAMPLE_BT, L), 1) // T_DEC
    seg_sum = jnp.where(row_b == _iota((SAMPLE_BT, L), 0), 1.0, 0.0)
    m_tok = jnp.zeros((L, LANES), F32)
    half = M_HEADS // 2
    for h in range(M_HEADS):
        v_ref, o_ref = (va_ref, oa_ref) if h < half else (vb_ref, ob_ref)
        vs = slice((h % half) * DV, (h % half + 1) * DV)
        q = q_ref[:, h * DK:(h + 1) * DK] * Q_SCALE
        k = k_ref[:, h * DK:(h + 1) * DK]
        v = v_ref[:, vs]
        m0 = m0_all[:, h:h + 1]
        b_c, m_t, w_inter, num_intra, den_intra = _mlstm_intra(table, table_t, mask, m0, h, q, k, v)
        qb = q.astype(BF16)
        num_inter = jnp.concatenate(
            [_dot_nt(qb[b * T_DEC:(b + 1) * T_DEC], c0_ref[b, h].astype(BF16)) for b in range(SAMPLE_BT)], axis=0)
        num = num_intra + w_inter * num_inter
        den = den_intra + w_inter * jnp.sum(q * n0t_ref[:, h * DK:(h + 1) * DK], axis=1, keepdims=True)
        hh = num / jnp.maximum(jnp.abs(den), jnp.exp(-m_t))
        y_ref[:, h * DV:(h + 1) * DV] = (_sigmoid(o_ref[:, vs]) * hh).astype(BF16)
        m_tok = jnp.where(lane == h, m_t, m_tok)

        def last_tok(x):
            x3 = x.reshape(SAMPLE_BT, T_DEC, 1)
            return jnp.broadcast_to(x3[:, T_DEC - 1:T_DEC, :], x3.shape).reshape(L, 1)

        ig_c = table[:, LANE_IG + h:LANE_IG + h + 1]
        m_new = last_tok(m_t)
        b_last = last_tok(b_c)
        w_s = jnp.exp(b_last - b_c + ig_c - m_new)
        decay = jnp.exp(b_last + m0 - m_new)
        wv = (w_s * v).astype(BF16)
        kb = k.astype(BF16)
        rowsel = _iota((L, 1), 0) // T_DEC
        for b in range(SAMPLE_BT):
            dec_b = decay[b * T_DEC + T_DEC - 1:(b + 1) * T_DEC, :]
            wv_b = jnp.where(rowsel == b, wv, jnp.zeros_like(wv))
            c_out[b, h] = dec_b * c0_ref[b, h] + _dot_tn(wv_b, kb)
        dec_rows = decay.reshape(SAMPLE_BT, T_DEC, 1)[:, T_DEC - 1, :]
        n_out[:, h * DK:(h + 1) * DK] = dec_rows * n0_ref[:, h * DK:(h + 1) * DK] + jnp.dot(
            seg_sum, w_s * k, preferred_element_type=F32, precision=lax.Precision.HIGHEST)
    m_out[...] = m_tok


def _mlstm_sample(z, zg, gate_bias, m0_tok, n0_tok, state_c, state_n):
    L = S_ROWS
    blk = M_HEADS * DK
    col = lambda c: pl.BlockSpec((L, blk), lambda i: (i, c))
    c_spec = pl.BlockSpec((SAMPLE_BT, M_HEADS, DV, DK), lambda i: (i, 0, 0, 0))
    n_spec = pl.BlockSpec((SAMPLE_BT, blk), lambda i: (i, 0))
    return pl.pallas_call(
        _mlstm_sample_kernel,
        grid=(N_BATCH // SAMPLE_BT,),
        in_specs=[pl.BlockSpec((1, LANES), lambda i: (0, 0)),
                  pl.BlockSpec((L, LANES), lambda i: (i, 0)),
                  pl.BlockSpec((L, LANES), lambda i: (i, 0)),
                  pl.BlockSpec((L, blk), lambda i: (i, 0)),
                  col(3), col(4), col(5), col(6), col(7), col(8),
                  c_spec, n_spec],
        out_specs=[pl.BlockSpec((L, M_WIDTH), lambda i: (i, 0)), c_spec, n_spec,
                   pl.BlockSpec((L, LANES), lambda i: (i, 0))],
        out_shape=[jax.ShapeDtypeStruct((N_SAMPLE, M_WIDTH), BF16),
                   jax.ShapeDtypeStruct((N_BATCH, M_HEADS, DV, DK), F32),
                   jax.ShapeDtypeStruct((N_BATCH, blk), F32),
                   jax.ShapeDtypeStruct((N_SAMPLE, LANES), F32)],
        compiler_params=_params(("arbitrary",)),
        name="mlstm_sample",
    )(gate_bias, zg, m0_tok, n0_tok, z, z, z, z, z, z, state_c, state_n)


def _outproj_kernel(ya_ref, ym_ref, wa_ref, wm_ref, x_ref, gt_ref, o_ref):
    mix = _dot(ya_ref[...], wa_ref[...]) + _dot(ym_ref[...], wm_ref[...])
    o_ref[...] = x_ref[...] + gt_ref[...] * mix


def _outproj(ya, ym, w_out_bf, x, mod, per_row):
    n = x.shape[0]
    TM = TM_ROWMOD if per_row else TM_SHARED
    mrows = TM if per_row else 1
    mi = (lambda i: i) if per_row else (lambda i: 0)
    gate1_col = 2 * (D // TN_OUT)
    return pl.pallas_call(
        _outproj_kernel,
        grid=(n // TM, D // TN_OUT),
        in_specs=[pl.BlockSpec((TM, ATT_WIDTH), lambda i, j: (i, 0)),
                  pl.BlockSpec((TM, M_WIDTH), lambda i, j: (i, 0)),
                  pl.BlockSpec((ATT_WIDTH, TN_OUT), lambda i, j: (0, j)),
                  pl.BlockSpec((M_WIDTH, TN_OUT), lambda i, j: (1, j)),
                  pl.BlockSpec((TM, TN_OUT), lambda i, j: (i, j)),
                  pl.BlockSpec((mrows, TN_OUT), lambda i, j: (mi(i), gate1_col + j))],
        out_specs=pl.BlockSpec((TM, TN_OUT), lambda i, j: (i, j)),
        out_shape=jax.ShapeDtypeStruct((n, D), F32),
        compiler_params=_params(("arbitrary", "arbitrary")),
        name="outproj",
    )(ya, ym, w_out_bf, w_out_bf, x, mod)


def _router_kernel(x_ref, sh_ref, sc_ref, g_ref, wr_ref, br_ref, h_ref, route_ref):
    x = x_ref[...]
    y = x * lax.rsqrt(jnp.mean(x * x, axis=-1, keepdims=True) + EPS) * g_ref[...]
    h2 = y * (1.0 + sc_ref[...]) + sh_ref[...]
    bits = lax.bitcast_convert_type(h2.astype(BF16).astype(F32), jnp.uint32)
    h_ref[...] = (bits[:, :D_PACK] >> 16) | (bits[:, D_PACK:] & jnp.uint32(0xFFFF0000))
    logits =jnp.dot(h2, wr_ref[...], preferred_element_type=F32, precision=lax.Precision.HIGHEST) + br_ref[...]
    lane = _iota(logits.shape, 1)

    def first_max(vals):
        vmax = jnp.max(vals, axis=1, keepdims=True)
        idx = jnp.min(jnp.where(vals == vmax, lane, LANES), axis=1, keepdims=True)
        return vmax, idx

    gl = jnp.where(lane < N_GROUPS, logits, NEG_INF)
    gmax, grp = first_max(gl)
    p_grp = 1.0 / jnp.sum(jnp.exp(gl - gmax), axis=1, keepdims=True)
    e_lane = lane - N_GROUPS
    in_grp = jnp.logical_and(e_lane >= 0, jnp.logical_and(e_lane < N_EXPERTS, e_lane // EXP_PER_GROUP == grp))
    el = jnp.where(in_grp, logits, NEG_INF)
    v1, i1 = first_max(el)
    v2, i2 = first_max(jnp.where(lane == i1, NEG_INF, el))
    e2w = jnp.exp(v2 - v1)
    w1 = 1.0 / (1.0 + e2w)
    w2 = e2w / (1.0 + e2w)
    route = jnp.where(lane == 0, (i1 - N_GROUPS).astype(F32),
                      jnp.where(lane == 1, (i2 - N_GROUPS).astype(F32),
                                jnp.where(lane == 2, p_grp * w1, jnp.where(lane == 3, p_grp * w2, 0.0))))
    route_ref[...] = route


PROMPT_TOK_BLOCKS = N_PROMPT // TM_TOK


def _router_merged_kernel(xp_ref, xs_ref, shp_ref, scp_ref, shs_ref, scs_ref, g_ref, wr_ref, br_ref,
                          h_ref, route_ref):
    i = pl.program_id(0)

    @pl.when(i < PROMPT_TOK_BLOCKS)
    def _():
        _router_kernel(xp_ref, shp_ref, scp_ref, g_ref, wr_ref, br_ref, h_ref, route_ref)

    @pl.when(i >= PROMPT_TOK_BLOCKS)
    def _():
        _router_kernel(xs_ref, shs_ref, scs_ref, g_ref, wr_ref, br_ref, h_ref, route_ref)


def _router(x1_p, x1_s, mod_p, mod_s, g_ffn, w_route, b_route):
    pi = lambda i: jnp.minimum(i, PROMPT_TOK_BLOCKS - 1)
    si = lambda i: jnp.maximum(i - PROMPT_TOK_BLOCKS, 0)
    return pl.pallas_call(
        _router_merged_kernel,
        grid=(N_TOK // TM_TOK,),
        in_specs=[pl.BlockSpec((TM_TOK, D), lambda i: (pi(i), 0)),
                  pl.BlockSpec((TM_TOK, D), lambda i: (si(i), 0)),
                  pl.BlockSpec((1, D), lambda i: (0, 3)),
                  pl.BlockSpec((1, D), lambda i: (0, 4)),
                  pl.BlockSpec((TM_TOK, D), lambda i: (si(i), 3)),
                  pl.BlockSpec((TM_TOK, D), lambda i: (si(i), 4)),
                  pl.BlockSpec((1, D), lambda i: (0, 0)),
                  pl.BlockSpec((D, LANES), lambda i: (0, 0)),
                  pl.BlockSpec((1, LANES), lambda i: (0, 0))],
        out_specs=[pl.BlockSpec((TM_TOK, D_PACK), lambda i: (i, 0)),
                   pl.BlockSpec((TM_TOK, LANES), lambda i: (i, 0))],
        out_shape=[jax.ShapeDtypeStruct((N_TOK, D_PACK), jnp.uint32),
                   jax.ShapeDtypeStruct((N_TOK, LANES), F32)],
        compiler_params=_params(("arbitrary",)),
        name="router",
    )(x1_p, x1_s, mod_p, mod_p, mod_s, mod_s, g_ffn, w_route, b_route)


def _rank_kernel(route_ref, rank_ref, cnt_ref, carry):
    step = pl.program_id(0)

    @pl.when(step == 0)
    def _():
        carry[...] = jnp.zeros_like(carry)

    route = route_ref[...]
    n = route.shape[0]
    lane = _iota((n, LANES), 1).astype(F32)
    o1 = jnp.where(lane == route[:, 0:1], 1.0, 0.0)
    o2 = jnp.where(lane == route[:, 1:2], 1.0, 0.0)
    both = o1 + o2
    strict = jnp.where(_iota((n, n), 1) < _iota((n, n), 0), 1.0, 0.0).astype(BF16)
    prior = _dot(strict, both.astype(BF16)) + carry[0:1, :]
    r1 = jnp.sum(o1 * prior, axis=1, keepdims=True)
    r2 = jnp.sum(o2 * prior, axis=1, keepdims=True)
    lane_i = _iota((n, LANES), 1)
    rank_ref[...] = jnp.where(lane_i == 0, r1, jnp.where(lane_i == 1, r2, 0.0))
    carry[...] = carry[...] + jnp.sum(both, axis=0, keepdims=True)
    cnt_ref[...] = carry[...]


def _rank(route):
    n = route.shape[0]
    return pl.pallas_call(
        _rank_kernel,
        grid=(n // TM_TOK,),
        in_specs=[pl.BlockSpec((TM_TOK, LANES), lambda i: (i, 0))],
        out_specs=[pl.BlockSpec((TM_TOK, LANES), lambda i: (i, 0)),
                   pl.BlockSpec((SUBLANES, LANES), lambda i: (0, 0))],
        out_shape=[jax.ShapeDtypeStruct((n, LANES), F32), jax.ShapeDtypeStruct((SUBLANES, LANES), F32)],
        scratch_shapes=[pltpu.VMEM((SUBLANES, LANES), F32)],
        compiler_params=_params(("arbitrary",)),
        name="rank",
    )(route)


def _row_copy(src, dst, sem):
    return pltpu.make_async_copy(src, dst, sem)


ROW_UNROLL = 8
D_PACK = D // 2

ZERO_FIRST, ZERO_ANYTIME = 1, 2


def _dispatch_kernel(d1_ref, d2_ref, zc_ref, h_ref, xs_out, zbuf, sem):
    step = pl.program_id(0)
    base = step * TM_TOK

    def zero_tile(t, s):
        rows = pl.ds(pl.multiple_of(t * TM_MOE, TM_MOE), TM_MOE)
        return pltpu.make_async_copy(zbuf, xs_out.at[rows], sem.at[s])

    def for_tiles(cls, s, act):
        def body(t, carry):
            @pl.when(zc_ref[t] == cls)
            def _():
                act(zero_tile(t, s))
            return carry
        lax.fori_loop(0, MAX_TILES, body, 0)

    @pl.when(step == 0)
    def _():
        zbuf[...] = jnp.zeros_like(zbuf)
        for_tiles(ZERO_FIRST, 1, lambda c: c.start())
        for_tiles(ZERO_ANYTIME, 2, lambda c: c.start())
        for_tiles(ZERO_FIRST, 1, lambda c: c.wait())

    @pl.when(step == pl.num_programs(0) - 1)
    def _():
        for_tiles(ZERO_ANYTIME, 2, lambda c: c.wait())

    def issue(r, carry):
        _row_copy(h_ref.at[pl.ds(r, 1)], xs_out.at[pl.ds(d1_ref[base + r], 1)], sem.at[0]).start()
        _row_copy(h_ref.at[pl.ds(r, 1)], xs_out.at[pl.ds(d2_ref[base + r], 1)], sem.at[0]).start()
        return carry

    lax.fori_loop(0, TM_TOK, issue, 0, unroll=ROW_UNROLL)
    for _ in range(TOP_K):
        _row_copy(h_ref, xs_out.at[pl.ds(0, TM_TOK)], sem.at[0]).wait()


def _dispatch(dests, zero_class, h2):
    grid_spec = pltpu.PrefetchScalarGridSpec(
        num_scalar_prefetch=3,
        grid=(N_TOK // TM_TOK,),
        in_specs=[pl.BlockSpec((TM_TOK, D_PACK), lambda i, *_: (i, 0))],
        out_specs=pl.BlockSpec(memory_space=pl.ANY),
        scratch_shapes=[pltpu.VMEM((TM_MOE, D_PACK), jnp.uint32), pltpu.SemaphoreType.DMA((3,))],
    )
    return pl.pallas_call(
        _dispatch_kernel,
        grid_spec=grid_spec,
        out_shape=jax.ShapeDtypeStruct((A_PAD, D_PACK), jnp.uint32),
        compiler_params=_params(("arbitrary",)),
        name="dispatch",
    )(*dests, zero_class, h2)


N_UP_CHUNKS = D_FF // TN_FF


def _expert_up_kernel(te_ref, tv_ref, first_ref, nxt_ref, run_ref, meta_ref, x_ref, wg_hbm, wu_hbm,
                      o_ref, wbuf, sem):
    n = pl.program_id(0)
    t = pl.program_id(1)
    slot = lax.rem(n * meta_ref[1] + run_ref[t], 2)

    def fetch(e, chunk, s):
        cols = pl.ds(pl.multiple_of(chunk * TN_FF, TN_FF), TN_FF)
        return (pltpu.make_async_copy(wg_hbm.at[e, :, cols], wbuf.at[s, 0], sem.at[s, 0]),
                pltpu.make_async_copy(wu_hbm.at[e, :, cols], wbuf.at[s, 1], sem.at[s, 1]))

    @pl.when(first_ref[t] == 1)
    def _():
        @pl.when(jnp.logical_and(n == 0, t == 0))
        def _():
            for c in fetch(te_ref[0], 0, 0):
                c.start()

        for c in fetch(te_ref[t], n, slot):
            c.wait()

        @pl.when(nxt_ref[t] >= 0)
        def _():
            for c in fetch(nxt_ref[t], n, 1 - slot):
                c.start()

        @pl.when(jnp.logical_and(nxt_ref[t] < 0, n + 1 < N_UP_CHUNKS))
        def _():
            for c in fetch(te_ref[0], n + 1, 1 - slot):
                c.start()

    @pl.when(tv_ref[t] == 1)
    def _():
        xw = x_ref[...]
        x = jnp.concatenate([lax.bitcast_convert_type(xw << 16, F32).astype(BF16),
                             lax.bitcast_convert_type(xw & jnp.uint32(0xFFFF0000), F32).astype(BF16)], axis=1)
        g = _dot(x, wbuf[slot, 0].astype(BF16))
        u = _dot(x, wbuf[slot, 1].astype(BF16))
        o_ref[...] = (g * _sigmoid(g) * u).astype(BF16)

    @pl.when(tv_ref[t] == 0)
    def _():
        o_ref[...] = jnp.zeros_like(o_ref)


def _expert_up(tiles, xs, w_gate, w_up):
    last = lambda t, meta: jnp.minimum(t, meta[0] - 1)
    grid_spec = pltpu.PrefetchScalarGridSpec(
        num_scalar_prefetch=6,
        grid=(N_UP_CHUNKS, MAX_TILES),
        in_specs=[pl.BlockSpec((TM_MOE, D_PACK), lambda n, t, *s: (last(t, s[5]), 0)),
                  pl.BlockSpec(memory_space=pl.ANY),
                  pl.BlockSpec(memory_space=pl.ANY)],
        out_specs=pl.BlockSpec((TM_MOE, TN_FF), lambda n, t, *s: (t, n)),
        scratch_shapes=[pltpu.VMEM((2, 2, D, TN_FF), F32), pltpu.SemaphoreType.DMA((2, 2))],
    )
    return pl.pallas_call(
        _expert_up_kernel,
        grid_spec=grid_spec,
        out_shape=jax.ShapeDtypeStruct((A_PAD, D_FF), BF16),
        compiler_params=_params(("arbitrary", "arbitrary")),
        name="expert_up",
    )(*tiles, xs, w_gate, w_up)


def _expert_down_kernel(te_ref, tv_ref, first_ref, nxt_ref, run_ref, meta_ref, h_ref, wd_hbm,
                        o_ref, wbuf, sem):
    t = pl.program_id(0)
    slot = lax.rem(run_ref[t], 2)

    def fetch(e, s):
        return pltpu.make_async_copy(wd_hbm.at[e], wbuf.at[s], sem.at[s])

    @pl.when(first_ref[t] == 1)
    def _():
        @pl.when(t == 0)
        def _():
            fetch(te_ref[0], 0).start()

        fetch(te_ref[t], slot).wait()

        @pl.when(nxt_ref[t] >= 0)
        def _():
            fetch(nxt_ref[t], 1 - slot).start()

    @pl.when(tv_ref[t] == 1)
    def _():
        o_ref[...] = _dot(h_ref[...], wbuf[slot].astype(BF16))

    @pl.when(tv_ref[t] == 0)
    def _():
        o_ref[...] = jnp.zeros_like(o_ref)


def _expert_down(tiles, h1, w_down):
    last = lambda t, meta: jnp.minimum(t, meta[0] - 1)
    grid_spec = pltpu.PrefetchScalarGridSpec(
        num_scalar_prefetch=6,
        grid=(MAX_TILES,),
        in_specs=[pl.BlockSpec((TM_MOE, D_FF), lambda t, *s: (last(t, s[5]), 0)),
                  pl.BlockSpec(memory_space=pl.ANY)],
        out_specs=pl.BlockSpec((TM_MOE, D), lambda t, *s: (t, 0)),
        scratch_shapes=[pltpu.VMEM((2, D_FF, D), F32), pltpu.SemaphoreType.DMA((2,))],
    )
    return pl.pallas_call(
        _expert_down_kernel,
        grid_spec=grid_spec,
        out_shape=jax.ShapeDtypeStruct((A_PAD, D), F32),
        compiler_params=_params(("arbitrary",)),
        name="expert_down",
    )(*tiles, h1, w_down)


def _combine_kernel(d1_ref, d2_ref, x_ref, route_ref, gt_ref, gf_ref, o_hbm,
                    y_ref, buf, sem, *, tok_offset):
    step = pl.program_id(0)
    slot = lax.rem(step, 2)

    def gather(s, into):
        base = tok_offset + s * TM_TOK

        def issue(r, carry):
            _row_copy(o_hbm.at[pl.ds(d1_ref[base + r], 1)], buf.at[into, 0, pl.ds(r, 1)], sem.at[into]).start()
            _row_copy(o_hbm.at[pl.ds(d2_ref[base + r], 1)], buf.at[into, 1, pl.ds(r, 1)], sem.at[into]).start()
            return carry

        lax.fori_loop(0, TM_TOK, issue, 0, unroll=ROW_UNROLL)

    @pl.when(step == 0)
    def _():
        gather(0, 0)

    @pl.when(step + 1 < pl.num_programs(0))
    def _():
        gather(step + 1, 1 - slot)

    for k in range(TOP_K):
        _row_copy(o_hbm.at[pl.ds(0, TM_TOK)], buf.at[slot, k], sem.at[slot]).wait()

    route = route_ref[...]
    moe = route[:, 2:3] * buf[slot, 0] + route[:, 3:4] * buf[slot, 1]
    x2 = x_ref[...] + gt_ref[...] * moe
    y_ref[...] = x2 * lax.rsqrt(jnp.mean(x2 * x2, axis=-1, keepdims=True) + EPS) * gf_ref[...]


def _combine(dests, x1, route, mod, per_row, g_final, o_rows, tok_offset):
    n = x1.shape[0]
    mrows = TM_TOK if per_row else 1
    mi = (lambda i: i) if per_row else (lambda i: 0)
    off = tok_offset // TM_TOK
    grid_spec = pltpu.PrefetchScalarGridSpec(
        num_scalar_prefetch=2,
        grid=(n // TM_TOK,),
        in_specs=[pl.BlockSpec((TM_TOK, D), lambda i, *_: (i, 0)),
                  pl.BlockSpec((TM_TOK, LANES), lambda i, *_: (i + off, 0)),
                  pl.BlockSpec((mrows, D), lambda i, *_: (mi(i), 5)),
                  pl.BlockSpec((1, D), lambda i, *_: (0, 0)),
                  pl.BlockSpec(memory_space=pl.ANY)],
        out_specs=pl.BlockSpec((TM_TOK, D), lambda i, *_: (i, 0)),
        scratch_shapes=[pltpu.VMEM((2, TOP_K, TM_TOK, D), F32), pltpu.SemaphoreType.DMA((2,))],
    )
    return pl.pallas_call(
        functools.partial(_combine_kernel, tok_offset=tok_offset),
        grid_spec=grid_spec,
        out_shape=jax.ShapeDtypeStruct((n, D), F32),
        compiler_params=_params(("arbitrary",)),
        name="combine",
    )(*dests, x1, route, mod, g_final, o_rows)


def kernel(x_prompt, x_sample, cache_k, cache_v, state_C, state_n, state_m, c_prompt, c_sample, rel_bias, w_ada, b_ada, g_mix, g_ffn, w_in, sinks, b_igate, b_fgate, w_out, w_router_grp, b_router_grp, w_router_exp, b_router_exp, w_gate, w_up, w_down, g_final):
    xp = x_prompt.reshape(N_PROMPT, D)
    xs = x_sample.reshape(N_SAMPLE, D)

    c_all = jnp.concatenate([c_prompt, c_sample, jnp.zeros((C_ROWS - 1 - N_BATCH, D), F32)], axis=0)
    mod = _ada(c_all, w_ada[0], b_ada)
    mod_p = mod[0:1]
    mod_s = jnp.repeat(mod[1:1 + N_BATCH], T_DEC, axis=0)

    w_in_bf = w_in[0].astype(BF16)
    w_gates = jnp.pad(w_in[0, :, Z_WIDTH:].astype(BF16), ((0, 0), (0, LANES - 2 * M_HEADS)))
    zp, zgp = _inproj(xp, mod_p, False, g_mix, w_in_bf, w_gates)
    zs, zgs = _inproj(xs, mod_s, True, g_mix, w_in_bf, w_gates)

    rb_flat = rel_bias.reshape(NUM_BUCKETS * ATT_HEADS)
    sink_v = sinks[0]
    ya_p = _swa_prompt(zp, rb_flat, sink_v)
    ya_s, nk_s, nv_s = _swa_sample(zs, cache_k.reshape(N_BATCH, CACHE_ROWS, HEAD_DIM),
                                   cache_v.reshape(N_BATCH, CACHE_ROWS, HEAD_DIM), rb_flat, sink_v)

    gate_bias = jnp.concatenate([b_igate[0], b_fgate[0], jnp.zeros((LANES - 2 * M_HEADS,), F32)]).reshape(1, LANES)
    ym_p, c_p, n_p, m_p = _mlstm_prompt(zp, zgp, gate_bias)
    m0_tok = jnp.pad(jnp.repeat(state_m[0], T_DEC, axis=0), ((0, 0), (0, LANES - M_HEADS)))
    n0_flat = state_n[0].reshape(N_BATCH, M_HEADS * DK)
    n0_tok = jnp.repeat(n0_flat, T_DEC, axis=0)
    ym_s, c_s, n_s, m_s = _mlstm_sample(zs, zgs, gate_bias, m0_tok, n0_tok, state_C[0], n0_flat)

    w_out_bf = w_out[0].astype(BF16)
    x1_p = _outproj(ya_p, ym_p, w_out_bf, xp, mod_p, False)
    x1_s = _outproj(ya_s, ym_s, w_out_bf, xs, mod_s, True)

    w_route = jnp.pad(jnp.concatenate([w_router_grp[0], w_router_exp[0]], axis=1),
                      ((0, 0), (0, LANES - N_GROUPS - N_EXPERTS)))
    b_route = jnp.pad(jnp.concatenate([b_router_grp[0], b_router_exp[0]]),
                      (0, LANES - N_GROUPS - N_EXPERTS)).reshape(1, LANES)
    h2, route = _router(x1_p, x1_s, mod_p, mod_s, g_ffn, w_route, b_route)

    rank, counts = _rank(route)

    i32 = lambda a: a.astype(jnp.int32)
    cnt = i32(counts[0, :N_EXPERTS])
    tiles_per = (cnt + TM_MOE - 1) // TM_MOE
    tile_end = jnp.cumsum(tiles_per)
    pad_start = (tile_end - tiles_per) * TM_MOE
    n_tiles = tile_end[-1]
    tile_ids = jnp.arange(MAX_TILES, dtype=jnp.int32)
    tile_expert = i32(jnp.minimum(jnp.searchsorted(tile_end, tile_ids, side="right"), N_EXPERTS - 1))
    last_expert = tile_expert[jnp.maximum(n_tiles - 1, 0)]
    tile_valid = tile_ids < n_tiles
    tile_expert = jnp.where(tile_valid, tile_expert, last_expert)
    prev_expert = jnp.concatenate([jnp.full((1,), -1, jnp.int32), tile_expert[:-1]])
    run_first = jnp.logical_and(tile_valid, tile_expert != prev_expert)
    run_id = jnp.maximum(jnp.cumsum(i32(run_first)) - 1, 0)
    expert_ids = jnp.arange(N_EXPERTS, dtype=jnp.int32)
    used = jnp.where(tiles_per > 0, expert_ids, N_EXPERTS)
    next_used = jnp.concatenate([lax.cummin(used[::-1])[::-1][1:], jnp.full((1,), N_EXPERTS, jnp.int32)])
    next_used = jnp.where(next_used >= N_EXPERTS, -1, next_used)
    tiles = (tile_expert, i32(tile_valid), i32(run_first), next_used[tile_expert], i32(run_id),
             jnp.stack([n_tiles, jnp.sum(i32(run_first))]).astype(jnp.int32))
    next_expert = jnp.concatenate([tile_expert[1:], jnp.full((1,), -1, jnp.int32)])
    run_last = jnp.logical_or(tile_expert != next_expert, tile_ids == n_tiles - 1)
    zero_class = jnp.where(tile_valid, jnp.where(run_last, ZERO_FIRST, 0), ZERO_ANYTIME)
    dests = tuple(i32(pad_start)[i32(route[:, k])] + i32(rank[:, k]) for k in range(TOP_K))

    xs_rows = _dispatch(dests, i32(zero_class), h2)
    h1 = _expert_up(tiles, xs_rows, w_gate[0], w_up[0])
    o_rows = _expert_down(tiles, h1, w_down[0])

    gf = g_final.reshape(1, D)
    y_p = _combine(dests, x1_p, route, mod_p, False, gf, o_rows, 0)
    y_s = _combine(dests, x1_s, route, mod_s, True, gf, o_rows, N_PROMPT)

    kv5 = lambda a: a.reshape(1, -1, WINDOW, KV_HEADS, HEAD_DIM)
    kcol = ATT_WIDTH
    nk_p = zp[N_PROMPT - WINDOW:, kcol:kcol + KV_WIDTH]
    nv_p = zp[N_PROMPT - WINDOW:, kcol + KV_WIDTH:kcol + 2 * KV_WIDTH]
    return (y_p.reshape(1, N_PROMPT, D), y_s.reshape(N_BATCH, T_DEC, D),
            kv5(nk_p), kv5(nv_p),
            c_p.reshape(1, 1, M_HEADS, DV, DK), n_p.reshape(1, 1, M_HEADS, DK), m_p[:, 0].reshape(1, 1, M_HEADS),
            kv5(nk_s), kv5(nv_s),
            c_s.reshape(1, N_BATCH, M_HEADS, DV, DK), n_s.reshape(1, N_BATCH, M_HEADS, DK),
            m_s.reshape(N_BATCH, T_DEC, LANES)[:, T_DEC - 1, :M_HEADS].reshape(1, N_BATCH, M_HEADS))
```

```python
import functools
import math

import numpy as np
import jax
import jax.numpy as jnp
from jax import lax
from jax.experimental import pallas as pl
from jax.experimental.pallas import tpu as pltpu

F32 = jnp.float32
BF16 = jnp.bfloat16
NEG_INF = float("-inf")

D = 4096
N_PROMPT = 8192
N_BATCH = 128
T_DEC = 8
N_SAMPLE = N_BATCH * T_DEC
N_TOK = N_PROMPT + N_SAMPLE
HEAD_DIM = 128
ATT_HEADS = 16
KV_HEADS = 4
GROUP = ATT_HEADS // KV_HEADS
WINDOW = 128
ATT_WIDTH = ATT_HEADS * HEAD_DIM
KV_WIDTH = KV_HEADS * HEAD_DIM
NUM_BUCKETS = 32
MAX_EXACT = 16
MAX_DISTANCE = 128
M_HEADS = 8
DK = 128
DV = 256
M_WIDTH = M_HEADS * DV
Z_WIDTH = ATT_WIDTH + 2 * KV_WIDTH + 2 * M_HEADS * DK + 2 * M_WIDTH
N_GROUPS = 4
EXP_PER_GROUP = 8
N_EXPERTS = N_GROUPS * EXP_PER_GROUP
TOP_K = 2
D_FF = 1024
EPS = 1e-6
ATT_SCALE = HEAD_DIM ** -0.5
Q_SCALE = DK ** -0.5

LANES = 128
SUBLANES = 8
VMEM_LIMIT = 56 * 1024 * 1024

TM_SHARED = 512
TM_ROWMOD = 256
TN_IN = 1024
TN_OUT = 1024
TN_ADA = 512
C_ROWS = 136
ATT_BLOCK = 128
SAMPLE_BT = 8
ML_CHUNK = 256
TM_MOE = 256
TN_FF = 512
N_ASSIGN = N_TOK * TOP_K
MAX_TILES = N_ASSIGN // TM_MOE + N_EXPERTS
A_PAD = MAX_TILES * TM_MOE
TM_TOK = 256


def _params(sem):
    return pltpu.CompilerParams(dimension_semantics=sem, vmem_limit_bytes=VMEM_LIMIT)


def _iota(shape, dim):
    return lax.broadcasted_iota(jnp.int32, shape, dim)


def _dot(a, b):
    return jnp.dot(a, b, preferred_element_type=F32)


def _dot_nt(a, b):
    return lax.dot_general(a, b, (((1,), (1,)), ((), ())), preferred_element_type=F32)


def _dot_tn(a, b):
    return lax.dot_general(a, b, (((0,), (0,)), ((), ())), preferred_element_type=F32)


def _split3(x):
    x1 = x.astype(BF16)
    r1 = x - x1.astype(F32)
    x2 = r1.astype(BF16)
    r2 = r1 - x2.astype(F32)
    return x1, x2, r2.astype(BF16)


def _dot_exact_lhs01(a01, x):
    x1, x2, x3 = _split3(x)
    return _dot(a01, x1) + _dot(a01, x2) + _dot(a01, x3)


def _pack_bf16_pairs(x):
    w = x.shape[1] // 2
    bits = lax.bitcast_convert_type(x.astype(BF16).astype(F32), jnp.uint32)
    return (bits[:, :w] >> 16) | (bits[:, w:] & jnp.uint32(0xFFFF0000))


def _unpack_bf16_pairs(words):
    return (lax.bitcast_convert_type(words << 16, F32),
            lax.bitcast_convert_type(words & jnp.uint32(0xFFFF0000), F32))


def _sigmoid(x):
    return 1.0 / (1.0 + jnp.exp(-x))


def _log_sigmoid(x):
    return jnp.minimum(x, 0.0) - jnp.log(1.0 + jnp.exp(-jnp.abs(x)))


def _ada_kernel(c_ref, w_ref, b_ref, o_ref):
    c = c_ref[...]
    s = (c * _sigmoid(c)).astype(BF16)
    o_ref[...] = _dot(s, w_ref[...].astype(BF16)) + b_ref[...]


def _ada(c_all, w_ada, b_ada):
    n = w_ada.shape[1]
    return pl.pallas_call(
        _ada_kernel,
        grid=(n // TN_ADA,),
        in_specs=[pl.BlockSpec((C_ROWS, D), lambda j: (0, 0)),
                  pl.BlockSpec((D, TN_ADA), lambda j: (0, j)),
                  pl.BlockSpec((1, TN_ADA), lambda j: (0, j))],
        out_specs=pl.BlockSpec((C_ROWS, TN_ADA), lambda j: (0, j)),
        out_shape=jax.ShapeDtypeStruct((C_ROWS, n), F32),
        compiler_params=_params(("arbitrary",)),
        name="ada",
    )(c_all, w_ada, b_ada)


def _inproj_kernel(x_ref, sh_ref, sc_ref, g_ref, w_ref, wg_ref, z_ref, zg_ref, h_scr):
    @pl.when(pl.program_id(1) == 0)
    def _():
        x = x_ref[...]
        y = x * lax.rsqrt(jnp.mean(x * x, axis=-1, keepdims=True) + EPS) * g_ref[...]
        hb = (y * (1.0 + sc_ref[...]) + sh_ref[...]).astype(BF16)
        h_scr[...] = hb
        zg_ref[...] = _dot(hb, wg_ref[...])

    z_ref[...] = _dot(h_scr[...], w_ref[...])


def _inproj(x, mod, per_row, g_mix, w_main, w_gate):
    n = x.shape[0]
    TM = TM_ROWMOD if per_row else TM_SHARED
    mrows = TM if per_row else 1
    mi = (lambda i: i) if per_row else (lambda i: 0)
    return pl.pallas_call(
        _inproj_kernel,
        grid=(n // TM, Z_WIDTH // TN_IN),
        in_specs=[pl.BlockSpec((TM, D), lambda i, j: (i, 0)),
                  pl.BlockSpec((mrows, D), lambda i, j: (mi(i), 0)),
                  pl.BlockSpec((mrows, D), lambda i, j: (mi(i), 1)),
                  pl.BlockSpec((1, D), lambda i, j: (0, 0)),
                  pl.BlockSpec((D, TN_IN), lambda i, j: (0, j)),
                  pl.BlockSpec((D, LANES), lambda i, j: (0, 0))],
        out_specs=[pl.BlockSpec((TM, TN_IN), lambda i, j: (i, j)),
                   pl.BlockSpec((TM, LANES), lambda i, j: (i, 0))],
        out_shape=[jax.ShapeDtypeStruct((n, Z_WIDTH), F32),
                   jax.ShapeDtypeStruct((n, LANES), F32)],
        scratch_shapes=[pltpu.VMEM((TM, D), BF16)],
        compiler_params=_params(("arbitrary", "arbitrary")),
        name="inproj",
    )(x, mod, mod, g_mix, w_main, w_gate)


def _t5_bucket_np(dist):
    n = np.maximum(dist, 0)
    nf = np.maximum(n, 1).astype(np.float32)
    large = MAX_EXACT + (np.log(nf / MAX_EXACT) / math.log(MAX_DISTANCE / MAX_EXACT)
                         * (NUM_BUCKETS - MAX_EXACT)).astype(np.int32)
    large = np.minimum(large, NUM_BUCKETS - 1)
    return np.where(n < MAX_EXACT, n, large).astype(np.int32)


def _bucket_table(n_q, n_keys_valid, n_keys_padded):
    t = np.arange(n_q)[:, None]
    j = np.arange(n_keys_padded)[None, :]
    dist = t + WINDOW - j
    valid = (dist >= 0) & (dist < WINDOW) & (j < n_keys_valid)
    return np.where(valid, _t5_bucket_np(dist), -1).astype(np.int32)


def _fill_bias(bucket_ref, rb_ref, bias_scr, rows, per_head_table=False):
    for h in range(ATT_HEADS):
        bk = bucket_ref[h * rows:(h + 1) * rows, :] if per_head_table else bucket_ref[...]
        acc = jnp.full(bk.shape, NEG_INF, F32)
        for b in range(NUM_BUCKETS):
            acc = jnp.where(bk == b, rb_ref[b * ATT_HEADS + h], acc)
        bias_scr[h * rows:(h + 1) * rows, :] = acc


def _with_ones(v2):
    return jnp.concatenate([v2, jnp.ones_like(v2)], axis=1)


def _sink_softmax_av(lg, sink, v2_ones):
    m = jnp.maximum(jnp.max(lg, axis=-1, keepdims=True), sink)
    p = jnp.exp(lg - m).astype(BF16)
    pv = _dot(p, v2_ones)
    return pv[:, :HEAD_DIM] / (pv[:, HEAD_DIM:] + jnp.exp(sink - m))


def _swa_prompt_kernel(rb_ref, sink_ref, bucket_ref, q_ref, kp_ref, kc_ref, vp_ref, vc_ref,
                       o_ref, bias_scr):
    i = pl.program_id(0)

    @pl.when(i == 0)
    def _():
        _fill_bias(bucket_ref, rb_ref, bias_scr, ATT_BLOCK)

    first_prev = jnp.logical_and(i == 0, _iota((ATT_BLOCK, 2 * ATT_BLOCK), 1) < ATT_BLOCK)
    for g in range(KV_HEADS):
        ks = slice(g * HEAD_DIM, (g + 1) * HEAD_DIM)
        k2 = jnp.concatenate([kp_ref[:, ks], kc_ref[:, ks]], axis=0).astype(BF16)
        v2 = _with_ones(jnp.concatenate([vp_ref[:, ks], vc_ref[:, ks]], axis=0).astype(BF16))
        for r in range(GROUP):
            h = g * GROUP + r
            hs = slice(h * HEAD_DIM, (h + 1) * HEAD_DIM)
            lg = _dot_nt(q_ref[:, hs].astype(BF16), k2) * ATT_SCALE + bias_scr[h * ATT_BLOCK:(h + 1) * ATT_BLOCK, :]
            lg = jnp.where(first_prev, NEG_INF, lg)
            o_ref[:, hs] = _sink_softmax_av(lg, sink_ref[h], v2).astype(BF16)


def _swa_prompt(z, rb_flat, sinks):
    nb = N_PROMPT // ATT_BLOCK
    bucket = jnp.asarray(_bucket_table(ATT_BLOCK, 2 * ATT_BLOCK, 2 * ATT_BLOCK))
    kcol = ATT_WIDTH // KV_WIDTH
    prev = lambda i: jnp.maximum(i - 1, 0)
    smem = pl.BlockSpec(memory_space=pltpu.SMEM)
    return pl.pallas_call(
        _swa_prompt_kernel,
        grid=(nb,),
        in_specs=[smem, smem,
                  pl.BlockSpec((ATT_BLOCK, 2 * ATT_BLOCK), lambda i: (0, 0)),
                  pl.BlockSpec((ATT_BLOCK, ATT_WIDTH), lambda i: (i, 0)),
                  pl.BlockSpec((ATT_BLOCK, KV_WIDTH), lambda i: (prev(i), kcol)),
                  pl.BlockSpec((ATT_BLOCK, KV_WIDTH), lambda i: (i, kcol)),
                  pl.BlockSpec((ATT_BLOCK, KV_WIDTH), lambda i: (prev(i), kcol + 1)),
                  pl.BlockSpec((ATT_BLOCK, KV_WIDTH), lambda i: (i, kcol + 1))],
        out_specs=pl.BlockSpec((ATT_BLOCK, ATT_WIDTH), lambda i: (i, 0)),
        out_shape=jax.ShapeDtypeStruct((N_PROMPT, ATT_WIDTH), BF16),
        scratch_shapes=[pltpu.VMEM((ATT_HEADS * ATT_BLOCK, 2 * ATT_BLOCK), F32)],
        compiler_params=_params(("arbitrary",)),
        name="swa_prompt",
    )(rb_flat, sinks, bucket, z, z, z, z, z)


CACHE_ROWS = WINDOW * KV_HEADS
NEW_ROWS = T_DEC * KV_HEADS
S_KEYS = 5 * LANES


def _sample_bucket_table():
    t = np.arange(T_DEC)[:, None]
    col = np.arange(S_KEYS)[None, :]
    in_cache = col < CACHE_ROWS
    in_new = (col >= CACHE_ROWS) & (col < CACHE_ROWS + NEW_ROWS)
    key_head = np.where(in_cache, col % KV_HEADS, (col - CACHE_ROWS) // T_DEC)
    key_pos = np.where(in_cache, col // KV_HEADS, WINDOW + (col - CACHE_ROWS) % T_DEC)
    dist = t + WINDOW - key_pos
    valid = (dist >= 0) & (dist < WINDOW) & (in_cache | in_new)
    per_query = np.where(valid, _t5_bucket_np(dist), -1)
    heads = np.arange(ATT_HEADS)[:, None, None] // GROUP
    table = np.where(heads == key_head[None], per_query[None], -1)
    return table.reshape(ATT_HEADS * T_DEC, S_KEYS).astype(np.int32)


def _swa_sample_kernel(rb_ref, sink_ref, bucket_ref, q_ref, kn_ref, vn_ref, ck_ref, cv_ref,
                       o_ref, nk_ref, nv_ref, bias_scr):
    @pl.when(pl.program_id(0) == 0)
    def _():
        _fill_bias(bucket_ref, rb_ref, bias_scr, T_DEC, per_head_table=True)

    nk_ref[:, 0:CACHE_ROWS - NEW_ROWS, :] = ck_ref[:, NEW_ROWS:CACHE_ROWS, :]
    nv_ref[:, 0:CACHE_ROWS - NEW_ROWS, :] = cv_ref[:, NEW_ROWS:CACHE_ROWS, :]

    pad = jnp.zeros((S_KEYS - CACHE_ROWS - NEW_ROWS, HEAD_DIM), F32)
    sink_col = jnp.concatenate([jnp.full((T_DEC, 1), sink_ref[h], F32) for h in range(ATT_HEADS)], axis=0)
    bias = bias_scr[...]
    for b in range(SAMPLE_BT):
        ts = slice(b * T_DEC, (b + 1) * T_DEC)
        head_cols = lambda ref, n: [ref[ts, h * HEAD_DIM:(h + 1) * HEAD_DIM] for h in range(n)]
        k_new, v_new = head_cols(kn_ref, KV_HEADS), head_cols(vn_ref, KV_HEADS)
        for g in range(KV_HEADS):
            new_rows = pl.ds(CACHE_ROWS - NEW_ROWS + g, T_DEC, stride=KV_HEADS)
            nk_ref[b, new_rows, :] = k_new[g]
            nv_ref[b, new_rows, :] = v_new[g]
        qa = jnp.concatenate(head_cols(q_ref, ATT_HEADS), axis=0).astype(BF16)
        k2 = jnp.concatenate([ck_ref[b]] + k_new + [pad], axis=0).astype(BF16)
        v2 = _with_ones(jnp.concatenate([cv_ref[b]] + v_new + [pad], axis=0).astype(BF16))
        o = _sink_softmax_av(_dot_nt(qa, k2) * ATT_SCALE + bias, sink_col, v2)
        for h in range(ATT_HEADS):
            o_ref[ts, h * HEAD_DIM:(h + 1) * HEAD_DIM] = o[h * T_DEC:(h + 1) * T_DEC].astype(BF16)


def _swa_sample(z, cache_k, cache_v, rb_flat, sinks):
    rows = SAMPLE_BT * T_DEC
    bucket = jnp.asarray(_sample_bucket_table())
    kcol = ATT_WIDTH // KV_WIDTH
    smem = pl.BlockSpec(memory_space=pltpu.SMEM)
    cache_spec = pl.BlockSpec((SAMPLE_BT, CACHE_ROWS, HEAD_DIM), lambda i: (i, 0, 0))
    cache_shape = jax.ShapeDtypeStruct((N_BATCH, CACHE_ROWS, HEAD_DIM), F32)
    return pl.pallas_call(
        _swa_sample_kernel,
        grid=(N_BATCH // SAMPLE_BT,),
        in_specs=[smem, smem,
                  pl.BlockSpec((ATT_HEADS * T_DEC, S_KEYS), lambda i: (0, 0)),
                  pl.BlockSpec((rows, ATT_WIDTH), lambda i: (i, 0)),
                  pl.BlockSpec((rows, KV_WIDTH), lambda i: (i, kcol)),
                  pl.BlockSpec((rows, KV_WIDTH), lambda i: (i, kcol + 1)),
                  cache_spec, cache_spec],
        out_specs=[pl.BlockSpec((rows, ATT_WIDTH), lambda i: (i, 0)), cache_spec, cache_spec],
        out_shape=[jax.ShapeDtypeStruct((N_SAMPLE, ATT_WIDTH), BF16), cache_shape, cache_shape],
        scratch_shapes=[pltpu.VMEM((ATT_HEADS * T_DEC, S_KEYS), F32)],
        compiler_params=_params(("arbitrary",)),
        name="swa_sample",
    )(rb_flat, sinks, bucket, z, z, z, cache_k, cache_v)


LANE_IG, LANE_LF, LANE_B = 0, M_HEADS, 2 * M_HEADS


def _gate_table(zg, gate_bias, seg_len):
    L = zg.shape[0]
    g = zg + gate_bias
    lane = _iota((L, LANES), 1)
    lf = _log_sigmoid(g)
    lf_only = jnp.where(jnp.logical_and(lane >= LANE_LF, lane < LANE_B), lf, 0.0)
    row = _iota((L, L), 0)
    col = _iota((L, L), 1)
    same_seg = (row // seg_len) == (col // seg_len)
    tril = jnp.where(jnp.logical_and(col <= row, same_seg), 1.0, 0.0).astype(BF16)
    cum = pltpu.roll(_dot_exact_lhs01(tril, lf_only), M_HEADS, axis=1)
    table = jnp.where(lane < LANE_LF, g, jnp.where(lane < LANE_B, lf, jnp.where(lane < LANE_B + M_HEADS, cum, 0.0)))
    return table, jnp.logical_and(col <= row, same_seg)


def _mlstm_intra(table, table_t, mask, m0_col, h, q, k, v):
    b_c = table[:, LANE_B + h:LANE_B + h + 1]
    b_r = table_t[LANE_B + h:LANE_B + h + 1, :]
    ig_r = table_t[LANE_IG + h:LANE_IG + h + 1, :]
    log_d = jnp.where(mask, b_c - b_r + ig_r, NEG_INF)
    log_inter = b_c + m0_col
    m_t = jnp.maximum(log_inter, jnp.max(log_d, axis=1, keepdims=True))
    d = jnp.exp(log_d - m_t)
    w_inter = jnp.exp(log_inter - m_t)
    s = _dot_nt(q.astype(BF16), k.astype(BF16)) * d
    num_intra = _dot(s.astype(BF16), v.astype(BF16))
    den_intra = jnp.sum(s, axis=1, keepdims=True)
    return b_c, m_t, w_inter, num_intra, den_intra


def _mlstm_prompt_kernel(gb_ref, zg_ref, q_ref, k_ref, va_ref, vb_ref, oa_ref, ob_ref,
                         y_ref, c_out, n_out, m_out, c_scr, n_scr, m_scr):
    step = pl.program_id(0)
    L = ML_CHUNK

    @pl.when(step == 0)
    def _():
        c_scr[...] = jnp.zeros_like(c_scr)
        n_scr[...] = jnp.zeros_like(n_scr)
        m_scr[...] = jnp.zeros_like(m_scr)

    table, mask = _gate_table(zg_ref[...], gb_ref[...], L)
    table_t = table.T
    half = M_HEADS // 2
    for h in range(M_HEADS):
        v_ref, o_ref = (va_ref, oa_ref) if h < half else (vb_ref, ob_ref)
        vs = slice((h % half) * DV, (h % half + 1) * DV)
        q = q_ref[:, h * DK:(h + 1) * DK] * Q_SCALE
        k = k_ref[:, h * DK:(h + 1) * DK]
        v = v_ref[:, vs]
        m0 = m_scr[h:h + 1, 0:1]
        b_c, m_t, w_inter, num_intra, den_intra = _mlstm_intra(table, table_t, mask, m0, h, q, k, v)
        c_old = c_scr[h]
        n_old = n_scr[h:h + 1, :]
        num = num_intra + w_inter * _dot_nt(q.astype(BF16), c_old.astype(BF16))
        den = den_intra + w_inter * jnp.sum(q * n_old, axis=1, keepdims=True)
        hh = num / jnp.maximum(jnp.abs(den), jnp.exp(-m_t))
        y_ref[:, h * DV:(h + 1) * DV] = (_sigmoid(o_ref[:, vs]) * hh).astype(BF16)

        ig_c = table[:, LANE_IG + h:LANE_IG + h + 1]
        m_new = m_t[L - 1:L, :]
        b_last = b_c[L - 1:L, :]
        w_s = jnp.exp(b_last - b_c + ig_c - m_new)
        decay = jnp.exp(b_last + m0 - m_new)
        c_scr[h] = decay * c_old + _dot_tn((w_s * v).astype(BF16), k.astype(BF16))
        n_scr[h:h + 1, :] = decay * n_old + jnp.sum(w_s * k, axis=0, keepdims=True)
        m_scr[h:h + 1, :] = jnp.broadcast_to(m_new, (1, LANES))

    @pl.when(step == pl.num_programs(0) - 1)
    def _():
        c_out[...] = c_scr[...]
        n_out[...] = n_scr[...]
        m_out[...] = m_scr[...]


def _mlstm_prompt(z, zg, gate_bias):
    L = ML_CHUNK
    blk = M_HEADS * DK
    col = lambda c: pl.BlockSpec((L, blk), lambda i: (i, c))
    const = lambda shape: pl.BlockSpec(shape, lambda i: tuple(0 for _ in shape))
    return pl.pallas_call(
        _mlstm_prompt_kernel,
        grid=(N_PROMPT // L,),
        in_specs=[const((1, LANES)),
                  pl.BlockSpec((L, LANES), lambda i: (i, 0)),
                  col(3), col(4), col(5), col(6), col(7), col(8)],
        out_specs=[pl.BlockSpec((L, M_WIDTH), lambda i: (i, 0)),
                   const((M_HEADS, DV, DK)), const((M_HEADS, DK)), const((M_HEADS, LANES))],
        out_shape=[jax.ShapeDtypeStruct((N_PROMPT, M_WIDTH), BF16),
                   jax.ShapeDtypeStruct((M_HEADS, DV, DK), F32),
                   jax.ShapeDtypeStruct((M_HEADS, DK), F32),
                   jax.ShapeDtypeStruct((M_HEADS, LANES), F32)],
        scratch_shapes=[pltpu.VMEM((M_HEADS, DV, DK), F32),
                        pltpu.VMEM((M_HEADS, DK), F32),
                        pltpu.VMEM((M_HEADS, LANES), F32)],
        compiler_params=_params(("arbitrary",)),
        name="mlstm_prompt",
    )(gate_bias, zg, z, z, z, z, z, z)


S_ROWS = SAMPLE_BT * T_DEC


def _mlstm_sample_kernel(gb_ref, zg_ref, m0_ref, n0t_ref, q_ref, k_ref, va_ref, vb_ref, oa_ref, ob_ref,
                         c0_ref, n0_ref, y_ref, c_out, n_out, m_out):
    L = S_ROWS
    table, mask = _gate_table(zg_ref[...], gb_ref[...], T_DEC)
    table_t = jnp.concatenate([table, jnp.zeros((LANES - L, LANES), F32)], axis=0).T[:, 0:L]
    m0_all = m0_ref[...]
    lane = _iota((L, LANES), 1)
    row_b = _iota((SAMPLE_BT, L), 1) // T_DEC
    seg_sum = jnp.where(row_b == _iota((SAMPLE_BT, L), 0), 1.0, 0.0)
    m_tok = jnp.zeros((L, LANES), F32)
    half = M_HEADS // 2
    for h in range(M_HEADS):
        v_ref, o_ref = (va_ref, oa_ref) if h < half else (vb_ref, ob_ref)
        vs = slice((h % half) * DV, (h % half + 1) * DV)
        q = q_ref[:, h * DK:(h + 1) * DK] * Q_SCALE
        k = k_ref[:, h * DK:(h + 1) * DK]
        v = v_ref[:, vs]
        m0 = m0_all[:, h:h + 1]
        b_c, m_t, w_inter, num_intra, den_intra = _mlstm_intra(table, table_t, mask, m0, h, q, k, v)
        qb = q.astype(BF16)
        num_inter = jnp.concatenate(
            [_dot_nt(qb[b * T_DEC:(b + 1) * T_DEC], c0_ref[b, h].astype(BF16)) for b in range(SAMPLE_BT)], axis=0)
        num = num_intra + w_inter * num_inter
        den = den_intra + w_inter * jnp.sum(q * n0t_ref[:, h * DK:(h + 1) * DK], axis=1, keepdims=True)
        hh = num / jnp.maximum(jnp.abs(den), jnp.exp(-m_t))
        y_ref[:, h * DV:(h + 1) * DV] = (_sigmoid(o_ref[:, vs]) * hh).astype(BF16)
        m_tok = jnp.where(lane == h, m_t, m_tok)

        def last_tok(x):
            x3 = x.reshape(SAMPLE_BT, T_DEC, 1)
            return jnp.broadcast_to(x3[:, T_DEC - 1:T_DEC, :], x3.shape).reshape(L, 1)

        ig_c = table[:, LANE_IG + h:LANE_IG + h + 1]
        m_new = last_tok(m_t)
        b_last = last_tok(b_c)
        w_s = jnp.exp(b_last - b_c + ig_c - m_new)
        decay = jnp.exp(b_last + m0 - m_new)
        wv = (w_s * v).astype(BF16)
        kb = k.astype(BF16)
        rowsel = _iota((L, 1), 0) // T_DEC
        for b in range(SAMPLE_BT):
            dec_b = decay[b * T_DEC + T_DEC - 1:(b + 1) * T_DEC, :]
            wv_b = jnp.where(rowsel == b, wv, jnp.zeros_like(wv))
            c_out[b, h] = dec_b * c0_ref[b, h] + _dot_tn(wv_b, kb)
        dec_rows = decay.reshape(SAMPLE_BT, T_DEC, 1)[:, T_DEC - 1, :]
        n_out[:, h * DK:(h + 1) * DK] = dec_rows * n0_ref[:, h * DK:(h + 1) * DK] + jnp.dot(
            seg_sum, w_s * k, preferred_element_type=F32, precision=lax.Precision.HIGHEST)
    m_out[...] = m_tok


def _mlstm_sample(z, zg, gate_bias, m0_tok, n0_tok, state_c, state_n):
    L = S_ROWS
    blk = M_HEADS * DK
    col = lambda c: pl.BlockSpec((L, blk), lambda i: (i, c))
    c_spec = pl.BlockSpec((SAMPLE_BT, M_HEADS, DV, DK), lambda i: (i, 0, 0, 0))
    n_spec = pl.BlockSpec((SAMPLE_BT, blk), lambda i: (i, 0))
    return pl.pallas_call(
        _mlstm_sample_kernel,
        grid=(N_BATCH // SAMPLE_BT,),
        in_specs=[pl.BlockSpec((1, LANES), lambda i: (0, 0)),
                  pl.BlockSpec((L, LANES), lambda i: (i, 0)),
                  pl.BlockSpec((L, LANES), lambda i: (i, 0)),
                  pl.BlockSpec((L, blk), lambda i: (i, 0)),
                  col(3), col(4), col(5), col(6), col(7), col(8),
                  c_spec, n_spec],
        out_specs=[pl.BlockSpec((L, M_WIDTH), lambda i: (i, 0)), c_spec, n_spec,
                   pl.BlockSpec((L, LANES), lambda i: (i, 0))],
        out_shape=[jax.ShapeDtypeStruct((N_SAMPLE, M_WIDTH), BF16),
                   jax.ShapeDtypeStruct((N_BATCH, M_HEADS, DV, DK), F32),
                   jax.ShapeDtypeStruct((N_BATCH, blk), F32),
                   jax.ShapeDtypeStruct((N_SAMPLE, LANES), F32)],
        compiler_params=_params(("arbitrary",)),
        name="mlstm_sample",
    )(gate_bias, zg, m0_tok, n0_tok, z, z, z, z, z, z, state_c, state_n)


def _outproj_kernel(ya_ref, ym_ref, wa_ref, wm_ref, x_ref, gt_ref, o_ref):
    mix = _dot(ya_ref[...], wa_ref[...]) + _dot(ym_ref[...], wm_ref[...])
    o_ref[...] = x_ref[...] + gt_ref[...] * mix


def _outproj(ya, ym, w_out_bf, x, mod, per_row):
    n = x.shape[0]
    TM = TM_ROWMOD if per_row else TM_SHARED
    mrows = TM if per_row else 1
    mi = (lambda i: i) if per_row else (lambda i: 0)
    gate1_col = 2 * (D // TN_OUT)
    return pl.pallas_call(
        _outproj_kernel,
        grid=(n // TM, D // TN_OUT),
        in_specs=[pl.BlockSpec((TM, ATT_WIDTH), lambda i, j: (i, 0)),
                  pl.BlockSpec((TM, M_WIDTH), lambda i, j: (i, 0)),
                  pl.BlockSpec((ATT_WIDTH, TN_OUT), lambda i, j: (0, j)),
                  pl.BlockSpec((M_WIDTH, TN_OUT), lambda i, j: (1, j)),
                  pl.BlockSpec((TM, TN_OUT), lambda i, j: (i, j)),
                  pl.BlockSpec((mrows, TN_OUT), lambda i, j: (mi(i), gate1_col + j))],
        out_specs=pl.BlockSpec((TM, TN_OUT), lambda i, j: (i, j)),
        out_shape=jax.ShapeDtypeStruct((n, D), F32),
        compiler_params=_params(("arbitrary", "arbitrary")),
        name="outproj",
    )(ya, ym, w_out_bf, w_out_bf, x, mod)


def _router_kernel(x_ref, sh_ref, sc_ref, g_ref, wr_ref, br_ref, h_ref, route_ref):
    x = x_ref[...]
    y = x * lax.rsqrt(jnp.mean(x * x, axis=-1, keepdims=True) + EPS) * g_ref[...]
    h2 = y * (1.0 + sc_ref[...]) + sh_ref[...]
    h_hi = h2.astype(BF16)
    h_ref[...] = _pack_bf16_pairs(h2)
    h_lo = (h2 - h_hi.astype(F32)).astype(BF16)
    logits = _dot(h_hi, wr_ref[0]) + (_dot(h_hi, wr_ref[1]) + _dot(h_lo, wr_ref[0])) + br_ref[...]
    lane = _iota(logits.shape, 1)

    def first_max(vals):
        vmax = jnp.max(vals, axis=1, keepdims=True)
        idx = jnp.min(jnp.where(vals == vmax, lane, LANES), axis=1, keepdims=True)
        return vmax, idx

    gl = jnp.where(lane < N_GROUPS, logits, NEG_INF)
    gmax, grp = first_max(gl)
    p_grp = 1.0 / jnp.sum(jnp.exp(gl - gmax), axis=1, keepdims=True)
    e_lane = lane - N_GROUPS
    in_grp = jnp.logical_and(e_lane >= 0, jnp.logical_and(e_lane < N_EXPERTS, e_lane // EXP_PER_GROUP == grp))
    el = jnp.where(in_grp, logits, NEG_INF)
    v1, i1 = first_max(el)
    v2, i2 = first_max(jnp.where(lane == i1, NEG_INF, el))
    e2w = jnp.exp(v2 - v1)
    w1 = 1.0 / (1.0 + e2w)
    w2 = e2w / (1.0 + e2w)
    route = jnp.where(lane == 0, (i1 - N_GROUPS).astype(F32),
                      jnp.where(lane == 1, (i2 - N_GROUPS).astype(F32),
                                jnp.where(lane == 2, p_grp * w1, jnp.where(lane == 3, p_grp * w2, 0.0))))
    route_ref[...] = route


PROMPT_TOK_BLOCKS = N_PROMPT // TM_TOK


def _router_merged_kernel(xp_ref, xs_ref, shp_ref, scp_ref, shs_ref, scs_ref, g_ref, wr_ref, br_ref,
                          h_ref, route_ref):
    i = pl.program_id(0)

    @pl.when(i < PROMPT_TOK_BLOCKS)
    def _():
        _router_kernel(xp_ref, shp_ref, scp_ref, g_ref, wr_ref, br_ref, h_ref, route_ref)

    @pl.when(i >= PROMPT_TOK_BLOCKS)
    def _():
        _router_kernel(xs_ref, shs_ref, scs_ref, g_ref, wr_ref, br_ref, h_ref, route_ref)


def _router(x1_p, x1_s, mod_p, mod_s, g_ffn, w_route, b_route):
    pi = lambda i: jnp.minimum(i, PROMPT_TOK_BLOCKS - 1)
    si = lambda i: jnp.maximum(i - PROMPT_TOK_BLOCKS, 0)
    return pl.pallas_call(
        _router_merged_kernel,
        grid=(N_TOK // TM_TOK,),
        in_specs=[pl.BlockSpec((TM_TOK, D), lambda i: (pi(i), 0)),
                  pl.BlockSpec((TM_TOK, D), lambda i: (si(i), 0)),
                  pl.BlockSpec((1, D), lambda i: (0, 3)),
                  pl.BlockSpec((1, D), lambda i: (0, 4)),
                  pl.BlockSpec((TM_TOK, D), lambda i: (si(i), 3)),
                  pl.BlockSpec((TM_TOK, D), lambda i: (si(i), 4)),
                  pl.BlockSpec((1, D), lambda i: (0, 0)),
                  pl.BlockSpec((2, D, LANES), lambda i: (0, 0, 0)),
                  pl.BlockSpec((1, LANES), lambda i: (0, 0))],
        out_specs=[pl.BlockSpec((TM_TOK, D_PACK), lambda i: (i, 0)),
                   pl.BlockSpec((TM_TOK, LANES), lambda i: (i, 0))],
        out_shape=[jax.ShapeDtypeStruct((N_TOK, D_PACK), jnp.uint32),
                   jax.ShapeDtypeStruct((N_TOK, LANES), F32)],
        compiler_params=_params(("arbitrary",)),
        name="router",
    )(x1_p, x1_s, mod_p, mod_p, mod_s, mod_s, g_ffn, w_route, b_route)


def _rank_kernel(route_ref, dest_ref, cnt_ref, carry, rank_scr):
    pas = pl.program_id(0)
    step = pl.program_id(1)

    @pl.when(jnp.logical_and(pas == 0, step == 0))
    def _():
        carry[...] = jnp.zeros_like(carry)

    route = route_ref[...]
    n = route.shape[0]
    lane = _iota((n, LANES), 1).astype(F32)
    o1 = jnp.where(lane == route[:, 0:1], 1.0, 0.0)
    o2 = jnp.where(lane == route[:, 1:2], 1.0, 0.0)
    lane_i = _iota((n, LANES), 1)
    rows = pl.ds(pl.multiple_of(step * TM_TOK, TM_TOK), TM_TOK)

    @pl.when(pas == 0)
    def _():
        both = o1 + o2
        strict = jnp.where(_iota((n, n), 1) < _iota((n, n), 0), 1.0, 0.0).astype(BF16)
        prior = _dot(strict, both.astype(BF16)) + carry[0:1, :]
        r1 = jnp.sum(o1 * prior, axis=1, keepdims=True)
        r2 = jnp.sum(o2 * prior, axis=1, keepdims=True)
        rank_scr[rows, :] = jnp.where(lane_i == 0, r1, jnp.where(lane_i == 1, r2, 0.0))
        carry[...] = carry[...] + jnp.sum(both, axis=0, keepdims=True)

    @pl.when(pas == 1)
    def _():
        counts = carry[...]
        tiles_per = jnp.floor((counts + (TM_MOE - 1)) * (1.0 / TM_MOE))
        before = jnp.where(_iota((LANES, LANES), 0) < _iota((LANES, LANES), 1), 1.0, 0.0).astype(BF16)
        pad_start = _dot(tiles_per.astype(BF16), before)[0:1, :] * TM_MOE
        rank = rank_scr[rows, :]
        d1 = jnp.sum(o1 * pad_start, axis=1, keepdims=True) + rank[:, 0:1]
        d2 = jnp.sum(o2 * pad_start, axis=1, keepdims=True) + rank[:, 1:2]
        dest_ref[...] = jnp.where(lane_i == 0, d1, jnp.where(lane_i == 1, d2, 0.0)).astype(jnp.int32)
        cnt_ref[...] = counts


def _rank(route):
    n = route.shape[0]
    return pl.pallas_call(
        _rank_kernel,
        grid=(2, n // TM_TOK),
        in_specs=[pl.BlockSpec((TM_TOK, LANES), lambda p, i: (i, 0))],
        out_specs=[pl.BlockSpec((TM_TOK, LANES), lambda p, i: (i * p, 0)),
                   pl.BlockSpec((SUBLANES, LANES), lambda p, i: (0, 0))],
        out_shape=[jax.ShapeDtypeStruct((n, LANES), jnp.int32), jax.ShapeDtypeStruct((SUBLANES, LANES), F32)],
        scratch_shapes=[pltpu.VMEM((SUBLANES, LANES), F32), pltpu.VMEM((n, LANES), F32)],
        compiler_params=_params(("arbitrary", "arbitrary")),
        name="rank",
    )(route)


def _row_copy(src, dst, sem):
    return pltpu.make_async_copy(src, dst, sem)


ROW_UNROLL = 8
D_PACK = D // 2

ZERO_FIRST, ZERO_ANYTIME = 1, 2


def _dispatch_kernel(d1_ref, d2_ref, zc_ref, h_ref, xs_out, zbuf, sem):
    step = pl.program_id(0)
    base = step * TM_TOK

    def zero_tile(t, s):
        rows = pl.ds(pl.multiple_of(t * TM_MOE, TM_MOE), TM_MOE)
        return pltpu.make_async_copy(zbuf, xs_out.at[rows], sem.at[s])

    def for_tiles(cls, s, act):
        def body(t, carry):
            @pl.when(zc_ref[t] == cls)
            def _():
                act(zero_tile(t, s))
            return carry
        lax.fori_loop(0, MAX_TILES, body, 0)

    @pl.when(step == 0)
    def _():
        zbuf[...] = jnp.zeros_like(zbuf)
        for_tiles(ZERO_FIRST, 1, lambda c: c.start())
        for_tiles(ZERO_ANYTIME, 2, lambda c: c.start())
        for_tiles(ZERO_FIRST, 1, lambda c: c.wait())

    @pl.when(step == pl.num_programs(0) - 1)
    def _():
        for_tiles(ZERO_ANYTIME, 2, lambda c: c.wait())

    def issue(r, carry):
        _row_copy(h_ref.at[pl.ds(r, 1)], xs_out.at[pl.ds(d1_ref[base + r], 1)], sem.at[0]).start()
        _row_copy(h_ref.at[pl.ds(r, 1)], xs_out.at[pl.ds(d2_ref[base + r], 1)], sem.at[0]).start()
        return carry

    lax.fori_loop(0, TM_TOK, issue, 0, unroll=ROW_UNROLL)
    for _ in range(TOP_K):
        _row_copy(h_ref, xs_out.at[pl.ds(0, TM_TOK)], sem.at[0]).wait()


def _dispatch(dests, zero_class, h2):
    grid_spec = pltpu.PrefetchScalarGridSpec(
        num_scalar_prefetch=3,
        grid=(N_TOK // TM_TOK,),
        in_specs=[pl.BlockSpec((TM_TOK, D_PACK), lambda i, *_: (i, 0))],
        out_specs=pl.BlockSpec(memory_space=pl.ANY),
        scratch_shapes=[pltpu.VMEM((TM_MOE, D_PACK), jnp.uint32), pltpu.SemaphoreType.DMA((3,))],
    )
    return pl.pallas_call(
        _dispatch_kernel,
        grid_spec=grid_spec,
        out_shape=jax.ShapeDtypeStruct((A_PAD, D_PACK), jnp.uint32),
        compiler_params=_params(("arbitrary",)),
        name="dispatch",
    )(*dests, zero_class, h2)


N_UP_CHUNKS = D_FF // TN_FF


def _expert_up_kernel(te_ref, tv_ref, first_ref, nxt_ref, run_ref, meta_ref, x_ref, wg_hbm, wu_hbm,
                      o_ref, wbuf, sem):
    n = pl.program_id(0)
    t = pl.program_id(1)
    slot = lax.rem(n * meta_ref[1] + run_ref[t], 2)

    def fetch(e, chunk, s):
        cols = pl.ds(pl.multiple_of(chunk * TN_FF, TN_FF), TN_FF)
        return (pltpu.make_async_copy(wg_hbm.at[e, :, cols], wbuf.at[s, 0], sem.at[s, 0]),
                pltpu.make_async_copy(wu_hbm.at[e, :, cols], wbuf.at[s, 1], sem.at[s, 1]))

    @pl.when(first_ref[t] == 1)
    def _():
        @pl.when(jnp.logical_and(n == 0, t == 0))
        def _():
            for c in fetch(te_ref[0], 0, 0):
                c.start()

        for c in fetch(te_ref[t], n, slot):
            c.wait()

        @pl.when(nxt_ref[t] >= 0)
        def _():
            for c in fetch(nxt_ref[t], n, 1 - slot):
                c.start()

        @pl.when(jnp.logical_and(nxt_ref[t] < 0, n + 1 < N_UP_CHUNKS))
        def _():
            for c in fetch(te_ref[0], n + 1, 1 - slot):
                c.start()

    @pl.when(tv_ref[t] == 1)
    def _():
        x = jnp.concatenate([half.astype(BF16) for half in _unpack_bf16_pairs(x_ref[...])], axis=1)
        g = _dot(x, wbuf[slot, 0].astype(BF16))
        u = _dot(x, wbuf[slot, 1].astype(BF16))
        o_ref[...] = (g * _sigmoid(g) * u).astype(BF16)

    @pl.when(tv_ref[t] == 0)
    def _():
        o_ref[...] = jnp.zeros_like(o_ref)


def _expert_up(tiles, xs, w_gate, w_up):
    last = lambda t, meta: jnp.minimum(t, meta[0] - 1)
    grid_spec = pltpu.PrefetchScalarGridSpec(
        num_scalar_prefetch=6,
        grid=(N_UP_CHUNKS, MAX_TILES),
        in_specs=[pl.BlockSpec((TM_MOE, D_PACK), lambda n, t, *s: (last(t, s[5]), 0)),
                  pl.BlockSpec(memory_space=pl.ANY),
                  pl.BlockSpec(memory_space=pl.ANY)],
        out_specs=pl.BlockSpec((TM_MOE, TN_FF), lambda n, t, *s: (t, n)),
        scratch_shapes=[pltpu.VMEM((2, 2, D, TN_FF), F32), pltpu.SemaphoreType.DMA((2, 2))],
    )
    return pl.pallas_call(
        _expert_up_kernel,
        grid_spec=grid_spec,
        out_shape=jax.ShapeDtypeStruct((A_PAD, D_FF), BF16),
        compiler_params=_params(("arbitrary", "arbitrary")),
        name="expert_up",
    )(*tiles, xs, w_gate, w_up)


def _expert_down_kernel(te_ref, tv_ref, first_ref, nxt_ref, run_ref, meta_ref, h_ref, wd_hbm,
                        o_ref, wbuf, sem):
    t = pl.program_id(0)
    slot = lax.rem(run_ref[t], 2)

    def fetch(e, s):
        return pltpu.make_async_copy(wd_hbm.at[e], wbuf.at[s], sem.at[s])

    @pl.when(first_ref[t] == 1)
    def _():
        @pl.when(t == 0)
        def _():
            fetch(te_ref[0], 0).start()

        fetch(te_ref[t], slot).wait()

        @pl.when(nxt_ref[t] >= 0)
        def _():
            fetch(nxt_ref[t], 1 - slot).start()

    @pl.when(tv_ref[t] == 1)
    def _():
        o_ref[...] = _pack_bf16_pairs(_dot(h_ref[...], wbuf[slot].astype(BF16)))

    @pl.when(tv_ref[t] == 0)
    def _():
        o_ref[...] = jnp.zeros_like(o_ref)


def _expert_down(tiles, h1, w_down):
    last = lambda t, meta: jnp.minimum(t, meta[0] - 1)
    grid_spec = pltpu.PrefetchScalarGridSpec(
        num_scalar_prefetch=6,
        grid=(MAX_TILES,),
        in_specs=[pl.BlockSpec((TM_MOE, D_FF), lambda t, *s: (last(t, s[5]), 0)),
                  pl.BlockSpec(memory_space=pl.ANY)],
        out_specs=pl.BlockSpec((TM_MOE, D_PACK), lambda t, *s: (t, 0)),
        scratch_shapes=[pltpu.VMEM((2, D_FF, D), F32), pltpu.SemaphoreType.DMA((2,))],
    )
    return pl.pallas_call(
        _expert_down_kernel,
        grid_spec=grid_spec,
        out_shape=jax.ShapeDtypeStruct((A_PAD, D_PACK), jnp.uint32),
        compiler_params=_params(("arbitrary",)),
        name="expert_down",
    )(*tiles, h1, w_down)


def _combine_kernel(d1_ref, d2_ref, x_ref, route_ref, gt_ref, gf_ref, o_hbm,
                    y_ref, buf, sem, *, tok_offset):
    step = pl.program_id(0)
    slot = lax.rem(step, 2)

    def gather(s, into):
        base = tok_offset + s * TM_TOK

        def issue(r, carry):
            _row_copy(o_hbm.at[pl.ds(d1_ref[base + r], 1)], buf.at[into, 0, pl.ds(r, 1)], sem.at[into]).start()
            _row_copy(o_hbm.at[pl.ds(d2_ref[base + r], 1)], buf.at[into, 1, pl.ds(r, 1)], sem.at[into]).start()
            return carry

        lax.fori_loop(0, TM_TOK, issue, 0, unroll=ROW_UNROLL)

    @pl.when(step == 0)
    def _():
        gather(0, 0)

    @pl.when(step + 1 < pl.num_programs(0))
    def _():
        gather(step + 1, 1 - slot)

    for k in range(TOP_K):
        _row_copy(o_hbm.at[pl.ds(0, TM_TOK)], buf.at[slot, k], sem.at[slot]).wait()

    route = route_ref[...]
    lo1, hi1 = _unpack_bf16_pairs(buf[slot, 0])
    lo2, hi2 = _unpack_bf16_pairs(buf[slot, 1])
    g1, g2 = route[:, 2:3], route[:, 3:4]
    moe = jnp.concatenate([g1 * lo1 + g2 * lo2, g1 * hi1 + g2 * hi2], axis=1)
    x2 = x_ref[...] + gt_ref[...] * moe
    y_ref[...] = x2 * lax.rsqrt(jnp.mean(x2 * x2, axis=-1, keepdims=True) + EPS) * gf_ref[...]


def _combine(dests, x1, route, mod, per_row, g_final, o_rows, tok_offset):
    n = x1.shape[0]
    mrows = TM_TOK if per_row else 1
    mi = (lambda i: i) if per_row else (lambda i: 0)
    off = tok_offset // TM_TOK
    grid_spec = pltpu.PrefetchScalarGridSpec(
        num_scalar_prefetch=2,
        grid=(n // TM_TOK,),
        in_specs=[pl.BlockSpec((TM_TOK, D), lambda i, *_: (i, 0)),
                  pl.BlockSpec((TM_TOK, LANES), lambda i, *_: (i + off, 0)),
                  pl.BlockSpec((mrows, D), lambda i, *_: (mi(i), 5)),
                  pl.BlockSpec((1, D), lambda i, *_: (0, 0)),
                  pl.BlockSpec(memory_space=pl.ANY)],
        out_specs=pl.BlockSpec((TM_TOK, D), lambda i, *_: (i, 0)),
        scratch_shapes=[pltpu.VMEM((2, TOP_K, TM_TOK, D_PACK), jnp.uint32), pltpu.SemaphoreType.DMA((2,))],
    )
    return pl.pallas_call(
        functools.partial(_combine_kernel, tok_offset=tok_offset),
        grid_spec=grid_spec,
        out_shape=jax.ShapeDtypeStruct((n, D), F32),
        compiler_params=_params(("arbitrary",)),
        name="combine",
    )(*dests, x1, route, mod, g_final, o_rows)


def kernel(x_prompt, x_sample, cache_k, cache_v, state_C, state_n, state_m, c_prompt, c_sample, rel_bias, w_ada, b_ada, g_mix, g_ffn, w_in, sinks, b_igate, b_fgate, w_out, w_router_grp, b_router_grp, w_router_exp, b_router_exp, w_gate, w_up, w_down, g_final):
    xp = x_prompt.reshape(N_PROMPT, D)
    xs = x_sample.reshape(N_SAMPLE, D)

    c_all = jnp.concatenate([c_prompt, c_sample, jnp.zeros((C_ROWS - 1 - N_BATCH, D), F32)], axis=0)
    mod = _ada(c_all, w_ada[0], b_ada)
    mod_p = mod[0:1]
    mod_s = jnp.repeat(mod[1:1 + N_BATCH], T_DEC, axis=0)

    w_in_bf = w_in[0].astype(BF16)
    w_gates = jnp.pad(w_in[0, :, Z_WIDTH:].astype(BF16), ((0, 0), (0, LANES - 2 * M_HEADS)))
    zp, zgp = _inproj(xp, mod_p, False, g_mix, w_in_bf, w_gates)
    zs, zgs = _inproj(xs, mod_s, True, g_mix, w_in_bf, w_gates)

    rb_flat = rel_bias.reshape(NUM_BUCKETS * ATT_HEADS)
    sink_v = sinks[0]
    ya_p = _swa_prompt(zp, rb_flat, sink_v)
    ya_s, nk_s, nv_s = _swa_sample(zs, cache_k.reshape(N_BATCH, CACHE_ROWS, HEAD_DIM),
                                   cache_v.reshape(N_BATCH, CACHE_ROWS, HEAD_DIM), rb_flat, sink_v)

    gate_bias = jnp.concatenate([b_igate[0], b_fgate[0], jnp.zeros((LANES - 2 * M_HEADS,), F32)]).reshape(1, LANES)
    ym_p, c_p, n_p, m_p = _mlstm_prompt(zp, zgp, gate_bias)
    m0_tok = jnp.pad(jnp.repeat(state_m[0], T_DEC, axis=0), ((0, 0), (0, LANES - M_HEADS)))
    n0_flat = state_n[0].reshape(N_BATCH, M_HEADS * DK)
    n0_tok = jnp.repeat(n0_flat, T_DEC, axis=0)
    ym_s, c_s, n_s, m_s = _mlstm_sample(zs, zgs, gate_bias, m0_tok, n0_tok, state_C[0], n0_flat)

    w_out_bf = w_out[0].astype(BF16)
    x1_p = _outproj(ya_p, ym_p, w_out_bf, xp, mod_p, False)
    x1_s = _outproj(ya_s, ym_s, w_out_bf, xs, mod_s, True)

    w_route = jnp.pad(jnp.concatenate([w_router_grp[0], w_router_exp[0]], axis=1),
                      ((0, 0), (0, LANES - N_GROUPS - N_EXPERTS)))
    b_route = jnp.pad(jnp.concatenate([b_router_grp[0], b_router_exp[0]]),
                      (0, LANES - N_GROUPS - N_EXPERTS)).reshape(1, LANES)
    w_route_hi = w_route.astype(BF16)
    w_route_split = jnp.stack([w_route_hi, (w_route - w_route_hi.astype(F32)).astype(BF16)])
    h2, route = _router(x1_p, x1_s, mod_p, mod_s, g_ffn, w_route_split, b_route)

    dest, counts = _rank(route)
    dests = (dest[:, 0], dest[:, 1])

    i32 = lambda a: a.astype(jnp.int32)
    cnt = i32(counts[0, :N_EXPERTS])
    tiles_per = (cnt + TM_MOE - 1) // TM_MOE
    tile_end = jnp.cumsum(tiles_per)
    n_tiles = tile_end[-1]
    tile_ids = jnp.arange(MAX_TILES, dtype=jnp.int32)
    tile_expert = i32(jnp.minimum(jnp.searchsorted(tile_end, tile_ids, side="right"), N_EXPERTS - 1))
    last_expert = tile_expert[jnp.maximum(n_tiles - 1, 0)]
    tile_valid = tile_ids < n_tiles
    tile_expert = jnp.where(tile_valid, tile_expert, last_expert)
    prev_expert = jnp.concatenate([jnp.full((1,), -1, jnp.int32), tile_expert[:-1]])
    run_first = jnp.logical_and(tile_valid, tile_expert != prev_expert)
    run_id = jnp.maximum(jnp.cumsum(i32(run_first)) - 1, 0)
    expert_ids = jnp.arange(N_EXPERTS, dtype=jnp.int32)
    used = jnp.where(tiles_per > 0, expert_ids, N_EXPERTS)
    next_used = jnp.concatenate([lax.cummin(used[::-1])[::-1][1:], jnp.full((1,), N_EXPERTS, jnp.int32)])
    next_used = jnp.where(next_used >= N_EXPERTS, -1, next_used)
    tiles = (tile_expert, i32(tile_valid), i32(run_first), next_used[tile_expert], i32(run_id),
             jnp.stack([n_tiles, jnp.sum(i32(run_first))]).astype(jnp.int32))
    next_expert = jnp.concatenate([tile_expert[1:], jnp.full((1,), -1, jnp.int32)])
    run_last = jnp.logical_or(tile_expert != next_expert, tile_ids == n_tiles - 1)
    zero_class = jnp.where(tile_valid, jnp.where(run_last, ZERO_FIRST, 0), ZERO_ANYTIME)
    xs_rows = _dispatch(dests, i32(zero_class), h2)
    h1 = _expert_up(tiles, xs_rows, w_gate[0], w_up[0])
    o_rows = _expert_down(tiles, h1, w_down[0])

    gf = g_final.reshape(1, D)
    y_p = _combine(dests, x1_p, route, mod_p, False, gf, o_rows, 0)
    y_s = _combine(dests, x1_s, route, mod_s, True, gf, o_rows, N_PROMPT)

    kv5 = lambda a: a.reshape(1, -1, WINDOW, KV_HEADS, HEAD_DIM)
    kcol = ATT_WIDTH
    nk_p = zp[N_PROMPT - WINDOW:, kcol:kcol + KV_WIDTH]
    nv_p = zp[N_PROMPT - WINDOW:, kcol + KV_WIDTH:kcol + 2 * KV_WIDTH]
    return (y_p.reshape(1, N_PROMPT, D), y_s.reshape(N_BATCH, T_DEC, D),
            kv5(nk_p), kv5(nv_p),
            c_p.reshape(1, 1, M_HEADS, DV, DK), n_p.reshape(1, 1, M_HEADS, DK), m_p[:, 0].reshape(1, 1, M_HEADS),
            kv5(nk_s), kv5(nv_s),
            c_s.reshape(1, N_BATCH, M_HEADS, DV, DK), n_s.reshape(1, N_BATCH, M_HEADS, DK),
            m_s.reshape(N_BATCH, T_DEC, LANES)[:, T_DEC - 1, :M_HEADS].reshape(1, N_BATCH, M_HEADS))
```

```python
import functools
import math

import numpy as np
import jax
import jax.numpy as jnp
from jax import lax
from jax.experimental import pallas as pl
from jax.experimental.pallas import tpu as pltpu

F32 = jnp.float32
BF16 = jnp.bfloat16
NEG_INF = float("-inf")

D = 4096
N_PROMPT = 8192
N_BATCH = 128
T_DEC = 8
N_SAMPLE = N_BATCH * T_DEC
N_TOK = N_PROMPT + N_SAMPLE
HEAD_DIM = 128
ATT_HEADS = 16
KV_HEADS = 4
GROUP = ATT_HEADS // KV_HEADS
WINDOW = 128
ATT_WIDTH = ATT_HEADS * HEAD_DIM
KV_WIDTH = KV_HEADS * HEAD_DIM
NUM_BUCKETS = 32
MAX_EXACT = 16
MAX_DISTANCE = 128
M_HEADS = 8
DK = 128
DV = 256
M_WIDTH = M_HEADS * DV
Z_WIDTH = ATT_WIDTH + 2 * KV_WIDTH + 2 * M_HEADS * DK + 2 * M_WIDTH
N_GROUPS = 4
EXP_PER_GROUP = 8
N_EXPERTS = N_GROUPS * EXP_PER_GROUP
TOP_K = 2
D_FF = 1024
EPS = 1e-6
ATT_SCALE = HEAD_DIM ** -0.5
Q_SCALE = DK ** -0.5

LANES = 128
SUBLANES = 8
VMEM_LIMIT = 56 * 1024 * 1024

TM_PROJ = 512
TN_IN = 1024
TN_OUT = 1024
TN_ADA = 512
C_ROWS = 136
ATT_BLOCK = 128
SAMPLE_BT = 8
ML_CHUNK = 256
TM_MOE = 256
TN_FF = 512
N_ASSIGN = N_TOK * TOP_K
MAX_TILES = N_ASSIGN // TM_MOE + N_EXPERTS
A_PAD = MAX_TILES * TM_MOE
TM_TOK = 256
TM_RANK = 1024


def _params(sem):
    return pltpu.CompilerParams(dimension_semantics=sem, vmem_limit_bytes=VMEM_LIMIT)


def _iota(shape, dim):
    return lax.broadcasted_iota(jnp.int32, shape, dim)


def _dot(a, b):
    return jnp.dot(a, b, preferred_element_type=F32)


def _dot_nt(a, b):
    return lax.dot_general(a, b, (((1,), (1,)), ((), ())), preferred_element_type=F32)


def _dot_tn(a, b):
    return lax.dot_general(a, b, (((0,), (0,)), ((), ())), preferred_element_type=F32)


def _split3(x):
    x1 = x.astype(BF16)
    r1 = x - x1.astype(F32)
    x2 = r1.astype(BF16)
    r2 = r1 - x2.astype(F32)
    return x1, x2, r2.astype(BF16)


def _dot_exact_lhs01(a01, x):
    x1, x2, x3 = _split3(x)
    return _dot(a01, x1) + _dot(a01, x2) + _dot(a01, x3)


def _pack_bf16_pairs(x):
    w = x.shape[1] // 2
    bits = lax.bitcast_convert_type(x.astype(BF16).astype(F32), jnp.uint32)
    return (bits[:, :w] >> 16) | (bits[:, w:] & jnp.uint32(0xFFFF0000))


def _unpack_bf16_pairs(words):
    return (lax.bitcast_convert_type(words << 16, F32),
            lax.bitcast_convert_type(words & jnp.uint32(0xFFFF0000), F32))


def _sigmoid(x):
    return 1.0 / (1.0 + jnp.exp(-x))


def _log_sigmoid(x):
    return jnp.minimum(x, 0.0) - jnp.log(1.0 + jnp.exp(-jnp.abs(x)))


def _mod_spec(per_batch, rows, width, col):
    if per_batch:
        return pl.BlockSpec((rows // T_DEC, 1, width), lambda i, *rest: (i, 0, col(*rest)))
    return pl.BlockSpec((1, width), lambda i, *rest: (0, col(*rest)))


def _mod_rows(ref):
    v = ref[...]
    if v.ndim == 2:
        return v
    nb, _, width = v.shape
    return jnp.broadcast_to(v, (nb, T_DEC, width)).reshape(nb * T_DEC, width)


def _ada_kernel(c_ref, w_ref, b_ref, o_ref):
    c = c_ref[...]
    s = (c * _sigmoid(c)).astype(BF16)
    o_ref[...] = _dot(s, w_ref[...].astype(BF16)) + b_ref[...]


def _ada(c_all, w_ada, b_ada):
    n = w_ada.shape[1]
    return pl.pallas_call(
        _ada_kernel,
        grid=(n // TN_ADA,),
        in_specs=[pl.BlockSpec((C_ROWS, D), lambda j: (0, 0)),
                  pl.BlockSpec((D, TN_ADA), lambda j: (0, j)),
                  pl.BlockSpec((1, TN_ADA), lambda j: (0, j))],
        out_specs=pl.BlockSpec((C_ROWS, TN_ADA), lambda j: (0, j)),
        out_shape=jax.ShapeDtypeStruct((C_ROWS, n), F32),
        compiler_params=_params(("arbitrary",)),
        name="ada",
    )(c_all, w_ada, b_ada)


def _inproj_kernel(x_ref, sh_ref, sc_ref, g_ref, w_ref, wg_ref, z_ref, zg_ref, h_scr):
    @pl.when(pl.program_id(1) == 0)
    def _():
        x = x_ref[...]
        y = x * lax.rsqrt(jnp.mean(x * x, axis=-1, keepdims=True) + EPS) * g_ref[...]
        hb = (y * (1.0 + _mod_rows(sc_ref)) + _mod_rows(sh_ref)).astype(BF16)
        h_scr[...] = hb
        zg_ref[...] = _dot(hb, wg_ref[...])

    z_ref[...] = _dot(h_scr[...], w_ref[...])


def _inproj(x, mod, per_row, g_mix, w_main, w_gate):
    n = x.shape[0]
    TM = TM_PROJ
    return pl.pallas_call(
        _inproj_kernel,
        grid=(n // TM, Z_WIDTH // TN_IN),
        in_specs=[pl.BlockSpec((TM, D), lambda i, j: (i, 0)),
                  _mod_spec(per_row, TM, D, lambda j: 0),
                  _mod_spec(per_row, TM, D, lambda j: 1),
                  pl.BlockSpec((1, D), lambda i, j: (0, 0)),
                  pl.BlockSpec((D, TN_IN), lambda i, j: (0, j)),
                  pl.BlockSpec((D, LANES), lambda i, j: (0, 0))],
        out_specs=[pl.BlockSpec((TM, TN_IN), lambda i, j: (i, j)),
                   pl.BlockSpec((TM, LANES), lambda i, j: (i, 0))],
        out_shape=[jax.ShapeDtypeStruct((n, Z_WIDTH), F32),
                   jax.ShapeDtypeStruct((n, LANES), F32)],
        scratch_shapes=[pltpu.VMEM((TM, D), BF16)],
        compiler_params=_params(("arbitrary", "arbitrary")),
        name="inproj",
    )(x, mod, mod, g_mix, w_main, w_gate)


def _t5_bucket_np(dist):
    n = np.maximum(dist, 0)
    nf = np.maximum(n, 1).astype(np.float32)
    large = MAX_EXACT + (np.log(nf / MAX_EXACT) / math.log(MAX_DISTANCE / MAX_EXACT)
                         * (NUM_BUCKETS - MAX_EXACT)).astype(np.int32)
    large = np.minimum(large, NUM_BUCKETS - 1)
    return np.where(n < MAX_EXACT, n, large).astype(np.int32)


def _bucket_table(n_q, n_keys_valid, n_keys_padded):
    t = np.arange(n_q)[:, None]
    j = np.arange(n_keys_padded)[None, :]
    dist = t + WINDOW - j
    valid = (dist >= 0) & (dist < WINDOW) & (j < n_keys_valid)
    return np.where(valid, _t5_bucket_np(dist), -1).astype(np.int32)


def _fill_bias(bucket_ref, rb_ref, bias_scr, rows, per_head_table=False):
    for h in range(ATT_HEADS):
        bk = bucket_ref[h * rows:(h + 1) * rows, :] if per_head_table else bucket_ref[...]
        acc = jnp.full(bk.shape, NEG_INF, F32)
        for b in range(NUM_BUCKETS):
            acc = jnp.where(bk == b, rb_ref[b * ATT_HEADS + h], acc)
        bias_scr[h * rows:(h + 1) * rows, :] = acc


def _with_ones(v2):
    return jnp.concatenate([v2, jnp.ones_like(v2)], axis=1)


def _sink_softmax_av(lg, sink, v2_ones):
    m = jnp.maximum(jnp.max(lg, axis=-1, keepdims=True), sink)
    p = jnp.exp(lg - m).astype(BF16)
    pv = _dot(p, v2_ones)
    return pv[:, :HEAD_DIM] / (pv[:, HEAD_DIM:] + jnp.exp(sink - m))


def _swa_prompt_kernel(rb_ref, sink_ref, bucket_ref, q_ref, kp_ref, kc_ref, vp_ref, vc_ref,
                       o_ref, bias_scr):
    i = pl.program_id(0)

    @pl.when(i == 0)
    def _():
        _fill_bias(bucket_ref, rb_ref, bias_scr, ATT_BLOCK)

    first_prev = jnp.logical_and(i == 0, _iota((ATT_BLOCK, 2 * ATT_BLOCK), 1) < ATT_BLOCK)
    for g in range(KV_HEADS):
        ks = slice(g * HEAD_DIM, (g + 1) * HEAD_DIM)
        k2 = jnp.concatenate([kp_ref[:, ks], kc_ref[:, ks]], axis=0).astype(BF16)
        v2 = _with_ones(jnp.concatenate([vp_ref[:, ks], vc_ref[:, ks]], axis=0).astype(BF16))
        for r in range(GROUP):
            h = g * GROUP + r
            hs = slice(h * HEAD_DIM, (h + 1) * HEAD_DIM)
            lg = _dot_nt(q_ref[:, hs].astype(BF16), k2) * ATT_SCALE + bias_scr[h * ATT_BLOCK:(h + 1) * ATT_BLOCK, :]
            lg = jnp.where(first_prev, NEG_INF, lg)
            o_ref[:, hs] = _sink_softmax_av(lg, sink_ref[h], v2).astype(BF16)


def _swa_prompt(z, rb_flat, sinks):
    nb = N_PROMPT // ATT_BLOCK
    bucket = jnp.asarray(_bucket_table(ATT_BLOCK, 2 * ATT_BLOCK, 2 * ATT_BLOCK))
    kcol = ATT_WIDTH // KV_WIDTH
    prev = lambda i: jnp.maximum(i - 1, 0)
    smem = pl.BlockSpec(memory_space=pltpu.SMEM)
    return pl.pallas_call(
        _swa_prompt_kernel,
        grid=(nb,),
        in_specs=[smem, smem,
                  pl.BlockSpec((ATT_BLOCK, 2 * ATT_BLOCK), lambda i: (0, 0)),
                  pl.BlockSpec((ATT_BLOCK, ATT_WIDTH), lambda i: (i, 0)),
                  pl.BlockSpec((ATT_BLOCK, KV_WIDTH), lambda i: (prev(i), kcol)),
                  pl.BlockSpec((ATT_BLOCK, KV_WIDTH), lambda i: (i, kcol)),
                  pl.BlockSpec((ATT_BLOCK, KV_WIDTH), lambda i: (prev(i), kcol + 1)),
                  pl.BlockSpec((ATT_BLOCK, KV_WIDTH), lambda i: (i, kcol + 1))],
        out_specs=pl.BlockSpec((ATT_BLOCK, ATT_WIDTH), lambda i: (i, 0)),
        out_shape=jax.ShapeDtypeStruct((N_PROMPT, ATT_WIDTH), BF16),
        scratch_shapes=[pltpu.VMEM((ATT_HEADS * ATT_BLOCK, 2 * ATT_BLOCK), F32)],
        compiler_params=_params(("arbitrary",)),
        name="swa_prompt",
    )(rb_flat, sinks, bucket, z, z, z, z, z)


CACHE_ROWS = WINDOW * KV_HEADS
NEW_ROWS = T_DEC * KV_HEADS
S_KEYS = 5 * LANES


def _sample_bucket_table():
    t = np.arange(T_DEC)[:, None]
    col = np.arange(S_KEYS)[None, :]
    in_cache = col < CACHE_ROWS
    in_new = (col >= CACHE_ROWS) & (col < CACHE_ROWS + NEW_ROWS)
    key_head = np.where(in_cache, col % KV_HEADS, (col - CACHE_ROWS) // T_DEC)
    key_pos = np.where(in_cache, col // KV_HEADS, WINDOW + (col - CACHE_ROWS) % T_DEC)
    dist = t + WINDOW - key_pos
    valid = (dist >= 0) & (dist < WINDOW) & (in_cache | in_new)
    per_query = np.where(valid, _t5_bucket_np(dist), -1)
    heads = np.arange(ATT_HEADS)[:, None, None] // GROUP
    table = np.where(heads == key_head[None], per_query[None], -1)
    return table.reshape(ATT_HEADS * T_DEC, S_KEYS).astype(np.int32)


def _swa_sample_kernel(rb_ref, sink_ref, bucket_ref, q_ref, kn_ref, vn_ref, ck_ref, cv_ref,
                       o_ref, nk_ref, nv_ref, bias_scr):
    @pl.when(pl.program_id(0) == 0)
    def _():
        _fill_bias(bucket_ref, rb_ref, bias_scr, T_DEC, per_head_table=True)

    nk_ref[:, 0:CACHE_ROWS - NEW_ROWS, :] = ck_ref[:, NEW_ROWS:CACHE_ROWS, :]
    nv_ref[:, 0:CACHE_ROWS - NEW_ROWS, :] = cv_ref[:, NEW_ROWS:CACHE_ROWS, :]

    pad = jnp.zeros((S_KEYS - CACHE_ROWS - NEW_ROWS, HEAD_DIM), F32)
    sink_col = jnp.concatenate([jnp.full((T_DEC, 1), sink_ref[h], F32) for h in range(ATT_HEADS)], axis=0)
    bias = bias_scr[...]
    for b in range(SAMPLE_BT):
        ts = slice(b * T_DEC, (b + 1) * T_DEC)
        head_cols = lambda ref, n: [ref[ts, h * HEAD_DIM:(h + 1) * HEAD_DIM] for h in range(n)]
        k_new, v_new = head_cols(kn_ref, KV_HEADS), head_cols(vn_ref, KV_HEADS)
        for g in range(KV_HEADS):
            new_rows = pl.ds(CACHE_ROWS - NEW_ROWS + g, T_DEC, stride=KV_HEADS)
            nk_ref[b, new_rows, :] = k_new[g]
            nv_ref[b, new_rows, :] = v_new[g]
        qa = jnp.concatenate(head_cols(q_ref, ATT_HEADS), axis=0).astype(BF16)
        k2 = jnp.concatenate([ck_ref[b]] + k_new + [pad], axis=0).astype(BF16)
        v2 = _with_ones(jnp.concatenate([cv_ref[b]] + v_new + [pad], axis=0).astype(BF16))
        o = _sink_softmax_av(_dot_nt(qa, k2) * ATT_SCALE + bias, sink_col, v2)
        for h in range(ATT_HEADS):
            o_ref[ts, h * HEAD_DIM:(h + 1) * HEAD_DIM] = o[h * T_DEC:(h + 1) * T_DEC].astype(BF16)


def _swa_sample(z, cache_k, cache_v, rb_flat, sinks):
    rows = SAMPLE_BT * T_DEC
    bucket = jnp.asarray(_sample_bucket_table())
    kcol = ATT_WIDTH // KV_WIDTH
    smem = pl.BlockSpec(memory_space=pltpu.SMEM)
    cache_spec = pl.BlockSpec((SAMPLE_BT, CACHE_ROWS, HEAD_DIM), lambda i: (i, 0, 0))
    cache_shape = jax.ShapeDtypeStruct((N_BATCH, CACHE_ROWS, HEAD_DIM), F32)
    return pl.pallas_call(
        _swa_sample_kernel,
        grid=(N_BATCH // SAMPLE_BT,),
        in_specs=[smem, smem,
                  pl.BlockSpec((ATT_HEADS * T_DEC, S_KEYS), lambda i: (0, 0)),
                  pl.BlockSpec((rows, ATT_WIDTH), lambda i: (i, 0)),
                  pl.BlockSpec((rows, KV_WIDTH), lambda i: (i, kcol)),
                  pl.BlockSpec((rows, KV_WIDTH), lambda i: (i, kcol + 1)),
                  cache_spec, cache_spec],
        out_specs=[pl.BlockSpec((rows, ATT_WIDTH), lambda i: (i, 0)), cache_spec, cache_spec],
        out_shape=[jax.ShapeDtypeStruct((N_SAMPLE, ATT_WIDTH), BF16), cache_shape, cache_shape],
        scratch_shapes=[pltpu.VMEM((ATT_HEADS * T_DEC, S_KEYS), F32)],
        compiler_params=_params(("arbitrary",)),
        name="swa_sample",
    )(rb_flat, sinks, bucket, z, z, z, cache_k, cache_v)


LANE_IG, LANE_LF, LANE_B = 0, M_HEADS, 2 * M_HEADS


def _gate_table(zg, gate_bias, seg_len):
    L = zg.shape[0]
    g = zg + gate_bias
    lane = _iota((L, LANES), 1)
    lf = _log_sigmoid(g)
    lf_only = jnp.where(jnp.logical_and(lane >= LANE_LF, lane < LANE_B), lf, 0.0)
    row = _iota((L, L), 0)
    col = _iota((L, L), 1)
    same_seg = (row // seg_len) == (col // seg_len)
    tril = jnp.where(jnp.logical_and(col <= row, same_seg), 1.0, 0.0).astype(BF16)
    cum = pltpu.roll(_dot_exact_lhs01(tril, lf_only), M_HEADS, axis=1)
    table = jnp.where(lane < LANE_LF, g, jnp.where(lane < LANE_B, lf, jnp.where(lane < LANE_B + M_HEADS, cum, 0.0)))
    return table, jnp.logical_and(col <= row, same_seg)


def _mlstm_intra(table, table_t, mask, m0_col, h, q, k, v):
    b_c = table[:, LANE_B + h:LANE_B + h + 1]
    b_r = table_t[LANE_B + h:LANE_B + h + 1, :]
    ig_r = table_t[LANE_IG + h:LANE_IG + h + 1, :]
    log_d = jnp.where(mask, b_c - b_r + ig_r, NEG_INF)
    log_inter = b_c + m0_col
    m_t = jnp.maximum(log_inter, jnp.max(log_d, axis=1, keepdims=True))
    d = jnp.exp(log_d - m_t)
    w_inter = jnp.exp(log_inter - m_t)
    s = _dot_nt(q.astype(BF16), k.astype(BF16)) * d
    num_intra = _dot(s.astype(BF16), v.astype(BF16))
    den_intra = jnp.sum(s, axis=1, keepdims=True)
    return b_c, m_t, w_inter, num_intra, den_intra


def _mlstm_prompt_kernel(gb_ref, zg_ref, q_ref, k_ref, va_ref, vb_ref, oa_ref, ob_ref,
                         y_ref, c_out, n_out, m_out, c_scr, n_scr, m_scr):
    step = pl.program_id(0)
    L = ML_CHUNK

    @pl.when(step == 0)
    def _():
        c_scr[...] = jnp.zeros_like(c_scr)
        n_scr[...] = jnp.zeros_like(n_scr)
        m_scr[...] = jnp.zeros_like(m_scr)

    table, mask = _gate_table(zg_ref[...], gb_ref[...], L)
    table_t = table.T
    half = M_HEADS // 2
    for h in range(M_HEADS):
        v_ref, o_ref = (va_ref, oa_ref) if h < half else (vb_ref, ob_ref)
        vs = slice((h % half) * DV, (h % half + 1) * DV)
        q = q_ref[:, h * DK:(h + 1) * DK] * Q_SCALE
        k = k_ref[:, h * DK:(h + 1) * DK]
        v = v_ref[:, vs]
        m0 = m_scr[h:h + 1, 0:1]
        b_c, m_t, w_inter, num_intra, den_intra = _mlstm_intra(table, table_t, mask, m0, h, q, k, v)
        c_old = c_scr[h]
        n_old = n_scr[h:h + 1, :]
        num = num_intra + w_inter * _dot_nt(q.astype(BF16), c_old.astype(BF16))
        den = den_intra + w_inter * jnp.sum(q * n_old, axis=1, keepdims=True)
        hh = num / jnp.maximum(jnp.abs(den), jnp.exp(-m_t))
        y_ref[:, h * DV:(h + 1) * DV] = (_sigmoid(o_ref[:, vs]) * hh).astype(BF16)

        ig_c = table[:, LANE_IG + h:LANE_IG + h + 1]
        m_new = m_t[L - 1:L, :]
        b_last = b_c[L - 1:L, :]
        w_s = jnp.exp(b_last - b_c + ig_c - m_new)
        decay = jnp.exp(b_last + m0 - m_new)
        c_scr[h] = decay * c_old + _dot_tn((w_s * v).astype(BF16), k.astype(BF16))
        n_scr[h:h + 1, :] = decay * n_old + jnp.sum(w_s * k, axis=0, keepdims=True)
        m_scr[h:h + 1, :] = jnp.broadcast_to(m_new, (1, LANES))

    @pl.when(step == pl.num_programs(0) - 1)
    def _():
        c_out[...] = c_scr[...]
        n_out[...] = n_scr[...]
        m_out[...] = m_scr[...]


def _mlstm_prompt(z, zg, gate_bias):
    L = ML_CHUNK
    blk = M_HEADS * DK
    col = lambda c: pl.BlockSpec((L, blk), lambda i: (i, c))
    const = lambda shape: pl.BlockSpec(shape, lambda i: tuple(0 for _ in shape))
    return pl.pallas_call(
        _mlstm_prompt_kernel,
        grid=(N_PROMPT // L,),
        in_specs=[const((1, LANES)),
                  pl.BlockSpec((L, LANES), lambda i: (i, 0)),
                  col(3), col(4), col(5), col(6), col(7), col(8)],
        out_specs=[pl.BlockSpec((L, M_WIDTH), lambda i: (i, 0)),
                   const((M_HEADS, DV, DK)), const((M_HEADS, DK)), const((M_HEADS, LANES))],
        out_shape=[jax.ShapeDtypeStruct((N_PROMPT, M_WIDTH), BF16),
                   jax.ShapeDtypeStruct((M_HEADS, DV, DK), F32),
                   jax.ShapeDtypeStruct((M_HEADS, DK), F32),
                   jax.ShapeDtypeStruct((M_HEADS, LANES), F32)],
        scratch_shapes=[pltpu.VMEM((M_HEADS, DV, DK), F32),
                        pltpu.VMEM((M_HEADS, DK), F32),
                        pltpu.VMEM((M_HEADS, LANES), F32)],
        compiler_params=_params(("arbitrary",)),
        name="mlstm_prompt",
    )(gate_bias, zg, z, z, z, z, z, z)


S_ROWS = SAMPLE_BT * T_DEC


def _mlstm_sample_kernel(gb_ref, zg_ref, m0_ref, n0t_ref, q_ref, k_ref, va_ref, vb_ref, oa_ref, ob_ref,
                         c0_ref, n0_ref, y_ref, c_out, n_out, m_out):
    L = S_ROWS
    table, mask = _gate_table(zg_ref[...], gb_ref[...], T_DEC)
    table_t = jnp.concatenate([table, jnp.zeros((LANES - L, LANES), F32)], axis=0).T[:, 0:L]
    m0_all = m0_ref[...]
    lane = _iota((L, LANES), 1)
    row_b = _iota((SAMPLE_BT, L), 1) // T_DEC
    seg_sum = jnp.where(row_b == _iota((SAMPLE_BT, L), 0), 1.0, 0.0)
    m_tok = jnp.zeros((L, LANES), F32)
    half = M_HEADS // 2
    for h in range(M_HEADS):
        v_ref, o_ref = (va_ref, oa_ref) if h < half else (vb_ref, ob_ref)
        vs = slice((h % half) * DV, (h % half + 1) * DV)
        q = q_ref[:, h * DK:(h + 1) * DK] * Q_SCALE
        k = k_ref[:, h * DK:(h + 1) * DK]
        v = v_ref[:, vs]
        m0 = m0_all[:, h:h + 1]
        b_c, m_t, w_inter, num_intra, den_intra = _mlstm_intra(table, table_t, mask, m0, h, q, k, v)
        qb = q.astype(BF16)
        num_inter = jnp.concatenate(
            [_dot_nt(qb[b * T_DEC:(b + 1) * T_DEC], c0_ref[b, h].astype(BF16)) for b in range(SAMPLE_BT)], axis=0)
        num = num_intra + w_inter * num_inter
        den = den_intra + w_inter * jnp.sum(q * n0t_ref[:, h * DK:(h + 1) * DK], axis=1, keepdims=True)
        hh = num / jnp.maximum(jnp.abs(den), jnp.exp(-m_t))
        y_ref[:, h * DV:(h + 1) * DV] = (_sigmoid(o_ref[:, vs]) * hh).astype(BF16)
        m_tok = jnp.where(lane == h, m_t, m_tok)

        def last_tok(x):
            x3 = x.reshape(SAMPLE_BT, T_DEC, 1)
            return jnp.broadcast_to(x3[:, T_DEC - 1:T_DEC, :], x3.shape).reshape(L, 1)

        ig_c = table[:, LANE_IG + h:LANE_IG + h + 1]
        m_new = last_tok(m_t)
        b_last = last_tok(b_c)
        w_s = jnp.exp(b_last - b_c + ig_c - m_new)
        decay = jnp.exp(b_last + m0 - m_new)
        wv = (w_s * v).astype(BF16)
        kb = k.astype(BF16)
        rowsel = _iota((L, 1), 0) // T_DEC
        for b in range(SAMPLE_BT):
            dec_b = decay[b * T_DEC + T_DEC - 1:(b + 1) * T_DEC, :]
            wv_b = jnp.where(rowsel == b, wv, jnp.zeros_like(wv))
            c_out[b, h] = dec_b * c0_ref[b, h] + _dot_tn(wv_b, kb)
        dec_rows = decay.reshape(SAMPLE_BT, T_DEC, 1)[:, T_DEC - 1, :]
        n_out[:, h * DK:(h + 1) * DK] = dec_rows * n0_ref[:, h * DK:(h + 1) * DK] + jnp.dot(
            seg_sum, w_s * k, preferred_element_type=F32, precision=lax.Precision.HIGHEST)
    m_out[...] = m_tok


def _mlstm_sample(z, zg, gate_bias, m0_tok, n0_tok, state_c, state_n):
    L = S_ROWS
    blk = M_HEADS * DK
    col = lambda c: pl.BlockSpec((L, blk), lambda i: (i, c))
    c_spec = pl.BlockSpec((SAMPLE_BT, M_HEADS, DV, DK), lambda i: (i, 0, 0, 0))
    n_spec = pl.BlockSpec((SAMPLE_BT, blk), lambda i: (i, 0))
    return pl.pallas_call(
        _mlstm_sample_kernel,
        grid=(N_BATCH // SAMPLE_BT,),
        in_specs=[pl.BlockSpec((1, LANES), lambda i: (0, 0)),
                  pl.BlockSpec((L, LANES), lambda i: (i, 0)),
                  pl.BlockSpec((L, LANES), lambda i: (i, 0)),
                  pl.BlockSpec((L, blk), lambda i: (i, 0)),
                  col(3), col(4), col(5), col(6), col(7), col(8),
                  c_spec, n_spec],
        out_specs=[pl.BlockSpec((L, M_WIDTH), lambda i: (i, 0)), c_spec, n_spec,
                   pl.BlockSpec((L, LANES), lambda i: (i, 0))],
        out_shape=[jax.ShapeDtypeStruct((N_SAMPLE, M_WIDTH), BF16),
                   jax.ShapeDtypeStruct((N_BATCH, M_HEADS, DV, DK), F32),
                   jax.ShapeDtypeStruct((N_BATCH, blk), F32),
                   jax.ShapeDtypeStruct((N_SAMPLE, LANES), F32)],
        compiler_params=_params(("arbitrary",)),
        name="mlstm_sample",
    )(gate_bias, zg, m0_tok, n0_tok, z, z, z, z, z, z, state_c, state_n)


def _outproj_kernel(ya_ref, ym_ref, wa_ref, wm_ref, x_ref, gt_ref, o_ref):
    mix = _dot(ya_ref[...], wa_ref[...]) + _dot(ym_ref[...], wm_ref[...])
    o_ref[...] = x_ref[...] + _mod_rows(gt_ref) * mix


def _outproj(ya, ym, w_out_bf, x, mod, per_row):
    n = x.shape[0]
    TM = TM_PROJ
    gate1_col = 2 * (D // TN_OUT)
    return pl.pallas_call(
        _outproj_kernel,
        grid=(n // TM, D // TN_OUT),
        in_specs=[pl.BlockSpec((TM, ATT_WIDTH), lambda i, j: (i, 0)),
                  pl.BlockSpec((TM, M_WIDTH), lambda i, j: (i, 0)),
                  pl.BlockSpec((ATT_WIDTH, TN_OUT), lambda i, j: (0, j)),
                  pl.BlockSpec((M_WIDTH, TN_OUT), lambda i, j: (1, j)),
                  pl.BlockSpec((TM, TN_OUT), lambda i, j: (i, j)),
                  _mod_spec(per_row, TM, TN_OUT, lambda j: gate1_col + j)],
        out_specs=pl.BlockSpec((TM, TN_OUT), lambda i, j: (i, j)),
        out_shape=jax.ShapeDtypeStruct((n, D), F32),
        compiler_params=_params(("arbitrary", "arbitrary")),
        name="outproj",
    )(ya, ym, w_out_bf, w_out_bf, x, mod)


def _router_kernel(x_ref, sh_ref, sc_ref, g_ref, wr_ref, br_ref, h_ref, route_ref):
    x = x_ref[...]
    y = x * lax.rsqrt(jnp.mean(x * x, axis=-1, keepdims=True) + EPS) * g_ref[...]
    h2 = y * (1.0 + _mod_rows(sc_ref)) + _mod_rows(sh_ref)
    h_hi = h2.astype(BF16)
    h_ref[...] = _pack_bf16_pairs(h2)
    h_lo = (h2 - h_hi.astype(F32)).astype(BF16)
    logits = _dot(h_hi, wr_ref[0]) + (_dot(h_hi, wr_ref[1]) + _dot(h_lo, wr_ref[0])) + br_ref[...]
    lane = _iota(logits.shape, 1)

    def first_max(vals):
        vmax = jnp.max(vals, axis=1, keepdims=True)
        idx = jnp.min(jnp.where(vals == vmax, lane, LANES), axis=1, keepdims=True)
        return vmax, idx

    gl = jnp.where(lane < N_GROUPS, logits, NEG_INF)
    gmax, grp = first_max(gl)
    p_grp = 1.0 / jnp.sum(jnp.exp(gl - gmax), axis=1, keepdims=True)
    e_lane = lane - N_GROUPS
    in_grp = jnp.logical_and(e_lane >= 0, jnp.logical_and(e_lane < N_EXPERTS, e_lane // EXP_PER_GROUP == grp))
    el = jnp.where(in_grp, logits, NEG_INF)
    v1, i1 = first_max(el)
    v2, i2 = first_max(jnp.where(lane == i1, NEG_INF, el))
    e2w = jnp.exp(v2 - v1)
    w1 = 1.0 / (1.0 + e2w)
    w2 = e2w / (1.0 + e2w)
    route = jnp.where(lane == 0, (i1 - N_GROUPS).astype(F32),
                      jnp.where(lane == 1, (i2 - N_GROUPS).astype(F32),
                                jnp.where(lane == 2, p_grp * w1, jnp.where(lane == 3, p_grp * w2, 0.0))))
    route_ref[...] = route


PROMPT_TOK_BLOCKS = N_PROMPT // TM_TOK


def _router_merged_kernel(xp_ref, xs_ref, shp_ref, scp_ref, shs_ref, scs_ref, g_ref, wr_ref, br_ref,
                          h_ref, route_ref):
    i = pl.program_id(0)

    @pl.when(i < PROMPT_TOK_BLOCKS)
    def _():
        _router_kernel(xp_ref, shp_ref, scp_ref, g_ref, wr_ref, br_ref, h_ref, route_ref)

    @pl.when(i >= PROMPT_TOK_BLOCKS)
    def _():
        _router_kernel(xs_ref, shs_ref, scs_ref, g_ref, wr_ref, br_ref, h_ref, route_ref)


def _router(x1_p, x1_s, mod_p, mod_s, g_ffn, w_route, b_route):
    pi = lambda i: jnp.minimum(i, PROMPT_TOK_BLOCKS - 1)
    si = lambda i: jnp.maximum(i - PROMPT_TOK_BLOCKS, 0)
    return pl.pallas_call(
        _router_merged_kernel,
        grid=(N_TOK // TM_TOK,),
        in_specs=[pl.BlockSpec((TM_TOK, D), lambda i: (pi(i), 0)),
                  pl.BlockSpec((TM_TOK, D), lambda i: (si(i), 0)),
                  pl.BlockSpec((1, D), lambda i: (0, 3)),
                  pl.BlockSpec((1, D), lambda i: (0, 4)),
                  pl.BlockSpec((TM_TOK // T_DEC, 1, D), lambda i: (si(i), 0, 3)),
                  pl.BlockSpec((TM_TOK // T_DEC, 1, D), lambda i: (si(i), 0, 4)),
                  pl.BlockSpec((1, D), lambda i: (0, 0)),
                  pl.BlockSpec((2, D, LANES), lambda i: (0, 0, 0)),
                  pl.BlockSpec((1, LANES), lambda i: (0, 0))],
        out_specs=[pl.BlockSpec((TM_TOK, D_PACK), lambda i: (i, 0)),
                   pl.BlockSpec((TM_TOK, LANES), lambda i: (i, 0))],
        out_shape=[jax.ShapeDtypeStruct((N_TOK, D_PACK), jnp.uint32),
                   jax.ShapeDtypeStruct((N_TOK, LANES), F32)],
        compiler_params=_params(("arbitrary",)),
        name="router",
    )(x1_p, x1_s, mod_p, mod_p, mod_s, mod_s, g_ffn, w_route, b_route)


def _rank_kernel(route_ref, dest_ref, cnt_ref, carry, rank_scr):
    pas = pl.program_id(0)
    step = pl.program_id(1)

    @pl.when(jnp.logical_and(pas == 0, step == 0))
    def _():
        carry[...] = jnp.zeros_like(carry)

    route = route_ref[...]
    n = route.shape[0]
    lane = _iota((n, LANES), 1).astype(F32)
    o1 = jnp.where(lane == route[:, 0:1], 1.0, 0.0)
    o2 = jnp.where(lane == route[:, 1:2], 1.0, 0.0)
    lane_i = _iota((n, LANES), 1)
    rows = pl.ds(pl.multiple_of(step * TM_RANK, TM_RANK), TM_RANK)

    @pl.when(pas == 0)
    def _():
        both = o1 + o2
        strict = jnp.where(_iota((n, n), 1) < _iota((n, n), 0), 1.0, 0.0).astype(BF16)
        prior = _dot(strict, both.astype(BF16)) + carry[0:1, :]
        r1 = jnp.sum(o1 * prior, axis=1, keepdims=True)
        r2 = jnp.sum(o2 * prior, axis=1, keepdims=True)
        rank_scr[rows, :] = jnp.where(lane_i == 0, r1, jnp.where(lane_i == 1, r2, 0.0))
        carry[...] = carry[...] + jnp.sum(both, axis=0, keepdims=True)

    @pl.when(pas == 1)
    def _():
        counts = carry[...]
        tiles_per = jnp.floor((counts + (TM_MOE - 1)) * (1.0 / TM_MOE))
        before = jnp.where(_iota((LANES, LANES), 0) < _iota((LANES, LANES), 1), 1.0, 0.0).astype(BF16)
        pad_start = _dot(tiles_per.astype(BF16), before)[0:1, :] * TM_MOE
        rank = rank_scr[rows, :]
        d1 = jnp.sum(o1 * pad_start, axis=1, keepdims=True) + rank[:, 0:1]
        d2 = jnp.sum(o2 * pad_start, axis=1, keepdims=True) + rank[:, 1:2]
        dest_ref[...] = jnp.where(lane_i == 0, d1, jnp.where(lane_i == 1, d2, 0.0)).astype(jnp.int32)
        cnt_ref[...] = counts


def _rank(route):
    n = route.shape[0]
    return pl.pallas_call(
        _rank_kernel,
        grid=(2, n // TM_RANK),
        in_specs=[pl.BlockSpec((TM_RANK, LANES), lambda p, i: (i, 0))],
        out_specs=[pl.BlockSpec((TM_RANK, LANES), lambda p, i: (i * p, 0)),
                   pl.BlockSpec((SUBLANES, LANES), lambda p, i: (0, 0))],
        out_shape=[jax.ShapeDtypeStruct((n, LANES), jnp.int32), jax.ShapeDtypeStruct((SUBLANES, LANES), F32)],
        scratch_shapes=[pltpu.VMEM((SUBLANES, LANES), F32), pltpu.VMEM((n, LANES), F32)],
        compiler_params=_params(("arbitrary", "arbitrary")),
        name="rank",
    )(route)


def _row_copy(src, dst, sem):
    return pltpu.make_async_copy(src, dst, sem)


ROW_UNROLL = 8
D_PACK = D // 2

ZERO_FIRST, ZERO_ANYTIME = 1, 2


def _dispatch_kernel(d1_ref, d2_ref, zc_ref, h_ref, xs_out, zbuf, sem):
    step = pl.program_id(0)
    base = step * TM_TOK

    def zero_tile(t, s):
        rows = pl.ds(pl.multiple_of(t * TM_MOE, TM_MOE), TM_MOE)
        return pltpu.make_async_copy(zbuf, xs_out.at[rows], sem.at[s])

    def for_tiles(cls, s, act):
        def body(t, carry):
            @pl.when(zc_ref[t] == cls)
            def _():
                act(zero_tile(t, s))
            return carry
        lax.fori_loop(0, MAX_TILES, body, 0)

    @pl.when(step == 0)
    def _():
        zbuf[...] = jnp.zeros_like(zbuf)
        for_tiles(ZERO_FIRST, 1, lambda c: c.start())
        for_tiles(ZERO_ANYTIME, 2, lambda c: c.start())
        for_tiles(ZERO_FIRST, 1, lambda c: c.wait())

    @pl.when(step == pl.num_programs(0) - 1)
    def _():
        for_tiles(ZERO_ANYTIME, 2, lambda c: c.wait())

    def issue(r, carry):
        _row_copy(h_ref.at[pl.ds(r, 1)], xs_out.at[pl.ds(d1_ref[base + r], 1)], sem.at[0]).start()
        _row_copy(h_ref.at[pl.ds(r, 1)], xs_out.at[pl.ds(d2_ref[base + r], 1)], sem.at[0]).start()
        return carry

    lax.fori_loop(0, TM_TOK, issue, 0, unroll=ROW_UNROLL)
    for _ in range(TOP_K):
        _row_copy(h_ref, xs_out.at[pl.ds(0, TM_TOK)], sem.at[0]).wait()


def _dispatch(dests, zero_class, h2):
    grid_spec = pltpu.PrefetchScalarGridSpec(
        num_scalar_prefetch=3,
        grid=(N_TOK // TM_TOK,),
        in_specs=[pl.BlockSpec((TM_TOK, D_PACK), lambda i, *_: (i, 0))],
        out_specs=pl.BlockSpec(memory_space=pl.ANY),
        scratch_shapes=[pltpu.VMEM((TM_MOE, D_PACK), jnp.uint32), pltpu.SemaphoreType.DMA((3,))],
    )
    return pl.pallas_call(
        _dispatch_kernel,
        grid_spec=grid_spec,
        out_shape=jax.ShapeDtypeStruct((A_PAD, D_PACK), jnp.uint32),
        compiler_params=_params(("arbitrary",)),
        name="dispatch",
    )(*dests, zero_class, h2)


N_UP_CHUNKS = D_FF // TN_FF


def _expert_up_kernel(te_ref, tv_ref, first_ref, nxt_ref, run_ref, meta_ref, x_ref, wg_hbm, wu_hbm,
                      o_ref, wbuf, sem):
    n = pl.program_id(0)
    t = pl.program_id(1)
    slot = lax.rem(n * meta_ref[1] + run_ref[t], 2)

    def fetch(e, chunk, s):
        cols = pl.ds(pl.multiple_of(chunk * TN_FF, TN_FF), TN_FF)
        return (pltpu.make_async_copy(wg_hbm.at[e, :, cols], wbuf.at[s, 0], sem.at[s, 0]),
                pltpu.make_async_copy(wu_hbm.at[e, :, cols], wbuf.at[s, 1], sem.at[s, 1]))

    @pl.when(first_ref[t] == 1)
    def _():
        @pl.when(jnp.logical_and(n == 0, t == 0))
        def _():
            for c in fetch(te_ref[0], 0, 0):
                c.start()

        for c in fetch(te_ref[t], n, slot):
            c.wait()

        @pl.when(nxt_ref[t] >= 0)
        def _():
            for c in fetch(nxt_ref[t], n, 1 - slot):
                c.start()

        @pl.when(jnp.logical_and(nxt_ref[t] < 0, n + 1 < N_UP_CHUNKS))
        def _():
            for c in fetch(te_ref[0], n + 1, 1 - slot):
                c.start()

    @pl.when(tv_ref[t] == 1)
    def _():
        x = jnp.concatenate([half.astype(BF16) for half in _unpack_bf16_pairs(x_ref[...])], axis=1)
        g = _dot(x, wbuf[slot, 0].astype(BF16))
        u = _dot(x, wbuf[slot, 1].astype(BF16))
        o_ref[...] = (g * _sigmoid(g) * u).astype(BF16)

    @pl.when(tv_ref[t] == 0)
    def _():
        o_ref[...] = jnp.zeros_like(o_ref)


def _expert_up(tiles, xs, w_gate, w_up):
    last = lambda t, meta: jnp.minimum(t, meta[0] - 1)
    grid_spec = pltpu.PrefetchScalarGridSpec(
        num_scalar_prefetch=6,
        grid=(N_UP_CHUNKS, MAX_TILES),
        in_specs=[pl.BlockSpec((TM_MOE, D_PACK), lambda n, t, *s: (last(t, s[5]), 0)),
                  pl.BlockSpec(memory_space=pl.ANY),
                  pl.BlockSpec(memory_space=pl.ANY)],
        out_specs=pl.BlockSpec((TM_MOE, TN_FF), lambda n, t, *s: (t, n)),
        scratch_shapes=[pltpu.VMEM((2, 2, D, TN_FF), F32), pltpu.SemaphoreType.DMA((2, 2))],
    )
    return pl.pallas_call(
        _expert_up_kernel,
        grid_spec=grid_spec,
        out_shape=jax.ShapeDtypeStruct((A_PAD, D_FF), BF16),
        compiler_params=_params(("arbitrary", "arbitrary")),
        name="expert_up",
    )(*tiles, xs, w_gate, w_up)


def _expert_down_kernel(te_ref, tv_ref, first_ref, nxt_ref, run_ref, meta_ref, h_ref, wd_hbm,
                        o_ref, wbuf, sem):
    t = pl.program_id(0)
    slot = lax.rem(run_ref[t], 2)

    def fetch(e, s):
        return pltpu.make_async_copy(wd_hbm.at[e], wbuf.at[s], sem.at[s])

    @pl.when(first_ref[t] == 1)
    def _():
        @pl.when(t == 0)
        def _():
            fetch(te_ref[0], 0).start()

        fetch(te_ref[t], slot).wait()

        @pl.when(nxt_ref[t] >= 0)
        def _():
            fetch(nxt_ref[t], 1 - slot).start()

    @pl.when(tv_ref[t] == 1)
    def _():
        o_ref[...] = _pack_bf16_pairs(_dot(h_ref[...], wbuf[slot].astype(BF16)))

    @pl.when(tv_ref[t] == 0)
    def _():
        o_ref[...] = jnp.zeros_like(o_ref)


def _expert_down(tiles, h1, w_down):
    last = lambda t, meta: jnp.minimum(t, meta[0] - 1)
    grid_spec = pltpu.PrefetchScalarGridSpec(
        num_scalar_prefetch=6,
        grid=(MAX_TILES,),
        in_specs=[pl.BlockSpec((TM_MOE, D_FF), lambda t, *s: (last(t, s[5]), 0)),
                  pl.BlockSpec(memory_space=pl.ANY)],
        out_specs=pl.BlockSpec((TM_MOE, D_PACK), lambda t, *s: (t, 0)),
        scratch_shapes=[pltpu.VMEM((2, D_FF, D), F32), pltpu.SemaphoreType.DMA((2,))],
    )
    return pl.pallas_call(
        _expert_down_kernel,
        grid_spec=grid_spec,
        out_shape=jax.ShapeDtypeStruct((A_PAD, D_PACK), jnp.uint32),
        compiler_params=_params(("arbitrary",)),
        name="expert_down",
    )(*tiles, h1, w_down)


def _combine_kernel(d1_ref, d2_ref, x_ref, route_ref, gt_ref, gf_ref, o_hbm,
                    y_ref, buf, sem, *, tok_offset):
    step = pl.program_id(0)
    slot = lax.rem(step, 2)

    def gather(s, into):
        base = tok_offset + s * TM_TOK

        def issue(r, carry):
            _row_copy(o_hbm.at[pl.ds(d1_ref[base + r], 1)], buf.at[into, 0, pl.ds(r, 1)], sem.at[into]).start()
            _row_copy(o_hbm.at[pl.ds(d2_ref[base + r], 1)], buf.at[into, 1, pl.ds(r, 1)], sem.at[into]).start()
            return carry

        lax.fori_loop(0, TM_TOK, issue, 0, unroll=ROW_UNROLL)

    @pl.when(step == 0)
    def _():
        gather(0, 0)

    @pl.when(step + 1 < pl.num_programs(0))
    def _():
        gather(step + 1, 1 - slot)

    for k in range(TOP_K):
        _row_copy(o_hbm.at[pl.ds(0, TM_TOK)], buf.at[slot, k], sem.at[slot]).wait()

    route = route_ref[...]
    lo1, hi1 = _unpack_bf16_pairs(buf[slot, 0])
    lo2, hi2 = _unpack_bf16_pairs(buf[slot, 1])
    g1, g2 = route[:, 2:3], route[:, 3:4]
    moe = jnp.concatenate([g1 * lo1 + g2 * lo2, g1 * hi1 + g2 * hi2], axis=1)
    x2 = x_ref[...] + _mod_rows(gt_ref) * moe
    y_ref[...] = x2 * lax.rsqrt(jnp.mean(x2 * x2, axis=-1, keepdims=True) + EPS) * gf_ref[...]


def _combine(dests, x1, route, mod, per_row, g_final, o_rows, tok_offset):
    n = x1.shape[0]
    off = tok_offset // TM_TOK
    grid_spec = pltpu.PrefetchScalarGridSpec(
        num_scalar_prefetch=2,
        grid=(n // TM_TOK,),
        in_specs=[pl.BlockSpec((TM_TOK, D), lambda i, *_: (i, 0)),
                  pl.BlockSpec((TM_TOK, LANES), lambda i, *_: (i + off, 0)),
                  _mod_spec(per_row, TM_TOK, D, lambda *_: 5),
                  pl.BlockSpec((1, D), lambda i, *_: (0, 0)),
                  pl.BlockSpec(memory_space=pl.ANY)],
        out_specs=pl.BlockSpec((TM_TOK, D), lambda i, *_: (i, 0)),
        scratch_shapes=[pltpu.VMEM((2, TOP_K, TM_TOK, D_PACK), jnp.uint32), pltpu.SemaphoreType.DMA((2,))],
    )
    return pl.pallas_call(
        functools.partial(_combine_kernel, tok_offset=tok_offset),
        grid_spec=grid_spec,
        out_shape=jax.ShapeDtypeStruct((n, D), F32),
        compiler_params=_params(("arbitrary",)),
        name="combine",
    )(*dests, x1, route, mod, g_final, o_rows)


def kernel(x_prompt, x_sample, cache_k, cache_v, state_C, state_n, state_m, c_prompt, c_sample, rel_bias, w_ada, b_ada, g_mix, g_ffn, w_in, sinks, b_igate, b_fgate, w_out, w_router_grp, b_router_grp, w_router_exp, b_router_exp, w_gate, w_up, w_down, g_final):
    xp = x_prompt.reshape(N_PROMPT, D)
    xs = x_sample.reshape(N_SAMPLE, D)

    c_all = jnp.concatenate([c_prompt, c_sample, jnp.zeros((C_ROWS - 1 - N_BATCH, D), F32)], axis=0)
    mod = _ada(c_all, w_ada[0], b_ada)
    mod_p = mod[0:1]
    mod_s = mod[1:1 + N_BATCH].reshape(N_BATCH, 1, -1)

    w_in_bf = w_in[0].astype(BF16)
    w_gates = jnp.pad(w_in[0, :, Z_WIDTH:].astype(BF16), ((0, 0), (0, LANES - 2 * M_HEADS)))
    zp, zgp = _inproj(xp, mod_p, False, g_mix, w_in_bf, w_gates)
    zs, zgs = _inproj(xs, mod_s, True, g_mix, w_in_bf, w_gates)

    rb_flat = rel_bias.reshape(NUM_BUCKETS * ATT_HEADS)
    sink_v = sinks[0]
    ya_p = _swa_prompt(zp, rb_flat, sink_v)
    ya_s, nk_s, nv_s = _swa_sample(zs, cache_k.reshape(N_BATCH, CACHE_ROWS, HEAD_DIM),
                                   cache_v.reshape(N_BATCH, CACHE_ROWS, HEAD_DIM), rb_flat, sink_v)

    gate_bias = jnp.concatenate([b_igate[0], b_fgate[0], jnp.zeros((LANES - 2 * M_HEADS,), F32)]).reshape(1, LANES)
    ym_p, c_p, n_p, m_p = _mlstm_prompt(zp, zgp, gate_bias)
    m0_tok = jnp.pad(jnp.repeat(state_m[0], T_DEC, axis=0), ((0, 0), (0, LANES - M_HEADS)))
    n0_flat = state_n[0].reshape(N_BATCH, M_HEADS * DK)
    n0_tok = jnp.repeat(n0_flat, T_DEC, axis=0)
    ym_s, c_s, n_s, m_s = _mlstm_sample(zs, zgs, gate_bias, m0_tok, n0_tok, state_C[0], n0_flat)

    w_out_bf = w_out[0].astype(BF16)
    x1_p = _outproj(ya_p, ym_p, w_out_bf, xp, mod_p, False)
    x1_s = _outproj(ya_s, ym_s, w_out_bf, xs, mod_s, True)

    w_route = jnp.pad(jnp.concatenate([w_router_grp[0], w_router_exp[0]], axis=1),
                      ((0, 0), (0, LANES - N_GROUPS - N_EXPERTS)))
    b_route = jnp.pad(jnp.concatenate([b_router_grp[0], b_router_exp[0]]),
                      (0, LANES - N_GROUPS - N_EXPERTS)).reshape(1, LANES)
    w_route_hi = w_route.astype(BF16)
    w_route_split = jnp.stack([w_route_hi, (w_route - w_route_hi.astype(F32)).astype(BF16)])
    h2, route = _router(x1_p, x1_s, mod_p, mod_s, g_ffn, w_route_split, b_route)

    dest, counts = _rank(route)
    dests = (dest[:, 0], dest[:, 1])

    i32 = lambda a: a.astype(jnp.int32)
    cnt = i32(counts[0, :N_EXPERTS])
    tiles_per = (cnt + TM_MOE - 1) // TM_MOE
    tile_end = jnp.cumsum(tiles_per)
    n_tiles = tile_end[-1]
    tile_ids = jnp.arange(MAX_TILES, dtype=jnp.int32)
    tile_expert = i32(jnp.minimum(jnp.searchsorted(tile_end, tile_ids, side="right"), N_EXPERTS - 1))
    last_expert = tile_expert[jnp.maximum(n_tiles - 1, 0)]
    tile_valid = tile_ids < n_tiles
    tile_expert = jnp.where(tile_valid, tile_expert, last_expert)
    prev_expert = jnp.concatenate([jnp.full((1,), -1, jnp.int32), tile_expert[:-1]])
    run_first = jnp.logical_and(tile_valid, tile_expert != prev_expert)
    run_id = jnp.maximum(jnp.cumsum(i32(run_first)) - 1, 0)
    expert_ids = jnp.arange(N_EXPERTS, dtype=jnp.int32)
    used = jnp.where(tiles_per > 0, expert_ids, N_EXPERTS)
    next_used = jnp.concatenate([lax.cummin(used[::-1])[::-1][1:], jnp.full((1,), N_EXPERTS, jnp.int32)])
    next_used = jnp.where(next_used >= N_EXPERTS, -1, next_used)
    tiles = (tile_expert, i32(tile_valid), i32(run_first), next_used[tile_expert], i32(run_id),
             jnp.stack([n_tiles, jnp.sum(i32(run_first))]).astype(jnp.int32))
    next_expert = jnp.concatenate([tile_expert[1:], jnp.full((1,), -1, jnp.int32)])
    run_last = jnp.logical_or(tile_expert != next_expert, tile_ids == n_tiles - 1)
    zero_class = jnp.where(tile_valid, jnp.where(run_last, ZERO_FIRST, 0), ZERO_ANYTIME)
    xs_rows = _dispatch(dests, i32(zero_class), h2)
    h1 = _expert_up(tiles, xs_rows, w_gate[0], w_up[0])
    o_rows = _expert_down(tiles, h1, w_down[0])

    gf = g_final.reshape(1, D)
    y_p = _combine(dests, x1_p, route, mod_p, False, gf, o_rows, 0)
    y_s = _combine(dests, x1_s, route, mod_s, True, gf, o_rows, N_PROMPT)

    kv5 = lambda a: a.reshape(1, -1, WINDOW, KV_HEADS, HEAD_DIM)
    kcol = ATT_WIDTH
    nk_p = zp[N_PROMPT - WINDOW:, kcol:kcol + KV_WIDTH]
    nv_p = zp[N_PROMPT - WINDOW:, kcol + KV_WIDTH:kcol + 2 * KV_WIDTH]
    return (y_p.reshape(1, N_PROMPT, D), y_s.reshape(N_BATCH, T_DEC, D),
            kv5(nk_p), kv5(nv_p),
            c_p.reshape(1, 1, M_HEADS, DV, DK), n_p.reshape(1, 1, M_HEADS, DK), m_p[:, 0].reshape(1, 1, M_HEADS),
            kv5(nk_s), kv5(nv_s),
            c_s.reshape(1, N_BATCH, M_HEADS, DV, DK), n_s.reshape(1, N_BATCH, M_HEADS, DK),
            m_s.reshape(N_BATCH, T_DEC, LANES)[:, T_DEC - 1, :M_HEADS].reshape(1, N_BATCH, M_HEADS))
```

```python
import functools
import math

import numpy as np
import jax
import jax.numpy as jnp
from jax import lax
from jax.experimental import pallas as pl
from jax.experimental.pallas import tpu as pltpu

F32 = jnp.float32
BF16 = jnp.bfloat16
NEG_INF = float("-inf")

D = 4096
N_PROMPT = 8192
N_BATCH = 128
T_DEC = 8
N_SAMPLE = N_BATCH * T_DEC
N_TOK = N_PROMPT + N_SAMPLE
HEAD_DIM = 128
ATT_HEADS = 16
KV_HEADS = 4
GROUP = ATT_HEADS // KV_HEADS
WINDOW = 128
ATT_WIDTH = ATT_HEADS * HEAD_DIM
KV_WIDTH = KV_HEADS * HEAD_DIM
NUM_BUCKETS = 32
MAX_EXACT = 16
MAX_DISTANCE = 128
M_HEADS = 8
DK = 128
DV = 256
M_WIDTH = M_HEADS * DV
Z_WIDTH = ATT_WIDTH + 2 * KV_WIDTH + 2 * M_HEADS * DK + 2 * M_WIDTH
N_GROUPS = 4
EXP_PER_GROUP = 8
N_EXPERTS = N_GROUPS * EXP_PER_GROUP
TOP_K = 2
D_FF = 1024
EPS = 1e-6
ATT_SCALE = HEAD_DIM ** -0.5
Q_SCALE = DK ** -0.5

LANES = 128
SUBLANES = 8
VMEM_LIMIT = 56 * 1024 * 1024

TM_PROJ = 512
TN_IN = 1024
TN_OUT = 1024
TN_ADA = 512
C_ROWS = 136
ATT_BLOCK = 128
SAMPLE_BT = 8
ML_CHUNK = 256
TM_MOE = 256
TN_FF = 512
N_ASSIGN = N_TOK * TOP_K
MAX_TILES = N_ASSIGN // TM_MOE + N_EXPERTS
A_PAD = MAX_TILES * TM_MOE
TM_TOK = 256
TM_RANK = 1024


def _params(sem):
    return pltpu.CompilerParams(dimension_semantics=sem, vmem_limit_bytes=VMEM_LIMIT)


def _iota(shape, dim):
    return lax.broadcasted_iota(jnp.int32, shape, dim)


def _dot(a, b):
    return jnp.dot(a, b, preferred_element_type=F32)


def _dot_nt(a, b):
    return lax.dot_general(a, b, (((1,), (1,)), ((), ())), preferred_element_type=F32)


def _dot_tn(a, b):
    return lax.dot_general(a, b, (((0,), (0,)), ((), ())), preferred_element_type=F32)


def _split3(x):
    x1 = x.astype(BF16)
    r1 = x - x1.astype(F32)
    x2 = r1.astype(BF16)
    r2 = r1 - x2.astype(F32)
    return x1, x2, r2.astype(BF16)


def _dot_exact_lhs01(a01, x):
    x1, x2, x3 = _split3(x)
    return _dot(a01, x1) + _dot(a01, x2) + _dot(a01, x3)


def _pack_bf16_pairs(x):
    w = x.shape[1] // 2
    bits = lax.bitcast_convert_type(x.astype(BF16).astype(F32), jnp.uint32)
    return (bits[:, :w] >> 16) | (bits[:, w:] & jnp.uint32(0xFFFF0000))


def _unpack_bf16_pairs(words):
    return (lax.bitcast_convert_type(words << 16, F32),
            lax.bitcast_convert_type(words & jnp.uint32(0xFFFF0000), F32))


def _sigmoid(x):
    return 1.0 / (1.0 + jnp.exp(-x))


def _log_sigmoid(x):
    return jnp.minimum(x, 0.0) - jnp.log(1.0 + jnp.exp(-jnp.abs(x)))


def _mod_spec(per_batch, rows, width, col):
    if per_batch:
        return pl.BlockSpec((rows // T_DEC, 1, width), lambda i, *rest: (i, 0, col(*rest)))
    return pl.BlockSpec((1, width), lambda i, *rest: (0, col(*rest)))


def _mod_rows(ref):
    v = ref[...]
    if v.ndim == 2:
        return v
    nb, _, width = v.shape
    return jnp.broadcast_to(v, (nb, T_DEC, width)).reshape(nb * T_DEC, width)


def _ada_kernel(c_ref, w_ref, b_ref, o_ref):
    c = c_ref[...]
    s = (c * _sigmoid(c)).astype(BF16)
    o_ref[...] = _dot(s, w_ref[...].astype(BF16)) + b_ref[...]


def _ada(c_all, w_ada, b_ada):
    n = w_ada.shape[1]
    return pl.pallas_call(
        _ada_kernel,
        grid=(n // TN_ADA,),
        in_specs=[pl.BlockSpec((C_ROWS, D), lambda j: (0, 0)),
                  pl.BlockSpec((D, TN_ADA), lambda j: (0, j)),
                  pl.BlockSpec((1, TN_ADA), lambda j: (0, j))],
        out_specs=pl.BlockSpec((C_ROWS, TN_ADA), lambda j: (0, j)),
        out_shape=jax.ShapeDtypeStruct((C_ROWS, n), F32),
        compiler_params=_params(("arbitrary",)),
        name="ada",
    )(c_all, w_ada, b_ada)


PROMPT_TOK_BLOCKS = N_PROMPT // TM_TOK


def _prompt_block(i):
    return jnp.minimum(i, PROMPT_TOK_BLOCKS - 1)


def _sample_block(i):
    return jnp.maximum(i - PROMPT_TOK_BLOCKS, 0)


def _norm_body(x_ref, sh_ref, sc_ref, g_ref, wg_ref, h_ref, zg_ref):
    x = x_ref[...]
    y = x * lax.rsqrt(jnp.mean(x * x, axis=-1, keepdims=True) + EPS) * g_ref[...]
    hb = (y * (1.0 + _mod_rows(sc_ref)) + _mod_rows(sh_ref)).astype(BF16)
    h_ref[...] = hb
    zg_ref[...] = _dot(hb, wg_ref[...])


def _norm_kernel(xp_ref, xs_ref, shp_ref, scp_ref, shs_ref, scs_ref, g_ref, wg_ref, h_ref, zg_ref):
    i = pl.program_id(0)

    @pl.when(i < PROMPT_TOK_BLOCKS)
    def _():
        _norm_body(xp_ref, shp_ref, scp_ref, g_ref, wg_ref, h_ref, zg_ref)

    @pl.when(i >= PROMPT_TOK_BLOCKS)
    def _():
        _norm_body(xs_ref, shs_ref, scs_ref, g_ref, wg_ref, h_ref, zg_ref)


def _norm(xp, xs, mod_p, mod_s, g_mix, w_gate):
    per_batch = lambda col: pl.BlockSpec((TM_TOK // T_DEC, 1, D), lambda i: (_sample_block(i), 0, col))
    return pl.pallas_call(
        _norm_kernel,
        grid=(N_TOK // TM_TOK,),
        in_specs=[pl.BlockSpec((TM_TOK, D), lambda i: (_prompt_block(i), 0)),
                  pl.BlockSpec((TM_TOK, D), lambda i: (_sample_block(i), 0)),
                  pl.BlockSpec((1, D), lambda i: (0, 0)),
                  pl.BlockSpec((1, D), lambda i: (0, 1)),
                  per_batch(0), per_batch(1),
                  pl.BlockSpec((1, D), lambda i: (0, 0)),
                  pl.BlockSpec((D, LANES), lambda i: (0, 0))],
        out_specs=[pl.BlockSpec((TM_TOK, D), lambda i: (i, 0)),
                   pl.BlockSpec((TM_TOK, LANES), lambda i: (i, 0))],
        out_shape=[jax.ShapeDtypeStruct((N_TOK, D), BF16), jax.ShapeDtypeStruct((N_TOK, LANES), F32)],
        compiler_params=_params(("arbitrary",)),
        name="norm",
    )(xp, xs, mod_p, mod_p, mod_s, mod_s, g_mix, w_gate)


N_IN_CHUNKS = Z_WIDTH // TN_IN


def _inproj_kernel(h_ref, w_hbm, z_ref, stage, w_bf, sem):
    j = pl.program_id(0)

    def fetch(chunk):
        cols = pl.ds(pl.multiple_of(chunk * TN_IN, TN_IN), TN_IN)
        return pltpu.make_async_copy(w_hbm.at[:, cols], stage, sem.at[0])

    @pl.when(pl.program_id(1) == 0)
    def _():
        @pl.when(j == 0)
        def _():
            fetch(0).start()

        fetch(j).wait()
        w_bf[...] = stage[...].astype(BF16)

        @pl.when(j + 1 < N_IN_CHUNKS)
        def _():
            fetch(j + 1).start()

    z_ref[...] = _dot(h_ref[...], w_bf[...])


def _inproj(h, w_in):
    return pl.pallas_call(
        _inproj_kernel,
        grid=(N_IN_CHUNKS, N_TOK // TM_PROJ),
        in_specs=[pl.BlockSpec((TM_PROJ, D), lambda j, i: (i, 0)),
                  pl.BlockSpec(memory_space=pl.ANY)],
        out_specs=pl.BlockSpec((TM_PROJ, TN_IN), lambda j, i: (i, j)),
        out_shape=jax.ShapeDtypeStruct((N_TOK, Z_WIDTH), F32),
        scratch_shapes=[pltpu.VMEM((D, TN_IN), F32), pltpu.VMEM((D, TN_IN), BF16), pltpu.SemaphoreType.DMA((1,))],
        compiler_params=_params(("arbitrary", "arbitrary")),
        name="inproj",
    )(h, w_in)


def _t5_bucket_np(dist):
    n = np.maximum(dist, 0)
    nf = np.maximum(n, 1).astype(np.float32)
    large = MAX_EXACT + (np.log(nf / MAX_EXACT) / math.log(MAX_DISTANCE / MAX_EXACT)
                         * (NUM_BUCKETS - MAX_EXACT)).astype(np.int32)
    large = np.minimum(large, NUM_BUCKETS - 1)
    return np.where(n < MAX_EXACT, n, large).astype(np.int32)


def _bucket_table(n_q, n_keys_valid, n_keys_padded):
    t = np.arange(n_q)[:, None]
    j = np.arange(n_keys_padded)[None, :]
    dist = t + WINDOW - j
    valid = (dist >= 0) & (dist < WINDOW) & (j < n_keys_valid)
    return np.where(valid, _t5_bucket_np(dist), -1).astype(np.int32)


def _fill_bias(bucket_ref, rb_ref, bias_scr, rows, per_head_table=False):
    for h in range(ATT_HEADS):
        bk = bucket_ref[h * rows:(h + 1) * rows, :] if per_head_table else bucket_ref[...]
        acc = jnp.full(bk.shape, NEG_INF, F32)
        for b in range(NUM_BUCKETS):
            acc = jnp.where(bk == b, rb_ref[b * ATT_HEADS + h], acc)
        bias_scr[h * rows:(h + 1) * rows, :] = acc


def _with_ones(v2):
    return jnp.concatenate([v2, jnp.ones_like(v2)], axis=1)


def _sink_softmax_av(lg, sink, v2_ones):
    m = jnp.maximum(jnp.max(lg, axis=-1, keepdims=True), sink)
    p = jnp.exp(lg - m).astype(BF16)
    pv = _dot(p, v2_ones)
    return pv[:, :HEAD_DIM] / (pv[:, HEAD_DIM:] + jnp.exp(sink - m))


def _swa_prompt_kernel(rb_ref, sink_ref, bucket_ref, q_ref, kp_ref, kc_ref, vp_ref, vc_ref,
                       o_ref, bias_scr):
    i = pl.program_id(0)

    @pl.when(i == 0)
    def _():
        _fill_bias(bucket_ref, rb_ref, bias_scr, ATT_BLOCK)

    first_prev = jnp.logical_and(i == 0, _iota((ATT_BLOCK, 2 * ATT_BLOCK), 1) < ATT_BLOCK)
    for g in range(KV_HEADS):
        ks = slice(g * HEAD_DIM, (g + 1) * HEAD_DIM)
        k2 = jnp.concatenate([kp_ref[:, ks], kc_ref[:, ks]], axis=0).astype(BF16)
        v2 = _with_ones(jnp.concatenate([vp_ref[:, ks], vc_ref[:, ks]], axis=0).astype(BF16))
        for r in range(GROUP):
            h = g * GROUP + r
            hs = slice(h * HEAD_DIM, (h + 1) * HEAD_DIM)
            lg = _dot_nt(q_ref[:, hs].astype(BF16), k2) * ATT_SCALE + bias_scr[h * ATT_BLOCK:(h + 1) * ATT_BLOCK, :]
            lg = jnp.where(first_prev, NEG_INF, lg)
            o_ref[:, hs] = _sink_softmax_av(lg, sink_ref[h], v2).astype(BF16)


def _swa_prompt(z, rb_flat, sinks):
    nb = N_PROMPT // ATT_BLOCK
    bucket = jnp.asarray(_bucket_table(ATT_BLOCK, 2 * ATT_BLOCK, 2 * ATT_BLOCK))
    kcol = ATT_WIDTH // KV_WIDTH
    prev = lambda i: jnp.maximum(i - 1, 0)
    smem = pl.BlockSpec(memory_space=pltpu.SMEM)
    return pl.pallas_call(
        _swa_prompt_kernel,
        grid=(nb,),
        in_specs=[smem, smem,
                  pl.BlockSpec((ATT_BLOCK, 2 * ATT_BLOCK), lambda i: (0, 0)),
                  pl.BlockSpec((ATT_BLOCK, ATT_WIDTH), lambda i: (i, 0)),
                  pl.BlockSpec((ATT_BLOCK, KV_WIDTH), lambda i: (prev(i), kcol)),
                  pl.BlockSpec((ATT_BLOCK, KV_WIDTH), lambda i: (i, kcol)),
                  pl.BlockSpec((ATT_BLOCK, KV_WIDTH), lambda i: (prev(i), kcol + 1)),
                  pl.BlockSpec((ATT_BLOCK, KV_WIDTH), lambda i: (i, kcol + 1))],
        out_specs=pl.BlockSpec((ATT_BLOCK, ATT_WIDTH), lambda i: (i, 0)),
        out_shape=jax.ShapeDtypeStruct((N_PROMPT, ATT_WIDTH), BF16),
        scratch_shapes=[pltpu.VMEM((ATT_HEADS * ATT_BLOCK, 2 * ATT_BLOCK), F32)],
        compiler_params=_params(("arbitrary",)),
        name="swa_prompt",
    )(rb_flat, sinks, bucket, z, z, z, z, z)


CACHE_ROWS = WINDOW * KV_HEADS
NEW_ROWS = T_DEC * KV_HEADS
S_KEYS = 5 * LANES


def _sample_bucket_table():
    t = np.arange(T_DEC)[:, None]
    col = np.arange(S_KEYS)[None, :]
    in_cache = col < CACHE_ROWS
    in_new = (col >= CACHE_ROWS) & (col < CACHE_ROWS + NEW_ROWS)
    key_head = np.where(in_cache, col % KV_HEADS, (col - CACHE_ROWS) // T_DEC)
    key_pos = np.where(in_cache, col // KV_HEADS, WINDOW + (col - CACHE_ROWS) % T_DEC)
    dist = t + WINDOW - key_pos
    valid = (dist >= 0) & (dist < WINDOW) & (in_cache | in_new)
    per_query = np.where(valid, _t5_bucket_np(dist), -1)
    heads = np.arange(ATT_HEADS)[:, None, None] // GROUP
    table = np.where(heads == key_head[None], per_query[None], -1)
    return table.reshape(ATT_HEADS * T_DEC, S_KEYS).astype(np.int32)


def _swa_sample_kernel(rb_ref, sink_ref, bucket_ref, q_ref, kn_ref, vn_ref, ck_ref, cv_ref,
                       o_ref, nk_ref, nv_ref, bias_scr):
    @pl.when(pl.program_id(0) == 0)
    def _():
        _fill_bias(bucket_ref, rb_ref, bias_scr, T_DEC, per_head_table=True)

    nk_ref[:, 0:CACHE_ROWS - NEW_ROWS, :] = ck_ref[:, NEW_ROWS:CACHE_ROWS, :]
    nv_ref[:, 0:CACHE_ROWS - NEW_ROWS, :] = cv_ref[:, NEW_ROWS:CACHE_ROWS, :]

    pad = jnp.zeros((S_KEYS - CACHE_ROWS - NEW_ROWS, HEAD_DIM), F32)
    sink_col = jnp.concatenate([jnp.full((T_DEC, 1), sink_ref[h], F32) for h in range(ATT_HEADS)], axis=0)
    bias = bias_scr[...]
    for b in range(SAMPLE_BT):
        ts = slice(b * T_DEC, (b + 1) * T_DEC)
        head_cols = lambda ref, n: [ref[ts, h * HEAD_DIM:(h + 1) * HEAD_DIM] for h in range(n)]
        k_new, v_new = head_cols(kn_ref, KV_HEADS), head_cols(vn_ref, KV_HEADS)
        for g in range(KV_HEADS):
            new_rows = pl.ds(CACHE_ROWS - NEW_ROWS + g, T_DEC, stride=KV_HEADS)
            nk_ref[b, new_rows, :] = k_new[g]
            nv_ref[b, new_rows, :] = v_new[g]
        qa = jnp.concatenate(head_cols(q_ref, ATT_HEADS), axis=0).astype(BF16)
        k2 = jnp.concatenate([ck_ref[b]] + k_new + [pad], axis=0).astype(BF16)
        v2 = _with_ones(jnp.concatenate([cv_ref[b]] + v_new + [pad], axis=0).astype(BF16))
        o = _sink_softmax_av(_dot_nt(qa, k2) * ATT_SCALE + bias, sink_col, v2)
        for h in range(ATT_HEADS):
            o_ref[ts, h * HEAD_DIM:(h + 1) * HEAD_DIM] = o[h * T_DEC:(h + 1) * T_DEC].astype(BF16)


def _swa_sample(z, cache_k, cache_v, rb_flat, sinks):
    rows = SAMPLE_BT * T_DEC
    z_off = N_PROMPT // rows
    bucket = jnp.asarray(_sample_bucket_table())
    kcol = ATT_WIDTH // KV_WIDTH
    smem = pl.BlockSpec(memory_space=pltpu.SMEM)
    cache_spec = pl.BlockSpec((SAMPLE_BT, CACHE_ROWS, HEAD_DIM), lambda i: (i, 0, 0))
    cache_shape = jax.ShapeDtypeStruct((N_BATCH, CACHE_ROWS, HEAD_DIM), F32)
    return pl.pallas_call(
        _swa_sample_kernel,
        grid=(N_BATCH // SAMPLE_BT,),
        in_specs=[smem, smem,
                  pl.BlockSpec((ATT_HEADS * T_DEC, S_KEYS), lambda i: (0, 0)),
                  pl.BlockSpec((rows, ATT_WIDTH), lambda i: (i + z_off, 0)),
                  pl.BlockSpec((rows, KV_WIDTH), lambda i: (i + z_off, kcol)),
                  pl.BlockSpec((rows, KV_WIDTH), lambda i: (i + z_off, kcol + 1)),
                  cache_spec, cache_spec],
        out_specs=[pl.BlockSpec((rows, ATT_WIDTH), lambda i: (i, 0)), cache_spec, cache_spec],
        out_shape=[jax.ShapeDtypeStruct((N_SAMPLE, ATT_WIDTH), BF16), cache_shape, cache_shape],
        scratch_shapes=[pltpu.VMEM((ATT_HEADS * T_DEC, S_KEYS), F32)],
        compiler_params=_params(("arbitrary",)),
        name="swa_sample",
    )(rb_flat, sinks, bucket, z, z, z, cache_k, cache_v)


LANE_IG, LANE_LF, LANE_B = 0, M_HEADS, 2 * M_HEADS


def _gate_table(zg, gate_bias, seg_len):
    L = zg.shape[0]
    g = zg + gate_bias
    lane = _iota((L, LANES), 1)
    lf = _log_sigmoid(g)
    lf_only = jnp.where(jnp.logical_and(lane >= LANE_LF, lane < LANE_B), lf, 0.0)
    row = _iota((L, L), 0)
    col = _iota((L, L), 1)
    same_seg = (row // seg_len) == (col // seg_len)
    tril = jnp.where(jnp.logical_and(col <= row, same_seg), 1.0, 0.0).astype(BF16)
    cum = pltpu.roll(_dot_exact_lhs01(tril, lf_only), M_HEADS, axis=1)
    table = jnp.where(lane < LANE_LF, g, jnp.where(lane < LANE_B, lf, jnp.where(lane < LANE_B + M_HEADS, cum, 0.0)))
    return table, jnp.logical_and(col <= row, same_seg)


def _mlstm_intra(table, table_t, mask, m0_col, h, q, k, v):
    b_c = table[:, LANE_B + h:LANE_B + h + 1]
    b_r = table_t[LANE_B + h:LANE_B + h + 1, :]
    ig_r = table_t[LANE_IG + h:LANE_IG + h + 1, :]
    log_d = jnp.where(mask, b_c - b_r + ig_r, NEG_INF)
    log_inter = b_c + m0_col
    m_t = jnp.maximum(log_inter, jnp.max(log_d, axis=1, keepdims=True))
    d = jnp.exp(log_d - m_t)
    w_inter = jnp.exp(log_inter - m_t)
    s = _dot_nt(q.astype(BF16), k.astype(BF16)) * d
    num_intra = _dot(s.astype(BF16), v.astype(BF16))
    den_intra = jnp.sum(s, axis=1, keepdims=True)
    return b_c, m_t, w_inter, num_intra, den_intra


def _mlstm_prompt_kernel(gb_ref, zg_ref, q_ref, k_ref, va_ref, vb_ref, oa_ref, ob_ref,
                         y_ref, c_out, n_out, m_out, c_scr, n_scr, m_scr):
    step = pl.program_id(0)
    L = ML_CHUNK

    @pl.when(step == 0)
    def _():
        c_scr[...] = jnp.zeros_like(c_scr)
        n_scr[...] = jnp.zeros_like(n_scr)
        m_scr[...] = jnp.zeros_like(m_scr)

    table, mask = _gate_table(zg_ref[...], gb_ref[...], L)
    table_t = table.T
    half = M_HEADS // 2
    for h in range(M_HEADS):
        v_ref, o_ref = (va_ref, oa_ref) if h < half else (vb_ref, ob_ref)
        vs = slice((h % half) * DV, (h % half + 1) * DV)
        q = q_ref[:, h * DK:(h + 1) * DK] * Q_SCALE
        k = k_ref[:, h * DK:(h + 1) * DK]
        v = v_ref[:, vs]
        m0 = m_scr[h:h + 1, 0:1]
        b_c, m_t, w_inter, num_intra, den_intra = _mlstm_intra(table, table_t, mask, m0, h, q, k, v)
        c_old = c_scr[h]
        n_old = n_scr[h:h + 1, :]
        num = num_intra + w_inter * _dot_nt(q.astype(BF16), c_old.astype(BF16))
        den = den_intra + w_inter * jnp.sum(q * n_old, axis=1, keepdims=True)
        hh = num / jnp.maximum(jnp.abs(den), jnp.exp(-m_t))
        y_ref[:, h * DV:(h + 1) * DV] = (_sigmoid(o_ref[:, vs]) * hh).astype(BF16)

        ig_c = table[:, LANE_IG + h:LANE_IG + h + 1]
        m_new = m_t[L - 1:L, :]
        b_last = b_c[L - 1:L, :]
        w_s = jnp.exp(b_last - b_c + ig_c - m_new)
        decay = jnp.exp(b_last + m0 - m_new)
        c_scr[h] = decay * c_old + _dot_tn((w_s * v).astype(BF16), k.astype(BF16))
        n_scr[h:h + 1, :] = decay * n_old + jnp.sum(w_s * k, axis=0, keepdims=True)
        m_scr[h:h + 1, :] = jnp.broadcast_to(m_new, (1, LANES))

    @pl.when(step == pl.num_programs(0) - 1)
    def _():
        c_out[...] = c_scr[...]
        n_out[...] = n_scr[...]
        m_out[...] = m_scr[...]


def _mlstm_prompt(z, zg, gate_bias):
    L = ML_CHUNK
    blk = M_HEADS * DK
    col = lambda c: pl.BlockSpec((L, blk), lambda i: (i, c))
    const = lambda shape: pl.BlockSpec(shape, lambda i: tuple(0 for _ in shape))
    return pl.pallas_call(
        _mlstm_prompt_kernel,
        grid=(N_PROMPT // L,),
        in_specs=[const((1, LANES)),
                  pl.BlockSpec((L, LANES), lambda i: (i, 0)),
                  col(3), col(4), col(5), col(6), col(7), col(8)],
        out_specs=[pl.BlockSpec((L, M_WIDTH), lambda i: (i, 0)),
                   const((M_HEADS, DV, DK)), const((M_HEADS, DK)), const((M_HEADS, LANES))],
        out_shape=[jax.ShapeDtypeStruct((N_PROMPT, M_WIDTH), BF16),
                   jax.ShapeDtypeStruct((M_HEADS, DV, DK), F32),
                   jax.ShapeDtypeStruct((M_HEADS, DK), F32),
                   jax.ShapeDtypeStruct((M_HEADS, LANES), F32)],
        scratch_shapes=[pltpu.VMEM((M_HEADS, DV, DK), F32),
                        pltpu.VMEM((M_HEADS, DK), F32),
                        pltpu.VMEM((M_HEADS, LANES), F32)],
        compiler_params=_params(("arbitrary",)),
        name="mlstm_prompt",
    )(gate_bias, zg, z, z, z, z, z, z)


S_ROWS = SAMPLE_BT * T_DEC


def _mlstm_sample_kernel(gb_ref, zg_ref, m0_ref, n0t_ref, q_ref, k_ref, va_ref, vb_ref, oa_ref, ob_ref,
                         c0_ref, n0_ref, y_ref, c_out, n_out, m_out):
    L = S_ROWS
    table, mask = _gate_table(zg_ref[...], gb_ref[...], T_DEC)
    table_t = jnp.concatenate([table, jnp.zeros((LANES - L, LANES), F32)], axis=0).T[:, 0:L]
    m0_all = m0_ref[...]
    lane = _iota((L, LANES), 1)
    row_b = _iota((SAMPLE_BT, L), 1) // T_DEC
    seg_sum = jnp.where(row_b == _iota((SAMPLE_BT, L), 0), 1.0, 0.0)
    m_tok = jnp.zeros((L, LANES), F32)
    half = M_HEADS // 2
    for h in range(M_HEADS):
        v_ref, o_ref = (va_ref, oa_ref) if h < half else (vb_ref, ob_ref)
        vs = slice((h % half) * DV, (h % half + 1) * DV)
        q = q_ref[:, h * DK:(h + 1) * DK] * Q_SCALE
        k = k_ref[:, h * DK:(h + 1) * DK]
        v = v_ref[:, vs]
        m0 = m0_all[:, h:h + 1]
        b_c, m_t, w_inter, num_intra, den_intra = _mlstm_intra(table, table_t, mask, m0, h, q, k, v)
        qb = q.astype(BF16)
        num_inter = jnp.concatenate(
            [_dot_nt(qb[b * T_DEC:(b + 1) * T_DEC], c0_ref[b, h].astype(BF16)) for b in range(SAMPLE_BT)], axis=0)
        num = num_intra + w_inter * num_inter
        den = den_intra + w_inter * jnp.sum(q * n0t_ref[:, h * DK:(h + 1) * DK], axis=1, keepdims=True)
        hh = num / jnp.maximum(jnp.abs(den), jnp.exp(-m_t))
        y_ref[:, h * DV:(h + 1) * DV] = (_sigmoid(o_ref[:, vs]) * hh).astype(BF16)
        m_tok = jnp.where(lane == h, m_t, m_tok)

        def last_tok(x):
            x3 = x.reshape(SAMPLE_BT, T_DEC, 1)
            return jnp.broadcast_to(x3[:, T_DEC - 1:T_DEC, :], x3.shape).reshape(L, 1)

        ig_c = table[:, LANE_IG + h:LANE_IG + h + 1]
        m_new = last_tok(m_t)
        b_last = last_tok(b_c)
        w_s = jnp.exp(b_last - b_c + ig_c - m_new)
        decay = jnp.exp(b_last + m0 - m_new)
        wv = (w_s * v).astype(BF16)
        kb = k.astype(BF16)
        rowsel = _iota((L, 1), 0) // T_DEC
        for b in range(SAMPLE_BT):
            dec_b = decay[b * T_DEC + T_DEC - 1:(b + 1) * T_DEC, :]
            wv_b = jnp.where(rowsel == b, wv, jnp.zeros_like(wv))
            c_out[b, h] = dec_b * c0_ref[b, h] + _dot_tn(wv_b, kb)
        dec_rows = decay.reshape(SAMPLE_BT, T_DEC, 1)[:, T_DEC - 1, :]
        n_out[:, h * DK:(h + 1) * DK] = dec_rows * n0_ref[:, h * DK:(h + 1) * DK] + jnp.dot(
            seg_sum, w_s * k, preferred_element_type=F32, precision=lax.Precision.HIGHEST)
    m_out[...] = m_tok


def _mlstm_sample(z, zg, gate_bias, m0_tok, n0_tok, state_c, state_n):
    L = S_ROWS
    blk = M_HEADS * DK
    z_off = N_PROMPT // L
    col = lambda c: pl.BlockSpec((L, blk), lambda i: (i + z_off, c))
    c_spec = pl.BlockSpec((SAMPLE_BT, M_HEADS, DV, DK), lambda i: (i, 0, 0, 0))
    n_spec = pl.BlockSpec((SAMPLE_BT, blk), lambda i: (i, 0))
    return pl.pallas_call(
        _mlstm_sample_kernel,
        grid=(N_BATCH // SAMPLE_BT,),
        in_specs=[pl.BlockSpec((1, LANES), lambda i: (0, 0)),
                  pl.BlockSpec((L, LANES), lambda i: (i + z_off, 0)),
                  pl.BlockSpec((L, LANES), lambda i: (i, 0)),
                  pl.BlockSpec((L, blk), lambda i: (i, 0)),
                  col(3), col(4), col(5), col(6), col(7), col(8),
                  c_spec, n_spec],
        out_specs=[pl.BlockSpec((L, M_WIDTH), lambda i: (i, 0)), c_spec, n_spec,
                   pl.BlockSpec((L, LANES), lambda i: (i, 0))],
        out_shape=[jax.ShapeDtypeStruct((N_SAMPLE, M_WIDTH), BF16),
                   jax.ShapeDtypeStruct((N_BATCH, M_HEADS, DV, DK), F32),
                   jax.ShapeDtypeStruct((N_BATCH, blk), F32),
                   jax.ShapeDtypeStruct((N_SAMPLE, LANES), F32)],
        compiler_params=_params(("arbitrary",)),
        name="mlstm_sample",
    )(gate_bias, zg, m0_tok, n0_tok, z, z, z, z, z, z, state_c, state_n)


def _outproj_kernel(ya_ref, ym_ref, wa_ref, wm_ref, x_ref, gt_ref, o_ref):
    mix = _dot(ya_ref[...], wa_ref[...]) + _dot(ym_ref[...], wm_ref[...])
    o_ref[...] = x_ref[...] + _mod_rows(gt_ref) * mix


def _outproj(ya, ym, w_out_bf, x, mod, per_row):
    n = x.shape[0]
    TM = TM_PROJ
    gate1_col = 2 * (D // TN_OUT)
    return pl.pallas_call(
        _outproj_kernel,
        grid=(n // TM, D // TN_OUT),
        in_specs=[pl.BlockSpec((TM, ATT_WIDTH), lambda i, j: (i, 0)),
                  pl.BlockSpec((TM, M_WIDTH), lambda i, j: (i, 0)),
                  pl.BlockSpec((ATT_WIDTH, TN_OUT), lambda i, j: (0, j)),
                  pl.BlockSpec((M_WIDTH, TN_OUT), lambda i, j: (1, j)),
                  pl.BlockSpec((TM, TN_OUT), lambda i, j: (i, j)),
                  _mod_spec(per_row, TM, TN_OUT, lambda j: gate1_col + j)],
        out_specs=pl.BlockSpec((TM, TN_OUT), lambda i, j: (i, j)),
        out_shape=jax.ShapeDtypeStruct((n, D), F32),
        compiler_params=_params(("arbitrary", "arbitrary")),
        name="outproj",
    )(ya, ym, w_out_bf, w_out_bf, x, mod)


def _router_kernel(x_ref, sh_ref, sc_ref, g_ref, wr_ref, br_ref, h_ref, route_ref):
    x = x_ref[...]
    y = x * lax.rsqrt(jnp.mean(x * x, axis=-1, keepdims=True) + EPS) * g_ref[...]
    h2 = y * (1.0 + _mod_rows(sc_ref)) + _mod_rows(sh_ref)
    h_hi = h2.astype(BF16)
    h_ref[...] = _pack_bf16_pairs(h2)
    h_lo = (h2 - h_hi.astype(F32)).astype(BF16)
    logits = _dot(h_hi, wr_ref[0]) + (_dot(h_hi, wr_ref[1]) + _dot(h_lo, wr_ref[0])) + br_ref[...]
    lane = _iota(logits.shape, 1)

    def first_max(vals):
        vmax = jnp.max(vals, axis=1, keepdims=True)
        idx = jnp.min(jnp.where(vals == vmax, lane, LANES), axis=1, keepdims=True)
        return vmax, idx

    gl = jnp.where(lane < N_GROUPS, logits, NEG_INF)
    gmax, grp = first_max(gl)
    p_grp = 1.0 / jnp.sum(jnp.exp(gl - gmax), axis=1, keepdims=True)
    e_lane = lane - N_GROUPS
    in_grp = jnp.logical_and(e_lane >= 0, jnp.logical_and(e_lane < N_EXPERTS, e_lane // EXP_PER_GROUP == grp))
    el = jnp.where(in_grp, logits, NEG_INF)
    v1, i1 = first_max(el)
    v2, i2 = first_max(jnp.where(lane == i1, NEG_INF, el))
    e2w = jnp.exp(v2 - v1)
    w1 = 1.0 / (1.0 + e2w)
    w2 = e2w / (1.0 + e2w)
    route = jnp.where(lane == 0, (i1 - N_GROUPS).astype(F32),
                      jnp.where(lane == 1, (i2 - N_GROUPS).astype(F32),
                                jnp.where(lane == 2, p_grp * w1, jnp.where(lane == 3, p_grp * w2, 0.0))))
    route_ref[...] = route


def _router_merged_kernel(xp_ref, xs_ref, shp_ref, scp_ref, shs_ref, scs_ref, g_ref, wr_ref, br_ref,
                          h_ref, route_ref):
    i = pl.program_id(0)

    @pl.when(i < PROMPT_TOK_BLOCKS)
    def _():
        _router_kernel(xp_ref, shp_ref, scp_ref, g_ref, wr_ref, br_ref, h_ref, route_ref)

    @pl.when(i >= PROMPT_TOK_BLOCKS)
    def _():
        _router_kernel(xs_ref, shs_ref, scs_ref, g_ref, wr_ref, br_ref, h_ref, route_ref)


def _router(x1_p, x1_s, mod_p, mod_s, g_ffn, w_route, b_route):
    pi, si = _prompt_block, _sample_block
    return pl.pallas_call(
        _router_merged_kernel,
        grid=(N_TOK // TM_TOK,),
        in_specs=[pl.BlockSpec((TM_TOK, D), lambda i: (pi(i), 0)),
                  pl.BlockSpec((TM_TOK, D), lambda i: (si(i), 0)),
                  pl.BlockSpec((1, D), lambda i: (0, 3)),
                  pl.BlockSpec((1, D), lambda i: (0, 4)),
                  pl.BlockSpec((TM_TOK // T_DEC, 1, D), lambda i: (si(i), 0, 3)),
                  pl.BlockSpec((TM_TOK // T_DEC, 1, D), lambda i: (si(i), 0, 4)),
                  pl.BlockSpec((1, D), lambda i: (0, 0)),
                  pl.BlockSpec((2, D, LANES), lambda i: (0, 0, 0)),
                  pl.BlockSpec((1, LANES), lambda i: (0, 0))],
        out_specs=[pl.BlockSpec((TM_TOK, D_PACK), lambda i: (i, 0)),
                   pl.BlockSpec((TM_TOK, LANES), lambda i: (i, 0))],
        out_shape=[jax.ShapeDtypeStruct((N_TOK, D_PACK), jnp.uint32),
                   jax.ShapeDtypeStruct((N_TOK, LANES), F32)],
        compiler_params=_params(("arbitrary",)),
        name="router",
    )(x1_p, x1_s, mod_p, mod_p, mod_s, mod_s, g_ffn, w_route, b_route)


def _rank_kernel(route_ref, dest_ref, cnt_ref, carry, rank_scr):
    pas = pl.program_id(0)
    step = pl.program_id(1)

    @pl.when(jnp.logical_and(pas == 0, step == 0))
    def _():
        carry[...] = jnp.zeros_like(carry)

    route = route_ref[...]
    n = route.shape[0]
    lane = _iota((n, LANES), 1).astype(F32)
    o1 = jnp.where(lane == route[:, 0:1], 1.0, 0.0)
    o2 = jnp.where(lane == route[:, 1:2], 1.0, 0.0)
    lane_i = _iota((n, LANES), 1)
    rows = pl.ds(pl.multiple_of(step * TM_RANK, TM_RANK), TM_RANK)

    @pl.when(pas == 0)
    def _():
        both = o1 + o2
        strict = jnp.where(_iota((n, n), 1) < _iota((n, n), 0), 1.0, 0.0).astype(BF16)
        prior = _dot(strict, both.astype(BF16)) + carry[0:1, :]
        r1 = jnp.sum(o1 * prior, axis=1, keepdims=True)
        r2 = jnp.sum(o2 * prior, axis=1, keepdims=True)
        rank_scr[rows, :] = jnp.where(lane_i == 0, r1, jnp.where(lane_i == 1, r2, 0.0))
        carry[...] = carry[...] + jnp.sum(both, axis=0, keepdims=True)

    @pl.when(pas == 1)
    def _():
        counts = carry[...]
        tiles_per = jnp.floor((counts + (TM_MOE - 1)) * (1.0 / TM_MOE))
        before = jnp.where(_iota((LANES, LANES), 0) < _iota((LANES, LANES), 1), 1.0, 0.0).astype(BF16)
        pad_start = _dot(tiles_per.astype(BF16), before)[0:1, :] * TM_MOE
        rank = rank_scr[rows, :]
        d1 = jnp.sum(o1 * pad_start, axis=1, keepdims=True) + rank[:, 0:1]
        d2 = jnp.sum(o2 * pad_start, axis=1, keepdims=True) + rank[:, 1:2]
        dest_ref[...] = jnp.where(lane_i == 0, d1, jnp.where(lane_i == 1, d2, 0.0)).astype(jnp.int32)
        cnt_ref[...] = counts


def _rank(route):
    n = route.shape[0]
    return pl.pallas_call(
        _rank_kernel,
        grid=(2, n // TM_RANK),
        in_specs=[pl.BlockSpec((TM_RANK, LANES), lambda p, i: (i, 0))],
        out_specs=[pl.BlockSpec((TM_RANK, LANES), lambda p, i: (i * p, 0)),
                   pl.BlockSpec((SUBLANES, LANES), lambda p, i: (0, 0))],
        out_shape=[jax.ShapeDtypeStruct((n, LANES), jnp.int32), jax.ShapeDtypeStruct((SUBLANES, LANES), F32)],
        scratch_shapes=[pltpu.VMEM((SUBLANES, LANES), F32), pltpu.VMEM((n, LANES), F32)],
        compiler_params=_params(("arbitrary", "arbitrary")),
        name="rank",
    )(route)


def _row_copy(src, dst, sem):
    return pltpu.make_async_copy(src, dst, sem)


ROW_UNROLL = 8
D_PACK = D // 2

ZERO_FIRST, ZERO_ANYTIME = 1, 2


def _dispatch_kernel(d1_ref, d2_ref, zc_ref, h_ref, xs_out, zbuf, sem):
    step = pl.program_id(0)
    base = step * TM_TOK

    def zero_tile(t, s):
        rows = pl.ds(pl.multiple_of(t * TM_MOE, TM_MOE), TM_MOE)
        return pltpu.make_async_copy(zbuf, xs_out.at[rows], sem.at[s])

    def for_tiles(cls, s, act):
        def body(t, carry):
            @pl.when(zc_ref[t] == cls)
            def _():
                act(zero_tile(t, s))
            return carry
        lax.fori_loop(0, MAX_TILES, body, 0)

    @pl.when(step == 0)
    def _():
        zbuf[...] = jnp.zeros_like(zbuf)
        for_tiles(ZERO_FIRST, 1, lambda c: c.start())
        for_tiles(ZERO_ANYTIME, 2, lambda c: c.start())
        for_tiles(ZERO_FIRST, 1, lambda c: c.wait())

    @pl.when(step == pl.num_programs(0) - 1)
    def _():
        for_tiles(ZERO_ANYTIME, 2, lambda c: c.wait())

    def issue(r, carry):
        _row_copy(h_ref.at[pl.ds(r, 1)], xs_out.at[pl.ds(d1_ref[base + r], 1)], sem.at[0]).start()
        _row_copy(h_ref.at[pl.ds(r, 1)], xs_out.at[pl.ds(d2_ref[base + r], 1)], sem.at[0]).start()
        return carry

    lax.fori_loop(0, TM_TOK, issue, 0, unroll=ROW_UNROLL)
    for _ in range(TOP_K):
        _row_copy(h_ref, xs_out.at[pl.ds(0, TM_TOK)], sem.at[0]).wait()


def _dispatch(dests, zero_class, h2):
    grid_spec = pltpu.PrefetchScalarGridSpec(
        num_scalar_prefetch=3,
        grid=(N_TOK // TM_TOK,),
        in_specs=[pl.BlockSpec((TM_TOK, D_PACK), lambda i, *_: (i, 0))],
        out_specs=pl.BlockSpec(memory_space=pl.ANY),
        scratch_shapes=[pltpu.VMEM((TM_MOE, D_PACK), jnp.uint32), pltpu.SemaphoreType.DMA((3,))],
    )
    return pl.pallas_call(
        _dispatch_kernel,
        grid_spec=grid_spec,
        out_shape=jax.ShapeDtypeStruct((A_PAD, D_PACK), jnp.uint32),
        compiler_params=_params(("arbitrary",)),
        name="dispatch",
    )(*dests, zero_class, h2)


N_UP_CHUNKS = D_FF // TN_FF


def _expert_up_kernel(te_ref, tv_ref, first_ref, nxt_ref, run_ref, meta_ref, x_ref, wg_hbm, wu_hbm,
                      o_ref, wbuf, sem):
    n = pl.program_id(0)
    t = pl.program_id(1)
    slot = lax.rem(n * meta_ref[1] + run_ref[t], 2)

    def fetch(e, chunk, s):
        cols = pl.ds(pl.multiple_of(chunk * TN_FF, TN_FF), TN_FF)
        return (pltpu.make_async_copy(wg_hbm.at[e, :, cols], wbuf.at[s, 0], sem.at[s, 0]),
                pltpu.make_async_copy(wu_hbm.at[e, :, cols], wbuf.at[s, 1], sem.at[s, 1]))

    @pl.when(first_ref[t] == 1)
    def _():
        @pl.when(jnp.logical_and(n == 0, t == 0))
        def _():
            for c in fetch(te_ref[0], 0, 0):
                c.start()

        for c in fetch(te_ref[t], n, slot):
            c.wait()

        @pl.when(nxt_ref[t] >= 0)
        def _():
            for c in fetch(nxt_ref[t], n, 1 - slot):
                c.start()

        @pl.when(jnp.logical_and(nxt_ref[t] < 0, n + 1 < N_UP_CHUNKS))
        def _():
            for c in fetch(te_ref[0], n + 1, 1 - slot):
                c.start()

    @pl.when(tv_ref[t] == 1)
    def _():
        x = jnp.concatenate([half.astype(BF16) for half in _unpack_bf16_pairs(x_ref[...])], axis=1)
        g = _dot(x, wbuf[slot, 0].astype(BF16))
        u = _dot(x, wbuf[slot, 1].astype(BF16))
        o_ref[...] = (g * _sigmoid(g) * u).astype(BF16)

    @pl.when(tv_ref[t] == 0)
    def _():
        o_ref[...] = jnp.zeros_like(o_ref)


def _expert_up(tiles, xs, w_gate, w_up):
    last = lambda t, meta: jnp.minimum(t, meta[0] - 1)
    grid_spec = pltpu.PrefetchScalarGridSpec(
        num_scalar_prefetch=6,
        grid=(N_UP_CHUNKS, MAX_TILES),
        in_specs=[pl.BlockSpec((TM_MOE, D_PACK), lambda n, t, *s: (last(t, s[5]), 0)),
                  pl.BlockSpec(memory_space=pl.ANY),
                  pl.BlockSpec(memory_space=pl.ANY)],
        out_specs=pl.BlockSpec((TM_MOE, TN_FF), lambda n, t, *s: (t, n)),
        scratch_shapes=[pltpu.VMEM((2, 2, D, TN_FF), F32), pltpu.SemaphoreType.DMA((2, 2))],
    )
    return pl.pallas_call(
        _expert_up_kernel,
        grid_spec=grid_spec,
        out_shape=jax.ShapeDtypeStruct((A_PAD, D_FF), BF16),
        compiler_params=_params(("arbitrary", "arbitrary")),
        name="expert_up",
    )(*tiles, xs, w_gate, w_up)


def _expert_down_kernel(te_ref, tv_ref, first_ref, nxt_ref, run_ref, meta_ref, h_ref, wd_hbm,
                        o_ref, wbuf, sem):
    t = pl.program_id(0)
    slot = lax.rem(run_ref[t], 2)

    def fetch(e, s):
        return pltpu.make_async_copy(wd_hbm.at[e], wbuf.at[s], sem.at[s])

    @pl.when(first_ref[t] == 1)
    def _():
        @pl.when(t == 0)
        def _():
            fetch(te_ref[0], 0).start()

        fetch(te_ref[t], slot).wait()

        @pl.when(nxt_ref[t] >= 0)
        def _():
            fetch(nxt_ref[t], 1 - slot).start()

    @pl.when(tv_ref[t] == 1)
    def _():
        o_ref[...] = _pack_bf16_pairs(_dot(h_ref[...], wbuf[slot].astype(BF16)))

    @pl.when(tv_ref[t] == 0)
    def _():
        o_ref[...] = jnp.zeros_like(o_ref)


def _expert_down(tiles, h1, w_down):
    last = lambda t, meta: jnp.minimum(t, meta[0] - 1)
    grid_spec = pltpu.PrefetchScalarGridSpec(
        num_scalar_prefetch=6,
        grid=(MAX_TILES,),
        in_specs=[pl.BlockSpec((TM_MOE, D_FF), lambda t, *s: (last(t, s[5]), 0)),
                  pl.BlockSpec(memory_space=pl.ANY)],
        out_specs=pl.BlockSpec((TM_MOE, D_PACK), lambda t, *s: (t, 0)),
        scratch_shapes=[pltpu.VMEM((2, D_FF, D), F32), pltpu.SemaphoreType.DMA((2,))],
    )
    return pl.pallas_call(
        _expert_down_kernel,
        grid_spec=grid_spec,
        out_shape=jax.ShapeDtypeStruct((A_PAD, D_PACK), jnp.uint32),
        compiler_params=_params(("arbitrary",)),
        name="expert_down",
    )(*tiles, h1, w_down)


def _combine_kernel(d1_ref, d2_ref, x_ref, route_ref, gt_ref, gf_ref, o_hbm,
                    y_ref, buf, sem, *, tok_offset):
    step = pl.program_id(0)
    slot = lax.rem(step, 2)

    def gather(s, into):
        base = tok_offset + s * TM_TOK

        def issue(r, carry):
            _row_copy(o_hbm.at[pl.ds(d1_ref[base + r], 1)], buf.at[into, 0, pl.ds(r, 1)], sem.at[into]).start()
            _row_copy(o_hbm.at[pl.ds(d2_ref[base + r], 1)], buf.at[into, 1, pl.ds(r, 1)], sem.at[into]).start()
            return carry

        lax.fori_loop(0, TM_TOK, issue, 0, unroll=ROW_UNROLL)

    @pl.when(step == 0)
    def _():
        gather(0, 0)

    @pl.when(step + 1 < pl.num_programs(0))
    def _():
        gather(step + 1, 1 - slot)

    for k in range(TOP_K):
        _row_copy(o_hbm.at[pl.ds(0, TM_TOK)], buf.at[slot, k], sem.at[slot]).wait()

    route = route_ref[...]
    lo1, hi1 = _unpack_bf16_pairs(buf[slot, 0])
    lo2, hi2 = _unpack_bf16_pairs(buf[slot, 1])
    g1, g2 = route[:, 2:3], route[:, 3:4]
    moe = jnp.concatenate([g1 * lo1 + g2 * lo2, g1 * hi1 + g2 * hi2], axis=1)
    x2 = x_ref[...] + _mod_rows(gt_ref) * moe
    y_ref[...] = x2 * lax.rsqrt(jnp.mean(x2 * x2, axis=-1, keepdims=True) + EPS) * gf_ref[...]


def _combine(dests, x1, route, mod, per_row, g_final, o_rows, tok_offset):
    n = x1.shape[0]
    off = tok_offset // TM_TOK
    grid_spec = pltpu.PrefetchScalarGridSpec(
        num_scalar_prefetch=2,
        grid=(n // TM_TOK,),
        in_specs=[pl.BlockSpec((TM_TOK, D), lambda i, *_: (i, 0)),
                  pl.BlockSpec((TM_TOK, LANES), lambda i, *_: (i + off, 0)),
                  _mod_spec(per_row, TM_TOK, D, lambda *_: 5),
                  pl.BlockSpec((1, D), lambda i, *_: (0, 0)),
                  pl.BlockSpec(memory_space=pl.ANY)],
        out_specs=pl.BlockSpec((TM_TOK, D), lambda i, *_: (i, 0)),
        scratch_shapes=[pltpu.VMEM((2, TOP_K, TM_TOK, D_PACK), jnp.uint32), pltpu.SemaphoreType.DMA((2,))],
    )
    return pl.pallas_call(
        functools.partial(_combine_kernel, tok_offset=tok_offset),
        grid_spec=grid_spec,
        out_shape=jax.ShapeDtypeStruct((n, D), F32),
        compiler_params=_params(("arbitrary",)),
        name="combine",
    )(*dests, x1, route, mod, g_final, o_rows)


def kernel(x_prompt, x_sample, cache_k, cache_v, state_C, state_n, state_m, c_prompt, c_sample, rel_bias, w_ada, b_ada, g_mix, g_ffn, w_in, sinks, b_igate, b_fgate, w_out, w_router_grp, b_router_grp, w_router_exp, b_router_exp, w_gate, w_up, w_down, g_final):
    xp = x_prompt.reshape(N_PROMPT, D)
    xs = x_sample.reshape(N_SAMPLE, D)

    c_all = jnp.concatenate([c_prompt, c_sample, jnp.zeros((C_ROWS - 1 - N_BATCH, D), F32)], axis=0)
    mod = _ada(c_all, w_ada[0], b_ada)
    mod_p = mod[0:1]
    mod_s = mod[1:1 + N_BATCH].reshape(N_BATCH, 1, -1)

    w_gates = jnp.pad(w_in[0, :, Z_WIDTH:].astype(BF16), ((0, 0), (0, LANES - 2 * M_HEADS)))
    h_all, zg = _norm(xp, xs, mod_p, mod_s, g_mix, w_gates)
    z = _inproj(h_all, w_in[0])

    rb_flat = rel_bias.reshape(NUM_BUCKETS * ATT_HEADS)
    sink_v = sinks[0]
    ya_p = _swa_prompt(z, rb_flat, sink_v)
    ya_s, nk_s, nv_s = _swa_sample(z,cache_k.reshape(N_BATCH, CACHE_ROWS, HEAD_DIM),
                                   cache_v.reshape(N_BATCH, CACHE_ROWS, HEAD_DIM), rb_flat, sink_v)

    gate_bias = jnp.concatenate([b_igate[0], b_fgate[0], jnp.zeros((LANES - 2 * M_HEADS,), F32)]).reshape(1, LANES)
    ym_p, c_p, n_p, m_p = _mlstm_prompt(z, zg, gate_bias)
    m0_tok = jnp.pad(jnp.repeat(state_m[0], T_DEC, axis=0), ((0, 0), (0, LANES - M_HEADS)))
    n0_flat = state_n[0].reshape(N_BATCH, M_HEADS * DK)
    n0_tok = jnp.repeat(n0_flat, T_DEC, axis=0)
    ym_s, c_s, n_s, m_s = _mlstm_sample(z, zg, gate_bias, m0_tok, n0_tok, state_C[0], n0_flat)

    w_out_bf = w_out[0].astype(BF16)
    x1_p = _outproj(ya_p, ym_p, w_out_bf, xp, mod_p, False)
    x1_s = _outproj(ya_s, ym_s, w_out_bf, xs, mod_s, True)

    w_route = jnp.pad(jnp.concatenate([w_router_grp[0], w_router_exp[0]], axis=1),
                      ((0, 0), (0, LANES - N_GROUPS - N_EXPERTS)))
    b_route = jnp.pad(jnp.concatenate([b_router_grp[0], b_router_exp[0]]),
                      (0, LANES - N_GROUPS - N_EXPERTS)).reshape(1, LANES)
    w_route_hi = w_route.astype(BF16)
    w_route_split = jnp.stack([w_route_hi, (w_route - w_route_hi.astype(F32)).astype(BF16)])
    h2, route = _router(x1_p, x1_s, mod_p, mod_s, g_ffn, w_route_split, b_route)

    dest, counts = _rank(route)
    dests = (dest[:, 0], dest[:, 1])

    i32 = lambda a: a.astype(jnp.int32)
    cnt = i32(counts[0, :N_EXPERTS])
    tiles_per = (cnt + TM_MOE - 1) // TM_MOE
    tile_end = jnp.cumsum(tiles_per)
    n_tiles = tile_end[-1]
    tile_ids = jnp.arange(MAX_TILES, dtype=jnp.int32)
    tile_expert = i32(jnp.minimum(jnp.searchsorted(tile_end, tile_ids, side="right"), N_EXPERTS - 1))
    last_expert = tile_expert[jnp.maximum(n_tiles - 1, 0)]
    tile_valid = tile_ids < n_tiles
    tile_expert = jnp.where(tile_valid, tile_expert, last_expert)
    prev_expert = jnp.concatenate([jnp.full((1,), -1, jnp.int32), tile_expert[:-1]])
    run_first = jnp.logical_and(tile_valid, tile_expert != prev_expert)
    run_id = jnp.maximum(jnp.cumsum(i32(run_first)) - 1, 0)
    expert_ids = jnp.arange(N_EXPERTS, dtype=jnp.int32)
    used = jnp.where(tiles_per > 0, expert_ids, N_EXPERTS)
    next_used = jnp.concatenate([lax.cummin(used[::-1])[::-1][1:], jnp.full((1,), N_EXPERTS, jnp.int32)])
    next_used = jnp.where(next_used >= N_EXPERTS, -1, next_used)
    tiles = (tile_expert, i32(tile_valid), i32(run_first), next_used[tile_expert], i32(run_id),
             jnp.stack([n_tiles, jnp.sum(i32(run_first))]).astype(jnp.int32))
    next_expert = jnp.concatenate([tile_expert[1:], jnp.full((1,), -1, jnp.int32)])
    run_last = jnp.logical_or(tile_expert != next_expert, tile_ids == n_tiles - 1)
    zero_class = jnp.where(tile_valid, jnp.where(run_last, ZERO_FIRST, 0), ZERO_ANYTIME)
    xs_rows = _dispatch(dests, i32(zero_class), h2)
    h1 = _expert_up(tiles, xs_rows, w_gate[0], w_up[0])
    o_rows = _expert_down(tiles, h1, w_down[0])

    gf = g_final.reshape(1, D)
    y_p = _combine(dests, x1_p, route, mod_p, False, gf, o_rows, 0)
    y_s = _combine(dests, x1_s, route, mod_s, True, gf, o_rows, N_PROMPT)

    kv5 = lambda a: a.reshape(1, -1, WINDOW, KV_HEADS, HEAD_DIM)
    kcol = ATT_WIDTH
    nk_p = z[N_PROMPT - WINDOW:N_PROMPT, kcol:kcol + KV_WIDTH]
    nv_p = z[N_PROMPT - WINDOW:N_PROMPT, kcol + KV_WIDTH:kcol + 2 * KV_WIDTH]
    return (y_p.reshape(1, N_PROMPT, D), y_s.reshape(N_BATCH, T_DEC, D),
            kv5(nk_p), kv5(nv_p),
            c_p.reshape(1, 1, M_HEADS, DV, DK), n_p.reshape(1, 1, M_HEADS, DK), m_p[:, 0].reshape(1, 1, M_HEADS),
            kv5(nk_s), kv5(nv_s),
            c_s.reshape(1, N_BATCH, M_HEADS, DV, DK), n_s.reshape(1, N_BATCH, M_HEADS, DK),
            m_s.reshape(N_BATCH, T_DEC, LANES)[:, T_DEC - 1, :M_HEADS].reshape(1, N_BATCH, M_HEADS))
```

```python
import functools
import math

import numpy as np
import jax
import jax.numpy as jnp
from jax import lax
from jax.experimental import pallas as pl
from jax.experimental.pallas import tpu as pltpu

F32 = jnp.float32
BF16 = jnp.bfloat16
NEG_INF = float("-inf")

D = 4096
N_PROMPT = 8192
N_BATCH = 128
T_DEC = 8
N_SAMPLE = N_BATCH * T_DEC
N_TOK = N_PROMPT + N_SAMPLE
HEAD_DIM = 128
ATT_HEADS = 16
KV_HEADS = 4
GROUP = ATT_HEADS // KV_HEADS
WINDOW = 128
ATT_WIDTH = ATT_HEADS * HEAD_DIM
KV_WIDTH = KV_HEADS * HEAD_DIM
NUM_BUCKETS = 32
MAX_EXACT = 16
MAX_DISTANCE = 128
M_HEADS = 8
DK = 128
DV = 256
M_WIDTH = M_HEADS * DV
Z_WIDTH = ATT_WIDTH + 2 * KV_WIDTH + 2 * M_HEADS * DK + 2 * M_WIDTH
N_GROUPS = 4
EXP_PER_GROUP = 8
N_EXPERTS = N_GROUPS * EXP_PER_GROUP
TOP_K = 2
D_FF = 1024
EPS = 1e-6
ATT_SCALE = HEAD_DIM ** -0.5
Q_SCALE = DK ** -0.5

LANES = 128
SUBLANES = 8
VMEM_LIMIT = 56 * 1024 * 1024

TM_PROJ = 512
TN_IN = 1024
TN_OUT = 1024
TN_ADA = 512
C_ROWS = 136
ATT_BLOCK = 128
SAMPLE_BT = 8
ML_CHUNK = 256
TM_MOE = 256
TN_FF = 512
N_ASSIGN = N_TOK * TOP_K
MAX_TILES = N_ASSIGN // TM_MOE + N_EXPERTS
A_PAD = MAX_TILES * TM_MOE
TM_TOK = 256
TM_RANK = 1024


def _params(sem):
    return pltpu.CompilerParams(dimension_semantics=sem, vmem_limit_bytes=VMEM_LIMIT)


def _iota(shape, dim):
    return lax.broadcasted_iota(jnp.int32, shape, dim)


def _dot(a, b):
    return jnp.dot(a, b, preferred_element_type=F32)


def _dot_nt(a, b):
    return lax.dot_general(a, b, (((1,), (1,)), ((), ())), preferred_element_type=F32)


def _dot_tn(a, b):
    return lax.dot_general(a, b, (((0,), (0,)), ((), ())), preferred_element_type=F32)


def _split3(x):
    x1 = x.astype(BF16)
    r1 = x - x1.astype(F32)
    x2 = r1.astype(BF16)
    r2 = r1 - x2.astype(F32)
    return x1, x2, r2.astype(BF16)


def _dot_exact_lhs01(a01, x):
    x1, x2, x3 = _split3(x)
    return _dot(a01, x1) + _dot(a01, x2) + _dot(a01, x3)


def _pack_bf16_pairs(x):
    w = x.shape[1] // 2
    bits = lax.bitcast_convert_type(x.astype(BF16).astype(F32), jnp.uint32)
    return (bits[:, :w] >> 16) | (bits[:, w:] & jnp.uint32(0xFFFF0000))


def _unpack_bf16_pairs(words):
    return (lax.bitcast_convert_type(words << 16, F32),
            lax.bitcast_convert_type(words & jnp.uint32(0xFFFF0000), F32))


def _sigmoid(x):
    return 1.0 / (1.0 + jnp.exp(-x))


def _log_sigmoid(x):
    return jnp.minimum(x, 0.0) - jnp.log(1.0 + jnp.exp(-jnp.abs(x)))


def _mod_spec(per_batch, rows, width, col):
    if per_batch:
        return pl.BlockSpec((rows // T_DEC, 1, width), lambda i, *rest: (i, 0, col(*rest)))
    return pl.BlockSpec((1, width), lambda i, *rest: (0, col(*rest)))


def _mod_rows(ref):
    v = ref[...]
    if v.ndim == 2:
        return v
    nb, _, width = v.shape
    return jnp.broadcast_to(v, (nb, T_DEC, width)).reshape(nb * T_DEC, width)


def _ada_kernel(c_ref, w_ref, b_ref, o_ref):
    c = c_ref[...]
    s = (c * _sigmoid(c)).astype(BF16)
    o_ref[...] = _dot(s, w_ref[...].astype(BF16)) + b_ref[...]


def _ada(c_all, w_ada, b_ada):
    n = w_ada.shape[1]
    return pl.pallas_call(
        _ada_kernel,
        grid=(n // TN_ADA,),
        in_specs=[pl.BlockSpec((C_ROWS, D), lambda j: (0, 0)),
                  pl.BlockSpec((D, TN_ADA), lambda j: (0, j)),
                  pl.BlockSpec((1, TN_ADA), lambda j: (0, j))],
        out_specs=pl.BlockSpec((C_ROWS, TN_ADA), lambda j: (0, j)),
        out_shape=jax.ShapeDtypeStruct((C_ROWS, n), F32),
        compiler_params=_params(("arbitrary",)),
        name="ada",
    )(c_all, w_ada, b_ada)


PROMPT_TOK_BLOCKS = N_PROMPT // TM_TOK


def _prompt_block(i):
    return jnp.minimum(i, PROMPT_TOK_BLOCKS - 1)


def _sample_block(i):
    return jnp.maximum(i - PROMPT_TOK_BLOCKS, 0)


def _norm_body(x_ref, sh_ref, sc_ref, g_ref, wg_ref, h_ref, zg_ref):
    x = x_ref[...]
    y = x * lax.rsqrt(jnp.mean(x * x, axis=-1, keepdims=True) + EPS) * g_ref[...]
    hb = (y * (1.0 + _mod_rows(sc_ref)) + _mod_rows(sh_ref)).astype(BF16)
    h_ref[...] = hb
    zg_ref[...] = _dot_nt(hb, wg_ref[...])


def _norm_kernel(xp_ref, xs_ref, shp_ref, scp_ref, shs_ref, scs_ref, g_ref, wg_ref, h_ref, zg_ref):
    i = pl.program_id(0)

    @pl.when(i < PROMPT_TOK_BLOCKS)
    def _():
        _norm_body(xp_ref, shp_ref, scp_ref, g_ref, wg_ref, h_ref, zg_ref)

    @pl.when(i >= PROMPT_TOK_BLOCKS)
    def _():
        _norm_body(xs_ref, shs_ref, scs_ref, g_ref, wg_ref, h_ref, zg_ref)


def _norm(xp, xs, mod_p, mod_s, g_mix, w_gate):
    per_batch = lambda col: pl.BlockSpec((TM_TOK // T_DEC, 1, D), lambda i: (_sample_block(i), 0, col))
    return pl.pallas_call(
        _norm_kernel,
        grid=(N_TOK // TM_TOK,),
        in_specs=[pl.BlockSpec((TM_TOK, D), lambda i: (_prompt_block(i), 0)),
                  pl.BlockSpec((TM_TOK, D), lambda i: (_sample_block(i), 0)),
                  pl.BlockSpec((1, D), lambda i: (0, 0)),
                  pl.BlockSpec((1, D), lambda i: (0, 1)),
                  per_batch(0), per_batch(1),
                  pl.BlockSpec((1, D), lambda i: (0, 0)),
                  pl.BlockSpec((LANES, D), lambda i: (0, 0))],
        out_specs=[pl.BlockSpec((TM_TOK, D), lambda i: (i, 0)),
                   pl.BlockSpec((TM_TOK, LANES), lambda i: (i, 0))],
        out_shape=[jax.ShapeDtypeStruct((N_TOK, D), BF16), jax.ShapeDtypeStruct((N_TOK, LANES), F32)],
        compiler_params=_params(("arbitrary",)),
        name="norm",
    )(xp, xs, mod_p, mod_p, mod_s, mod_s, g_mix, w_gate)


N_IN_CHUNKS = Z_WIDTH // TN_IN


def _inproj_kernel(h_ref, wt_hbm, z_ref, stage, w_bf, sem):
    j = pl.program_id(0)

    def fetch(chunk):
        rows = pl.ds(pl.multiple_of(chunk * TN_IN, TN_IN), TN_IN)
        return pltpu.make_async_copy(wt_hbm.at[rows], stage, sem.at[0])

    @pl.when(pl.program_id(1) == 0)
    def _():
        @pl.when(j == 0)
        def _():
            fetch(0).start()

        fetch(j).wait()
        w_bf[...] = stage[...].astype(BF16)

        @pl.when(j + 1 < N_IN_CHUNKS)
        def _():
            fetch(j + 1).start()

    z_ref[...] = _dot_nt(h_ref[...], w_bf[...])


def _inproj(h, w_in_t):
    return pl.pallas_call(
        _inproj_kernel,
        grid=(N_IN_CHUNKS, N_TOK // TM_PROJ),
        in_specs=[pl.BlockSpec((TM_PROJ, D), lambda j, i: (i, 0)),
                  pl.BlockSpec(memory_space=pl.ANY)],
        out_specs=pl.BlockSpec((TM_PROJ, TN_IN), lambda j, i: (i, j)),
        out_shape=jax.ShapeDtypeStruct((N_TOK, Z_WIDTH), F32),
        scratch_shapes=[pltpu.VMEM((TN_IN, D), F32), pltpu.VMEM((TN_IN, D), BF16), pltpu.SemaphoreType.DMA((1,))],
        compiler_params=_params(("arbitrary", "arbitrary")),
        name="inproj",
    )(h, w_in_t)


def _t5_bucket_np(dist):
    n = np.maximum(dist, 0)
    nf = np.maximum(n, 1).astype(np.float32)
    large = MAX_EXACT + (np.log(nf / MAX_EXACT) / math.log(MAX_DISTANCE / MAX_EXACT)
                         * (NUM_BUCKETS - MAX_EXACT)).astype(np.int32)
    large = np.minimum(large, NUM_BUCKETS - 1)
    return np.where(n < MAX_EXACT, n, large).astype(np.int32)


def _bucket_table(n_q, n_keys_valid, n_keys_padded):
    t = np.arange(n_q)[:, None]
    j = np.arange(n_keys_padded)[None, :]
    dist = t + WINDOW - j
    valid = (dist >= 0) & (dist < WINDOW) & (j < n_keys_valid)
    return np.where(valid, _t5_bucket_np(dist), -1).astype(np.int32)


def _fill_bias(bucket_ref, rb_ref, bias_scr, rows, per_head_table=False):
    for h in range(ATT_HEADS):
        bk = bucket_ref[h * rows:(h + 1) * rows, :] if per_head_table else bucket_ref[...]
        acc = jnp.full(bk.shape, NEG_INF, F32)
        for b in range(NUM_BUCKETS):
            acc = jnp.where(bk == b, rb_ref[b * ATT_HEADS + h], acc)
        bias_scr[h * rows:(h + 1) * rows, :] = acc


def _with_ones(v2):
    return jnp.concatenate([v2, jnp.ones_like(v2)], axis=1)


def _sink_softmax_av(lg, sink, v2_ones):
    m = jnp.maximum(jnp.max(lg, axis=-1, keepdims=True), sink)
    p = jnp.exp(lg - m).astype(BF16)
    pv = _dot(p, v2_ones)
    return pv[:, :HEAD_DIM] / (pv[:, HEAD_DIM:] + jnp.exp(sink - m))


def _swa_prompt_kernel(rb_ref, sink_ref, bucket_ref, q_ref, kp_ref, kc_ref, vp_ref, vc_ref,
                       o_ref, bias_scr):
    i = pl.program_id(0)

    @pl.when(i == 0)
    def _():
        _fill_bias(bucket_ref, rb_ref, bias_scr, ATT_BLOCK)

    first_prev = jnp.logical_and(i == 0, _iota((ATT_BLOCK, 2 * ATT_BLOCK), 1) < ATT_BLOCK)
    for g in range(KV_HEADS):
        ks = slice(g * HEAD_DIM, (g + 1) * HEAD_DIM)
        k2 = jnp.concatenate([kp_ref[:, ks], kc_ref[:, ks]], axis=0).astype(BF16)
        v2 = _with_ones(jnp.concatenate([vp_ref[:, ks], vc_ref[:, ks]], axis=0).astype(BF16))
        for r in range(GROUP):
            h = g * GROUP + r
            hs = slice(h * HEAD_DIM, (h + 1) * HEAD_DIM)
            lg = _dot_nt(q_ref[:, hs].astype(BF16), k2) * ATT_SCALE + bias_scr[h * ATT_BLOCK:(h + 1) * ATT_BLOCK, :]
            lg = jnp.where(first_prev, NEG_INF, lg)
            o_ref[:, hs] = _sink_softmax_av(lg, sink_ref[h], v2).astype(BF16)


def _swa_prompt(z, rb_flat, sinks):
    nb = N_PROMPT // ATT_BLOCK
    bucket = jnp.asarray(_bucket_table(ATT_BLOCK, 2 * ATT_BLOCK, 2 * ATT_BLOCK))
    kcol = ATT_WIDTH // KV_WIDTH
    prev = lambda i: jnp.maximum(i - 1, 0)
    smem = pl.BlockSpec(memory_space=pltpu.SMEM)
    return pl.pallas_call(
        _swa_prompt_kernel,
        grid=(nb,),
        in_specs=[smem, smem,
                  pl.BlockSpec((ATT_BLOCK, 2 * ATT_BLOCK), lambda i: (0, 0)),
                  pl.BlockSpec((ATT_BLOCK, ATT_WIDTH), lambda i: (i, 0)),
                  pl.BlockSpec((ATT_BLOCK, KV_WIDTH), lambda i: (prev(i), kcol)),
                  pl.BlockSpec((ATT_BLOCK, KV_WIDTH), lambda i: (i, kcol)),
                  pl.BlockSpec((ATT_BLOCK, KV_WIDTH), lambda i: (prev(i), kcol + 1)),
                  pl.BlockSpec((ATT_BLOCK, KV_WIDTH), lambda i: (i, kcol + 1))],
        out_specs=pl.BlockSpec((ATT_BLOCK, ATT_WIDTH), lambda i: (i, 0)),
        out_shape=jax.ShapeDtypeStruct((N_PROMPT, ATT_WIDTH), BF16),
        scratch_shapes=[pltpu.VMEM((ATT_HEADS * ATT_BLOCK, 2 * ATT_BLOCK), F32)],
        compiler_params=_params(("arbitrary",)),
        name="swa_prompt",
    )(rb_flat, sinks, bucket, z, z, z, z, z)


CACHE_ROWS = WINDOW * KV_HEADS
NEW_ROWS = T_DEC * KV_HEADS
S_KEYS = 5 * LANES


def _sample_bucket_table():
    t = np.arange(T_DEC)[:, None]
    col = np.arange(S_KEYS)[None, :]
    in_cache = col < CACHE_ROWS
    in_new = (col >= CACHE_ROWS) & (col < CACHE_ROWS + NEW_ROWS)
    key_head = np.where(in_cache, col % KV_HEADS, (col - CACHE_ROWS) // T_DEC)
    key_pos = np.where(in_cache, col // KV_HEADS, WINDOW + (col - CACHE_ROWS) % T_DEC)
    dist = t + WINDOW - key_pos
    valid = (dist >= 0) & (dist < WINDOW) & (in_cache | in_new)
    per_query = np.where(valid, _t5_bucket_np(dist), -1)
    heads = np.arange(ATT_HEADS)[:, None, None] // GROUP
    table = np.where(heads == key_head[None], per_query[None], -1)
    return table.reshape(ATT_HEADS * T_DEC, S_KEYS).astype(np.int32)


def _swa_sample_kernel(rb_ref, sink_ref, bucket_ref, q_ref, kn_ref, vn_ref, ck_ref, cv_ref,
                       o_ref, nk_ref, nv_ref, bias_scr):
    @pl.when(pl.program_id(0) == 0)
    def _():
        _fill_bias(bucket_ref, rb_ref, bias_scr, T_DEC, per_head_table=True)

    nk_ref[:, 0:CACHE_ROWS - NEW_ROWS, :] = ck_ref[:, NEW_ROWS:CACHE_ROWS, :]
    nv_ref[:, 0:CACHE_ROWS - NEW_ROWS, :] = cv_ref[:, NEW_ROWS:CACHE_ROWS, :]

    pad = jnp.zeros((S_KEYS - CACHE_ROWS - NEW_ROWS, HEAD_DIM), F32)
    sink_col = jnp.concatenate([jnp.full((T_DEC, 1), sink_ref[h], F32) for h in range(ATT_HEADS)], axis=0)
    bias = bias_scr[...]
    for b in range(SAMPLE_BT):
        ts = slice(b * T_DEC, (b + 1) * T_DEC)
        head_cols = lambda ref, n: [ref[ts, h * HEAD_DIM:(h + 1) * HEAD_DIM] for h in range(n)]
        k_new, v_new = head_cols(kn_ref, KV_HEADS), head_cols(vn_ref, KV_HEADS)
        for g in range(KV_HEADS):
            new_rows = pl.ds(CACHE_ROWS - NEW_ROWS + g, T_DEC, stride=KV_HEADS)
            nk_ref[b, new_rows, :] = k_new[g]
            nv_ref[b, new_rows, :] = v_new[g]
        qa = jnp.concatenate(head_cols(q_ref, ATT_HEADS), axis=0).astype(BF16)
        k2 = jnp.concatenate([ck_ref[b]] + k_new + [pad], axis=0).astype(BF16)
        v2 = _with_ones(jnp.concatenate([cv_ref[b]] + v_new + [pad], axis=0).astype(BF16))
        o = _sink_softmax_av(_dot_nt(qa, k2) * ATT_SCALE + bias, sink_col, v2)
        for h in range(ATT_HEADS):
            o_ref[ts, h * HEAD_DIM:(h + 1) * HEAD_DIM] = o[h * T_DEC:(h + 1) * T_DEC].astype(BF16)


def _swa_sample(z, cache_k, cache_v, rb_flat, sinks):
    rows = SAMPLE_BT * T_DEC
    z_off = N_PROMPT // rows
    bucket = jnp.asarray(_sample_bucket_table())
    kcol = ATT_WIDTH // KV_WIDTH
    smem = pl.BlockSpec(memory_space=pltpu.SMEM)
    cache_spec = pl.BlockSpec((SAMPLE_BT, CACHE_ROWS, HEAD_DIM), lambda i: (i, 0, 0))
    cache_shape = jax.ShapeDtypeStruct((N_BATCH, CACHE_ROWS, HEAD_DIM), F32)
    return pl.pallas_call(
        _swa_sample_kernel,
        grid=(N_BATCH // SAMPLE_BT,),
        in_specs=[smem, smem,
                  pl.BlockSpec((ATT_HEADS * T_DEC, S_KEYS), lambda i: (0, 0)),
                  pl.BlockSpec((rows, ATT_WIDTH), lambda i: (i + z_off, 0)),
                  pl.BlockSpec((rows, KV_WIDTH), lambda i: (i + z_off, kcol)),
                  pl.BlockSpec((rows, KV_WIDTH), lambda i: (i + z_off, kcol + 1)),
                  cache_spec, cache_spec],
        out_specs=[pl.BlockSpec((rows, ATT_WIDTH), lambda i: (i, 0)), cache_spec, cache_spec],
        out_shape=[jax.ShapeDtypeStruct((N_SAMPLE, ATT_WIDTH), BF16), cache_shape, cache_shape],
        scratch_shapes=[pltpu.VMEM((ATT_HEADS * T_DEC, S_KEYS), F32)],
        compiler_params=_params(("arbitrary",)),
        name="swa_sample",
    )(rb_flat, sinks, bucket, z, z, z, cache_k, cache_v)


LANE_IG, LANE_LF, LANE_B = 0, M_HEADS, 2 * M_HEADS


def _gate_table(zg, gate_bias, seg_len):
    L = zg.shape[0]
    g = zg + gate_bias
    lane = _iota((L, LANES), 1)
    lf = _log_sigmoid(g)
    lf_only = jnp.where(jnp.logical_and(lane >= LANE_LF, lane < LANE_B), lf, 0.0)
    row = _iota((L, L), 0)
    col = _iota((L, L), 1)
    same_seg = (row // seg_len) == (col // seg_len)
    tril = jnp.where(jnp.logical_and(col <= row, same_seg), 1.0, 0.0).astype(BF16)
    cum = pltpu.roll(_dot_exact_lhs01(tril, lf_only), M_HEADS, axis=1)
    table = jnp.where(lane < LANE_LF, g, jnp.where(lane < LANE_B, lf, jnp.where(lane < LANE_B + M_HEADS, cum, 0.0)))
    return table, jnp.logical_and(col <= row, same_seg)


def _mlstm_intra(table, table_t, mask, m0_col, h, q, k, v):
    b_c = table[:, LANE_B + h:LANE_B + h + 1]
    b_r = table_t[LANE_B + h:LANE_B + h + 1, :]
    ig_r = table_t[LANE_IG + h:LANE_IG + h + 1, :]
    log_d = jnp.where(mask, b_c - b_r + ig_r, NEG_INF)
    log_inter = b_c + m0_col
    m_t = jnp.maximum(log_inter, jnp.max(log_d, axis=1, keepdims=True))
    d = jnp.exp(log_d - m_t)
    w_inter = jnp.exp(log_inter - m_t)
    s = _dot_nt(q.astype(BF16), k.astype(BF16)) * d
    num_intra = _dot(s.astype(BF16), v.astype(BF16))
    den_intra = jnp.sum(s, axis=1, keepdims=True)
    return b_c, m_t, w_inter, num_intra, den_intra


def _mlstm_prompt_kernel(gb_ref, zg_ref, q_ref, k_ref, va_ref, vb_ref, oa_ref, ob_ref,
                         y_ref, c_out, n_out, m_out, c_scr, n_scr, m_scr):
    step = pl.program_id(0)
    L = ML_CHUNK

    @pl.when(step == 0)
    def _():
        c_scr[...] = jnp.zeros_like(c_scr)
        n_scr[...] = jnp.zeros_like(n_scr)
        m_scr[...] = jnp.zeros_like(m_scr)

    table, mask = _gate_table(zg_ref[...], gb_ref[...], L)
    table_t = table.T
    half = M_HEADS // 2
    for h in range(M_HEADS):
        v_ref, o_ref = (va_ref, oa_ref) if h < half else (vb_ref, ob_ref)
        vs = slice((h % half) * DV, (h % half + 1) * DV)
        q = q_ref[:, h * DK:(h + 1) * DK] * Q_SCALE
        k = k_ref[:, h * DK:(h + 1) * DK]
        v = v_ref[:, vs]
        m0 = m_scr[h:h + 1, 0:1]
        b_c, m_t, w_inter, num_intra, den_intra = _mlstm_intra(table, table_t, mask, m0, h, q, k, v)
        c_old = c_scr[h]
        n_old = n_scr[h:h + 1, :]
        num = num_intra + w_inter * _dot_nt(q.astype(BF16), c_old.astype(BF16))
        den = den_intra + w_inter * jnp.sum(q * n_old, axis=1, keepdims=True)
        hh = num / jnp.maximum(jnp.abs(den), jnp.exp(-m_t))
        y_ref[:, h * DV:(h + 1) * DV] = (_sigmoid(o_ref[:, vs]) * hh).astype(BF16)

        ig_c = table[:, LANE_IG + h:LANE_IG + h + 1]
        m_new = m_t[L - 1:L, :]
        b_last = b_c[L - 1:L, :]
        w_s = jnp.exp(b_last - b_c + ig_c - m_new)
        decay = jnp.exp(b_last + m0 - m_new)
        c_scr[h] = decay * c_old + _dot_tn((w_s * v).astype(BF16), k.astype(BF16))
        n_scr[h:h + 1, :] = decay * n_old + jnp.sum(w_s * k, axis=0, keepdims=True)
        m_scr[h:h + 1, :] = jnp.broadcast_to(m_new, (1, LANES))

    @pl.when(step == pl.num_programs(0) - 1)
    def _():
        c_out[...] = c_scr[...]
        n_out[...] = n_scr[...]
        m_out[...] = m_scr[...]


def _mlstm_prompt(z, zg, gate_bias):
    L = ML_CHUNK
    blk = M_HEADS * DK
    col = lambda c: pl.BlockSpec((L, blk), lambda i: (i, c))
    const = lambda shape: pl.BlockSpec(shape, lambda i: tuple(0 for _ in shape))
    return pl.pallas_call(
        _mlstm_prompt_kernel,
        grid=(N_PROMPT // L,),
        in_specs=[const((1, LANES)),
                  pl.BlockSpec((L, LANES), lambda i: (i, 0)),
                  col(3), col(4), col(5), col(6), col(7), col(8)],
        out_specs=[pl.BlockSpec((L, M_WIDTH), lambda i: (i, 0)),
                   const((M_HEADS, DV, DK)), const((M_HEADS, DK)), const((M_HEADS, LANES))],
        out_shape=[jax.ShapeDtypeStruct((N_PROMPT, M_WIDTH), BF16),
                   jax.ShapeDtypeStruct((M_HEADS, DV, DK), F32),
                   jax.ShapeDtypeStruct((M_HEADS, DK), F32),
                   jax.ShapeDtypeStruct((M_HEADS, LANES), F32)],
        scratch_shapes=[pltpu.VMEM((M_HEADS, DV, DK), F32),
                        pltpu.VMEM((M_HEADS, DK), F32),
                        pltpu.VMEM((M_HEADS, LANES), F32)],
        compiler_params=_params(("arbitrary",)),
        name="mlstm_prompt",
    )(gate_bias, zg, z, z, z, z, z, z)


S_ROWS = SAMPLE_BT * T_DEC


def _mlstm_sample_kernel(gb_ref, zg_ref, m0_ref, n0t_ref, q_ref, k_ref, va_ref, vb_ref, oa_ref, ob_ref,
                         c0_ref, n0_ref, y_ref, c_out, n_out, m_out):
    L = S_ROWS
    table, mask = _gate_table(zg_ref[...], gb_ref[...], T_DEC)
    table_t = jnp.concatenate([table, jnp.zeros((LANES - L, LANES), F32)], axis=0).T[:, 0:L]
    m0_all = m0_ref[...]
    lane = _iota((L, LANES), 1)
    row_b = _iota((SAMPLE_BT, L), 1) // T_DEC
    seg_sum = jnp.where(row_b == _iota((SAMPLE_BT, L), 0), 1.0, 0.0)
    m_tok = jnp.zeros((L, LANES), F32)
    half = M_HEADS // 2
    for h in range(M_HEADS):
        v_ref, o_ref = (va_ref, oa_ref) if h < half else (vb_ref, ob_ref)
        vs = slice((h % half) * DV, (h % half + 1) * DV)
        q = q_ref[:, h * DK:(h + 1) * DK] * Q_SCALE
        k = k_ref[:, h * DK:(h + 1) * DK]
        v = v_ref[:, vs]
        m0 = m0_all[:, h:h + 1]
        b_c, m_t, w_inter, num_intra, den_intra = _mlstm_intra(table, table_t, mask, m0, h, q, k, v)
        qb = q.astype(BF16)
        num_inter = jnp.concatenate(
            [_dot_nt(qb[b * T_DEC:(b + 1) * T_DEC], c0_ref[b, h].astype(BF16)) for b in range(SAMPLE_BT)], axis=0)
        num = num_intra + w_inter * num_inter
        den = den_intra + w_inter * jnp.sum(q * n0t_ref[:, h * DK:(h + 1) * DK], axis=1, keepdims=True)
        hh = num / jnp.maximum(jnp.abs(den), jnp.exp(-m_t))
        y_ref[:, h * DV:(h + 1) * DV] = (_sigmoid(o_ref[:, vs]) * hh).astype(BF16)
        m_tok = jnp.where(lane == h, m_t, m_tok)

        def last_tok(x):
            x3 = x.reshape(SAMPLE_BT, T_DEC, 1)
            return jnp.broadcast_to(x3[:, T_DEC - 1:T_DEC, :], x3.shape).reshape(L, 1)

        ig_c = table[:, LANE_IG + h:LANE_IG + h + 1]
        m_new = last_tok(m_t)
        b_last = last_tok(b_c)
        w_s = jnp.exp(b_last - b_c + ig_c - m_new)
        decay = jnp.exp(b_last + m0 - m_new)
        wv = (w_s * v).astype(BF16)
        kb = k.astype(BF16)
        rowsel = _iota((L, 1), 0) // T_DEC
        for b in range(SAMPLE_BT):
            dec_b = decay[b * T_DEC + T_DEC - 1:(b + 1) * T_DEC, :]
            wv_b = jnp.where(rowsel == b, wv, jnp.zeros_like(wv))
            c_out[b, h] = dec_b * c0_ref[b, h] + _dot_tn(wv_b, kb)
        dec_rows = decay.reshape(SAMPLE_BT, T_DEC, 1)[:, T_DEC - 1, :]
        n_out[:, h * DK:(h + 1) * DK] = dec_rows * n0_ref[:, h * DK:(h + 1) * DK] + jnp.dot(
            seg_sum, w_s * k, preferred_element_type=F32, precision=lax.Precision.HIGHEST)
    m_out[...] = m_tok


def _mlstm_sample(z, zg, gate_bias, m0_tok, n0_tok, state_c, state_n):
    L = S_ROWS
    blk = M_HEADS * DK
    z_off = N_PROMPT // L
    col = lambda c: pl.BlockSpec((L, blk), lambda i: (i + z_off, c))
    c_spec = pl.BlockSpec((SAMPLE_BT, M_HEADS, DV, DK), lambda i: (i, 0, 0, 0))
    n_spec = pl.BlockSpec((SAMPLE_BT, blk), lambda i: (i, 0))
    return pl.pallas_call(
        _mlstm_sample_kernel,
        grid=(N_BATCH // SAMPLE_BT,),
        in_specs=[pl.BlockSpec((1, LANES), lambda i: (0, 0)),
                  pl.BlockSpec((L, LANES), lambda i: (i + z_off, 0)),
                  pl.BlockSpec((L, LANES), lambda i: (i, 0)),
                  pl.BlockSpec((L, blk), lambda i: (i, 0)),
                  col(3), col(4), col(5), col(6), col(7), col(8),
                  c_spec, n_spec],
        out_specs=[pl.BlockSpec((L, M_WIDTH), lambda i: (i, 0)), c_spec, n_spec,
                   pl.BlockSpec((L, LANES), lambda i: (i, 0))],
        out_shape=[jax.ShapeDtypeStruct((N_SAMPLE, M_WIDTH), BF16),
                   jax.ShapeDtypeStruct((N_BATCH, M_HEADS, DV, DK), F32),
                   jax.ShapeDtypeStruct((N_BATCH, blk), F32),
                   jax.ShapeDtypeStruct((N_SAMPLE, LANES), F32)],
        compiler_params=_params(("arbitrary",)),
        name="mlstm_sample",
    )(gate_bias, zg, m0_tok, n0_tok, z, z, z, z, z, z, state_c, state_n)


def _outproj_kernel(ya_ref, ym_ref, wa_ref, wm_ref, x_ref, gt_ref, o_ref):
    mix = _dot(ya_ref[...], wa_ref[...]) + _dot(ym_ref[...], wm_ref[...])
    o_ref[...] = x_ref[...] + _mod_rows(gt_ref) * mix


def _outproj(ya, ym, w_out_bf, x, mod, per_row):
    n = x.shape[0]
    TM = TM_PROJ
    gate1_col = 2 * (D // TN_OUT)
    return pl.pallas_call(
        _outproj_kernel,
        grid=(n // TM, D // TN_OUT),
        in_specs=[pl.BlockSpec((TM, ATT_WIDTH), lambda i, j: (i, 0)),
                  pl.BlockSpec((TM, M_WIDTH), lambda i, j: (i, 0)),
                  pl.BlockSpec((ATT_WIDTH, TN_OUT), lambda i, j: (0, j)),
                  pl.BlockSpec((M_WIDTH, TN_OUT), lambda i, j: (1, j)),
                  pl.BlockSpec((TM, TN_OUT), lambda i, j: (i, j)),
                  _mod_spec(per_row, TM, TN_OUT, lambda j: gate1_col + j)],
        out_specs=pl.BlockSpec((TM, TN_OUT), lambda i, j: (i, j)),
        out_shape=jax.ShapeDtypeStruct((n, D), F32),
        compiler_params=_params(("arbitrary", "arbitrary")),
        name="outproj",
    )(ya, ym, w_out_bf, w_out_bf, x, mod)


def _router_kernel(x_ref, sh_ref, sc_ref, g_ref, wr_ref, br_ref, h_ref, route_ref):
    x = x_ref[...]
    y = x * lax.rsqrt(jnp.mean(x * x, axis=-1, keepdims=True) + EPS) * g_ref[...]
    h2 = y * (1.0 + _mod_rows(sc_ref)) + _mod_rows(sh_ref)
    h_hi = h2.astype(BF16)
    h_ref[...] = _pack_bf16_pairs(h2)
    h_lo = (h2 - h_hi.astype(F32)).astype(BF16)
    logits = _dot(h_hi, wr_ref[0]) + (_dot(h_hi, wr_ref[1]) + _dot(h_lo, wr_ref[0])) + br_ref[...]
    lane = _iota(logits.shape, 1)

    def first_max(vals):
        vmax = jnp.max(vals, axis=1, keepdims=True)
        idx = jnp.min(jnp.where(vals == vmax, lane, LANES), axis=1, keepdims=True)
        return vmax, idx

    gl = jnp.where(lane < N_GROUPS, logits, NEG_INF)
    gmax, grp = first_max(gl)
    p_grp = 1.0 / jnp.sum(jnp.exp(gl - gmax), axis=1, keepdims=True)
    e_lane = lane - N_GROUPS
    in_grp = jnp.logical_and(e_lane >= 0, jnp.logical_and(e_lane < N_EXPERTS, e_lane // EXP_PER_GROUP == grp))
    el = jnp.where(in_grp, logits, NEG_INF)
    v1, i1 = first_max(el)
    v2, i2 = first_max(jnp.where(lane == i1, NEG_INF, el))
    e2w = jnp.exp(v2 - v1)
    w1 = 1.0 / (1.0 + e2w)
    w2 = e2w / (1.0 + e2w)
    route = jnp.where(lane == 0, (i1 - N_GROUPS).astype(F32),
                      jnp.where(lane == 1, (i2 - N_GROUPS).astype(F32),
                                jnp.where(lane == 2, p_grp * w1, jnp.where(lane == 3, p_grp * w2, 0.0))))
    route_ref[...] = route


def _router_merged_kernel(xp_ref, xs_ref, shp_ref, scp_ref, shs_ref, scs_ref, g_ref, wr_ref, br_ref,
                          h_ref, route_ref):
    i = pl.program_id(0)

    @pl.when(i < PROMPT_TOK_BLOCKS)
    def _():
        _router_kernel(xp_ref, shp_ref, scp_ref, g_ref, wr_ref, br_ref, h_ref, route_ref)

    @pl.when(i >= PROMPT_TOK_BLOCKS)
    def _():
        _router_kernel(xs_ref, shs_ref, scs_ref, g_ref, wr_ref, br_ref, h_ref, route_ref)


def _router(x1_p, x1_s, mod_p, mod_s, g_ffn, w_route, b_route):
    pi, si = _prompt_block, _sample_block
    return pl.pallas_call(
        _router_merged_kernel,
        grid=(N_TOK // TM_TOK,),
        in_specs=[pl.BlockSpec((TM_TOK, D), lambda i: (pi(i), 0)),
                  pl.BlockSpec((TM_TOK, D), lambda i: (si(i), 0)),
                  pl.BlockSpec((1, D), lambda i: (0, 3)),
                  pl.BlockSpec((1, D), lambda i: (0, 4)),
                  pl.BlockSpec((TM_TOK // T_DEC, 1, D), lambda i: (si(i), 0, 3)),
                  pl.BlockSpec((TM_TOK // T_DEC, 1, D), lambda i: (si(i), 0, 4)),
                  pl.BlockSpec((1, D), lambda i: (0, 0)),
                  pl.BlockSpec((2, D, LANES), lambda i: (0, 0, 0)),
                  pl.BlockSpec((1, LANES), lambda i: (0, 0))],
        out_specs=[pl.BlockSpec((TM_TOK, D_PACK), lambda i: (i, 0)),
                   pl.BlockSpec((TM_TOK, LANES), lambda i: (i, 0))],
        out_shape=[jax.ShapeDtypeStruct((N_TOK, D_PACK), jnp.uint32),
                   jax.ShapeDtypeStruct((N_TOK, LANES), F32)],
        compiler_params=_params(("arbitrary",)),
        name="router",
    )(x1_p, x1_s, mod_p, mod_p, mod_s, mod_s, g_ffn, w_route, b_route)


def _rank_kernel(route_ref, dest_ref, cnt_ref, carry, rank_scr):
    pas = pl.program_id(0)
    step = pl.program_id(1)

    @pl.when(jnp.logical_and(pas == 0, step == 0))
    def _():
        carry[...] = jnp.zeros_like(carry)

    route = route_ref[...]
    n = route.shape[0]
    lane = _iota((n, LANES), 1).astype(F32)
    o1 = jnp.where(lane == route[:, 0:1], 1.0, 0.0)
    o2 = jnp.where(lane == route[:, 1:2], 1.0, 0.0)
    lane_i = _iota((n, LANES), 1)
    rows = pl.ds(pl.multiple_of(step * TM_RANK, TM_RANK), TM_RANK)

    @pl.when(pas == 0)
    def _():
        both = o1 + o2
        strict = jnp.where(_iota((n, n), 1) < _iota((n, n), 0), 1.0, 0.0).astype(BF16)
        prior = _dot(strict, both.astype(BF16)) + carry[0:1, :]
        r1 = jnp.sum(o1 * prior, axis=1, keepdims=True)
        r2 = jnp.sum(o2 * prior, axis=1, keepdims=True)
        rank_scr[rows, :] = jnp.where(lane_i == 0, r1, jnp.where(lane_i == 1, r2, 0.0))
        carry[...] = carry[...] + jnp.sum(both, axis=0, keepdims=True)

    @pl.when(pas == 1)
    def _():
        counts = carry[...]
        tiles_per = jnp.floor((counts + (TM_MOE - 1)) * (1.0 / TM_MOE))
        before = jnp.where(_iota((LANES, LANES), 0) < _iota((LANES, LANES), 1), 1.0, 0.0).astype(BF16)
        pad_start = _dot(tiles_per.astype(BF16), before)[0:1, :] * TM_MOE
        rank = rank_scr[rows, :]
        d1 = jnp.sum(o1 * pad_start, axis=1, keepdims=True) + rank[:, 0:1]
        d2 = jnp.sum(o2 * pad_start, axis=1, keepdims=True) + rank[:, 1:2]
        dest_ref[...] = jnp.where(lane_i == 0, d1, jnp.where(lane_i == 1, d2, 0.0)).astype(jnp.int32)
        cnt_ref[...] = counts


def _rank(route):
    n = route.shape[0]
    return pl.pallas_call(
        _rank_kernel,
        grid=(2, n // TM_RANK),
        in_specs=[pl.BlockSpec((TM_RANK, LANES), lambda p, i: (i, 0))],
        out_specs=[pl.BlockSpec((TM_RANK, LANES), lambda p, i: (i * p, 0)),
                   pl.BlockSpec((SUBLANES, LANES), lambda p, i: (0, 0))],
        out_shape=[jax.ShapeDtypeStruct((n, LANES), jnp.int32), jax.ShapeDtypeStruct((SUBLANES, LANES), F32)],
        scratch_shapes=[pltpu.VMEM((SUBLANES, LANES), F32), pltpu.VMEM((n, LANES), F32)],
        compiler_params=_params(("arbitrary", "arbitrary")),
        name="rank",
    )(route)


def _row_copy(src, dst, sem):
    return pltpu.make_async_copy(src, dst, sem)


ROW_UNROLL = 8
assert T_DEC == SUBLANES
D_PACK = D // 2

ZERO_FIRST, ZERO_ANYTIME = 1, 2


def _dispatch_kernel(d1_ref, d2_ref, zc_ref, h_ref, xs_out, zbuf, sem):
    step = pl.program_id(0)
    base = step * TM_TOK

    def zero_tile(t, s):
        rows = pl.ds(pl.multiple_of(t * TM_MOE, TM_MOE), TM_MOE)
        return pltpu.make_async_copy(zbuf, xs_out.at[rows], sem.at[s])

    def for_tiles(cls, s, act):
        def body(t, carry):
            @pl.when(zc_ref[t] == cls)
            def _():
                act(zero_tile(t, s))
            return carry
        lax.fori_loop(0, MAX_TILES, body, 0)

    @pl.when(step == 0)
    def _():
        zbuf[...] = jnp.zeros_like(zbuf)
        for_tiles(ZERO_FIRST, 1, lambda c: c.start())
        for_tiles(ZERO_ANYTIME, 2, lambda c: c.start())
        for_tiles(ZERO_FIRST, 1, lambda c: c.wait())

    @pl.when(step == pl.num_programs(0) - 1)
    def _():
        for_tiles(ZERO_ANYTIME, 2, lambda c: c.wait())

    def issue(r, carry):
        _row_copy(h_ref.at[pl.ds(r, 1)], xs_out.at[pl.ds(d1_ref[base + r], 1)], sem.at[0]).start()
        _row_copy(h_ref.at[pl.ds(r, 1)], xs_out.at[pl.ds(d2_ref[base + r], 1)], sem.at[0]).start()
        return carry

    lax.fori_loop(0, TM_TOK, issue, 0, unroll=ROW_UNROLL)
    for _ in range(TOP_K):
        _row_copy(h_ref, xs_out.at[pl.ds(0, TM_TOK)], sem.at[0]).wait()


def _dispatch(dests, zero_class, h2):
    grid_spec = pltpu.PrefetchScalarGridSpec(
        num_scalar_prefetch=3,
        grid=(N_TOK // TM_TOK,),
        in_specs=[pl.BlockSpec((TM_TOK, D_PACK), lambda i, *_: (i, 0))],
        out_specs=pl.BlockSpec(memory_space=pl.ANY),
        scratch_shapes=[pltpu.VMEM((TM_MOE, D_PACK), jnp.uint32), pltpu.SemaphoreType.DMA((3,))],
    )
    return pl.pallas_call(
        _dispatch_kernel,
        grid_spec=grid_spec,
        out_shape=jax.ShapeDtypeStruct((A_PAD, D_PACK), jnp.uint32),
        compiler_params=_params(("arbitrary",)),
        name="dispatch",
    )(*dests, zero_class, h2)


N_UP_CHUNKS = D_FF // TN_FF


def _expert_up_kernel(te_ref, tv_ref, first_ref, nxt_ref, run_ref, meta_ref, x_ref, wg_hbm, wu_hbm,
                      o_ref, wbuf, sem):
    n = pl.program_id(0)
    t = pl.program_id(1)
    slot = lax.rem(n * meta_ref[1] + run_ref[t], 2)

    def fetch(e, chunk, s):
        cols = pl.ds(pl.multiple_of(chunk * TN_FF, TN_FF), TN_FF)
        return (pltpu.make_async_copy(wg_hbm.at[e, :, cols], wbuf.at[s, 0], sem.at[s, 0]),
                pltpu.make_async_copy(wu_hbm.at[e, :, cols], wbuf.at[s, 1], sem.at[s, 1]))

    @pl.when(first_ref[t] == 1)
    def _():
        @pl.when(jnp.logical_and(n == 0, t == 0))
        def _():
            for c in fetch(te_ref[0], 0, 0):
                c.start()

        for c in fetch(te_ref[t], n, slot):
            c.wait()

        @pl.when(nxt_ref[t] >= 0)
        def _():
            for c in fetch(nxt_ref[t], n, 1 - slot):
                c.start()

        @pl.when(jnp.logical_and(nxt_ref[t] < 0, n + 1 < N_UP_CHUNKS))
        def _():
            for c in fetch(te_ref[0], n + 1, 1 - slot):
                c.start()

    @pl.when(tv_ref[t] == 1)
    def _():
        x = jnp.concatenate([half.astype(BF16) for half in _unpack_bf16_pairs(x_ref[...])], axis=1)
        g = _dot(x, wbuf[slot, 0].astype(BF16))
        u = _dot(x, wbuf[slot, 1].astype(BF16))
        o_ref[...] = (g * _sigmoid(g) * u).astype(BF16)

    @pl.when(tv_ref[t] == 0)
    def _():
        o_ref[...] = jnp.zeros_like(o_ref)


def _expert_up(tiles, xs, w_gate, w_up):
    last = lambda t, meta: jnp.minimum(t, meta[0] - 1)
    grid_spec = pltpu.PrefetchScalarGridSpec(
        num_scalar_prefetch=6,
        grid=(N_UP_CHUNKS, MAX_TILES),
        in_specs=[pl.BlockSpec((TM_MOE, D_PACK), lambda n, t, *s: (last(t, s[5]), 0)),
                  pl.BlockSpec(memory_space=pl.ANY),
                  pl.BlockSpec(memory_space=pl.ANY)],
        out_specs=pl.BlockSpec((TM_MOE, TN_FF), lambda n, t, *s: (t, n)),
        scratch_shapes=[pltpu.VMEM((2, 2, D, TN_FF), F32), pltpu.SemaphoreType.DMA((2, 2))],
    )
    return pl.pallas_call(
        _expert_up_kernel,
        grid_spec=grid_spec,
        out_shape=jax.ShapeDtypeStruct((A_PAD, D_FF), BF16),
        compiler_params=_params(("arbitrary", "arbitrary")),
        name="expert_up",
    )(*tiles, xs, w_gate, w_up)


def _expert_down_kernel(te_ref, tv_ref, first_ref, nxt_ref, run_ref, meta_ref, h_ref, wd_hbm,
                        o_ref, wbuf, sem):
    t = pl.program_id(0)
    slot = lax.rem(run_ref[t], 2)

    def fetch(e, s):
        return pltpu.make_async_copy(wd_hbm.at[e], wbuf.at[s], sem.at[s])

    @pl.when(first_ref[t] == 1)
    def _():
        @pl.when(t == 0)
        def _():
            fetch(te_ref[0], 0).start()

        fetch(te_ref[t], slot).wait()

        @pl.when(nxt_ref[t] >= 0)
        def _():
            fetch(nxt_ref[t], 1 - slot).start()

    @pl.when(tv_ref[t] == 1)
    def _():
        o_ref[...] = _pack_bf16_pairs(_dot(h_ref[...], wbuf[slot].astype(BF16)))

    @pl.when(tv_ref[t] == 0)
    def _():
        o_ref[...] = jnp.zeros_like(o_ref)


def _expert_down(tiles, h1, w_down):
    last = lambda t, meta: jnp.minimum(t, meta[0] - 1)
    grid_spec = pltpu.PrefetchScalarGridSpec(
        num_scalar_prefetch=6,
        grid=(MAX_TILES,),
        in_specs=[pl.BlockSpec((TM_MOE, D_FF), lambda t, *s: (last(t, s[5]), 0)),
                  pl.BlockSpec(memory_space=pl.ANY)],
        out_specs=pl.BlockSpec((TM_MOE, D_PACK), lambda t, *s: (t, 0)),
        scratch_shapes=[pltpu.VMEM((2, D_FF, D), F32), pltpu.SemaphoreType.DMA((2,))],
    )
    return pl.pallas_call(
        _expert_down_kernel,
        grid_spec=grid_spec,
        out_shape=jax.ShapeDtypeStruct((A_PAD, D_PACK), jnp.uint32),
        compiler_params=_params(("arbitrary",)),
        name="expert_down",
    )(*tiles, h1, w_down)


def _combine_kernel(d1_ref, d2_ref, x_ref, route_ref, gt_ref, gf_ref, o_hbm,
                    y_ref, buf, sem, *, tok_offset):
    step = pl.program_id(0)
    slot = lax.rem(step, 2)

    def gather(s, into):
        base = tok_offset + s * TM_TOK

        def issue(r, carry):
            _row_copy(o_hbm.at[pl.ds(d1_ref[base + r], 1)], buf.at[into, 0, pl.ds(r, 1)], sem.at[into]).start()
            _row_copy(o_hbm.at[pl.ds(d2_ref[base + r], 1)], buf.at[into, 1, pl.ds(r, 1)], sem.at[into]).start()
            return carry

        lax.fori_loop(0, TM_TOK, issue, 0, unroll=ROW_UNROLL)

    @pl.when(step == 0)
    def _():
        gather(0, 0)

    @pl.when(step + 1 < pl.num_programs(0))
    def _():
        gather(step + 1, 1 - slot)

    for k in range(TOP_K):
        _row_copy(o_hbm.at[pl.ds(0, TM_TOK)], buf.at[slot, k], sem.at[slot]).wait()

    route = route_ref[...]
    lo1, hi1 = _unpack_bf16_pairs(buf[slot, 0])
    lo2, hi2 = _unpack_bf16_pairs(buf[slot, 1])
    g1, g2 = route[:, 2:3], route[:, 3:4]
    moe = jnp.concatenate([g1 * lo1 + g2 * lo2, g1 * hi1 + g2 * hi2], axis=1)
    x2 = x_ref[...] + _mod_rows(gt_ref) * moe
    y_ref[...] = x2 * lax.rsqrt(jnp.mean(x2 * x2, axis=-1, keepdims=True) + EPS) * gf_ref[...]


def _combine(dests, x1, route, mod, per_row, g_final, o_rows, tok_offset):
    n = x1.shape[0]
    off = tok_offset // TM_TOK
    grid_spec = pltpu.PrefetchScalarGridSpec(
        num_scalar_prefetch=2,
        grid=(n // TM_TOK,),
        in_specs=[pl.BlockSpec((TM_TOK, D), lambda i, *_: (i, 0)),
                  pl.BlockSpec((TM_TOK, LANES), lambda i, *_: (i + off, 0)),
                  _mod_spec(per_row, TM_TOK, D, lambda *_: 5),
                  pl.BlockSpec((1, D), lambda i, *_: (0, 0)),
                  pl.BlockSpec(memory_space=pl.ANY)],
        out_specs=pl.BlockSpec((TM_TOK, D), lambda i, *_: (i, 0)),
        scratch_shapes=[pltpu.VMEM((2, TOP_K, TM_TOK, D_PACK), jnp.uint32), pltpu.SemaphoreType.DMA((2,))],
    )
    return pl.pallas_call(
        functools.partial(_combine_kernel, tok_offset=tok_offset),
        grid_spec=grid_spec,
        out_shape=jax.ShapeDtypeStruct((n, D), F32),
        compiler_params=_params(("arbitrary",)),
        name="combine",
    )(*dests, x1, route, mod, g_final, o_rows)


def kernel(x_prompt, x_sample, cache_k, cache_v, state_C, state_n, state_m, c_prompt, c_sample, rel_bias, w_ada, b_ada, g_mix, g_ffn, w_in, sinks, b_igate, b_fgate, w_out, w_router_grp, b_router_grp, w_router_exp, b_router_exp, w_gate, w_up, w_down, g_final):
    xp = x_prompt.reshape(N_PROMPT, D)
    xs = x_sample.reshape(N_SAMPLE, D)

    c_all = jnp.concatenate([c_prompt, c_sample, jnp.zeros((C_ROWS - 1 - N_BATCH, D), F32)], axis=0)
    mod = _ada(c_all, w_ada[0], b_ada)
    mod_p = mod[0:1]
    mod_s = mod[1:1 + N_BATCH].reshape(N_BATCH, 1, -1)

    w_in_t = jnp.swapaxes(w_in[0], 0, 1)
    w_gates_t = jnp.pad(w_in_t[Z_WIDTH:].astype(BF16), ((0, LANES - 2 * M_HEADS), (0, 0)))
    h_all, zg = _norm(xp, xs, mod_p, mod_s, g_mix, w_gates_t)
    z = _inproj(h_all, w_in_t)

    rb_flat = rel_bias.reshape(NUM_BUCKETS * ATT_HEADS)
    sink_v = sinks[0]
    ya_p = _swa_prompt(z, rb_flat, sink_v)
    ya_s, nk_s, nv_s = _swa_sample(z,cache_k.reshape(N_BATCH, CACHE_ROWS, HEAD_DIM),
                                   cache_v.reshape(N_BATCH, CACHE_ROWS, HEAD_DIM), rb_flat, sink_v)

    gate_bias = jnp.concatenate([b_igate[0], b_fgate[0], jnp.zeros((LANES - 2 * M_HEADS,), F32)]).reshape(1, LANES)
    ym_p, c_p, n_p, m_p = _mlstm_prompt(z, zg, gate_bias)
    m0_tok = jnp.pad(jnp.repeat(state_m[0], T_DEC, axis=0), ((0, 0), (0, LANES - M_HEADS)))
    n0_flat = state_n[0].reshape(N_BATCH, M_HEADS * DK)
    n0_tok = jnp.repeat(n0_flat, T_DEC, axis=0)
    ym_s, c_s, n_s, m_s = _mlstm_sample(z, zg, gate_bias, m0_tok, n0_tok, state_C[0], n0_flat)

    w_out_bf = w_out[0].astype(BF16)
    x1_p = _outproj(ya_p, ym_p, w_out_bf, xp, mod_p, False)
    x1_s = _outproj(ya_s, ym_s, w_out_bf, xs, mod_s, True)

    w_route = jnp.pad(jnp.concatenate([w_router_grp[0], w_router_exp[0]], axis=1),
                      ((0, 0), (0, LANES - N_GROUPS - N_EXPERTS)))
    b_route = jnp.pad(jnp.concatenate([b_router_grp[0], b_router_exp[0]]),
                      (0, LANES - N_GROUPS - N_EXPERTS)).reshape(1, LANES)
    w_route_hi = w_route.astype(BF16)
    w_route_split = jnp.stack([w_route_hi, (w_route - w_route_hi.astype(F32)).astype(BF16)])
    h2, route = _router(x1_p, x1_s, mod_p, mod_s, g_ffn, w_route_split, b_route)

    dest, counts = _rank(route)
    dests = (dest[:, 0], dest[:, 1])

    i32 = lambda a: a.astype(jnp.int32)
    cnt = i32(counts[0, :N_EXPERTS])
    tiles_per = (cnt + TM_MOE - 1) // TM_MOE
    tile_end = jnp.cumsum(tiles_per)
    n_tiles = tile_end[-1]
    tile_ids = jnp.arange(MAX_TILES, dtype=jnp.int32)
    tile_expert = i32(jnp.minimum(jnp.searchsorted(tile_end, tile_ids, side="right"), N_EXPERTS - 1))
    last_expert = tile_expert[jnp.maximum(n_tiles - 1, 0)]
    tile_valid = tile_ids < n_tiles
    tile_expert = jnp.where(tile_valid, tile_expert, last_expert)
    prev_expert = jnp.concatenate([jnp.full((1,), -1, jnp.int32), tile_expert[:-1]])
    run_first = jnp.logical_and(tile_valid, tile_expert != prev_expert)
    run_id = jnp.maximum(jnp.cumsum(i32(run_first)) - 1, 0)
    expert_ids = jnp.arange(N_EXPERTS, dtype=jnp.int32)
    used = jnp.where(tiles_per > 0, expert_ids, N_EXPERTS)
    next_used = jnp.concatenate([lax.cummin(used[::-1])[::-1][1:], jnp.full((1,), N_EXPERTS, jnp.int32)])
    next_used = jnp.where(next_used >= N_EXPERTS, -1, next_used)
    tiles = (tile_expert, i32(tile_valid), i32(run_first), next_used[tile_expert], i32(run_id),
             jnp.stack([n_tiles, jnp.sum(i32(run_first))]).astype(jnp.int32))
    next_expert = jnp.concatenate([tile_expert[1:], jnp.full((1,), -1, jnp.int32)])
    run_last = jnp.logical_or(tile_expert != next_expert, tile_ids == n_tiles - 1)
    zero_class = jnp.where(tile_valid, jnp.where(run_last, ZERO_FIRST, 0), ZERO_ANYTIME)
    xs_rows = _dispatch(dests, i32(zero_class), h2)
    h1 = _expert_up(tiles, xs_rows, w_gate[0], w_up[0])
    o_rows = _expert_down(tiles, h1, w_down[0])

    gf = g_final.reshape(1, D)
    y_p = _combine(dests, x1_p, route, mod_p, False, gf, o_rows, 0)
    y_s = _combine(dests, x1_s, route, mod_s, True, gf, o_rows, N_PROMPT)

    kv5 = lambda a: a.reshape(1, -1, WINDOW, KV_HEADS, HEAD_DIM)
    kcol = ATT_WIDTH
    nk_p = z[N_PROMPT - WINDOW:N_PROMPT, kcol:kcol + KV_WIDTH]
    nv_p = z[N_PROMPT - WINDOW:N_PROMPT, kcol + KV_WIDTH:kcol + 2 * KV_WIDTH]
    return (y_p.reshape(1, N_PROMPT, D), y_s.reshape(N_BATCH, T_DEC, D),
            kv5(nk_p), kv5(nv_p),
            c_p.reshape(1, 1, M_HEADS, DV, DK), n_p.reshape(1, 1, M_HEADS, DK), m_p[:, 0].reshape(1, 1, M_HEADS),
            kv5(nk_s), kv5(nv_s),
            c_s.reshape(1, N_BATCH, M_HEADS, DV, DK), n_s.reshape(1, N_BATCH, M_HEADS, DK),
            m_s.reshape(N_BATCH, T_DEC, LANES)[:, T_DEC - 1, :M_HEADS].reshape(1, N_BATCH, M_HEADS))
```

```python
import functools
import math

import numpy as np
import jax
import jax.numpy as jnp
from jax import lax
from jax.experimental import pallas as pl
from jax.experimental.pallas import tpu as pltpu

F32 = jnp.float32
BF16 = jnp.bfloat16
NEG_INF = float("-inf")

D = 4096
N_PROMPT = 8192
N_BATCH = 128
T_DEC = 8
N_SAMPLE = N_BATCH * T_DEC
N_TOK = N_PROMPT + N_SAMPLE
HEAD_DIM = 128
ATT_HEADS = 16
KV_HEADS = 4
GROUP = ATT_HEADS // KV_HEADS
WINDOW = 128
ATT_WIDTH = ATT_HEADS * HEAD_DIM
KV_WIDTH = KV_HEADS * HEAD_DIM
NUM_BUCKETS = 32
MAX_EXACT = 16
MAX_DISTANCE = 128
M_HEADS = 8
DK = 128
DV = 256
M_WIDTH = M_HEADS * DV
Z_WIDTH = ATT_WIDTH + 2 * KV_WIDTH + 2 * M_HEADS * DK + 2 * M_WIDTH
N_GROUPS = 4
EXP_PER_GROUP = 8
N_EXPERTS = N_GROUPS * EXP_PER_GROUP
TOP_K = 2
D_FF = 1024
EPS = 1e-6
ATT_SCALE = HEAD_DIM ** -0.5
Q_SCALE = DK ** -0.5

LANES = 128
SUBLANES = 8
VMEM_LIMIT = 56 * 1024 * 1024

TM_PROJ = 512
TN_IN = 1024
TN_OUT = 1024
TN_ADA = 512
C_ROWS = 136
ATT_BLOCK = 128
SAMPLE_BT = 8
ML_CHUNK = 256
TM_MOE = 256
TN_FF = 512
N_ASSIGN = N_TOK * TOP_K
MAX_TILES = N_ASSIGN // TM_MOE + N_EXPERTS
A_PAD = MAX_TILES * TM_MOE
TM_TOK = 256
TM_RANK = 1024


def _params(sem):
    return pltpu.CompilerParams(dimension_semantics=sem, vmem_limit_bytes=VMEM_LIMIT)


def _iota(shape, dim):
    return lax.broadcasted_iota(jnp.int32, shape, dim)


def _dot(a, b):
    return jnp.dot(a, b, preferred_element_type=F32)


def _dot_nt(a, b):
    return lax.dot_general(a, b, (((1,), (1,)), ((), ())), preferred_element_type=F32)


def _dot_tn(a, b):
    return lax.dot_general(a, b, (((0,), (0,)), ((), ())), preferred_element_type=F32)


def _split3(x):
    x1 = x.astype(BF16)
    r1 = x - x1.astype(F32)
    x2 = r1.astype(BF16)
    r2 = r1 - x2.astype(F32)
    return x1, x2, r2.astype(BF16)


def _dot_exact_lhs01(a01, x):
    x1, x2, x3 = _split3(x)
    return _dot(a01, x1) + _dot(a01, x2) + _dot(a01, x3)


def _pack_bf16_pairs(x):
    w = x.shape[1] // 2
    bits = lax.bitcast_convert_type(x.astype(BF16).astype(F32), jnp.uint32)
    return (bits[:, :w] >> 16) | (bits[:, w:] & jnp.uint32(0xFFFF0000))


def _unpack_bf16_pairs(words):
    return (lax.bitcast_convert_type(words << 16, F32),
            lax.bitcast_convert_type(words & jnp.uint32(0xFFFF0000), F32))


def _sigmoid(x):
    return 1.0 / (1.0 + jnp.exp(-x))


def _log_sigmoid(x):
    return jnp.minimum(x, 0.0) - jnp.log(1.0 + jnp.exp(-jnp.abs(x)))


def _mod_spec(per_batch, rows, width, col):
    if per_batch:
        return pl.BlockSpec((rows // T_DEC, 1, width), lambda i, *rest: (i, 0, col(*rest)))
    return pl.BlockSpec((1, width), lambda i, *rest: (0, col(*rest)))


def _mod_rows(ref):
    v = ref[...]
    if v.ndim == 2:
        return v
    nb, _, width = v.shape
    return jnp.broadcast_to(v, (nb, T_DEC, width)).reshape(nb * T_DEC, width)


def _ada_kernel(c_ref, w_ref, b_ref, o_ref):
    c = c_ref[...]
    s = (c * _sigmoid(c)).astype(BF16)
    o_ref[...] = _dot(s, w_ref[...].astype(BF16)) + b_ref[...]


def _ada(c_all, w_ada, b_ada):
    n = w_ada.shape[1]
    return pl.pallas_call(
        _ada_kernel,
        grid=(n // TN_ADA,),
        in_specs=[pl.BlockSpec((C_ROWS, D), lambda j: (0, 0)),
                  pl.BlockSpec((D, TN_ADA), lambda j: (0, j)),
                  pl.BlockSpec((1, TN_ADA), lambda j: (0, j))],
        out_specs=pl.BlockSpec((C_ROWS, TN_ADA), lambda j: (0, j)),
        out_shape=jax.ShapeDtypeStruct((C_ROWS, n), F32),
        compiler_params=_params(("arbitrary",)),
        name="ada",
    )(c_all, w_ada, b_ada)


PROMPT_TOK_BLOCKS = N_PROMPT // TM_TOK


def _prompt_block(i):
    return jnp.minimum(i, PROMPT_TOK_BLOCKS - 1)


def _sample_block(i):
    return jnp.maximum(i - PROMPT_TOK_BLOCKS, 0)


def _norm_body(x_ref, sh_ref, sc_ref, g_ref, wg_ref, h_ref, zg_ref):
    x = x_ref[...]
    y = x * lax.rsqrt(jnp.mean(x * x, axis=-1, keepdims=True) + EPS) * g_ref[...]
    hb = (y * (1.0 + _mod_rows(sc_ref)) + _mod_rows(sh_ref)).astype(BF16)
    h_ref[...] = hb
    zg_ref[...] = _dot_nt(hb, wg_ref[...])


def _norm_kernel(xp_ref, xs_ref, shp_ref, scp_ref, shs_ref, scs_ref, g_ref, wg_ref, h_ref, zg_ref):
    i = pl.program_id(0)

    @pl.when(i < PROMPT_TOK_BLOCKS)
    def _():
        _norm_body(xp_ref, shp_ref, scp_ref, g_ref, wg_ref, h_ref, zg_ref)

    @pl.when(i >= PROMPT_TOK_BLOCKS)
    def _():
        _norm_body(xs_ref, shs_ref, scs_ref, g_ref, wg_ref, h_ref, zg_ref)


def _norm(xp, xs, mod_p, mod_s, g_mix, w_gate):
    per_batch = lambda col: pl.BlockSpec((TM_TOK // T_DEC, 1, D), lambda i: (_sample_block(i), 0, col))
    return pl.pallas_call(
        _norm_kernel,
        grid=(N_TOK // TM_TOK,),
        in_specs=[pl.BlockSpec((TM_TOK, D), lambda i: (_prompt_block(i), 0)),
                  pl.BlockSpec((TM_TOK, D), lambda i: (_sample_block(i), 0)),
                  pl.BlockSpec((1, D), lambda i: (0, 0)),
                  pl.BlockSpec((1, D), lambda i: (0, 1)),
                  per_batch(0), per_batch(1),
                  pl.BlockSpec((1, D), lambda i: (0, 0)),
                  pl.BlockSpec((LANES, D), lambda i: (0, 0))],
        out_specs=[pl.BlockSpec((TM_TOK, D), lambda i: (i, 0)),
                   pl.BlockSpec((TM_TOK, LANES), lambda i: (i, 0))],
        out_shape=[jax.ShapeDtypeStruct((N_TOK, D), BF16), jax.ShapeDtypeStruct((N_TOK, LANES), F32)],
        compiler_params=_params(("arbitrary",)),
        name="norm",
    )(xp, xs, mod_p, mod_p, mod_s, mod_s, g_mix, w_gate)


N_IN_CHUNKS = Z_WIDTH // TN_IN


def _inproj_kernel(h_ref, wt_hbm, z_ref, stage, w_bf, sem):
    j = pl.program_id(0)

    def fetch(chunk):
        rows = pl.ds(pl.multiple_of(chunk * TN_IN, TN_IN), TN_IN)
        return pltpu.make_async_copy(wt_hbm.at[rows], stage, sem.at[0])

    @pl.when(pl.program_id(1) == 0)
    def _():
        @pl.when(j == 0)
        def _():
            fetch(0).start()

        fetch(j).wait()
        w_bf[...] = stage[...].astype(BF16)

        @pl.when(j + 1 < N_IN_CHUNKS)
        def _():
            fetch(j + 1).start()

    z_ref[...] = _dot_nt(h_ref[...], w_bf[...])


def _inproj(h, w_in_t):
    return pl.pallas_call(
        _inproj_kernel,
        grid=(N_IN_CHUNKS, N_TOK // TM_PROJ),
        in_specs=[pl.BlockSpec((TM_PROJ, D), lambda j, i: (i, 0)),
                  pl.BlockSpec(memory_space=pl.ANY)],
        out_specs=pl.BlockSpec((TM_PROJ, TN_IN), lambda j, i: (i, j)),
        out_shape=jax.ShapeDtypeStruct((N_TOK, Z_WIDTH), F32),
        scratch_shapes=[pltpu.VMEM((TN_IN, D), F32), pltpu.VMEM((TN_IN, D), BF16), pltpu.SemaphoreType.DMA((1,))],
        compiler_params=_params(("arbitrary", "arbitrary")),
        name="inproj",
    )(h, w_in_t)


def _t5_bucket_np(dist):
    n = np.maximum(dist, 0)
    nf = np.maximum(n, 1).astype(np.float32)
    large = MAX_EXACT + (np.log(nf / MAX_EXACT) / math.log(MAX_DISTANCE / MAX_EXACT)
                         * (NUM_BUCKETS - MAX_EXACT)).astype(np.int32)
    large = np.minimum(large, NUM_BUCKETS - 1)
    return np.where(n < MAX_EXACT, n, large).astype(np.int32)


def _bucket_table(n_q, n_keys_valid, n_keys_padded):
    t = np.arange(n_q)[:, None]
    j = np.arange(n_keys_padded)[None, :]
    dist = t + WINDOW - j
    valid = (dist >= 0) & (dist < WINDOW) & (j < n_keys_valid)
    return np.where(valid, _t5_bucket_np(dist), -1).astype(np.int32)


def _fill_bias(bucket_ref, rb_ref, bias_scr, rows, per_head_table=False):
    for h in range(ATT_HEADS):
        bk = bucket_ref[h * rows:(h + 1) * rows, :] if per_head_table else bucket_ref[...]
        acc = jnp.full(bk.shape, NEG_INF, F32)
        for b in range(NUM_BUCKETS):
            acc = jnp.where(bk == b, rb_ref[b * ATT_HEADS + h], acc)
        bias_scr[h * rows:(h + 1) * rows, :] = acc


def _with_ones(v2):
    return jnp.concatenate([v2, jnp.ones_like(v2)], axis=1)


def _sink_softmax_av(lg, sink, v2_ones):
    m = jnp.maximum(jnp.max(lg, axis=-1, keepdims=True), sink)
    p = jnp.exp(lg - m).astype(BF16)
    pv = _dot(p, v2_ones)
    return pv[:, :HEAD_DIM] / (pv[:, HEAD_DIM:] + jnp.exp(sink - m))


def _swa_prompt_kernel(rb_ref, sink_ref, bucket_ref, q_ref, kp_ref, kc_ref, vp_ref, vc_ref,
                       o_ref, bias_scr):
    i = pl.program_id(0)

    @pl.when(i == 0)
    def _():
        _fill_bias(bucket_ref, rb_ref, bias_scr, ATT_BLOCK)

    first_prev = jnp.logical_and(i == 0, _iota((ATT_BLOCK, 2 * ATT_BLOCK), 1) < ATT_BLOCK)
    for g in range(KV_HEADS):
        ks = slice(g * HEAD_DIM, (g + 1) * HEAD_DIM)
        k2 = jnp.concatenate([kp_ref[:, ks], kc_ref[:, ks]], axis=0).astype(BF16)
        v2 = _with_ones(jnp.concatenate([vp_ref[:, ks], vc_ref[:, ks]], axis=0).astype(BF16))
        for r in range(GROUP):
            h = g * GROUP + r
            hs = slice(h * HEAD_DIM, (h + 1) * HEAD_DIM)
            lg = _dot_nt(q_ref[:, hs].astype(BF16), k2) * ATT_SCALE + bias_scr[h * ATT_BLOCK:(h + 1) * ATT_BLOCK, :]
            lg = jnp.where(first_prev, NEG_INF, lg)
            o_ref[:, hs] = _sink_softmax_av(lg, sink_ref[h], v2).astype(BF16)


def _swa_prompt(z, rb_flat, sinks):
    nb = N_PROMPT // ATT_BLOCK
    bucket = jnp.asarray(_bucket_table(ATT_BLOCK, 2 * ATT_BLOCK, 2 * ATT_BLOCK))
    kcol = ATT_WIDTH // KV_WIDTH
    prev = lambda i: jnp.maximum(i - 1, 0)
    smem = pl.BlockSpec(memory_space=pltpu.SMEM)
    return pl.pallas_call(
        _swa_prompt_kernel,
        grid=(nb,),
        in_specs=[smem, smem,
                  pl.BlockSpec((ATT_BLOCK, 2 * ATT_BLOCK), lambda i: (0, 0)),
                  pl.BlockSpec((ATT_BLOCK, ATT_WIDTH), lambda i: (i, 0)),
                  pl.BlockSpec((ATT_BLOCK, KV_WIDTH), lambda i: (prev(i), kcol)),
                  pl.BlockSpec((ATT_BLOCK, KV_WIDTH), lambda i: (i, kcol)),
                  pl.BlockSpec((ATT_BLOCK, KV_WIDTH), lambda i: (prev(i), kcol + 1)),
                  pl.BlockSpec((ATT_BLOCK, KV_WIDTH), lambda i: (i, kcol + 1))],
        out_specs=pl.BlockSpec((ATT_BLOCK, ATT_WIDTH), lambda i: (i, 0)),
        out_shape=jax.ShapeDtypeStruct((N_PROMPT, ATT_WIDTH), BF16),
        scratch_shapes=[pltpu.VMEM((ATT_HEADS * ATT_BLOCK, 2 * ATT_BLOCK), F32)],
        compiler_params=_params(("arbitrary",)),
        name="swa_prompt",
    )(rb_flat, sinks, bucket, z, z, z, z, z)


CACHE_ROWS = WINDOW * KV_HEADS
NEW_ROWS = T_DEC * KV_HEADS
S_KEYS = 5 * LANES


def _sample_bucket_table():
    t = np.arange(T_DEC)[:, None]
    col = np.arange(S_KEYS)[None, :]
    in_cache = col < CACHE_ROWS
    in_new = (col >= CACHE_ROWS) & (col < CACHE_ROWS + NEW_ROWS)
    key_head = np.where(in_cache, col % KV_HEADS, (col - CACHE_ROWS) // T_DEC)
    key_pos = np.where(in_cache, col // KV_HEADS, WINDOW + (col - CACHE_ROWS) % T_DEC)
    dist = t + WINDOW - key_pos
    valid = (dist >= 0) & (dist < WINDOW) & (in_cache | in_new)
    per_query = np.where(valid, _t5_bucket_np(dist), -1)
    heads = np.arange(ATT_HEADS)[:, None, None] // GROUP
    table = np.where(heads == key_head[None], per_query[None], -1)
    return table.reshape(ATT_HEADS * T_DEC, S_KEYS).astype(np.int32)


def _swa_sample_kernel(rb_ref, sink_ref, bucket_ref, q_ref, kn_ref, vn_ref, ck_ref, cv_ref,
                       o_ref, nk_ref, nv_ref, bias_scr):
    @pl.when(pl.program_id(0) == 0)
    def _():
        _fill_bias(bucket_ref, rb_ref, bias_scr, T_DEC, per_head_table=True)

    nk_ref[:, 0:CACHE_ROWS - NEW_ROWS, :] = ck_ref[:, NEW_ROWS:CACHE_ROWS, :]
    nv_ref[:, 0:CACHE_ROWS - NEW_ROWS, :] = cv_ref[:, NEW_ROWS:CACHE_ROWS, :]

    pad = jnp.zeros((S_KEYS - CACHE_ROWS - NEW_ROWS, HEAD_DIM), F32)
    sink_col = jnp.concatenate([jnp.full((T_DEC, 1), sink_ref[h], F32) for h in range(ATT_HEADS)], axis=0)
    bias = bias_scr[...]
    for b in range(SAMPLE_BT):
        ts = slice(b * T_DEC, (b + 1) * T_DEC)
        head_cols = lambda ref, n: [ref[ts, h * HEAD_DIM:(h + 1) * HEAD_DIM] for h in range(n)]
        k_new, v_new = head_cols(kn_ref, KV_HEADS), head_cols(vn_ref, KV_HEADS)
        for g in range(KV_HEADS):
            new_rows = pl.ds(CACHE_ROWS - NEW_ROWS + g, T_DEC, stride=KV_HEADS)
            nk_ref[b, new_rows, :] = k_new[g]
            nv_ref[b, new_rows, :] = v_new[g]
        qa = jnp.concatenate(head_cols(q_ref, ATT_HEADS), axis=0).astype(BF16)
        k2 = jnp.concatenate([ck_ref[b]] + k_new + [pad], axis=0).astype(BF16)
        v2 = _with_ones(jnp.concatenate([cv_ref[b]] + v_new + [pad], axis=0).astype(BF16))
        o = _sink_softmax_av(_dot_nt(qa, k2) * ATT_SCALE + bias, sink_col, v2)
        for h in range(ATT_HEADS):
            o_ref[ts, h * HEAD_DIM:(h + 1) * HEAD_DIM] = o[h * T_DEC:(h + 1) * T_DEC].astype(BF16)


def _swa_sample(z, cache_k, cache_v, rb_flat, sinks):
    rows = SAMPLE_BT * T_DEC
    z_off = N_PROMPT // rows
    bucket = jnp.asarray(_sample_bucket_table())
    kcol = ATT_WIDTH // KV_WIDTH
    smem = pl.BlockSpec(memory_space=pltpu.SMEM)
    cache_spec = pl.BlockSpec((SAMPLE_BT, CACHE_ROWS, HEAD_DIM), lambda i: (i, 0, 0))
    cache_shape = jax.ShapeDtypeStruct((N_BATCH, CACHE_ROWS, HEAD_DIM), F32)
    return pl.pallas_call(
        _swa_sample_kernel,
        grid=(N_BATCH // SAMPLE_BT,),
        in_specs=[smem, smem,
                  pl.BlockSpec((ATT_HEADS * T_DEC, S_KEYS), lambda i: (0, 0)),
                  pl.BlockSpec((rows, ATT_WIDTH), lambda i: (i + z_off, 0)),
                  pl.BlockSpec((rows, KV_WIDTH), lambda i: (i + z_off, kcol)),
                  pl.BlockSpec((rows, KV_WIDTH), lambda i: (i + z_off, kcol + 1)),
                  cache_spec, cache_spec],
        out_specs=[pl.BlockSpec((rows, ATT_WIDTH), lambda i: (i, 0)), cache_spec, cache_spec],
        out_shape=[jax.ShapeDtypeStruct((N_SAMPLE, ATT_WIDTH), BF16), cache_shape, cache_shape],
        scratch_shapes=[pltpu.VMEM((ATT_HEADS * T_DEC, S_KEYS), F32)],
        compiler_params=_params(("arbitrary",)),
        name="swa_sample",
    )(rb_flat, sinks, bucket, z, z, z, cache_k, cache_v)


LANE_IG, LANE_LF, LANE_B = 0, M_HEADS, 2 * M_HEADS


def _gate_table(zg, gate_bias, seg_len):
    L = zg.shape[0]
    g = zg + gate_bias
    lane = _iota((L, LANES), 1)
    lf = _log_sigmoid(g)
    lf_only = jnp.where(jnp.logical_and(lane >= LANE_LF, lane < LANE_B), lf, 0.0)
    row = _iota((L, L), 0)
    col = _iota((L, L), 1)
    same_seg = (row // seg_len) == (col // seg_len)
    tril = jnp.where(jnp.logical_and(col <= row, same_seg), 1.0, 0.0).astype(BF16)
    cum = pltpu.roll(_dot_exact_lhs01(tril, lf_only), M_HEADS, axis=1)
    table = jnp.where(lane < LANE_LF, g, jnp.where(lane < LANE_B, lf, jnp.where(lane < LANE_B + M_HEADS, cum, 0.0)))
    return table, jnp.logical_and(col <= row, same_seg)


def _mlstm_intra(table, table_t, mask, m0_col, h, q, k, v):
    b_c = table[:, LANE_B + h:LANE_B + h + 1]
    b_r = table_t[LANE_B + h:LANE_B + h + 1, :]
    ig_r = table_t[LANE_IG + h:LANE_IG + h + 1, :]
    log_d = jnp.where(mask, b_c - b_r + ig_r, NEG_INF)
    log_inter = b_c + m0_col
    m_t = jnp.maximum(log_inter, jnp.max(log_d, axis=1, keepdims=True))
    d = jnp.exp(log_d - m_t)
    w_inter = jnp.exp(log_inter - m_t)
    s = _dot_nt(q.astype(BF16), k.astype(BF16)) * d
    num_intra = _dot(s.astype(BF16), v.astype(BF16))
    den_intra = jnp.sum(s, axis=1, keepdims=True)
    return b_c, m_t, w_inter, num_intra, den_intra


def _mlstm_prompt_kernel(gb_ref, zg_ref, q_ref, k_ref, va_ref, vb_ref, oa_ref, ob_ref,
                         y_ref, c_out, n_out, m_out, c_scr, n_scr, m_scr):
    step = pl.program_id(0)
    L = ML_CHUNK

    @pl.when(step == 0)
    def _():
        c_scr[...] = jnp.zeros_like(c_scr)
        n_scr[...] = jnp.zeros_like(n_scr)
        m_scr[...] = jnp.zeros_like(m_scr)

    table, mask = _gate_table(zg_ref[...], gb_ref[...], L)
    table_t = table.T
    half = M_HEADS // 2
    for h in range(M_HEADS):
        v_ref, o_ref = (va_ref, oa_ref) if h < half else (vb_ref, ob_ref)
        vs = slice((h % half) * DV, (h % half + 1) * DV)
        q = q_ref[:, h * DK:(h + 1) * DK] * Q_SCALE
        k = k_ref[:, h * DK:(h + 1) * DK]
        v = v_ref[:, vs]
        m0 = m_scr[h:h + 1, 0:1]
        b_c, m_t, w_inter, num_intra, den_intra = _mlstm_intra(table, table_t, mask, m0, h, q, k, v)
        c_old = c_scr[h]
        n_old = n_scr[h:h + 1, :]
        num = num_intra + w_inter * _dot_nt(q.astype(BF16), c_old.astype(BF16))
        den = den_intra + w_inter * jnp.sum(q * n_old, axis=1, keepdims=True)
        hh = num / jnp.maximum(jnp.abs(den), jnp.exp(-m_t))
        y_ref[:, h * DV:(h + 1) * DV] = (_sigmoid(o_ref[:, vs]) * hh).astype(BF16)

        ig_c = table[:, LANE_IG + h:LANE_IG + h + 1]
        m_new = m_t[L - 1:L, :]
        b_last = b_c[L - 1:L, :]
        w_s = jnp.exp(b_last - b_c + ig_c - m_new)
        decay = jnp.exp(b_last + m0 - m_new)
        c_scr[h] = decay * c_old + _dot_tn((w_s * v).astype(BF16), k.astype(BF16))
        n_scr[h:h + 1, :] = decay * n_old + jnp.sum(w_s * k, axis=0, keepdims=True)
        m_scr[h:h + 1, :] = jnp.broadcast_to(m_new, (1, LANES))

    @pl.when(step == pl.num_programs(0) - 1)
    def _():
        c_out[...] = c_scr[...]
        n_out[...] = n_scr[...]
        m_out[...] = m_scr[...]


def _mlstm_prompt(z, zg, gate_bias):
    L = ML_CHUNK
    blk = M_HEADS * DK
    col = lambda c: pl.BlockSpec((L, blk), lambda i: (i, c))
    const = lambda shape: pl.BlockSpec(shape, lambda i: tuple(0 for _ in shape))
    return pl.pallas_call(
        _mlstm_prompt_kernel,
        grid=(N_PROMPT // L,),
        in_specs=[const((1, LANES)),
                  pl.BlockSpec((L, LANES), lambda i: (i, 0)),
                  col(3), col(4), col(5), col(6), col(7), col(8)],
        out_specs=[pl.BlockSpec((L, M_WIDTH), lambda i: (i, 0)),
                   const((M_HEADS, DV, DK)), const((M_HEADS, DK)), const((M_HEADS, LANES))],
        out_shape=[jax.ShapeDtypeStruct((N_PROMPT, M_WIDTH), BF16),
                   jax.ShapeDtypeStruct((M_HEADS, DV, DK), F32),
                   jax.ShapeDtypeStruct((M_HEADS, DK), F32),
                   jax.ShapeDtypeStruct((M_HEADS, LANES), F32)],
        scratch_shapes=[pltpu.VMEM((M_HEADS, DV, DK), F32),
                        pltpu.VMEM((M_HEADS, DK), F32),
                        pltpu.VMEM((M_HEADS, LANES), F32)],
        compiler_params=_params(("arbitrary",)),
        name="mlstm_prompt",
    )(gate_bias, zg, z, z, z, z, z, z)


S_ROWS = SAMPLE_BT * T_DEC


def _mlstm_sample_kernel(gb_ref, zg_ref, m0_ref, n0t_ref, q_ref, k_ref, va_ref, vb_ref, oa_ref, ob_ref,
                         c0_ref, n0_ref, y_ref, c_out, n_out, m_out):
    L = S_ROWS
    table, mask = _gate_table(zg_ref[...], gb_ref[...], T_DEC)
    table_t = jnp.concatenate([table, jnp.zeros((LANES - L, LANES), F32)], axis=0).T[:, 0:L]
    m0_all = m0_ref[...]
    lane = _iota((L, LANES), 1)
    row_b = _iota((SAMPLE_BT, L), 1) // T_DEC
    seg_sum = jnp.where(row_b == _iota((SAMPLE_BT, L), 0), 1.0, 0.0)
    m_tok = jnp.zeros((L, LANES), F32)
    half = M_HEADS // 2
    for h in range(M_HEADS):
        v_ref, o_ref = (va_ref, oa_ref) if h < half else (vb_ref, ob_ref)
        vs = slice((h % half) * DV, (h % half + 1) * DV)
        q = q_ref[:, h * DK:(h + 1) * DK] * Q_SCALE
        k = k_ref[:, h * DK:(h + 1) * DK]
        v = v_ref[:, vs]
        m0 = m0_all[:, h:h + 1]
        b_c, m_t, w_inter, num_intra, den_intra = _mlstm_intra(table, table_t, mask, m0, h, q, k, v)
        qb = q.astype(BF16)
        num_inter = jnp.concatenate(
            [_dot_nt(qb[b * T_DEC:(b + 1) * T_DEC], c0_ref[b, h].astype(BF16)) for b in range(SAMPLE_BT)], axis=0)
        num = num_intra + w_inter * num_inter
        den = den_intra + w_inter * jnp.sum(q * n0t_ref[:, h * DK:(h + 1) * DK], axis=1, keepdims=True)
        hh = num / jnp.maximum(jnp.abs(den), jnp.exp(-m_t))
        y_ref[:, h * DV:(h + 1) * DV] = (_sigmoid(o_ref[:, vs]) * hh).astype(BF16)
        m_tok = jnp.where(lane == h, m_t, m_tok)

        def last_tok(x):
            x3 = x.reshape(SAMPLE_BT, T_DEC, 1)
            return jnp.broadcast_to(x3[:, T_DEC - 1:T_DEC, :], x3.shape).reshape(L, 1)

        ig_c = table[:, LANE_IG + h:LANE_IG + h + 1]
        m_new = last_tok(m_t)
        b_last = last_tok(b_c)
        w_s = jnp.exp(b_last - b_c + ig_c - m_new)
        decay = jnp.exp(b_last + m0 - m_new)
        wv = (w_s * v).astype(BF16)
        kb = k.astype(BF16)
        rowsel = _iota((L, 1), 0) // T_DEC
        for b in range(SAMPLE_BT):
            dec_b = decay[b * T_DEC + T_DEC - 1:(b + 1) * T_DEC, :]
            wv_b = jnp.where(rowsel == b, wv, jnp.zeros_like(wv))
            c_out[b, h] = dec_b * c0_ref[b, h] + _dot_tn(wv_b, kb)
        dec_rows = decay.reshape(SAMPLE_BT, T_DEC, 1)[:, T_DEC - 1, :]
        n_out[:, h * DK:(h + 1) * DK] = dec_rows * n0_ref[:, h * DK:(h + 1) * DK] + jnp.dot(
            seg_sum, w_s * k, preferred_element_type=F32, precision=lax.Precision.HIGHEST)
    m_out[...] = m_tok


def _mlstm_sample(z, zg, gate_bias, m0_tok, n0_tok, state_c, state_n):
    L = S_ROWS
    blk = M_HEADS * DK
    z_off = N_PROMPT // L
    col = lambda c: pl.BlockSpec((L, blk), lambda i: (i + z_off, c))
    c_spec = pl.BlockSpec((SAMPLE_BT, M_HEADS, DV, DK), lambda i: (i, 0, 0, 0))
    n_spec = pl.BlockSpec((SAMPLE_BT, blk), lambda i: (i, 0))
    return pl.pallas_call(
        _mlstm_sample_kernel,
        grid=(N_BATCH // SAMPLE_BT,),
        in_specs=[pl.BlockSpec((1, LANES), lambda i: (0, 0)),
                  pl.BlockSpec((L, LANES), lambda i: (i + z_off, 0)),
                  pl.BlockSpec((L, LANES), lambda i: (i, 0)),
                  pl.BlockSpec((L, blk), lambda i: (i, 0)),
                  col(3), col(4), col(5), col(6), col(7), col(8),
                  c_spec, n_spec],
        out_specs=[pl.BlockSpec((L, M_WIDTH), lambda i: (i, 0)), c_spec, n_spec,
                   pl.BlockSpec((L, LANES), lambda i: (i, 0))],
        out_shape=[jax.ShapeDtypeStruct((N_SAMPLE, M_WIDTH), BF16),
                   jax.ShapeDtypeStruct((N_BATCH, M_HEADS, DV, DK), F32),
                   jax.ShapeDtypeStruct((N_BATCH, blk), F32),
                   jax.ShapeDtypeStruct((N_SAMPLE, LANES), F32)],
        compiler_params=_params(("arbitrary",)),
        name="mlstm_sample",
    )(gate_bias, zg, m0_tok, n0_tok, z, z, z, z, z, z, state_c, state_n)


def _outproj_kernel(ya_ref, ym_ref, wa_ref, wm_ref, x_ref, gt_ref, o_ref):
    mix = _dot(ya_ref[...], wa_ref[...]) + _dot(ym_ref[...], wm_ref[...])
    o_ref[...] = x_ref[...] + _mod_rows(gt_ref) * mix


def _outproj(ya, ym, w_out_bf, x, mod, per_row):
    n = x.shape[0]
    TM = TM_PROJ
    gate1_col = 2 * (D // TN_OUT)
    return pl.pallas_call(
        _outproj_kernel,
        grid=(n // TM, D // TN_OUT),
        in_specs=[pl.BlockSpec((TM, ATT_WIDTH), lambda i, j: (i, 0)),
                  pl.BlockSpec((TM, M_WIDTH), lambda i, j: (i, 0)),
                  pl.BlockSpec((ATT_WIDTH, TN_OUT), lambda i, j: (0, j)),
                  pl.BlockSpec((M_WIDTH, TN_OUT), lambda i, j: (1, j)),
                  pl.BlockSpec((TM, TN_OUT), lambda i, j: (i, j)),
                  _mod_spec(per_row, TM, TN_OUT, lambda j: gate1_col + j)],
        out_specs=pl.BlockSpec((TM, TN_OUT), lambda i, j: (i, j)),
        out_shape=jax.ShapeDtypeStruct((n, D), F32),
        compiler_params=_params(("arbitrary", "arbitrary")),
        name="outproj",
    )(ya, ym, w_out_bf, w_out_bf, x, mod)


def _router_kernel(x_ref, sh_ref, sc_ref, g_ref, wr_ref, br_ref, h_ref, route_ref):
    x = x_ref[...]
    y = x * lax.rsqrt(jnp.mean(x * x, axis=-1, keepdims=True) + EPS) * g_ref[...]
    h2 = y * (1.0 + _mod_rows(sc_ref)) + _mod_rows(sh_ref)
    h_hi = h2.astype(BF16)
    h_ref[...] = _pack_bf16_pairs(h2)
    h_lo = (h2 - h_hi.astype(F32)).astype(BF16)
    logits = _dot(h_hi, wr_ref[0]) + (_dot(h_hi, wr_ref[1]) + _dot(h_lo, wr_ref[0])) + br_ref[...]
    lane = _iota(logits.shape, 1)

    def first_max(vals):
        vmax = jnp.max(vals, axis=1, keepdims=True)
        idx = jnp.min(jnp.where(vals == vmax, lane, LANES), axis=1, keepdims=True)
        return vmax, idx

    gl = jnp.where(lane < N_GROUPS, logits, NEG_INF)
    gmax, grp = first_max(gl)
    p_grp = 1.0 / jnp.sum(jnp.exp(gl - gmax), axis=1, keepdims=True)
    e_lane = lane - N_GROUPS
    in_grp = jnp.logical_and(e_lane >= 0, jnp.logical_and(e_lane < N_EXPERTS, e_lane // EXP_PER_GROUP == grp))
    el = jnp.where(in_grp, logits, NEG_INF)
    v1, i1 = first_max(el)
    v2, i2 = first_max(jnp.where(lane == i1, NEG_INF, el))
    e2w = jnp.exp(v2 - v1)
    w1 = 1.0 / (1.0 + e2w)
    w2 = e2w / (1.0 + e2w)
    route = jnp.where(lane == 0, (i1 - N_GROUPS).astype(F32),
                      jnp.where(lane == 1, (i2 - N_GROUPS).astype(F32),
                                jnp.where(lane == 2, p_grp * w1, jnp.where(lane == 3, p_grp * w2, 0.0))))
    route_ref[...] = route


def _router_merged_kernel(xp_ref, xs_ref, shp_ref, scp_ref, shs_ref, scs_ref, g_ref, wr_ref, br_ref,
                          h_ref, route_ref):
    i = pl.program_id(0)

    @pl.when(i < PROMPT_TOK_BLOCKS)
    def _():
        _router_kernel(xp_ref, shp_ref, scp_ref, g_ref, wr_ref, br_ref, h_ref, route_ref)

    @pl.when(i >= PROMPT_TOK_BLOCKS)
    def _():
        _router_kernel(xs_ref, shs_ref, scs_ref, g_ref, wr_ref, br_ref, h_ref, route_ref)


def _router(x1_p, x1_s, mod_p, mod_s, g_ffn, w_route, b_route):
    pi, si = _prompt_block, _sample_block
    return pl.pallas_call(
        _router_merged_kernel,
        grid=(N_TOK // TM_TOK,),
        in_specs=[pl.BlockSpec((TM_TOK, D), lambda i: (pi(i), 0)),
                  pl.BlockSpec((TM_TOK, D), lambda i: (si(i), 0)),
                  pl.BlockSpec((1, D), lambda i: (0, 3)),
                  pl.BlockSpec((1, D), lambda i: (0, 4)),
                  pl.BlockSpec((TM_TOK // T_DEC, 1, D), lambda i: (si(i), 0, 3)),
                  pl.BlockSpec((TM_TOK // T_DEC, 1, D), lambda i: (si(i), 0, 4)),
                  pl.BlockSpec((1, D), lambda i: (0, 0)),
                  pl.BlockSpec((2, D, LANES), lambda i: (0, 0, 0)),
                  pl.BlockSpec((1, LANES), lambda i: (0, 0))],
        out_specs=[pl.BlockSpec((TM_TOK, D_PACK), lambda i: (i, 0)),
                   pl.BlockSpec((TM_TOK, LANES), lambda i: (i, 0))],
        out_shape=[jax.ShapeDtypeStruct((N_TOK, D_PACK), jnp.uint32),
                   jax.ShapeDtypeStruct((N_TOK, LANES), F32)],
        compiler_params=_params(("arbitrary",)),
        name="router",
    )(x1_p, x1_s, mod_p, mod_p, mod_s, mod_s, g_ffn, w_route, b_route)


def _rank_kernel(route_ref, dest_ref, cnt_ref, carry, rank_scr):
    pas = pl.program_id(0)
    step = pl.program_id(1)

    @pl.when(jnp.logical_and(pas == 0, step == 0))
    def _():
        carry[...] = jnp.zeros_like(carry)

    route = route_ref[...]
    n = route.shape[0]
    lane = _iota((n, LANES), 1).astype(F32)
    o1 = jnp.where(lane == route[:, 0:1], 1.0, 0.0)
    o2 = jnp.where(lane == route[:, 1:2], 1.0, 0.0)
    lane_i = _iota((n, LANES), 1)
    rows = pl.ds(pl.multiple_of(step * TM_RANK, TM_RANK), TM_RANK)

    @pl.when(pas == 0)
    def _():
        both = o1 + o2
        strict = jnp.where(_iota((n, n), 1) < _iota((n, n), 0), 1.0, 0.0).astype(BF16)
        prior = _dot(strict, both.astype(BF16)) + carry[0:1, :]
        r1 = jnp.sum(o1 * prior, axis=1, keepdims=True)
        r2 = jnp.sum(o2 * prior, axis=1, keepdims=True)
        rank_scr[rows, :] = jnp.where(lane_i == 0, r1, jnp.where(lane_i == 1, r2, 0.0))
        carry[...] = carry[...] + jnp.sum(both, axis=0, keepdims=True)

    @pl.when(pas == 1)
    def _():
        counts = carry[...]
        tiles_per = jnp.floor((counts + (TM_MOE - 1)) * (1.0 / TM_MOE))
        before = jnp.where(_iota((LANES, LANES), 0) < _iota((LANES, LANES), 1), 1.0, 0.0).astype(BF16)
        pad_start = _dot(tiles_per.astype(BF16), before)[0:1, :] * TM_MOE
        rank = rank_scr[rows, :]
        d1 = jnp.sum(o1 * pad_start, axis=1, keepdims=True) + rank[:, 0:1]
        d2 = jnp.sum(o2 * pad_start, axis=1, keepdims=True) + rank[:, 1:2]
        dest_ref[...] = jnp.where(lane_i == 0, d1, jnp.where(lane_i == 1, d2, 0.0)).astype(jnp.int32)
        cnt_ref[...] = counts


def _rank(route):
    n = route.shape[0]
    return pl.pallas_call(
        _rank_kernel,
        grid=(2, n // TM_RANK),
        in_specs=[pl.BlockSpec((TM_RANK, LANES), lambda p, i: (i, 0))],
        out_specs=[pl.BlockSpec((TM_RANK, LANES), lambda p, i: (i * p, 0)),
                   pl.BlockSpec((SUBLANES, LANES), lambda p, i: (0, 0))],
        out_shape=[jax.ShapeDtypeStruct((n, LANES), jnp.int32), jax.ShapeDtypeStruct((SUBLANES, LANES), F32)],
        scratch_shapes=[pltpu.VMEM((SUBLANES, LANES), F32), pltpu.VMEM((n, LANES), F32)],
        compiler_params=_params(("arbitrary", "arbitrary")),
        name="rank",
    )(route)


def _row_copy(src, dst, sem):
    return pltpu.make_async_copy(src, dst, sem)


ROW_UNROLL = 8
assert T_DEC == SUBLANES
D_PACK = D // 2

ZERO_FIRST, ZERO_ANYTIME = 1, 2


def _dispatch_kernel(d1_ref, d2_ref, zc_ref, h_ref, xs_out, zbuf, sem):
    step = pl.program_id(0)
    base = step * TM_TOK

    def zero_tile(t, s):
        rows = pl.ds(pl.multiple_of(t * TM_MOE, TM_MOE), TM_MOE)
        return pltpu.make_async_copy(zbuf, xs_out.at[rows], sem.at[s])

    def for_tiles(cls, s, act):
        def body(t, carry):
            @pl.when(zc_ref[t] == cls)
            def _():
                act(zero_tile(t, s))
            return carry
        lax.fori_loop(0, MAX_TILES, body, 0)

    @pl.when(step == 0)
    def _():
        zbuf[...] = jnp.zeros_like(zbuf)
        for_tiles(ZERO_FIRST, 1, lambda c: c.start())
        for_tiles(ZERO_ANYTIME, 2, lambda c: c.start())
        for_tiles(ZERO_FIRST, 1, lambda c: c.wait())

    @pl.when(step == pl.num_programs(0) - 1)
    def _():
        for_tiles(ZERO_ANYTIME, 2, lambda c: c.wait())

    def issue(r, carry):
        _row_copy(h_ref.at[pl.ds(r, 1)], xs_out.at[pl.ds(d1_ref[base + r], 1)], sem.at[0]).start()
        _row_copy(h_ref.at[pl.ds(r, 1)], xs_out.at[pl.ds(d2_ref[base + r], 1)], sem.at[0]).start()
        return carry

    lax.fori_loop(0, TM_TOK, issue, 0, unroll=ROW_UNROLL)
    for _ in range(TOP_K):
        _row_copy(h_ref, xs_out.at[pl.ds(0, TM_TOK)], sem.at[0]).wait()


def _dispatch(dests, zero_class, h2):
    grid_spec = pltpu.PrefetchScalarGridSpec(
        num_scalar_prefetch=3,
        grid=(N_TOK // TM_TOK,),
        in_specs=[pl.BlockSpec((TM_TOK, D_PACK), lambda i, *_: (i, 0))],
        out_specs=pl.BlockSpec(memory_space=pl.ANY),
        scratch_shapes=[pltpu.VMEM((TM_MOE, D_PACK), jnp.uint32), pltpu.SemaphoreType.DMA((3,))],
    )
    return pl.pallas_call(
        _dispatch_kernel,
        grid_spec=grid_spec,
        out_shape=jax.ShapeDtypeStruct((A_PAD, D_PACK), jnp.uint32),
        compiler_params=_params(("arbitrary",)),
        name="dispatch",
    )(*dests, zero_class, h2)


N_UP_CHUNKS = D_FF // TN_FF
ROW_CAPS = tuple(range(TM_MOE // 4, TM_MOE + 1, TM_MOE // 4))


def _for_row_cap(rows, body):
    lo = 0
    for cap in ROW_CAPS:
        pl.when(jnp.logical_and(rows > lo, rows <= cap))(functools.partial(body, cap))
        lo = cap


def _expert_up_kernel(te_ref, rows_ref, first_ref, nxt_ref, run_ref, meta_ref, x_ref, wg_hbm, wu_hbm,
                      o_ref, wbuf, sem):
    n = pl.program_id(0)
    t = pl.program_id(1)
    slot = lax.rem(n * meta_ref[1] + run_ref[t], 2)

    def fetch(e, chunk, s):
        cols = pl.ds(pl.multiple_of(chunk * TN_FF, TN_FF), TN_FF)
        return (pltpu.make_async_copy(wg_hbm.at[e, :, cols], wbuf.at[s, 0], sem.at[s, 0]),
                pltpu.make_async_copy(wu_hbm.at[e, :, cols], wbuf.at[s, 1], sem.at[s, 1]))

    @pl.when(first_ref[t] == 1)
    def _():
        @pl.when(jnp.logical_and(n == 0, t == 0))
        def _():
            for c in fetch(te_ref[0], 0, 0):
                c.start()

        for c in fetch(te_ref[t], n, slot):
            c.wait()

        @pl.when(nxt_ref[t] >= 0)
        def _():
            for c in fetch(nxt_ref[t], n, 1 - slot):
                c.start()

        @pl.when(jnp.logical_and(nxt_ref[t] < 0, n + 1 < N_UP_CHUNKS))
        def _():
            for c in fetch(te_ref[0], n + 1, 1 - slot):
                c.start()

    def compute(cap):
        x = jnp.concatenate([half.astype(BF16) for half in _unpack_bf16_pairs(x_ref[0:cap, :])], axis=1)
        g = _dot(x, wbuf[slot, 0].astype(BF16))
        u = _dot(x, wbuf[slot, 1].astype(BF16))
        o_ref[0:cap, :] = (g * _sigmoid(g) * u).astype(BF16)
        if cap < TM_MOE:
            o_ref[cap:, :] = jnp.zeros((TM_MOE - cap, TN_FF), BF16)

    _for_row_cap(rows_ref[t], compute)

    @pl.when(rows_ref[t] == 0)
    def _():
        o_ref[...] = jnp.zeros_like(o_ref)


def _expert_up(tiles, xs, w_gate, w_up):
    last = lambda t, meta: jnp.minimum(t, meta[0] - 1)
    grid_spec = pltpu.PrefetchScalarGridSpec(
        num_scalar_prefetch=6,
        grid=(N_UP_CHUNKS, MAX_TILES),
        in_specs=[pl.BlockSpec((TM_MOE, D_PACK), lambda n, t, *s: (last(t, s[5]), 0)),
                  pl.BlockSpec(memory_space=pl.ANY),
                  pl.BlockSpec(memory_space=pl.ANY)],
        out_specs=pl.BlockSpec((TM_MOE, TN_FF), lambda n, t, *s: (t, n)),
        scratch_shapes=[pltpu.VMEM((2, 2, D, TN_FF), F32), pltpu.SemaphoreType.DMA((2, 2))],
    )
    return pl.pallas_call(
        _expert_up_kernel,
        grid_spec=grid_spec,
        out_shape=jax.ShapeDtypeStruct((A_PAD, D_FF), BF16),
        compiler_params=_params(("arbitrary", "arbitrary")),
        name="expert_up",
    )(*tiles, xs, w_gate, w_up)


def _expert_down_kernel(te_ref, rows_ref, first_ref, nxt_ref, run_ref, meta_ref, h_ref, wd_hbm,
                        o_ref, wbuf, sem):
    t = pl.program_id(0)
    slot = lax.rem(run_ref[t], 2)

    def fetch(e, s):
        return pltpu.make_async_copy(wd_hbm.at[e], wbuf.at[s], sem.at[s])

    @pl.when(first_ref[t] == 1)
    def _():
        @pl.when(t == 0)
        def _():
            fetch(te_ref[0], 0).start()

        fetch(te_ref[t], slot).wait()

        @pl.when(nxt_ref[t] >= 0)
        def _():
            fetch(nxt_ref[t], 1 - slot).start()

    def compute(cap):
        o_ref[0:cap, :] = _pack_bf16_pairs(_dot(h_ref[0:cap, :], wbuf[slot].astype(BF16)))
        if cap < TM_MOE:
            o_ref[cap:, :] = jnp.zeros((TM_MOE - cap, D_PACK), jnp.uint32)

    _for_row_cap(rows_ref[t], compute)

    @pl.when(rows_ref[t] == 0)
    def _():
        o_ref[...] = jnp.zeros_like(o_ref)


def _expert_down(tiles, h1, w_down):
    last = lambda t, meta: jnp.minimum(t, meta[0] - 1)
    grid_spec = pltpu.PrefetchScalarGridSpec(
        num_scalar_prefetch=6,
        grid=(MAX_TILES,),
        in_specs=[pl.BlockSpec((TM_MOE, D_FF), lambda t, *s: (last(t, s[5]), 0)),
                  pl.BlockSpec(memory_space=pl.ANY)],
        out_specs=pl.BlockSpec((TM_MOE, D_PACK), lambda t, *s: (t, 0)),
        scratch_shapes=[pltpu.VMEM((2, D_FF, D), F32), pltpu.SemaphoreType.DMA((2,))],
    )
    return pl.pallas_call(
        _expert_down_kernel,
        grid_spec=grid_spec,
        out_shape=jax.ShapeDtypeStruct((A_PAD, D_PACK), jnp.uint32),
        compiler_params=_params(("arbitrary",)),
        name="expert_down",
    )(*tiles, h1, w_down)


def _combine_kernel(d1_ref, d2_ref, x_ref, route_ref, gt_ref, gf_ref, o_hbm,
                    y_ref, buf, sem, *, tok_offset):
    step = pl.program_id(0)
    slot = lax.rem(step, 2)

    def gather(s, into):
        base = tok_offset + s * TM_TOK

        def issue(r, carry):
            _row_copy(o_hbm.at[pl.ds(d1_ref[base + r], 1)], buf.at[into, 0, pl.ds(r, 1)], sem.at[into]).start()
            _row_copy(o_hbm.at[pl.ds(d2_ref[base + r], 1)], buf.at[into, 1, pl.ds(r, 1)], sem.at[into]).start()
            return carry

        lax.fori_loop(0, TM_TOK, issue, 0, unroll=ROW_UNROLL)

    @pl.when(step == 0)
    def _():
        gather(0, 0)

    @pl.when(step + 1 < pl.num_programs(0))
    def _():
        gather(step + 1, 1 - slot)

    for k in range(TOP_K):
        _row_copy(o_hbm.at[pl.ds(0, TM_TOK)], buf.at[slot, k], sem.at[slot]).wait()

    route = route_ref[...]
    lo1, hi1 = _unpack_bf16_pairs(buf[slot, 0])
    lo2, hi2 = _unpack_bf16_pairs(buf[slot, 1])
    g1, g2 = route[:, 2:3], route[:, 3:4]
    moe = jnp.concatenate([g1 * lo1 + g2 * lo2, g1 * hi1 + g2 * hi2], axis=1)
    x2 = x_ref[...] + _mod_rows(gt_ref) * moe
    y_ref[...] = x2 * lax.rsqrt(jnp.mean(x2 * x2, axis=-1, keepdims=True) + EPS) * gf_ref[...]


def _combine(dests, x1, route, mod, per_row, g_final, o_rows, tok_offset):
    n = x1.shape[0]
    off = tok_offset // TM_TOK
    grid_spec = pltpu.PrefetchScalarGridSpec(
        num_scalar_prefetch=2,
        grid=(n // TM_TOK,),
        in_specs=[pl.BlockSpec((TM_TOK, D), lambda i, *_: (i, 0)),
                  pl.BlockSpec((TM_TOK, LANES), lambda i, *_: (i + off, 0)),
                  _mod_spec(per_row, TM_TOK, D, lambda *_: 5),
                  pl.BlockSpec((1, D), lambda i, *_: (0, 0)),
                  pl.BlockSpec(memory_space=pl.ANY)],
        out_specs=pl.BlockSpec((TM_TOK, D), lambda i, *_: (i, 0)),
        scratch_shapes=[pltpu.VMEM((2, TOP_K, TM_TOK, D_PACK), jnp.uint32), pltpu.SemaphoreType.DMA((2,))],
    )
    return pl.pallas_call(
        functools.partial(_combine_kernel, tok_offset=tok_offset),
        grid_spec=grid_spec,
        out_shape=jax.ShapeDtypeStruct((n, D), F32),
        compiler_params=_params(("arbitrary",)),
        name="combine",
    )(*dests, x1, route, mod, g_final, o_rows)


def kernel(x_prompt, x_sample, cache_k, cache_v, state_C, state_n, state_m, c_prompt, c_sample, rel_bias, w_ada, b_ada, g_mix, g_ffn, w_in, sinks, b_igate, b_fgate, w_out, w_router_grp, b_router_grp, w_router_exp, b_router_exp, w_gate, w_up, w_down, g_final):
    xp = x_prompt.reshape(N_PROMPT, D)
    xs = x_sample.reshape(N_SAMPLE, D)

    c_all = jnp.concatenate([c_prompt, c_sample, jnp.zeros((C_ROWS - 1 - N_BATCH, D), F32)], axis=0)
    mod = _ada(c_all, w_ada[0], b_ada)
    mod_p = mod[0:1]
    mod_s = mod[1:1 + N_BATCH].reshape(N_BATCH, 1, -1)

    w_in_t = jnp.swapaxes(w_in[0], 0, 1)
    w_gates_t = jnp.pad(w_in_t[Z_WIDTH:].astype(BF16), ((0, LANES - 2 * M_HEADS), (0, 0)))
    h_all, zg = _norm(xp, xs, mod_p, mod_s, g_mix, w_gates_t)
    z = _inproj(h_all, w_in_t)

    rb_flat = rel_bias.reshape(NUM_BUCKETS * ATT_HEADS)
    sink_v = sinks[0]
    ya_p = _swa_prompt(z, rb_flat, sink_v)
    ya_s, nk_s, nv_s = _swa_sample(z,cache_k.reshape(N_BATCH, CACHE_ROWS, HEAD_DIM),
                                   cache_v.reshape(N_BATCH, CACHE_ROWS, HEAD_DIM), rb_flat, sink_v)

    gate_bias = jnp.concatenate([b_igate[0], b_fgate[0], jnp.zeros((LANES - 2 * M_HEADS,), F32)]).reshape(1, LANES)
    ym_p, c_p, n_p, m_p = _mlstm_prompt(z, zg, gate_bias)
    m0_tok = jnp.pad(jnp.repeat(state_m[0], T_DEC, axis=0), ((0, 0), (0, LANES - M_HEADS)))
    n0_flat = state_n[0].reshape(N_BATCH, M_HEADS * DK)
    n0_tok = jnp.repeat(n0_flat, T_DEC, axis=0)
    ym_s, c_s, n_s, m_s = _mlstm_sample(z, zg, gate_bias, m0_tok, n0_tok, state_C[0], n0_flat)

    w_out_bf = w_out[0].astype(BF16)
    x1_p = _outproj(ya_p, ym_p, w_out_bf, xp, mod_p, False)
    x1_s = _outproj(ya_s, ym_s, w_out_bf, xs, mod_s, True)

    w_route = jnp.pad(jnp.concatenate([w_router_grp[0], w_router_exp[0]], axis=1),
                      ((0, 0), (0, LANES - N_GROUPS - N_EXPERTS)))
    b_route = jnp.pad(jnp.concatenate([b_router_grp[0], b_router_exp[0]]),
                      (0, LANES - N_GROUPS - N_EXPERTS)).reshape(1, LANES)
    w_route_hi = w_route.astype(BF16)
    w_route_split = jnp.stack([w_route_hi, (w_route - w_route_hi.astype(F32)).astype(BF16)])
    h2, route = _router(x1_p, x1_s, mod_p, mod_s, g_ffn, w_route_split, b_route)

    dest, counts = _rank(route)
    dests = (dest[:, 0], dest[:, 1])

    i32 = lambda a: a.astype(jnp.int32)
    cnt = i32(counts[0, :N_EXPERTS])
    tiles_per = (cnt + TM_MOE - 1) // TM_MOE
    tile_end = jnp.cumsum(tiles_per)
    n_tiles = tile_end[-1]
    tile_ids = jnp.arange(MAX_TILES, dtype=jnp.int32)
    tile_expert = i32(jnp.minimum(jnp.searchsorted(tile_end, tile_ids, side="right"), N_EXPERTS - 1))
    last_expert = tile_expert[jnp.maximum(n_tiles - 1, 0)]
    tile_valid = tile_ids < n_tiles
    tile_expert = jnp.where(tile_valid, tile_expert, last_expert)
    prev_expert = jnp.concatenate([jnp.full((1,), -1, jnp.int32), tile_expert[:-1]])
    run_first = jnp.logical_and(tile_valid, tile_expert != prev_expert)
    run_id = jnp.maximum(jnp.cumsum(i32(run_first)) - 1, 0)
    expert_ids = jnp.arange(N_EXPERTS, dtype=jnp.int32)
    used = jnp.where(tiles_per > 0, expert_ids, N_EXPERTS)
    next_used = jnp.concatenate([lax.cummin(used[::-1])[::-1][1:], jnp.full((1,), N_EXPERTS, jnp.int32)])
    next_used = jnp.where(next_used >= N_EXPERTS, -1, next_used)
    first_tile = tile_end - tiles_per
    tile_rows = jnp.clip(cnt[tile_expert] - (tile_ids - first_tile[tile_expert]) * TM_MOE, 0, TM_MOE)
    tile_rows = jnp.where(tile_valid, tile_rows, 0)
    tiles = (tile_expert, i32(tile_rows), i32(run_first), next_used[tile_expert], i32(run_id),
             jnp.stack([n_tiles, jnp.sum(i32(run_first))]).astype(jnp.int32))
    next_expert = jnp.concatenate([tile_expert[1:], jnp.full((1,), -1, jnp.int32)])
    run_last = jnp.logical_or(tile_expert != next_expert, tile_ids == n_tiles - 1)
    zero_class = jnp.where(tile_valid, jnp.where(run_last, ZERO_FIRST, 0), ZERO_ANYTIME)
    xs_rows = _dispatch(dests, i32(zero_class), h2)
    h1 = _expert_up(tiles, xs_rows, w_gate[0], w_up[0])
    o_rows = _expert_down(tiles, h1, w_down[0])

    gf = g_final.reshape(1, D)
    y_p = _combine(dests, x1_p, route, mod_p, False, gf, o_rows, 0)
    y_s = _combine(dests, x1_s, route, mod_s, True, gf, o_rows, N_PROMPT)

    kv5 = lambda a: a.reshape(1, -1, WINDOW, KV_HEADS, HEAD_DIM)
    kcol = ATT_WIDTH
    nk_p = z[N_PROMPT - WINDOW:N_PROMPT, kcol:kcol + KV_WIDTH]
    nv_p = z[N_PROMPT - WINDOW:N_PROMPT, kcol + KV_WIDTH:kcol + 2 * KV_WIDTH]
    return (y_p.reshape(1, N_PROMPT, D), y_s.reshape(N_BATCH, T_DEC, D),
            kv5(nk_p), kv5(nv_p),
            c_p.reshape(1, 1, M_HEADS, DV, DK), n_p.reshape(1, 1, M_HEADS, DK), m_p[:, 0].reshape(1, 1, M_HEADS),
            kv5(nk_s), kv5(nv_s),
            c_s.reshape(1, N_BATCH, M_HEADS, DV, DK), n_s.reshape(1, N_BATCH, M_HEADS, DK),
            m_s.reshape(N_BATCH, T_DEC, LANES)[:, T_DEC - 1, :M_HEADS].reshape(1, N_BATCH, M_HEADS))
```

```python
import functools
import math

import numpy as np
import jax
import jax.numpy as jnp
from jax import lax
from jax.experimental import pallas as pl
from jax.experimental.pallas import tpu as pltpu

F32 = jnp.float32
BF16 = jnp.bfloat16
NEG_INF = float("-inf")

D = 4096
N_PROMPT = 8192
N_BATCH = 128
T_DEC = 8
N_SAMPLE = N_BATCH * T_DEC
N_TOK = N_PROMPT + N_SAMPLE
HEAD_DIM = 128
ATT_HEADS = 16
KV_HEADS = 4
GROUP = ATT_HEADS // KV_HEADS
WINDOW = 128
ATT_WIDTH = ATT_HEADS * HEAD_DIM
KV_WIDTH = KV_HEADS * HEAD_DIM
NUM_BUCKETS = 32
MAX_EXACT = 16
MAX_DISTANCE = 128
M_HEADS = 8
DK = 128
DV = 256
M_WIDTH = M_HEADS * DV
Z_WIDTH = ATT_WIDTH + 2 * KV_WIDTH + 2 * M_HEADS * DK + 2 * M_WIDTH
N_GROUPS = 4
EXP_PER_GROUP = 8
N_EXPERTS = N_GROUPS * EXP_PER_GROUP
TOP_K = 2
D_FF = 1024
EPS = 1e-6
ATT_SCALE = HEAD_DIM ** -0.5
Q_SCALE = DK ** -0.5

LANES = 128
SUBLANES = 8
VMEM_LIMIT = 56 * 1024 * 1024

TM_PROJ = 512
TN_IN = 1024
TN_OUT = 1024
TN_ADA = 512
C_ROWS = 136
ATT_BLOCK = 128
SAMPLE_BT = 8
ML_CHUNK = 256
TM_MOE = 256
TN_FF = 512
N_ASSIGN = N_TOK * TOP_K
MAX_TILES = N_ASSIGN // TM_MOE + N_EXPERTS
A_PAD = MAX_TILES * TM_MOE
TM_TOK = 256
TM_RANK = 1024


def _params(sem):
    return pltpu.CompilerParams(dimension_semantics=sem, vmem_limit_bytes=VMEM_LIMIT)


def _iota(shape, dim):
    return lax.broadcasted_iota(jnp.int32, shape, dim)


def _dot(a, b):
    return jnp.dot(a, b, preferred_element_type=F32)


def _dot_nt(a, b):
    return lax.dot_general(a, b, (((1,), (1,)), ((), ())), preferred_element_type=F32)


def _dot_tn(a, b):
    return lax.dot_general(a, b, (((0,), (0,)), ((), ())), preferred_element_type=F32)


def _split3(x):
    x1 = x.astype(BF16)
    r1 = x - x1.astype(F32)
    x2 = r1.astype(BF16)
    r2 = r1 - x2.astype(F32)
    return x1, x2, r2.astype(BF16)


def _dot_exact_lhs01(a01, x):
    x1, x2, x3 = _split3(x)
    return _dot(a01, x1) + _dot(a01, x2) + _dot(a01, x3)


def _pack_bf16_pairs(x):
    w = x.shape[1] // 2
    bits = lax.bitcast_convert_type(x.astype(BF16).astype(F32), jnp.uint32)
    return (bits[:, :w] >> 16) | (bits[:, w:] & jnp.uint32(0xFFFF0000))


def _unpack_bf16_pairs(words):
    return (lax.bitcast_convert_type(words << 16, F32),
            lax.bitcast_convert_type(words & jnp.uint32(0xFFFF0000), F32))


def _sigmoid(x):
    return 1.0 / (1.0 + jnp.exp(-x))


def _log_sigmoid(x):
    return jnp.minimum(x, 0.0) - jnp.log(1.0 + jnp.exp(-jnp.abs(x)))


def _mod_spec(per_batch, rows, width, col):
    if per_batch:
        return pl.BlockSpec((rows // T_DEC, 1, width), lambda i, *rest: (i, 0, col(*rest)))
    return pl.BlockSpec((1, width), lambda i, *rest: (0, col(*rest)))


def _mod_rows(ref):
    v = ref[...]
    if v.ndim == 2:
        return v
    nb, _, width = v.shape
    return jnp.broadcast_to(v, (nb, T_DEC, width)).reshape(nb * T_DEC, width)


def _ada_kernel(c_ref, w_ref, b_ref, o_ref):
    c = c_ref[...]
    s = (c * _sigmoid(c)).astype(BF16)
    o_ref[...] = _dot(s, w_ref[...].astype(BF16)) + b_ref[...]


def _ada(c_all, w_ada, b_ada):
    n = w_ada.shape[1]
    return pl.pallas_call(
        _ada_kernel,
        grid=(n // TN_ADA,),
        in_specs=[pl.BlockSpec((C_ROWS, D), lambda j: (0, 0)),
                  pl.BlockSpec((D, TN_ADA), lambda j: (0, j)),
                  pl.BlockSpec((1, TN_ADA), lambda j: (0, j))],
        out_specs=pl.BlockSpec((C_ROWS, TN_ADA), lambda j: (0, j)),
        out_shape=jax.ShapeDtypeStruct((C_ROWS, n), F32),
        compiler_params=_params(("arbitrary",)),
        name="ada",
    )(c_all, w_ada, b_ada)


PROMPT_TOK_BLOCKS = N_PROMPT // TM_TOK


def _prompt_block(i):
    return jnp.minimum(i, PROMPT_TOK_BLOCKS - 1)


def _sample_block(i):
    return jnp.maximum(i - PROMPT_TOK_BLOCKS, 0)


def _norm_body(x_ref, sh_ref, sc_ref, g_ref, wg_ref, h_ref, zg_ref):
    x = x_ref[...]
    y = x * lax.rsqrt(jnp.mean(x * x, axis=-1, keepdims=True) + EPS) * g_ref[...]
    hb = (y * (1.0 + _mod_rows(sc_ref)) + _mod_rows(sh_ref)).astype(BF16)
    h_ref[...] = hb
    zg_ref[...] = _dot_nt(hb, wg_ref[...])


def _norm_kernel(xp_ref, xs_ref, shp_ref, scp_ref, shs_ref, scs_ref, g_ref, wg_ref, h_ref, zg_ref):
    i = pl.program_id(0)

    @pl.when(i < PROMPT_TOK_BLOCKS)
    def _():
        _norm_body(xp_ref, shp_ref, scp_ref, g_ref, wg_ref, h_ref, zg_ref)

    @pl.when(i >= PROMPT_TOK_BLOCKS)
    def _():
        _norm_body(xs_ref, shs_ref, scs_ref, g_ref, wg_ref, h_ref, zg_ref)


def _norm(xp, xs, mod_p, mod_s, g_mix, w_gate):
    per_batch = lambda col: pl.BlockSpec((TM_TOK // T_DEC, 1, D), lambda i: (_sample_block(i), 0, col))
    return pl.pallas_call(
        _norm_kernel,
        grid=(N_TOK // TM_TOK,),
        in_specs=[pl.BlockSpec((TM_TOK, D), lambda i: (_prompt_block(i), 0)),
                  pl.BlockSpec((TM_TOK, D), lambda i: (_sample_block(i), 0)),
                  pl.BlockSpec((1, D), lambda i: (0, 0)),
                  pl.BlockSpec((1, D), lambda i: (0, 1)),
                  per_batch(0), per_batch(1),
                  pl.BlockSpec((1, D), lambda i: (0, 0)),
                  pl.BlockSpec((LANES, D), lambda i: (0, 0))],
        out_specs=[pl.BlockSpec((TM_TOK, D), lambda i: (i, 0)),
                   pl.BlockSpec((TM_TOK, LANES), lambda i: (i, 0))],
        out_shape=[jax.ShapeDtypeStruct((N_TOK, D), BF16), jax.ShapeDtypeStruct((N_TOK, LANES), F32)],
        compiler_params=_params(("arbitrary",)),
        name="norm",
    )(xp, xs, mod_p, mod_p, mod_s, mod_s, g_mix, w_gate)


N_IN_CHUNKS = Z_WIDTH // TN_IN


def _inproj_kernel(h_ref, wt_hbm, z_ref, stage, w_bf, sem):
    j = pl.program_id(0)

    def fetch(chunk):
        rows = pl.ds(pl.multiple_of(chunk * TN_IN, TN_IN), TN_IN)
        return pltpu.make_async_copy(wt_hbm.at[rows], stage, sem.at[0])

    @pl.when(pl.program_id(1) == 0)
    def _():
        @pl.when(j == 0)
        def _():
            fetch(0).start()

        fetch(j).wait()
        w_bf[...] = stage[...].astype(BF16)

        @pl.when(j + 1 < N_IN_CHUNKS)
        def _():
            fetch(j + 1).start()

    z_ref[...] = _dot_nt(h_ref[...], w_bf[...])


def _inproj(h, w_in_t):
    return pl.pallas_call(
        _inproj_kernel,
        grid=(N_IN_CHUNKS, N_TOK // TM_PROJ),
        in_specs=[pl.BlockSpec((TM_PROJ, D), lambda j, i: (i, 0)),
                  pl.BlockSpec(memory_space=pl.ANY)],
        out_specs=pl.BlockSpec((TM_PROJ, TN_IN), lambda j, i: (i, j)),
        out_shape=jax.ShapeDtypeStruct((N_TOK, Z_WIDTH), F32),
        scratch_shapes=[pltpu.VMEM((TN_IN, D), F32), pltpu.VMEM((TN_IN, D), BF16), pltpu.SemaphoreType.DMA((1,))],
        compiler_params=_params(("arbitrary", "arbitrary")),
        name="inproj",
    )(h, w_in_t)


def _t5_bucket_np(dist):
    n = np.maximum(dist, 0)
    nf = np.maximum(n, 1).astype(np.float32)
    large = MAX_EXACT + (np.log(nf / MAX_EXACT) / math.log(MAX_DISTANCE / MAX_EXACT)
                         * (NUM_BUCKETS - MAX_EXACT)).astype(np.int32)
    large = np.minimum(large, NUM_BUCKETS - 1)
    return np.where(n < MAX_EXACT, n, large).astype(np.int32)


def _bucket_table(n_q, n_keys_valid, n_keys_padded):
    t = np.arange(n_q)[:, None]
    j = np.arange(n_keys_padded)[None, :]
    dist = t + WINDOW - j
    valid = (dist >= 0) & (dist < WINDOW) & (j < n_keys_valid)
    return np.where(valid, _t5_bucket_np(dist), -1).astype(np.int32)


def _fill_bias(bucket_ref, rb_ref, bias_scr, rows, per_head_table=False):
    for h in range(ATT_HEADS):
        bk = bucket_ref[h * rows:(h + 1) * rows, :] if per_head_table else bucket_ref[...]
        acc = jnp.full(bk.shape, NEG_INF, F32)
        for b in range(NUM_BUCKETS):
            acc = jnp.where(bk == b, rb_ref[b * ATT_HEADS + h], acc)
        bias_scr[h * rows:(h + 1) * rows, :] = acc


def _with_ones(v2):
    return jnp.concatenate([v2, jnp.ones_like(v2)], axis=1)


def _sink_softmax_av(lg, sink, v2_ones):
    m = jnp.maximum(jnp.max(lg, axis=-1, keepdims=True), sink)
    p = jnp.exp(lg - m).astype(BF16)
    pv = _dot(p, v2_ones)
    return pv[:, :HEAD_DIM] / (pv[:, HEAD_DIM:] + jnp.exp(sink - m))


def _swa_prompt_kernel(rb_ref, sink_ref, bucket_ref, q_ref, kp_ref, kc_ref, vp_ref, vc_ref,
                       o_ref, bias_scr):
    i = pl.program_id(0)

    @pl.when(i == 0)
    def _():
        _fill_bias(bucket_ref, rb_ref, bias_scr, ATT_BLOCK)

    first_prev = jnp.logical_and(i == 0, _iota((ATT_BLOCK, 2 * ATT_BLOCK), 1) < ATT_BLOCK)
    for g in range(KV_HEADS):
        ks = slice(g * HEAD_DIM, (g + 1) * HEAD_DIM)
        k2 = jnp.concatenate([kp_ref[:, ks], kc_ref[:, ks]], axis=0).astype(BF16)
        v2 = _with_ones(jnp.concatenate([vp_ref[:, ks], vc_ref[:, ks]], axis=0).astype(BF16))
        for r in range(GROUP):
            h = g * GROUP + r
            hs = slice(h * HEAD_DIM, (h + 1) * HEAD_DIM)
            lg = _dot_nt(q_ref[:, hs].astype(BF16), k2) * ATT_SCALE + bias_scr[h * ATT_BLOCK:(h + 1) * ATT_BLOCK, :]
            lg = jnp.where(first_prev, NEG_INF, lg)
            o_ref[:, hs] = _sink_softmax_av(lg, sink_ref[h], v2).astype(BF16)


def _swa_prompt(z, rb_flat, sinks):
    nb = N_PROMPT // ATT_BLOCK
    bucket = jnp.asarray(_bucket_table(ATT_BLOCK, 2 * ATT_BLOCK, 2 * ATT_BLOCK))
    kcol = ATT_WIDTH // KV_WIDTH
    prev = lambda i: jnp.maximum(i - 1, 0)
    smem = pl.BlockSpec(memory_space=pltpu.SMEM)
    return pl.pallas_call(
        _swa_prompt_kernel,
        grid=(nb,),
        in_specs=[smem, smem,
                  pl.BlockSpec((ATT_BLOCK, 2 * ATT_BLOCK), lambda i: (0, 0)),
                  pl.BlockSpec((ATT_BLOCK, ATT_WIDTH), lambda i: (i, 0)),
                  pl.BlockSpec((ATT_BLOCK, KV_WIDTH), lambda i: (prev(i), kcol)),
                  pl.BlockSpec((ATT_BLOCK, KV_WIDTH), lambda i: (i, kcol)),
                  pl.BlockSpec((ATT_BLOCK, KV_WIDTH), lambda i: (prev(i), kcol + 1)),
                  pl.BlockSpec((ATT_BLOCK, KV_WIDTH), lambda i: (i, kcol + 1))],
        out_specs=pl.BlockSpec((ATT_BLOCK, ATT_WIDTH), lambda i: (i, 0)),
        out_shape=jax.ShapeDtypeStruct((N_PROMPT, ATT_WIDTH), BF16),
        scratch_shapes=[pltpu.VMEM((ATT_HEADS * ATT_BLOCK, 2 * ATT_BLOCK), F32)],
        compiler_params=_params(("arbitrary",)),
        name="swa_prompt",
    )(rb_flat, sinks, bucket, z, z, z, z, z)


CACHE_ROWS = WINDOW * KV_HEADS
NEW_ROWS = T_DEC * KV_HEADS
S_KEYS = 5 * LANES


def _sample_bucket_table():
    t = np.arange(T_DEC)[:, None]
    col = np.arange(S_KEYS)[None, :]
    in_cache = col < CACHE_ROWS
    in_new = (col >= CACHE_ROWS) & (col < CACHE_ROWS + NEW_ROWS)
    key_head = np.where(in_cache, col % KV_HEADS, (col - CACHE_ROWS) // T_DEC)
    key_pos = np.where(in_cache, col // KV_HEADS, WINDOW + (col - CACHE_ROWS) % T_DEC)
    dist = t + WINDOW - key_pos
    valid = (dist >= 0) & (dist < WINDOW) & (in_cache | in_new)
    per_query = np.where(valid, _t5_bucket_np(dist), -1)
    heads = np.arange(ATT_HEADS)[:, None, None] // GROUP
    table = np.where(heads == key_head[None], per_query[None], -1)
    return table.reshape(ATT_HEADS * T_DEC, S_KEYS).astype(np.int32)


def _swa_sample_kernel(rb_ref, sink_ref, bucket_ref, q_ref, kn_ref, vn_ref, ck_ref, cv_ref,
                       o_ref, nk_ref, nv_ref, bias_scr):
    @pl.when(pl.program_id(0) == 0)
    def _():
        _fill_bias(bucket_ref, rb_ref, bias_scr, T_DEC, per_head_table=True)

    nk_ref[:, 0:CACHE_ROWS - NEW_ROWS, :] = ck_ref[:, NEW_ROWS:CACHE_ROWS, :]
    nv_ref[:, 0:CACHE_ROWS - NEW_ROWS, :] = cv_ref[:, NEW_ROWS:CACHE_ROWS, :]

    pad = jnp.zeros((S_KEYS - CACHE_ROWS - NEW_ROWS, HEAD_DIM), F32)
    sink_col = jnp.concatenate([jnp.full((T_DEC, 1), sink_ref[h], F32) for h in range(ATT_HEADS)], axis=0)
    bias = bias_scr[...]
    for b in range(SAMPLE_BT):
        ts = slice(b * T_DEC, (b + 1) * T_DEC)
        head_cols = lambda ref, n: [ref[ts, h * HEAD_DIM:(h + 1) * HEAD_DIM] for h in range(n)]
        k_new, v_new = head_cols(kn_ref, KV_HEADS), head_cols(vn_ref, KV_HEADS)
        for g in range(KV_HEADS):
            new_rows = pl.ds(CACHE_ROWS - NEW_ROWS + g, T_DEC, stride=KV_HEADS)
            nk_ref[b, new_rows, :] = k_new[g]
            nv_ref[b, new_rows, :] = v_new[g]
        qa = jnp.concatenate(head_cols(q_ref, ATT_HEADS), axis=0).astype(BF16)
        k2 = jnp.concatenate([ck_ref[b]] + k_new + [pad], axis=0).astype(BF16)
        v2 = _with_ones(jnp.concatenate([cv_ref[b]] + v_new + [pad], axis=0).astype(BF16))
        o = _sink_softmax_av(_dot_nt(qa, k2) * ATT_SCALE + bias, sink_col, v2)
        for h in range(ATT_HEADS):
            o_ref[ts, h * HEAD_DIM:(h + 1) * HEAD_DIM] = o[h * T_DEC:(h + 1) * T_DEC].astype(BF16)


def _swa_sample(z, cache_k, cache_v, rb_flat, sinks):
    rows = SAMPLE_BT * T_DEC
    z_off = N_PROMPT // rows
    bucket = jnp.asarray(_sample_bucket_table())
    kcol = ATT_WIDTH // KV_WIDTH
    smem = pl.BlockSpec(memory_space=pltpu.SMEM)
    cache_spec = pl.BlockSpec((SAMPLE_BT, CACHE_ROWS, HEAD_DIM), lambda i: (i, 0, 0))
    cache_shape = jax.ShapeDtypeStruct((N_BATCH, CACHE_ROWS, HEAD_DIM), F32)
    return pl.pallas_call(
        _swa_sample_kernel,
        grid=(N_BATCH // SAMPLE_BT,),
        in_specs=[smem, smem,
                  pl.BlockSpec((ATT_HEADS * T_DEC, S_KEYS), lambda i: (0, 0)),
                  pl.BlockSpec((rows, ATT_WIDTH), lambda i: (i + z_off, 0)),
                  pl.BlockSpec((rows, KV_WIDTH), lambda i: (i + z_off, kcol)),
                  pl.BlockSpec((rows, KV_WIDTH), lambda i: (i + z_off, kcol + 1)),
                  cache_spec, cache_spec],
        out_specs=[pl.BlockSpec((rows, ATT_WIDTH), lambda i: (i, 0)), cache_spec, cache_spec],
        out_shape=[jax.ShapeDtypeStruct((N_SAMPLE, ATT_WIDTH), BF16), cache_shape, cache_shape],
        scratch_shapes=[pltpu.VMEM((ATT_HEADS * T_DEC, S_KEYS), F32)],
        compiler_params=_params(("arbitrary",)),
        name="swa_sample",
    )(rb_flat, sinks, bucket, z, z, z, cache_k, cache_v)


LANE_IG, LANE_LF, LANE_B = 0, M_HEADS, 2 * M_HEADS


def _gate_table(zg, gate_bias, seg_len):
    L = zg.shape[0]
    g = zg + gate_bias
    lane = _iota((L, LANES), 1)
    lf = _log_sigmoid(g)
    lf_only = jnp.where(jnp.logical_and(lane >= LANE_LF, lane < LANE_B), lf, 0.0)
    row = _iota((L, L), 0)
    col = _iota((L, L), 1)
    same_seg = (row // seg_len) == (col // seg_len)
    tril = jnp.where(jnp.logical_and(col <= row, same_seg), 1.0, 0.0).astype(BF16)
    cum = pltpu.roll(_dot_exact_lhs01(tril, lf_only), M_HEADS, axis=1)
    table = jnp.where(lane < LANE_LF, g, jnp.where(lane < LANE_B, lf, jnp.where(lane < LANE_B + M_HEADS, cum, 0.0)))
    return table, jnp.logical_and(col <= row, same_seg)


def _mlstm_decay(table, table_t, mask, m0_col, h):
    b_c = table[:, LANE_B + h:LANE_B + h + 1]
    b_r = table_t[LANE_B + h:LANE_B + h + 1, :]
    ig_r = table_t[LANE_IG + h:LANE_IG + h + 1, :]
    log_d = jnp.where(mask, b_c - b_r + ig_r, NEG_INF)
    log_inter = b_c + m0_col
    m_t = jnp.maximum(log_inter, jnp.max(log_d, axis=1, keepdims=True))
    return b_c, m_t, jnp.exp(log_inter - m_t), jnp.exp(log_d - m_t)


def _mlstm_intra(table, table_t, mask, m0_col, h, q, k, v):
    b_c, m_t, w_inter, d = _mlstm_decay(table, table_t, mask, m0_col, h)
    s = _dot_nt(q.astype(BF16), k.astype(BF16)) * d
    num_intra = _dot(s.astype(BF16), v.astype(BF16))
    den_intra = jnp.sum(s, axis=1, keepdims=True)
    return b_c, m_t, w_inter, num_intra, den_intra


ST_WIDTH = DV + LANES


def _mlstm_prompt_kernel(gb_ref, zg_ref, q_ref, k_ref, va_ref, vb_ref, oa_ref, ob_ref,
                         y_ref, c_out, n_out, m_out, st_scr, m_scr):
    step = pl.program_id(0)
    L = ML_CHUNK

    @pl.when(step == 0)
    def _():
        st_scr[...] = jnp.zeros_like(st_scr)
        m_scr[...] = jnp.zeros_like(m_scr)

    table, mask = _gate_table(zg_ref[...], gb_ref[...], L)
    table_t = table.T
    ones = jnp.ones((L, LANES), F32)
    half = M_HEADS // 2
    for h in range(M_HEADS):
        v_ref, o_ref = (va_ref, oa_ref) if h < half else (vb_ref, ob_ref)
        vs = slice((h % half) * DV, (h % half + 1) * DV)
        qb = (q_ref[:, h * DK:(h + 1) * DK] * Q_SCALE).astype(BF16)
        kb = k_ref[:, h * DK:(h + 1) * DK].astype(BF16)
        v1 = jnp.concatenate([v_ref[:, vs], ones], axis=1)
        m0 = m_scr[h:h + 1, 0:1]
        b_c, m_t, w_inter, d = _mlstm_decay(table, table_t, mask, m0, h)
        s = _dot_nt(qb, kb) * d
        st = st_scr[h]
        tot = _dot(s.astype(BF16), v1.astype(BF16)) + w_inter * _dot(qb, st.astype(BF16))
        den = tot[:, DV:]
        hh = tot[:, :DV] / jnp.maximum(jnp.abs(jnp.concatenate([den, den], axis=1)), jnp.exp(-m_t))
        y_ref[:, h * DV:(h + 1) * DV] = (_sigmoid(o_ref[:, vs]) * hh).astype(BF16)

        ig_c = table[:, LANE_IG + h:LANE_IG + h + 1]
        m_new = m_t[L - 1:L, :]
        b_last = b_c[L - 1:L, :]
        w_s = jnp.exp(b_last - b_c + ig_c - m_new)
        decay = jnp.exp(b_last + m0 - m_new)
        st_scr[h] = decay * st + _dot_tn(kb, (w_s * v1).astype(BF16))
        m_scr[h:h + 1, :] = jnp.broadcast_to(m_new, (1, LANES))

    @pl.when(step == pl.num_programs(0) - 1)
    def _():
        for h in range(M_HEADS):
            st_t = st_scr[h].T
            c_out[h] = st_t[:DV, :]
            n_out[h:h + 1, :] = st_t[DV:DV + 1, :]
        m_out[...] = m_scr[...]


def _mlstm_prompt(z, zg, gate_bias):
    L = ML_CHUNK
    blk = M_HEADS * DK
    col = lambda c: pl.BlockSpec((L, blk), lambda i: (i, c))
    const = lambda shape: pl.BlockSpec(shape, lambda i: tuple(0 for _ in shape))
    return pl.pallas_call(
        _mlstm_prompt_kernel,
        grid=(N_PROMPT // L,),
        in_specs=[const((1, LANES)),
                  pl.BlockSpec((L, LANES), lambda i: (i, 0)),
                  col(3), col(4), col(5), col(6), col(7), col(8)],
        out_specs=[pl.BlockSpec((L, M_WIDTH), lambda i: (i, 0)),
                   const((M_HEADS, DV, DK)), const((M_HEADS, DK)), const((M_HEADS, LANES))],
        out_shape=[jax.ShapeDtypeStruct((N_PROMPT, M_WIDTH), BF16),
                   jax.ShapeDtypeStruct((M_HEADS, DV, DK), F32),
                   jax.ShapeDtypeStruct((M_HEADS, DK), F32),
                   jax.ShapeDtypeStruct((M_HEADS, LANES), F32)],
        scratch_shapes=[pltpu.VMEM((M_HEADS, DK, ST_WIDTH), F32),
                        pltpu.VMEM((M_HEADS, LANES), F32)],
        compiler_params=_params(("arbitrary",)),
        name="mlstm_prompt",
    )(gate_bias, zg, z, z, z, z, z, z)


S_ROWS = SAMPLE_BT * T_DEC


def _mlstm_sample_kernel(gb_ref, zg_ref, m0_ref, n0t_ref, q_ref, k_ref, va_ref, vb_ref, oa_ref, ob_ref,
                         c0_ref, n0_ref, y_ref, c_out, n_out, m_out):
    L = S_ROWS
    table, mask = _gate_table(zg_ref[...], gb_ref[...], T_DEC)
    table_t = jnp.concatenate([table, jnp.zeros((LANES - L, LANES), F32)], axis=0).T[:, 0:L]
    m0_all = m0_ref[...]
    lane = _iota((L, LANES), 1)
    row_b = _iota((SAMPLE_BT, L), 1) // T_DEC
    seg_sum = jnp.where(row_b == _iota((SAMPLE_BT, L), 0), 1.0, 0.0)
    m_tok = jnp.zeros((L, LANES), F32)
    half = M_HEADS // 2
    for h in range(M_HEADS):
        v_ref, o_ref = (va_ref, oa_ref) if h < half else (vb_ref, ob_ref)
        vs = slice((h % half) * DV, (h % half + 1) * DV)
        q = q_ref[:, h * DK:(h + 1) * DK] * Q_SCALE
        k = k_ref[:, h * DK:(h + 1) * DK]
        v = v_ref[:, vs]
        m0 = m0_all[:, h:h + 1]
        b_c, m_t, w_inter, num_intra, den_intra = _mlstm_intra(table, table_t, mask, m0, h, q, k, v)
        qb = q.astype(BF16)
        num_inter = jnp.concatenate(
            [_dot_nt(qb[b * T_DEC:(b + 1) * T_DEC], c0_ref[b, h].astype(BF16)) for b in range(SAMPLE_BT)], axis=0)
        num = num_intra + w_inter * num_inter
        den = den_intra + w_inter * jnp.sum(q * n0t_ref[:, h * DK:(h + 1) * DK], axis=1, keepdims=True)
        hh = num / jnp.maximum(jnp.abs(den), jnp.exp(-m_t))
        y_ref[:, h * DV:(h + 1) * DV] = (_sigmoid(o_ref[:, vs]) * hh).astype(BF16)
        m_tok = jnp.where(lane == h, m_t, m_tok)

        def last_tok(x):
            x3 = x.reshape(SAMPLE_BT, T_DEC, 1)
            return jnp.broadcast_to(x3[:, T_DEC - 1:T_DEC, :], x3.shape).reshape(L, 1)

        ig_c = table[:, LANE_IG + h:LANE_IG + h + 1]
        m_new = last_tok(m_t)
        b_last = last_tok(b_c)
        w_s = jnp.exp(b_last - b_c + ig_c - m_new)
        decay = jnp.exp(b_last + m0 - m_new)
        wv = (w_s * v).astype(BF16)
        kb = k.astype(BF16)
        rowsel = _iota((L, 1), 0) // T_DEC
        for b in range(SAMPLE_BT):
            dec_b = decay[b * T_DEC + T_DEC - 1:(b + 1) * T_DEC, :]
            wv_b = jnp.where(rowsel == b, wv, jnp.zeros_like(wv))
            c_out[b, h] = dec_b * c0_ref[b, h] + _dot_tn(wv_b, kb)
        dec_rows = decay.reshape(SAMPLE_BT, T_DEC, 1)[:, T_DEC - 1, :]
        n_out[:, h * DK:(h + 1) * DK] = dec_rows * n0_ref[:, h * DK:(h + 1) * DK] + jnp.dot(
            seg_sum, w_s * k, preferred_element_type=F32, precision=lax.Precision.HIGHEST)
    m_out[...] = m_tok


def _mlstm_sample(z, zg, gate_bias, m0_tok, n0_tok, state_c, state_n):
    L = S_ROWS
    blk = M_HEADS * DK
    z_off = N_PROMPT // L
    col = lambda c: pl.BlockSpec((L, blk), lambda i: (i + z_off, c))
    c_spec = pl.BlockSpec((SAMPLE_BT, M_HEADS, DV, DK), lambda i: (i, 0, 0, 0))
    n_spec = pl.BlockSpec((SAMPLE_BT, blk), lambda i: (i, 0))
    return pl.pallas_call(
        _mlstm_sample_kernel,
        grid=(N_BATCH // SAMPLE_BT,),
        in_specs=[pl.BlockSpec((1, LANES), lambda i: (0, 0)),
                  pl.BlockSpec((L, LANES), lambda i: (i + z_off, 0)),
                  pl.BlockSpec((L, LANES), lambda i: (i, 0)),
                  pl.BlockSpec((L, blk), lambda i: (i, 0)),
                  col(3), col(4), col(5), col(6), col(7), col(8),
                  c_spec, n_spec],
        out_specs=[pl.BlockSpec((L, M_WIDTH), lambda i: (i, 0)), c_spec, n_spec,
                   pl.BlockSpec((L, LANES), lambda i: (i, 0))],
        out_shape=[jax.ShapeDtypeStruct((N_SAMPLE, M_WIDTH), BF16),
                   jax.ShapeDtypeStruct((N_BATCH, M_HEADS, DV, DK), F32),
                   jax.ShapeDtypeStruct((N_BATCH, blk), F32),
                   jax.ShapeDtypeStruct((N_SAMPLE, LANES), F32)],
        compiler_params=_params(("arbitrary",)),
        name="mlstm_sample",
    )(gate_bias, zg, m0_tok, n0_tok, z, z, z, z, z, z, state_c, state_n)


N_OUT_CHUNKS = D // TN_OUT


def _outproj_kernel(ya_ref, ym_ref, w_hbm, x_ref, gt_ref, o_ref, stage, w_bf, sem):
    j = pl.program_id(0)

    def fetch(chunk):
        cols = pl.ds(pl.multiple_of(chunk * TN_OUT, TN_OUT), TN_OUT)
        return pltpu.make_async_copy(w_hbm.at[:, cols], stage, sem.at[0])

    @pl.when(pl.program_id(1) == 0)
    def _():
        @pl.when(j == 0)
        def _():
            fetch(0).start()

        fetch(j).wait()
        w_bf[...] = stage[...].astype(BF16)

        @pl.when(j + 1 < N_OUT_CHUNKS)
        def _():
            fetch(j + 1).start()

    mix = _dot(ya_ref[...], w_bf[0:ATT_WIDTH, :]) + _dot(ym_ref[...], w_bf[ATT_WIDTH:, :])
    o_ref[...] = x_ref[...] + _mod_rows(gt_ref) * mix


def _outproj(ya, ym, w_out, x, mod, per_row):
    n = x.shape[0]
    TM = TM_PROJ
    gate1_col = 2 * N_OUT_CHUNKS
    if per_row:
        gate_spec = pl.BlockSpec((TM // T_DEC, 1, TN_OUT), lambda j, i: (i, 0, gate1_col + j))
    else:
        gate_spec = pl.BlockSpec((1, TN_OUT), lambda j, i: (0, gate1_col + j))
    return pl.pallas_call(
        _outproj_kernel,
        grid=(N_OUT_CHUNKS, n // TM),
        in_specs=[pl.BlockSpec((TM, ATT_WIDTH), lambda j, i: (i, 0)),
                  pl.BlockSpec((TM, M_WIDTH), lambda j, i: (i, 0)),
                  pl.BlockSpec(memory_space=pl.ANY),
                  pl.BlockSpec((TM, TN_OUT), lambda j, i: (i, j)),
                  gate_spec],
        out_specs=pl.BlockSpec((TM, TN_OUT), lambda j, i: (i, j)),
        out_shape=jax.ShapeDtypeStruct((n, D), F32),
        scratch_shapes=[pltpu.VMEM((D, TN_OUT), F32), pltpu.VMEM((D, TN_OUT), BF16), pltpu.SemaphoreType.DMA((1,))],
        compiler_params=_params(("arbitrary", "arbitrary")),
        name="outproj",
    )(ya, ym, w_out, x, mod)


def _router_kernel(x_ref, sh_ref, sc_ref, g_ref, wr_ref, br_ref, h_ref, route_ref):
    x = x_ref[...]
    y = x * lax.rsqrt(jnp.mean(x * x, axis=-1, keepdims=True) + EPS) * g_ref[...]
    h2 = y * (1.0 + _mod_rows(sc_ref)) + _mod_rows(sh_ref)
    h_hi = h2.astype(BF16)
    h_ref[...] = _pack_bf16_pairs(h2)
    h_lo = (h2 - h_hi.astype(F32)).astype(BF16)
    logits = _dot(h_hi, wr_ref[0]) + (_dot(h_hi, wr_ref[1]) + _dot(h_lo, wr_ref[0])) + br_ref[...]
    lane = _iota(logits.shape, 1)

    def first_max(vals):
        vmax = jnp.max(vals, axis=1, keepdims=True)
        idx = jnp.min(jnp.where(vals == vmax, lane, LANES), axis=1, keepdims=True)
        return vmax, idx

    gl = jnp.where(lane < N_GROUPS, logits, NEG_INF)
    gmax, grp = first_max(gl)
    p_grp = 1.0 / jnp.sum(jnp.exp(gl - gmax), axis=1, keepdims=True)
    e_lane = lane - N_GROUPS
    in_grp = jnp.logical_and(e_lane >= 0, jnp.logical_and(e_lane < N_EXPERTS, e_lane // EXP_PER_GROUP == grp))
    el = jnp.where(in_grp, logits, NEG_INF)
    v1, i1 = first_max(el)
    v2, i2 = first_max(jnp.where(lane == i1, NEG_INF, el))
    e2w = jnp.exp(v2 - v1)
    w1 = 1.0 / (1.0 + e2w)
    w2 = e2w / (1.0 + e2w)
    route = jnp.where(lane == 0, (i1 - N_GROUPS).astype(F32),
                      jnp.where(lane == 1, (i2 - N_GROUPS).astype(F32),
                                jnp.where(lane == 2, p_grp * w1, jnp.where(lane == 3, p_grp * w2, 0.0))))
    route_ref[...] = route


def _router_merged_kernel(xp_ref, xs_ref, shp_ref, scp_ref, shs_ref, scs_ref, g_ref, wr_ref, br_ref,
                          h_ref, route_ref):
    i = pl.program_id(0)

    @pl.when(i < PROMPT_TOK_BLOCKS)
    def _():
        _router_kernel(xp_ref, shp_ref, scp_ref, g_ref, wr_ref, br_ref, h_ref, route_ref)

    @pl.when(i >= PROMPT_TOK_BLOCKS)
    def _():
        _router_kernel(xs_ref, shs_ref, scs_ref, g_ref, wr_ref, br_ref, h_ref, route_ref)


def _router(x1_p, x1_s, mod_p, mod_s, g_ffn, w_route, b_route):
    pi, si = _prompt_block, _sample_block
    return pl.pallas_call(
        _router_merged_kernel,
        grid=(N_TOK // TM_TOK,),
        in_specs=[pl.BlockSpec((TM_TOK, D), lambda i: (pi(i), 0)),
                  pl.BlockSpec((TM_TOK, D), lambda i: (si(i), 0)),
                  pl.BlockSpec((1, D), lambda i: (0, 3)),
                  pl.BlockSpec((1, D), lambda i: (0, 4)),
                  pl.BlockSpec((TM_TOK // T_DEC, 1, D), lambda i: (si(i), 0, 3)),
                  pl.BlockSpec((TM_TOK // T_DEC, 1, D), lambda i: (si(i), 0, 4)),
                  pl.BlockSpec((1, D), lambda i: (0, 0)),
                  pl.BlockSpec((2, D, LANES), lambda i: (0, 0, 0)),
                  pl.BlockSpec((1, LANES), lambda i: (0, 0))],
        out_specs=[pl.BlockSpec((TM_TOK, D_PACK), lambda i: (i, 0)),
                   pl.BlockSpec((TM_TOK, LANES), lambda i: (i, 0))],
        out_shape=[jax.ShapeDtypeStruct((N_TOK, D_PACK), jnp.uint32),
                   jax.ShapeDtypeStruct((N_TOK, LANES), F32)],
        compiler_params=_params(("arbitrary",)),
        name="router",
    )(x1_p, x1_s, mod_p, mod_p, mod_s, mod_s, g_ffn, w_route, b_route)


def _rank_kernel(route_ref, dest_ref, cnt_ref, carry, rank_scr):
    pas = pl.program_id(0)
    step = pl.program_id(1)

    @pl.when(jnp.logical_and(pas == 0, step == 0))
    def _():
        carry[...] = jnp.zeros_like(carry)

    route = route_ref[...]
    n = route.shape[0]
    lane = _iota((n, LANES), 1).astype(F32)
    o1 = jnp.where(lane == route[:, 0:1], 1.0, 0.0)
    o2 = jnp.where(lane == route[:, 1:2], 1.0, 0.0)
    lane_i = _iota((n, LANES), 1)
    rows = pl.ds(pl.multiple_of(step * TM_RANK, TM_RANK), TM_RANK)

    @pl.when(pas == 0)
    def _():
        both = o1 + o2
        strict = jnp.where(_iota((n, n), 1) < _iota((n, n), 0), 1.0, 0.0).astype(BF16)
        prior = _dot(strict, both.astype(BF16)) + carry[0:1, :]
        r1 = jnp.sum(o1 * prior, axis=1, keepdims=True)
        r2 = jnp.sum(o2 * prior, axis=1, keepdims=True)
        rank_scr[rows, :] = jnp.where(lane_i == 0, r1, jnp.where(lane_i == 1, r2, 0.0))
        carry[...] = carry[...] + jnp.sum(both, axis=0, keepdims=True)

    @pl.when(pas == 1)
    def _():
        counts = carry[...]
        tiles_per = jnp.floor((counts + (TM_MOE - 1)) * (1.0 / TM_MOE))
        before = jnp.where(_iota((LANES, LANES), 0) < _iota((LANES, LANES), 1), 1.0, 0.0).astype(BF16)
        pad_start = _dot(tiles_per.astype(BF16), before)[0:1, :] * TM_MOE
        rank = rank_scr[rows, :]
        d1 = jnp.sum(o1 * pad_start, axis=1, keepdims=True) + rank[:, 0:1]
        d2 = jnp.sum(o2 * pad_start, axis=1, keepdims=True) + rank[:, 1:2]
        dest_ref[...] = jnp.where(lane_i == 0, d1, jnp.where(lane_i == 1, d2, 0.0)).astype(jnp.int32)
        cnt_ref[...] = counts


def _rank(route):
    n = route.shape[0]
    return pl.pallas_call(
        _rank_kernel,
        grid=(2, n // TM_RANK),
        in_specs=[pl.BlockSpec((TM_RANK, LANES), lambda p, i: (i, 0))],
        out_specs=[pl.BlockSpec((TM_RANK, LANES), lambda p, i: (i * p, 0)),
                   pl.BlockSpec((SUBLANES, LANES), lambda p, i: (0, 0))],
        out_shape=[jax.ShapeDtypeStruct((n, LANES), jnp.int32), jax.ShapeDtypeStruct((SUBLANES, LANES), F32)],
        scratch_shapes=[pltpu.VMEM((SUBLANES, LANES), F32), pltpu.VMEM((n, LANES), F32)],
        compiler_params=_params(("arbitrary", "arbitrary")),
        name="rank",
    )(route)


def _row_copy(src, dst, sem):
    return pltpu.make_async_copy(src, dst, sem)


ROW_UNROLL = 8
assert T_DEC == SUBLANES
D_PACK = D // 2

ZERO_FIRST, ZERO_ANYTIME = 1, 2


def _dispatch_kernel(d1_ref, d2_ref, zc_ref, h_ref, xs_out, zbuf, sem):
    step = pl.program_id(0)
    base = step * TM_TOK

    def zero_tile(t, s):
        rows = pl.ds(pl.multiple_of(t * TM_MOE, TM_MOE), TM_MOE)
        return pltpu.make_async_copy(zbuf, xs_out.at[rows], sem.at[s])

    def for_tiles(cls, s, act):
        def body(t, carry):
            @pl.when(zc_ref[t] == cls)
            def _():
                act(zero_tile(t, s))
            return carry
        lax.fori_loop(0, MAX_TILES, body, 0)

    @pl.when(step == 0)
    def _():
        zbuf[...] = jnp.zeros_like(zbuf)
        for_tiles(ZERO_FIRST, 1, lambda c: c.start())
        for_tiles(ZERO_ANYTIME, 2, lambda c: c.start())
        for_tiles(ZERO_FIRST, 1, lambda c: c.wait())

    @pl.when(step == pl.num_programs(0) - 1)
    def _():
        for_tiles(ZERO_ANYTIME, 2, lambda c: c.wait())

    def issue(r, carry):
        _row_copy(h_ref.at[pl.ds(r, 1)], xs_out.at[pl.ds(d1_ref[base + r], 1)], sem.at[0]).start()
        _row_copy(h_ref.at[pl.ds(r, 1)], xs_out.at[pl.ds(d2_ref[base + r], 1)], sem.at[0]).start()
        return carry

    lax.fori_loop(0, TM_TOK, issue, 0, unroll=ROW_UNROLL)
    for _ in range(TOP_K):
        _row_copy(h_ref, xs_out.at[pl.ds(0, TM_TOK)], sem.at[0]).wait()


def _dispatch(dests, zero_class, h2):
    grid_spec = pltpu.PrefetchScalarGridSpec(
        num_scalar_prefetch=3,
        grid=(N_TOK // TM_TOK,),
        in_specs=[pl.BlockSpec((TM_TOK, D_PACK), lambda i, *_: (i, 0))],
        out_specs=pl.BlockSpec(memory_space=pl.ANY),
        scratch_shapes=[pltpu.VMEM((TM_MOE, D_PACK), jnp.uint32), pltpu.SemaphoreType.DMA((3,))],
    )
    return pl.pallas_call(
        _dispatch_kernel,
        grid_spec=grid_spec,
        out_shape=jax.ShapeDtypeStruct((A_PAD, D_PACK), jnp.uint32),
        compiler_params=_params(("arbitrary",)),
        name="dispatch",
    )(*dests, zero_class, h2)


N_UP_CHUNKS = D_FF // TN_FF
ROW_CAPS = tuple(range(TM_MOE // 4, TM_MOE + 1, TM_MOE // 4))


def _for_row_cap(rows, body):
    lo = 0
    for cap in ROW_CAPS:
        pl.when(jnp.logical_and(rows > lo, rows <= cap))(functools.partial(body, cap))
        lo = cap


def _expert_up_kernel(te_ref, rows_ref, first_ref, nxt_ref, run_ref, meta_ref, x_ref, wg_hbm, wu_hbm,
                      o_ref, wbuf, sem):
    n = pl.program_id(0)
    t = pl.program_id(1)
    slot = lax.rem(n * meta_ref[1] + run_ref[t], 2)

    def fetch(e, chunk, s):
        cols = pl.ds(pl.multiple_of(chunk * TN_FF, TN_FF), TN_FF)
        return (pltpu.make_async_copy(wg_hbm.at[e, :, cols], wbuf.at[s, 0], sem.at[s, 0]),
                pltpu.make_async_copy(wu_hbm.at[e, :, cols], wbuf.at[s, 1], sem.at[s, 1]))

    @pl.when(first_ref[t] == 1)
    def _():
        @pl.when(jnp.logical_and(n == 0, t == 0))
        def _():
            for c in fetch(te_ref[0], 0, 0):
                c.start()

        for c in fetch(te_ref[t], n, slot):
            c.wait()

        @pl.when(nxt_ref[t] >= 0)
        def _():
            for c in fetch(nxt_ref[t], n, 1 - slot):
                c.start()

        @pl.when(jnp.logical_and(nxt_ref[t] < 0, n + 1 < N_UP_CHUNKS))
        def _():
            for c in fetch(te_ref[0], n + 1, 1 - slot):
                c.start()

    def compute(cap):
        x = jnp.concatenate([half.astype(BF16) for half in _unpack_bf16_pairs(x_ref[0:cap, :])], axis=1)
        g = _dot(x, wbuf[slot, 0].astype(BF16))
        u = _dot(x, wbuf[slot, 1].astype(BF16))
        o_ref[0:cap, :] = (g * _sigmoid(g) * u).astype(BF16)
        if cap < TM_MOE:
            o_ref[cap:, :] = jnp.zeros((TM_MOE - cap, TN_FF), BF16)

    _for_row_cap(rows_ref[t], compute)

    @pl.when(rows_ref[t] == 0)
    def _():
        o_ref[...] = jnp.zeros_like(o_ref)


def _expert_up(tiles, xs, w_gate, w_up):
    last = lambda t, meta: jnp.minimum(t, meta[0] - 1)
    grid_spec = pltpu.PrefetchScalarGridSpec(
        num_scalar_prefetch=6,
        grid=(N_UP_CHUNKS, MAX_TILES),
        in_specs=[pl.BlockSpec((TM_MOE, D_PACK), lambda n, t, *s: (last(t, s[5]), 0)),
                  pl.BlockSpec(memory_space=pl.ANY),
                  pl.BlockSpec(memory_space=pl.ANY)],
        out_specs=pl.BlockSpec((TM_MOE, TN_FF), lambda n, t, *s: (t, n)),
        scratch_shapes=[pltpu.VMEM((2, 2, D, TN_FF), F32), pltpu.SemaphoreType.DMA((2, 2))],
    )
    return pl.pallas_call(
        _expert_up_kernel,
        grid_spec=grid_spec,
        out_shape=jax.ShapeDtypeStruct((A_PAD, D_FF), BF16),
        compiler_params=_params(("arbitrary", "arbitrary")),
        name="expert_up",
    )(*tiles, xs, w_gate, w_up)


def _expert_down_kernel(te_ref, rows_ref, first_ref, nxt_ref, run_ref, meta_ref, h_ref, wd_hbm,
                        o_ref, wbuf, sem):
    t = pl.program_id(0)
    slot = lax.rem(run_ref[t], 2)

    def fetch(e, s):
        return pltpu.make_async_copy(wd_hbm.at[e], wbuf.at[s], sem.at[s])

    @pl.when(first_ref[t] == 1)
    def _():
        @pl.when(t == 0)
        def _():
            fetch(te_ref[0], 0).start()

        fetch(te_ref[t], slot).wait()

        @pl.when(nxt_ref[t] >= 0)
        def _():
            fetch(nxt_ref[t], 1 - slot).start()

    def compute(cap):
        o_ref[0:cap, :] = _pack_bf16_pairs(_dot(h_ref[0:cap, :], wbuf[slot].astype(BF16)))
        if cap < TM_MOE:
            o_ref[cap:, :] = jnp.zeros((TM_MOE - cap, D_PACK), jnp.uint32)

    _for_row_cap(rows_ref[t], compute)

    @pl.when(rows_ref[t] == 0)
    def _():
        o_ref[...] = jnp.zeros_like(o_ref)


def _expert_down(tiles, h1, w_down):
    last = lambda t, meta: jnp.minimum(t, meta[0] - 1)
    grid_spec = pltpu.PrefetchScalarGridSpec(
        num_scalar_prefetch=6,
        grid=(MAX_TILES,),
        in_specs=[pl.BlockSpec((TM_MOE, D_FF), lambda t, *s: (last(t, s[5]), 0)),
                  pl.BlockSpec(memory_space=pl.ANY)],
        out_specs=pl.BlockSpec((TM_MOE, D_PACK), lambda t, *s: (t, 0)),
        scratch_shapes=[pltpu.VMEM((2, D_FF, D), F32), pltpu.SemaphoreType.DMA((2,))],
    )
    return pl.pallas_call(
        _expert_down_kernel,
        grid_spec=grid_spec,
        out_shape=jax.ShapeDtypeStruct((A_PAD, D_PACK), jnp.uint32),
        compiler_params=_params(("arbitrary",)),
        name="expert_down",
    )(*tiles, h1, w_down)


def _combine_kernel(d1_ref, d2_ref, x_ref, route_ref, gt_ref, gf_ref, o_hbm,
                    y_ref, buf, sem, *, tok_offset):
    step = pl.program_id(0)
    slot = lax.rem(step, 2)

    def gather(s, into):
        base = tok_offset + s * TM_TOK

        def issue(r, carry):
            _row_copy(o_hbm.at[pl.ds(d1_ref[base + r], 1)], buf.at[into, 0, pl.ds(r, 1)], sem.at[into]).start()
            _row_copy(o_hbm.at[pl.ds(d2_ref[base + r], 1)], buf.at[into, 1, pl.ds(r, 1)], sem.at[into]).start()
            return carry

        lax.fori_loop(0, TM_TOK, issue, 0, unroll=ROW_UNROLL)

    @pl.when(step == 0)
    def _():
        gather(0, 0)

    @pl.when(step + 1 < pl.num_programs(0))
    def _():
        gather(step + 1, 1 - slot)

    for k in range(TOP_K):
        _row_copy(o_hbm.at[pl.ds(0, TM_TOK)], buf.at[slot, k], sem.at[slot]).wait()

    route = route_ref[...]
    lo1, hi1 = _unpack_bf16_pairs(buf[slot, 0])
    lo2, hi2 = _unpack_bf16_pairs(buf[slot, 1])
    g1, g2 = route[:, 2:3], route[:, 3:4]
    moe = jnp.concatenate([g1 * lo1 + g2 * lo2, g1 * hi1 + g2 * hi2], axis=1)
    x2 = x_ref[...] + _mod_rows(gt_ref) * moe
    y_ref[...] = x2 * lax.rsqrt(jnp.mean(x2 * x2, axis=-1, keepdims=True) + EPS) * gf_ref[...]


def _combine(dests, x1, route, mod, per_row, g_final, o_rows, tok_offset):
    n = x1.shape[0]
    off = tok_offset // TM_TOK
    grid_spec = pltpu.PrefetchScalarGridSpec(
        num_scalar_prefetch=2,
        grid=(n // TM_TOK,),
        in_specs=[pl.BlockSpec((TM_TOK, D), lambda i, *_: (i, 0)),
                  pl.BlockSpec((TM_TOK, LANES), lambda i, *_: (i + off, 0)),
                  _mod_spec(per_row, TM_TOK, D, lambda *_: 5),
                  pl.BlockSpec((1, D), lambda i, *_: (0, 0)),
                  pl.BlockSpec(memory_space=pl.ANY)],
        out_specs=pl.BlockSpec((TM_TOK, D), lambda i, *_: (i, 0)),
        scratch_shapes=[pltpu.VMEM((2, TOP_K, TM_TOK, D_PACK), jnp.uint32), pltpu.SemaphoreType.DMA((2,))],
    )
    return pl.pallas_call(
        functools.partial(_combine_kernel, tok_offset=tok_offset),
        grid_spec=grid_spec,
        out_shape=jax.ShapeDtypeStruct((n, D), F32),
        compiler_params=_params(("arbitrary",)),
        name="combine",
    )(*dests, x1, route, mod, g_final, o_rows)


def kernel(x_prompt, x_sample, cache_k, cache_v, state_C, state_n, state_m, c_prompt, c_sample, rel_bias, w_ada, b_ada, g_mix, g_ffn, w_in, sinks, b_igate, b_fgate, w_out, w_router_grp, b_router_grp, w_router_exp, b_router_exp, w_gate, w_up, w_down, g_final):
    xp = x_prompt.reshape(N_PROMPT, D)
    xs = x_sample.reshape(N_SAMPLE, D)

    c_all = jnp.concatenate([c_prompt, c_sample, jnp.zeros((C_ROWS - 1 - N_BATCH, D), F32)], axis=0)
    mod = _ada(c_all, w_ada[0], b_ada)
    mod_p = mod[0:1]
    mod_s = mod[1:1 + N_BATCH].reshape(N_BATCH, 1, -1)

    w_in_t = jnp.swapaxes(w_in[0], 0, 1)
    w_gates_t = jnp.pad(w_in_t[Z_WIDTH:].astype(BF16), ((0, LANES - 2 * M_HEADS), (0, 0)))
    h_all, zg = _norm(xp, xs, mod_p, mod_s, g_mix, w_gates_t)
    z = _inproj(h_all, w_in_t)

    rb_flat = rel_bias.reshape(NUM_BUCKETS * ATT_HEADS)
    sink_v = sinks[0]
    ya_p = _swa_prompt(z, rb_flat, sink_v)
    ya_s, nk_s, nv_s = _swa_sample(z,cache_k.reshape(N_BATCH, CACHE_ROWS, HEAD_DIM),
                                   cache_v.reshape(N_BATCH, CACHE_ROWS, HEAD_DIM), rb_flat, sink_v)

    gate_bias = jnp.concatenate([b_igate[0], b_fgate[0], jnp.zeros((LANES - 2 * M_HEADS,), F32)]).reshape(1, LANES)
    ym_p, c_p, n_p, m_p = _mlstm_prompt(z, zg, gate_bias)
    m0_tok = jnp.pad(jnp.repeat(state_m[0], T_DEC, axis=0), ((0, 0), (0, LANES - M_HEADS)))
    n0_flat = state_n[0].reshape(N_BATCH, M_HEADS * DK)
    n0_tok = jnp.repeat(n0_flat, T_DEC, axis=0)
    ym_s, c_s, n_s, m_s = _mlstm_sample(z, zg, gate_bias, m0_tok, n0_tok, state_C[0], n0_flat)

    x1_p = _outproj(ya_p, ym_p, w_out[0], xp, mod_p, False)
    x1_s = _outproj(ya_s, ym_s, w_out[0], xs, mod_s, True)

    w_route = jnp.pad(jnp.concatenate([w_router_grp[0], w_router_exp[0]], axis=1),
                      ((0, 0), (0, LANES - N_GROUPS - N_EXPERTS)))
    b_route = jnp.pad(jnp.concatenate([b_router_grp[0], b_router_exp[0]]),
                      (0, LANES - N_GROUPS - N_EXPERTS)).reshape(1, LANES)
    w_route_hi = w_route.astype(BF16)
    w_route_split = jnp.stack([w_route_hi, (w_route - w_route_hi.astype(F32)).astype(BF16)])
    h2, route = _router(x1_p, x1_s, mod_p, mod_s, g_ffn, w_route_split, b_route)

    dest, counts = _rank(route)
    dests = (dest[:, 0], dest[:, 1])

    i32 = lambda a: a.astype(jnp.int32)
    cnt = i32(counts[0, :N_EXPERTS])
    tiles_per = (cnt + TM_MOE - 1) // TM_MOE
    tile_end = jnp.cumsum(tiles_per)
    n_tiles = tile_end[-1]
    tile_ids = jnp.arange(MAX_TILES, dtype=jnp.int32)
    tile_expert = i32(jnp.minimum(jnp.searchsorted(tile_end, tile_ids, side="right"), N_EXPERTS - 1))
    last_expert = tile_expert[jnp.maximum(n_tiles - 1, 0)]
    tile_valid = tile_ids < n_tiles
    tile_expert = jnp.where(tile_valid, tile_expert, last_expert)
    prev_expert = jnp.concatenate([jnp.full((1,), -1, jnp.int32), tile_expert[:-1]])
    run_first = jnp.logical_and(tile_valid, tile_expert != prev_expert)
    run_id = jnp.maximum(jnp.cumsum(i32(run_first)) - 1, 0)
    expert_ids = jnp.arange(N_EXPERTS, dtype=jnp.int32)
    used = jnp.where(tiles_per > 0, expert_ids, N_EXPERTS)
    next_used = jnp.concatenate([lax.cummin(used[::-1])[::-1][1:], jnp.full((1,), N_EXPERTS, jnp.int32)])
    next_used = jnp.where(next_used >= N_EXPERTS, -1, next_used)
    first_tile = tile_end - tiles_per
    tile_rows = jnp.clip(cnt[tile_expert] - (tile_ids - first_tile[tile_expert]) * TM_MOE, 0, TM_MOE)
    tile_rows = jnp.where(tile_valid, tile_rows, 0)
    tiles = (tile_expert, i32(tile_rows), i32(run_first), next_used[tile_expert], i32(run_id),
             jnp.stack([n_tiles, jnp.sum(i32(run_first))]).astype(jnp.int32))
    next_expert = jnp.concatenate([tile_expert[1:], jnp.full((1,), -1, jnp.int32)])
    run_last = jnp.logical_or(tile_expert != next_expert, tile_ids == n_tiles - 1)
    zero_class = jnp.where(tile_valid, jnp.where(run_last, ZERO_FIRST, 0), ZERO_ANYTIME)
    xs_rows = _dispatch(dests, i32(zero_class), h2)
    h1 = _expert_up(tiles, xs_rows, w_gate[0], w_up[0])
    o_rows = _expert_down(tiles, h1, w_down[0])

    gf = g_final.reshape(1, D)
    y_p = _combine(dests, x1_p, route, mod_p, False, gf, o_rows, 0)
    y_s = _combine(dests, x1_s, route, mod_s, True, gf, o_rows, N_PROMPT)

    kv5 = lambda a: a.reshape(1, -1, WINDOW, KV_HEADS, HEAD_DIM)
    kcol = ATT_WIDTH
    nk_p = z[N_PROMPT - WINDOW:N_PROMPT, kcol:kcol + KV_WIDTH]
    nv_p = z[N_PROMPT - WINDOW:N_PROMPT, kcol + KV_WIDTH:kcol + 2 * KV_WIDTH]
    return (y_p.reshape(1, N_PROMPT, D), y_s.reshape(N_BATCH, T_DEC, D),
            kv5(nk_p), kv5(nv_p),
            c_p.reshape(1, 1, M_HEADS, DV, DK), n_p.reshape(1, 1, M_HEADS, DK), m_p[:, 0].reshape(1, 1, M_HEADS),
            kv5(nk_s), kv5(nv_s),
            c_s.reshape(1, N_BATCH, M_HEADS, DV, DK), n_s.reshape(1, N_BATCH, M_HEADS, DK),
            m_s.reshape(N_BATCH, T_DEC, LANES)[:, T_DEC - 1, :M_HEADS].reshape(1, N_BATCH, M_HEADS))
```

```python
import functools
import math

import numpy as np
import jax
import jax.numpy as jnp
from jax import lax
from jax.experimental import pallas as pl
from jax.experimental.pallas import tpu as pltpu

F32 = jnp.float32
BF16 = jnp.bfloat16
NEG_INF = float("-inf")

D = 4096
N_PROMPT = 8192
N_BATCH = 128
T_DEC = 8
N_SAMPLE = N_BATCH * T_DEC
N_TOK = N_PROMPT + N_SAMPLE
HEAD_DIM = 128
ATT_HEADS = 16
KV_HEADS = 4
GROUP = ATT_HEADS // KV_HEADS
WINDOW = 128
ATT_WIDTH = ATT_HEADS * HEAD_DIM
KV_WIDTH = KV_HEADS * HEAD_DIM
NUM_BUCKETS = 32
MAX_EXACT = 16
MAX_DISTANCE = 128
M_HEADS = 8
DK = 128
DV = 256
M_WIDTH = M_HEADS * DV
Z_WIDTH = ATT_WIDTH + 2 * KV_WIDTH + 2 * M_HEADS * DK + 2 * M_WIDTH
N_GROUPS = 4
EXP_PER_GROUP = 8
N_EXPERTS = N_GROUPS * EXP_PER_GROUP
TOP_K = 2
D_FF = 1024
EPS = 1e-6
ATT_SCALE = HEAD_DIM ** -0.5
Q_SCALE = DK ** -0.5

LANES = 128
SUBLANES = 8
VMEM_LIMIT = 56 * 1024 * 1024

TM_PROJ = 512
TN_IN = 1024
TN_OUT = 1024
TN_ADA = 512
C_ROWS = 136
ATT_BLOCK = 128
SAMPLE_BT = 8
ML_CHUNK = 256
TM_MOE = 256
TN_FF = 512
N_ASSIGN = N_TOK * TOP_K
MAX_TILES = N_ASSIGN // TM_MOE + N_EXPERTS
A_PAD = MAX_TILES * TM_MOE
TM_TOK = 256
TM_RANK = 1024


def _params(sem):
    return pltpu.CompilerParams(dimension_semantics=sem, vmem_limit_bytes=VMEM_LIMIT)


def _iota(shape, dim):
    return lax.broadcasted_iota(jnp.int32, shape, dim)


def _dot(a, b):
    return jnp.dot(a, b, preferred_element_type=F32)


def _dot_nt(a, b):
    return lax.dot_general(a, b, (((1,), (1,)), ((), ())), preferred_element_type=F32)


def _dot_tn(a, b):
    return lax.dot_general(a, b, (((0,), (0,)), ((), ())), preferred_element_type=F32)


def _split3(x):
    x1 = x.astype(BF16)
    r1 = x - x1.astype(F32)
    x2 = r1.astype(BF16)
    r2 = r1 - x2.astype(F32)
    return x1, x2, r2.astype(BF16)


def _dot_exact_lhs01(a01, x):
    x1, x2, x3 = _split3(x)
    return _dot(a01, x1) + _dot(a01, x2) + _dot(a01, x3)


def _pack_bf16_pairs(x):
    w = x.shape[1] // 2
    bits = lax.bitcast_convert_type(x.astype(BF16).astype(F32), jnp.uint32)
    return (bits[:, :w] >> 16) | (bits[:, w:] & jnp.uint32(0xFFFF0000))


def _unpack_bf16_pairs(words):
    return (lax.bitcast_convert_type(words << 16, F32),
            lax.bitcast_convert_type(words & jnp.uint32(0xFFFF0000), F32))


def _sigmoid(x):
    return 1.0 / (1.0 + jnp.exp(-x))


def _log_sigmoid(x):
    return jnp.minimum(x, 0.0) - jnp.log(1.0 + jnp.exp(-jnp.abs(x)))


def _mod_spec(per_batch, rows, width, col):
    if per_batch:
        return pl.BlockSpec((rows // T_DEC, 1, width), lambda i, *rest: (i, 0, col(*rest)))
    return pl.BlockSpec((1, width), lambda i, *rest: (0, col(*rest)))


def _mod_rows(ref):
    v = ref[...]
    if v.ndim == 2:
        return v
    nb, _, width = v.shape
    return jnp.broadcast_to(v, (nb, T_DEC, width)).reshape(nb * T_DEC, width)


def _ada_kernel(c_ref, w_ref, b_ref, o_ref):
    c = c_ref[...]
    s = (c * _sigmoid(c)).astype(BF16)
    mod = _dot(s, w_ref[...].astype(BF16)) + b_ref[...]
    o_ref[...] = mod.reshape(C_ROWS, 1, TN_ADA)


def _ada(c_all, w_ada, b_ada):
    n = w_ada.shape[1]
    return pl.pallas_call(
        _ada_kernel,
        grid=(n // TN_ADA,),
        in_specs=[pl.BlockSpec((C_ROWS, D), lambda j: (0, 0)),
                  pl.BlockSpec((D, TN_ADA), lambda j: (0, j)),
                  pl.BlockSpec((1, TN_ADA), lambda j: (0, j))],
        out_specs=pl.BlockSpec((C_ROWS, 1, TN_ADA), lambda j: (0, 0, j)),
        out_shape=jax.ShapeDtypeStruct((C_ROWS, 1, n), F32),
        compiler_params=_params(("arbitrary",)),
        name="ada",
    )(c_all, w_ada, b_ada)


PROMPT_TOK_BLOCKS = N_PROMPT // TM_TOK


def _prompt_block(i):
    return jnp.minimum(i, PROMPT_TOK_BLOCKS - 1)


def _sample_block(i):
    return jnp.maximum(i - PROMPT_TOK_BLOCKS, 0)


def _norm_body(x_ref, sh_ref, sc_ref, g_ref, wg_ref, h_ref, zg_ref):
    x = x_ref[...]
    y = x * lax.rsqrt(jnp.mean(x * x, axis=-1, keepdims=True) + EPS) * g_ref[...]
    hb = (y * (1.0 + _mod_rows(sc_ref)) + _mod_rows(sh_ref)).astype(BF16)
    h_ref[...] = hb
    zg_ref[...] = _dot_nt(hb, wg_ref[...])


def _norm_kernel(xp_ref, xs_ref, shp_ref, scp_ref, shs_ref, scs_ref, g_ref, wg_ref, h_ref, zg_ref):
    i = pl.program_id(0)

    @pl.when(i < PROMPT_TOK_BLOCKS)
    def _():
        _norm_body(xp_ref, shp_ref, scp_ref, g_ref, wg_ref, h_ref, zg_ref)

    @pl.when(i >= PROMPT_TOK_BLOCKS)
    def _():
        _norm_body(xs_ref, shs_ref, scs_ref, g_ref, wg_ref, h_ref, zg_ref)


def _norm(xp, xs, mod_p, mod_s, g_mix, w_gate):
    per_batch = lambda col: pl.BlockSpec((TM_TOK // T_DEC, 1, D), lambda i: (_sample_block(i), 0, col))
    return pl.pallas_call(
        _norm_kernel,
        grid=(N_TOK // TM_TOK,),
        in_specs=[pl.BlockSpec((TM_TOK, D), lambda i: (_prompt_block(i), 0)),
                  pl.BlockSpec((TM_TOK, D), lambda i: (_sample_block(i), 0)),
                  pl.BlockSpec((1, D), lambda i: (0, 0)),
                  pl.BlockSpec((1, D), lambda i: (0, 1)),
                  per_batch(0), per_batch(1),
                  pl.BlockSpec((1, D), lambda i: (0, 0)),
                  pl.BlockSpec((LANES, D), lambda i: (0, 0))],
        out_specs=[pl.BlockSpec((TM_TOK, D), lambda i: (i, 0)),
                   pl.BlockSpec((TM_TOK, LANES), lambda i: (i, 0))],
        out_shape=[jax.ShapeDtypeStruct((N_TOK, D), BF16), jax.ShapeDtypeStruct((N_TOK, LANES), F32)],
        compiler_params=_params(("arbitrary",)),
        name="norm",
    )(xp, xs, mod_p, mod_p, mod_s, mod_s, g_mix, w_gate)


N_IN_CHUNKS = Z_WIDTH // TN_IN


def _inproj_kernel(h_ref, wt_hbm, z_ref, stage, w_bf, sem):
    j = pl.program_id(0)

    def fetch(chunk):
        rows = pl.ds(pl.multiple_of(chunk * TN_IN, TN_IN), TN_IN)
        return pltpu.make_async_copy(wt_hbm.at[rows], stage, sem.at[0])

    @pl.when(pl.program_id(1) == 0)
    def _():
        @pl.when(j == 0)
        def _():
            fetch(0).start()

        fetch(j).wait()
        w_bf[...] = stage[...].astype(BF16)

        @pl.when(j + 1 < N_IN_CHUNKS)
        def _():
            fetch(j + 1).start()

    z_ref[...] = _dot_nt(h_ref[...], w_bf[...])


def _inproj(h, w_in_t):
    return pl.pallas_call(
        _inproj_kernel,
        grid=(N_IN_CHUNKS, N_TOK // TM_PROJ),
        in_specs=[pl.BlockSpec((TM_PROJ, D), lambda j, i: (i, 0)),
                  pl.BlockSpec(memory_space=pl.ANY)],
        out_specs=pl.BlockSpec((TM_PROJ, TN_IN), lambda j, i: (i, j)),
        out_shape=jax.ShapeDtypeStruct((N_TOK, Z_WIDTH), F32),
        scratch_shapes=[pltpu.VMEM((TN_IN, D), F32), pltpu.VMEM((TN_IN, D), BF16), pltpu.SemaphoreType.DMA((1,))],
        compiler_params=_params(("arbitrary", "arbitrary")),
        name="inproj",
    )(h, w_in_t)


def _t5_bucket_np(dist):
    n = np.maximum(dist, 0)
    nf = np.maximum(n, 1).astype(np.float32)
    large = MAX_EXACT + (np.log(nf / MAX_EXACT) / math.log(MAX_DISTANCE / MAX_EXACT)
                         * (NUM_BUCKETS - MAX_EXACT)).astype(np.int32)
    large = np.minimum(large, NUM_BUCKETS - 1)
    return np.where(n < MAX_EXACT, n, large).astype(np.int32)


def _bucket_table(n_q, n_keys_valid, n_keys_padded):
    t = np.arange(n_q)[:, None]
    j = np.arange(n_keys_padded)[None, :]
    dist = t + WINDOW - j
    valid = (dist >= 0) & (dist < WINDOW) & (j < n_keys_valid)
    return np.where(valid, _t5_bucket_np(dist), -1).astype(np.int32)


def _fill_bias(bucket_ref, rb_ref, bias_scr, rows, per_head_table=False):
    for h in range(ATT_HEADS):
        bk = bucket_ref[h * rows:(h + 1) * rows, :] if per_head_table else bucket_ref[...]
        acc = jnp.full(bk.shape, NEG_INF, F32)
        for b in range(NUM_BUCKETS):
            acc = jnp.where(bk == b, rb_ref[b * ATT_HEADS + h], acc)
        bias_scr[h * rows:(h + 1) * rows, :] = acc


def _with_ones(v2):
    return jnp.concatenate([v2, jnp.ones_like(v2)], axis=1)


def _sink_softmax_av(lg, sink, v2_ones):
    m = jnp.maximum(jnp.max(lg, axis=-1, keepdims=True), sink)
    p = jnp.exp(lg - m).astype(BF16)
    pv = _dot(p, v2_ones)
    return pv[:, :HEAD_DIM] / (pv[:, HEAD_DIM:] + jnp.exp(sink - m))


def _swa_prompt_kernel(rb_ref, sink_ref, bucket_ref, q_ref, kp_ref, kc_ref, vp_ref, vc_ref,
                       o_ref, bias_scr):
    i = pl.program_id(0)

    @pl.when(i == 0)
    def _():
        _fill_bias(bucket_ref, rb_ref, bias_scr, ATT_BLOCK)

    first_prev = jnp.logical_and(i == 0, _iota((ATT_BLOCK, 2 * ATT_BLOCK), 1) < ATT_BLOCK)
    for g in range(KV_HEADS):
        ks = slice(g * HEAD_DIM, (g + 1) * HEAD_DIM)
        k2 = jnp.concatenate([kp_ref[:, ks], kc_ref[:, ks]], axis=0).astype(BF16)
        v2 = _with_ones(jnp.concatenate([vp_ref[:, ks], vc_ref[:, ks]], axis=0).astype(BF16))
        for r in range(GROUP):
            h = g * GROUP + r
            hs = slice(h * HEAD_DIM, (h + 1) * HEAD_DIM)
            lg = _dot_nt(q_ref[:, hs].astype(BF16), k2) * ATT_SCALE + bias_scr[h * ATT_BLOCK:(h + 1) * ATT_BLOCK, :]
            lg = jnp.where(first_prev, NEG_INF, lg)
            o_ref[:, hs] = _sink_softmax_av(lg, sink_ref[h], v2).astype(BF16)


def _swa_prompt(z, rb_flat, sinks):
    nb = N_PROMPT // ATT_BLOCK
    bucket = jnp.asarray(_bucket_table(ATT_BLOCK, 2 * ATT_BLOCK, 2 * ATT_BLOCK))
    kcol = ATT_WIDTH // KV_WIDTH
    prev = lambda i: jnp.maximum(i - 1, 0)
    smem = pl.BlockSpec(memory_space=pltpu.SMEM)
    return pl.pallas_call(
        _swa_prompt_kernel,
        grid=(nb,),
        in_specs=[smem, smem,
                  pl.BlockSpec((ATT_BLOCK, 2 * ATT_BLOCK), lambda i: (0, 0)),
                  pl.BlockSpec((ATT_BLOCK, ATT_WIDTH), lambda i: (i, 0)),
                  pl.BlockSpec((ATT_BLOCK, KV_WIDTH), lambda i: (prev(i), kcol)),
                  pl.BlockSpec((ATT_BLOCK, KV_WIDTH), lambda i: (i, kcol)),
                  pl.BlockSpec((ATT_BLOCK, KV_WIDTH), lambda i: (prev(i), kcol + 1)),
                  pl.BlockSpec((ATT_BLOCK, KV_WIDTH), lambda i: (i, kcol + 1))],
        out_specs=pl.BlockSpec((ATT_BLOCK, ATT_WIDTH), lambda i: (i, 0)),
        out_shape=jax.ShapeDtypeStruct((N_PROMPT, ATT_WIDTH), BF16),
        scratch_shapes=[pltpu.VMEM((ATT_HEADS * ATT_BLOCK, 2 * ATT_BLOCK), F32)],
        compiler_params=_params(("arbitrary",)),
        name="swa_prompt",
    )(rb_flat, sinks, bucket, z, z, z, z, z)


CACHE_ROWS = WINDOW * KV_HEADS
NEW_ROWS = T_DEC * KV_HEADS
S_KEYS = 5 * LANES


def _sample_bucket_table():
    t = np.arange(T_DEC)[:, None]
    col = np.arange(S_KEYS)[None, :]
    in_cache = col < CACHE_ROWS
    in_new = (col >= CACHE_ROWS) & (col < CACHE_ROWS + NEW_ROWS)
    key_head = np.where(in_cache, col % KV_HEADS, (col - CACHE_ROWS) // T_DEC)
    key_pos = np.where(in_cache, col // KV_HEADS, WINDOW + (col - CACHE_ROWS) % T_DEC)
    dist = t + WINDOW - key_pos
    valid = (dist >= 0) & (dist < WINDOW) & (in_cache | in_new)
    per_query = np.where(valid, _t5_bucket_np(dist), -1)
    heads = np.arange(ATT_HEADS)[:, None, None] // GROUP
    table = np.where(heads == key_head[None], per_query[None], -1)
    return table.reshape(ATT_HEADS * T_DEC, S_KEYS).astype(np.int32)


def _swa_sample_kernel(rb_ref, sink_ref, bucket_ref, q_ref, kn_ref, vn_ref, ck_ref, cv_ref,
                       o_ref, nk_ref, nv_ref, bias_scr):
    @pl.when(pl.program_id(0) == 0)
    def _():
        _fill_bias(bucket_ref, rb_ref, bias_scr, T_DEC, per_head_table=True)

    nk_ref[:, 0:CACHE_ROWS - NEW_ROWS, :] = ck_ref[:, NEW_ROWS:CACHE_ROWS, :]
    nv_ref[:, 0:CACHE_ROWS - NEW_ROWS, :] = cv_ref[:, NEW_ROWS:CACHE_ROWS, :]

    pad = jnp.zeros((S_KEYS - CACHE_ROWS - NEW_ROWS, HEAD_DIM), F32)
    sink_col = jnp.concatenate([jnp.full((T_DEC, 1), sink_ref[h], F32) for h in range(ATT_HEADS)], axis=0)
    bias = bias_scr[...]
    for b in range(SAMPLE_BT):
        ts = slice(b * T_DEC, (b + 1) * T_DEC)
        head_cols = lambda ref, n: [ref[ts, h * HEAD_DIM:(h + 1) * HEAD_DIM] for h in range(n)]
        k_new, v_new = head_cols(kn_ref, KV_HEADS), head_cols(vn_ref, KV_HEADS)
        for g in range(KV_HEADS):
            new_rows = pl.ds(CACHE_ROWS - NEW_ROWS + g, T_DEC, stride=KV_HEADS)
            nk_ref[b, new_rows, :] = k_new[g]
            nv_ref[b, new_rows, :] = v_new[g]
        qa = jnp.concatenate(head_cols(q_ref, ATT_HEADS), axis=0).astype(BF16)
        k2 = jnp.concatenate([ck_ref[b]] + k_new + [pad], axis=0).astype(BF16)
        v2 = _with_ones(jnp.concatenate([cv_ref[b]] + v_new + [pad], axis=0).astype(BF16))
        o = _sink_softmax_av(_dot_nt(qa, k2) * ATT_SCALE + bias, sink_col, v2)
        for h in range(ATT_HEADS):
            o_ref[ts, h * HEAD_DIM:(h + 1) * HEAD_DIM] = o[h * T_DEC:(h + 1) * T_DEC].astype(BF16)


def _swa_sample(z, cache_k, cache_v, rb_flat, sinks):
    rows = SAMPLE_BT * T_DEC
    z_off = N_PROMPT // rows
    bucket = jnp.asarray(_sample_bucket_table())
    kcol = ATT_WIDTH // KV_WIDTH
    smem = pl.BlockSpec(memory_space=pltpu.SMEM)
    cache_spec = pl.BlockSpec((SAMPLE_BT, CACHE_ROWS, HEAD_DIM), lambda i: (i, 0, 0))
    cache_shape = jax.ShapeDtypeStruct((N_BATCH, CACHE_ROWS, HEAD_DIM), F32)
    return pl.pallas_call(
        _swa_sample_kernel,
        grid=(N_BATCH // SAMPLE_BT,),
        in_specs=[smem, smem,
                  pl.BlockSpec((ATT_HEADS * T_DEC, S_KEYS), lambda i: (0, 0)),
                  pl.BlockSpec((rows, ATT_WIDTH), lambda i: (i + z_off, 0)),
                  pl.BlockSpec((rows, KV_WIDTH), lambda i: (i + z_off, kcol)),
                  pl.BlockSpec((rows, KV_WIDTH), lambda i: (i + z_off, kcol + 1)),
                  cache_spec, cache_spec],
        out_specs=[pl.BlockSpec((rows, ATT_WIDTH), lambda i: (i, 0)), cache_spec, cache_spec],
        out_shape=[jax.ShapeDtypeStruct((N_SAMPLE, ATT_WIDTH), BF16), cache_shape, cache_shape],
        scratch_shapes=[pltpu.VMEM((ATT_HEADS * T_DEC, S_KEYS), F32)],
        compiler_params=_params(("arbitrary",)),
        name="swa_sample",
    )(rb_flat, sinks, bucket, z, z, z, cache_k, cache_v)


LANE_IG, LANE_LF, LANE_B = 0, M_HEADS, 2 * M_HEADS


def _gate_table(zg, gate_bias, seg_len):
    L = zg.shape[0]
    g = zg + gate_bias
    lane = _iota((L, LANES), 1)
    lf = _log_sigmoid(g)
    lf_only = jnp.where(jnp.logical_and(lane >= LANE_LF, lane < LANE_B), lf, 0.0)
    row = _iota((L, L), 0)
    col = _iota((L, L), 1)
    same_seg = (row // seg_len) == (col // seg_len)
    tril = jnp.where(jnp.logical_and(col <= row, same_seg), 1.0, 0.0).astype(BF16)
    cum = pltpu.roll(_dot_exact_lhs01(tril, lf_only), M_HEADS, axis=1)
    table = jnp.where(lane < LANE_LF, g, jnp.where(lane < LANE_B, lf, jnp.where(lane < LANE_B + M_HEADS, cum, 0.0)))
    return table, jnp.logical_and(col <= row, same_seg)


def _mlstm_decay(table, table_t, mask, m0_col, h):
    b_c = table[:, LANE_B + h:LANE_B + h + 1]
    b_r = table_t[LANE_B + h:LANE_B + h + 1, :]
    ig_r = table_t[LANE_IG + h:LANE_IG + h + 1, :]
    log_d = jnp.where(mask, b_c - b_r + ig_r, NEG_INF)
    log_inter = b_c + m0_col
    m_t = jnp.maximum(log_inter, jnp.max(log_d, axis=1, keepdims=True))
    return b_c, m_t, jnp.exp(log_inter - m_t), jnp.exp(log_d - m_t)


def _mlstm_intra(table, table_t, mask, m0_col, h, q, k, v):
    b_c, m_t, w_inter, d = _mlstm_decay(table, table_t, mask, m0_col, h)
    s = _dot_nt(q.astype(BF16), k.astype(BF16)) * d
    num_intra = _dot(s.astype(BF16), v.astype(BF16))
    den_intra = jnp.sum(s, axis=1, keepdims=True)
    return b_c, m_t, w_inter, num_intra, den_intra


ST_WIDTH = DV + LANES


def _mlstm_prompt_kernel(gb_ref, zg_ref, q_ref, k_ref, va_ref, vb_ref, oa_ref, ob_ref,
                         y_ref, c_out, n_out, m_out, st_scr, m_scr):
    step = pl.program_id(0)
    L = ML_CHUNK

    @pl.when(step == 0)
    def _():
        st_scr[...] = jnp.zeros_like(st_scr)
        m_scr[...] = jnp.zeros_like(m_scr)

    table, mask = _gate_table(zg_ref[...], gb_ref[...], L)
    table_t = table.T
    ones = jnp.ones((L, LANES), F32)
    half = M_HEADS // 2
    for h in range(M_HEADS):
        v_ref, o_ref = (va_ref, oa_ref) if h < half else (vb_ref, ob_ref)
        vs = slice((h % half) * DV, (h % half + 1) * DV)
        qb = (q_ref[:, h * DK:(h + 1) * DK] * Q_SCALE).astype(BF16)
        kb = k_ref[:, h * DK:(h + 1) * DK].astype(BF16)
        v1 = jnp.concatenate([v_ref[:, vs], ones], axis=1)
        m0 = m_scr[h:h + 1, 0:1]
        b_c, m_t, w_inter, d = _mlstm_decay(table, table_t, mask, m0, h)
        s = _dot_nt(qb, kb) * d
        st = st_scr[h]
        tot = _dot(s.astype(BF16), v1.astype(BF16)) + w_inter * _dot(qb, st.astype(BF16))
        den = tot[:, DV:]
        hh = tot[:, :DV] / jnp.maximum(jnp.abs(jnp.concatenate([den, den], axis=1)), jnp.exp(-m_t))
        y_ref[:, h * DV:(h + 1) * DV] = (_sigmoid(o_ref[:, vs]) * hh).astype(BF16)

        ig_c = table[:, LANE_IG + h:LANE_IG + h + 1]
        m_new = m_t[L - 1:L, :]
        b_last = b_c[L - 1:L, :]
        w_s = jnp.exp(b_last - b_c + ig_c - m_new)
        decay = jnp.exp(b_last + m0 - m_new)
        st_scr[h] = decay * st + _dot_tn(kb, (w_s * v1).astype(BF16))
        m_scr[h:h + 1, :] = jnp.broadcast_to(m_new, (1, LANES))

    @pl.when(step == pl.num_programs(0) - 1)
    def _():
        for h in range(M_HEADS):
            st_t = st_scr[h].T
            c_out[h] = st_t[:DV, :]
            n_out[h:h + 1, :] = st_t[DV:DV + 1, :]
        m_out[...] = m_scr[...]


def _mlstm_prompt(z, zg, gate_bias):
    L = ML_CHUNK
    blk = M_HEADS * DK
    col = lambda c: pl.BlockSpec((L, blk), lambda i: (i, c))
    const = lambda shape: pl.BlockSpec(shape, lambda i: tuple(0 for _ in shape))
    return pl.pallas_call(
        _mlstm_prompt_kernel,
        grid=(N_PROMPT // L,),
        in_specs=[const((1, LANES)),
                  pl.BlockSpec((L, LANES), lambda i: (i, 0)),
                  col(3), col(4), col(5), col(6), col(7), col(8)],
        out_specs=[pl.BlockSpec((L, M_WIDTH), lambda i: (i, 0)),
                   const((M_HEADS, DV, DK)), const((M_HEADS, DK)), const((M_HEADS, LANES))],
        out_shape=[jax.ShapeDtypeStruct((N_PROMPT, M_WIDTH), BF16),
                   jax.ShapeDtypeStruct((M_HEADS, DV, DK), F32),
                   jax.ShapeDtypeStruct((M_HEADS, DK), F32),
                   jax.ShapeDtypeStruct((M_HEADS, LANES), F32)],
        scratch_shapes=[pltpu.VMEM((M_HEADS, DK, ST_WIDTH), F32),
                        pltpu.VMEM((M_HEADS, LANES), F32)],
        compiler_params=_params(("arbitrary",)),
        name="mlstm_prompt",
    )(gate_bias, zg, z, z, z, z, z, z)


S_ROWS = SAMPLE_BT * T_DEC


def _mlstm_sample_kernel(gb_ref, zg_ref, m0_ref, n0t_ref, q_ref, k_ref, va_ref, vb_ref, oa_ref, ob_ref,
                         c0_ref, n0_ref, y_ref, c_out, n_out, m_out):
    L = S_ROWS
    table, mask = _gate_table(zg_ref[...], gb_ref[...], T_DEC)
    table_t = jnp.concatenate([table, jnp.zeros((LANES - L, LANES), F32)], axis=0).T[:, 0:L]
    m0_all = m0_ref[...]
    lane = _iota((L, LANES), 1)
    row_b = _iota((SAMPLE_BT, L), 1) // T_DEC
    seg_sum = jnp.where(row_b == _iota((SAMPLE_BT, L), 0), 1.0, 0.0)
    m_tok = jnp.zeros((L, LANES), F32)
    half = M_HEADS // 2
    for h in range(M_HEADS):
        v_ref, o_ref = (va_ref, oa_ref) if h < half else (vb_ref, ob_ref)
        vs = slice((h % half) * DV, (h % half + 1) * DV)
        q = q_ref[:, h * DK:(h + 1) * DK] * Q_SCALE
        k = k_ref[:, h * DK:(h + 1) * DK]
        v = v_ref[:, vs]
        m0 = m0_all[:, h:h + 1]
        b_c, m_t, w_inter, num_intra, den_intra = _mlstm_intra(table, table_t, mask, m0, h, q, k, v)
        qb = q.astype(BF16)
        num_inter = jnp.concatenate(
            [_dot_nt(qb[b * T_DEC:(b + 1) * T_DEC], c0_ref[b, h].astype(BF16)) for b in range(SAMPLE_BT)], axis=0)
        num = num_intra + w_inter * num_inter
        den = den_intra + w_inter * jnp.sum(q * n0t_ref[:, h * DK:(h + 1) * DK], axis=1, keepdims=True)
        hh = num / jnp.maximum(jnp.abs(den), jnp.exp(-m_t))
        y_ref[:, h * DV:(h + 1) * DV] = (_sigmoid(o_ref[:, vs]) * hh).astype(BF16)
        m_tok = jnp.where(lane == h, m_t, m_tok)

        def last_tok(x):
            x3 = x.reshape(SAMPLE_BT, T_DEC, 1)
            return jnp.broadcast_to(x3[:, T_DEC - 1:T_DEC, :], x3.shape).reshape(L, 1)

        ig_c = table[:, LANE_IG + h:LANE_IG + h + 1]
        m_new = last_tok(m_t)
        b_last = last_tok(b_c)
        w_s = jnp.exp(b_last - b_c + ig_c - m_new)
        decay = jnp.exp(b_last + m0 - m_new)
        wv = (w_s * v).astype(BF16)
        kb = k.astype(BF16)
        rowsel = _iota((L, 1), 0) // T_DEC
        for b in range(SAMPLE_BT):
            dec_b = decay[b * T_DEC + T_DEC - 1:(b + 1) * T_DEC, :]
            wv_b = jnp.where(rowsel == b, wv, jnp.zeros_like(wv))
            c_out[b, h] = dec_b * c0_ref[b, h] + _dot_tn(wv_b, kb)
        dec_rows = decay.reshape(SAMPLE_BT, T_DEC, 1)[:, T_DEC - 1, :]
        n_out[:, h * DK:(h + 1) * DK] = dec_rows * n0_ref[:, h * DK:(h + 1) * DK] + jnp.dot(
            seg_sum, w_s * k, preferred_element_type=F32, precision=lax.Precision.HIGHEST)
    m_out[...] = m_tok


def _mlstm_sample(z, zg, gate_bias, m0_tok, n0_tok, state_c, state_n):
    L = S_ROWS
    blk = M_HEADS * DK
    z_off = N_PROMPT // L
    col = lambda c: pl.BlockSpec((L, blk), lambda i: (i + z_off, c))
    c_spec = pl.BlockSpec((SAMPLE_BT, M_HEADS, DV, DK), lambda i: (i, 0, 0, 0))
    n_spec = pl.BlockSpec((SAMPLE_BT, blk), lambda i: (i, 0))
    return pl.pallas_call(
        _mlstm_sample_kernel,
        grid=(N_BATCH // SAMPLE_BT,),
        in_specs=[pl.BlockSpec((1, LANES), lambda i: (0, 0)),
                  pl.BlockSpec((L, LANES), lambda i: (i + z_off, 0)),
                  pl.BlockSpec((L, LANES), lambda i: (i, 0)),
                  pl.BlockSpec((L, blk), lambda i: (i, 0)),
                  col(3), col(4), col(5), col(6), col(7), col(8),
                  c_spec, n_spec],
        out_specs=[pl.BlockSpec((L, M_WIDTH), lambda i: (i, 0)), c_spec, n_spec,
                   pl.BlockSpec((L, LANES), lambda i: (i, 0))],
        out_shape=[jax.ShapeDtypeStruct((N_SAMPLE, M_WIDTH), BF16),
                   jax.ShapeDtypeStruct((N_BATCH, M_HEADS, DV, DK), F32),
                   jax.ShapeDtypeStruct((N_BATCH, blk), F32),
                   jax.ShapeDtypeStruct((N_SAMPLE, LANES), F32)],
        compiler_params=_params(("arbitrary",)),
        name="mlstm_sample",
    )(gate_bias, zg, m0_tok, n0_tok, z, z, z, z, z, z, state_c, state_n)


N_OUT_CHUNKS = D // TN_OUT


def _outproj_kernel(ya_ref, ym_ref, w_hbm, x_ref, gt_ref, o_ref, stage, w_bf, sem):
    j = pl.program_id(0)

    def fetch(chunk):
        cols = pl.ds(pl.multiple_of(chunk * TN_OUT, TN_OUT), TN_OUT)
        return pltpu.make_async_copy(w_hbm.at[:, cols], stage, sem.at[0])

    @pl.when(pl.program_id(1) == 0)
    def _():
        @pl.when(j == 0)
        def _():
            fetch(0).start()

        fetch(j).wait()
        w_bf[...] = stage[...].astype(BF16)

        @pl.when(j + 1 < N_OUT_CHUNKS)
        def _():
            fetch(j + 1).start()

    mix = _dot(ya_ref[...], w_bf[0:ATT_WIDTH, :]) + _dot(ym_ref[...], w_bf[ATT_WIDTH:, :])
    o_ref[...] = x_ref[...] + _mod_rows(gt_ref) * mix


def _outproj(ya, ym, w_out, x, mod, per_row):
    n = x.shape[0]
    TM = TM_PROJ
    gate1_col = 2 * N_OUT_CHUNKS
    if per_row:
        gate_spec = pl.BlockSpec((TM // T_DEC, 1, TN_OUT), lambda j, i: (i, 0, gate1_col + j))
    else:
        gate_spec = pl.BlockSpec((1, TN_OUT), lambda j, i: (0, gate1_col + j))
    return pl.pallas_call(
        _outproj_kernel,
        grid=(N_OUT_CHUNKS, n // TM),
        in_specs=[pl.BlockSpec((TM, ATT_WIDTH), lambda j, i: (i, 0)),
                  pl.BlockSpec((TM, M_WIDTH), lambda j, i: (i, 0)),
                  pl.BlockSpec(memory_space=pl.ANY),
                  pl.BlockSpec((TM, TN_OUT), lambda j, i: (i, j)),
                  gate_spec],
        out_specs=pl.BlockSpec((TM, TN_OUT), lambda j, i: (i, j)),
        out_shape=jax.ShapeDtypeStruct((n, D), F32),
        scratch_shapes=[pltpu.VMEM((D, TN_OUT), F32), pltpu.VMEM((D, TN_OUT), BF16), pltpu.SemaphoreType.DMA((1,))],
        compiler_params=_params(("arbitrary", "arbitrary")),
        name="outproj",
    )(ya, ym, w_out, x, mod)


def _router_kernel(x_ref, sh_ref, sc_ref, g_ref, wr_ref, br_ref, h_ref, route_ref):
    x = x_ref[...]
    y = x * lax.rsqrt(jnp.mean(x * x, axis=-1, keepdims=True) + EPS) * g_ref[...]
    h2 = y * (1.0 + _mod_rows(sc_ref)) + _mod_rows(sh_ref)
    h_hi = h2.astype(BF16)
    h_ref[...] = _pack_bf16_pairs(h2)
    h_lo = (h2 - h_hi.astype(F32)).astype(BF16)
    logits = _dot(h_hi, wr_ref[0]) + (_dot(h_hi, wr_ref[1]) + _dot(h_lo, wr_ref[0])) + br_ref[...]
    lane = _iota(logits.shape, 1)

    def first_max(vals):
        vmax = jnp.max(vals, axis=1, keepdims=True)
        idx = jnp.min(jnp.where(vals == vmax, lane, LANES), axis=1, keepdims=True)
        return vmax, idx

    gl = jnp.where(lane < N_GROUPS, logits, NEG_INF)
    gmax, grp = first_max(gl)
    p_grp = 1.0 / jnp.sum(jnp.exp(gl - gmax), axis=1, keepdims=True)
    e_lane = lane - N_GROUPS
    in_grp = jnp.logical_and(e_lane >= 0, jnp.logical_and(e_lane < N_EXPERTS, e_lane // EXP_PER_GROUP == grp))
    el = jnp.where(in_grp, logits, NEG_INF)
    v1, i1 = first_max(el)
    v2, i2 = first_max(jnp.where(lane == i1, NEG_INF, el))
    e2w = jnp.exp(v2 - v1)
    w1 = 1.0 / (1.0 + e2w)
    w2 = e2w / (1.0 + e2w)
    route = jnp.where(lane == 0, (i1 - N_GROUPS).astype(F32),
                      jnp.where(lane == 1, (i2 - N_GROUPS).astype(F32),
                                jnp.where(lane == 2, p_grp * w1, jnp.where(lane == 3, p_grp * w2, 0.0))))
    route_ref[...] = route


def _router_merged_kernel(xp_ref, xs_ref, shp_ref, scp_ref, shs_ref, scs_ref, g_ref, wr_ref, br_ref,
                          h_ref, route_ref):
    i = pl.program_id(0)

    @pl.when(i < PROMPT_TOK_BLOCKS)
    def _():
        _router_kernel(xp_ref, shp_ref, scp_ref, g_ref, wr_ref, br_ref, h_ref, route_ref)

    @pl.when(i >= PROMPT_TOK_BLOCKS)
    def _():
        _router_kernel(xs_ref, shs_ref, scs_ref, g_ref, wr_ref, br_ref, h_ref, route_ref)


def _router(x1_p, x1_s, mod_p, mod_s, g_ffn, w_route, b_route):
    pi, si = _prompt_block, _sample_block
    return pl.pallas_call(
        _router_merged_kernel,
        grid=(N_TOK // TM_TOK,),
        in_specs=[pl.BlockSpec((TM_TOK, D), lambda i: (pi(i), 0)),
                  pl.BlockSpec((TM_TOK, D), lambda i: (si(i), 0)),
                  pl.BlockSpec((1, D), lambda i: (0, 3)),
                  pl.BlockSpec((1, D), lambda i: (0, 4)),
                  pl.BlockSpec((TM_TOK // T_DEC, 1, D), lambda i: (si(i), 0, 3)),
                  pl.BlockSpec((TM_TOK // T_DEC, 1, D), lambda i: (si(i), 0, 4)),
                  pl.BlockSpec((1, D), lambda i: (0, 0)),
                  pl.BlockSpec((2, D, LANES), lambda i: (0, 0, 0)),
                  pl.BlockSpec((1, LANES), lambda i: (0, 0))],
        out_specs=[pl.BlockSpec((TM_TOK, D_PACK), lambda i: (i, 0)),
                   pl.BlockSpec((TM_TOK, LANES), lambda i: (i, 0))],
        out_shape=[jax.ShapeDtypeStruct((N_TOK, D_PACK), jnp.uint32),
                   jax.ShapeDtypeStruct((N_TOK, LANES), F32)],
        compiler_params=_params(("arbitrary",)),
        name="router",
    )(x1_p, x1_s, mod_p, mod_p, mod_s, mod_s, g_ffn, w_route, b_route)


def _rank_kernel(route_ref, dest_ref, cnt_ref, carry, rank_scr):
    pas = pl.program_id(0)
    step = pl.program_id(1)

    @pl.when(jnp.logical_and(pas == 0, step == 0))
    def _():
        carry[...] = jnp.zeros_like(carry)

    route = route_ref[...]
    n = route.shape[0]
    lane = _iota((n, LANES), 1).astype(F32)
    o1 = jnp.where(lane == route[:, 0:1], 1.0, 0.0)
    o2 = jnp.where(lane == route[:, 1:2], 1.0, 0.0)
    lane_i = _iota((n, LANES), 1)
    rows = pl.ds(pl.multiple_of(step * TM_RANK, TM_RANK), TM_RANK)

    @pl.when(pas == 0)
    def _():
        both = o1 + o2
        strict = jnp.where(_iota((n, n), 1) < _iota((n, n), 0), 1.0, 0.0).astype(BF16)
        prior = _dot(strict, both.astype(BF16)) + carry[0:1, :]
        r1 = jnp.sum(o1 * prior, axis=1, keepdims=True)
        r2 = jnp.sum(o2 * prior, axis=1, keepdims=True)
        rank_scr[rows, :] = jnp.where(lane_i == 0, r1, jnp.where(lane_i == 1, r2, 0.0))
        carry[...] = carry[...] + jnp.sum(both, axis=0, keepdims=True)

    @pl.when(pas == 1)
    def _():
        counts = carry[...]
        tiles_per = jnp.floor((counts + (TM_MOE - 1)) * (1.0 / TM_MOE))
        before = jnp.where(_iota((LANES, LANES), 0) < _iota((LANES, LANES), 1), 1.0, 0.0).astype(BF16)
        pad_start = _dot(tiles_per.astype(BF16), before)[0:1, :] * TM_MOE
        rank = rank_scr[rows, :]
        d1 = jnp.sum(o1 * pad_start, axis=1, keepdims=True) + rank[:, 0:1]
        d2 = jnp.sum(o2 * pad_start, axis=1, keepdims=True) + rank[:, 1:2]
        dest_ref[...] = jnp.where(lane_i == 0, d1, jnp.where(lane_i == 1, d2, 0.0)).astype(jnp.int32)
        cnt_ref[...] = counts


def _rank(route):
    n = route.shape[0]
    return pl.pallas_call(
        _rank_kernel,
        grid=(2, n // TM_RANK),
        in_specs=[pl.BlockSpec((TM_RANK, LANES), lambda p, i: (i, 0))],
        out_specs=[pl.BlockSpec((TM_RANK, LANES), lambda p, i: (i * p, 0)),
                   pl.BlockSpec((SUBLANES, LANES), lambda p, i: (0, 0))],
        out_shape=[jax.ShapeDtypeStruct((n, LANES), jnp.int32), jax.ShapeDtypeStruct((SUBLANES, LANES), F32)],
        scratch_shapes=[pltpu.VMEM((SUBLANES, LANES), F32), pltpu.VMEM((n, LANES), F32)],
        compiler_params=_params(("arbitrary", "arbitrary")),
        name="rank",
    )(route)


def _row_copy(src, dst, sem):
    return pltpu.make_async_copy(src, dst, sem)


ROW_UNROLL = 8
assert T_DEC == SUBLANES
D_PACK = D // 2

ZERO_FIRST, ZERO_ANYTIME = 1, 2


def _dispatch_kernel(d1_ref, d2_ref, zc_ref, h_ref, xs_out, zbuf, sem):
    step = pl.program_id(0)
    base = step * TM_TOK

    def zero_tile(t, s):
        rows = pl.ds(pl.multiple_of(t * TM_MOE, TM_MOE), TM_MOE)
        return pltpu.make_async_copy(zbuf, xs_out.at[rows], sem.at[s])

    def for_tiles(cls, s, act):
        def body(t, carry):
            @pl.when(zc_ref[t] == cls)
            def _():
                act(zero_tile(t, s))
            return carry
        lax.fori_loop(0, MAX_TILES, body, 0)

    @pl.when(step == 0)
    def _():
        zbuf[...] = jnp.zeros_like(zbuf)
        for_tiles(ZERO_FIRST, 1, lambda c: c.start())
        for_tiles(ZERO_ANYTIME, 2, lambda c: c.start())
        for_tiles(ZERO_FIRST, 1, lambda c: c.wait())

    @pl.when(step == pl.num_programs(0) - 1)
    def _():
        for_tiles(ZERO_ANYTIME, 2, lambda c: c.wait())

    def issue(r, carry):
        _row_copy(h_ref.at[pl.ds(r, 1)], xs_out.at[pl.ds(d1_ref[base + r], 1)], sem.at[0]).start()
        _row_copy(h_ref.at[pl.ds(r, 1)], xs_out.at[pl.ds(d2_ref[base + r], 1)], sem.at[0]).start()
        return carry

    lax.fori_loop(0, TM_TOK, issue, 0, unroll=ROW_UNROLL)
    for _ in range(TOP_K):
        _row_copy(h_ref, xs_out.at[pl.ds(0, TM_TOK)], sem.at[0]).wait()


def _dispatch(dests, zero_class, h2):
    grid_spec = pltpu.PrefetchScalarGridSpec(
        num_scalar_prefetch=3,
        grid=(N_TOK // TM_TOK,),
        in_specs=[pl.BlockSpec((TM_TOK, D_PACK), lambda i, *_: (i, 0))],
        out_specs=pl.BlockSpec(memory_space=pl.ANY),
        scratch_shapes=[pltpu.VMEM((TM_MOE, D_PACK), jnp.uint32), pltpu.SemaphoreType.DMA((3,))],
    )
    return pl.pallas_call(
        _dispatch_kernel,
        grid_spec=grid_spec,
        out_shape=jax.ShapeDtypeStruct((A_PAD, D_PACK), jnp.uint32),
        compiler_params=_params(("arbitrary",)),
        name="dispatch",
    )(*dests, zero_class, h2)


N_UP_CHUNKS = D_FF // TN_FF
ROW_CAPS = tuple(range(TM_MOE // 4, TM_MOE + 1, TM_MOE // 4))


def _for_row_cap(rows, body):
    lo = 0
    for cap in ROW_CAPS:
        pl.when(jnp.logical_and(rows > lo, rows <= cap))(functools.partial(body, cap))
        lo = cap


def _expert_up_kernel(te_ref, rows_ref, first_ref, nxt_ref, run_ref, meta_ref, x_ref, wg_hbm, wu_hbm,
                      o_ref, wbuf, sem):
    n = pl.program_id(0)
    t = pl.program_id(1)
    slot = lax.rem(n * meta_ref[1] + run_ref[t], 2)

    def fetch(e, chunk, s):
        cols = pl.ds(pl.multiple_of(chunk * TN_FF, TN_FF), TN_FF)
        return (pltpu.make_async_copy(wg_hbm.at[e, :, cols], wbuf.at[s, 0], sem.at[s, 0]),
                pltpu.make_async_copy(wu_hbm.at[e, :, cols], wbuf.at[s, 1], sem.at[s, 1]))

    def start(copies):
        for priority, c in enumerate(copies):
            c.start(priority=priority)

    @pl.when(first_ref[t] == 1)
    def _():
        @pl.when(jnp.logical_and(n == 0, t == 0))
        def _():
            start(fetch(te_ref[0], 0, 0))

        for c in fetch(te_ref[t], n, slot):
            c.wait()

        @pl.when(nxt_ref[t] >= 0)
        def _():
            start(fetch(nxt_ref[t], n, 1 - slot))

        @pl.when(jnp.logical_and(nxt_ref[t] < 0, n + 1 < N_UP_CHUNKS))
        def _():
            start(fetch(te_ref[0], n + 1, 1 - slot))

    def compute(cap):
        x = jnp.concatenate([half.astype(BF16) for half in _unpack_bf16_pairs(x_ref[0:cap, :])], axis=1)
        g = _dot(x, wbuf[slot, 0].astype(BF16))
        u = _dot(x, wbuf[slot, 1].astype(BF16))
        o_ref[0:cap, :] = (g * _sigmoid(g) * u).astype(BF16)
        if cap < TM_MOE:
            o_ref[cap:, :] = jnp.zeros((TM_MOE - cap, TN_FF), BF16)

    _for_row_cap(rows_ref[t], compute)

    @pl.when(rows_ref[t] == 0)
    def _():
        o_ref[...] = jnp.zeros_like(o_ref)


def _expert_up(tiles, xs, w_gate, w_up):
    last = lambda t, meta: jnp.minimum(t, meta[0] - 1)
    grid_spec = pltpu.PrefetchScalarGridSpec(
        num_scalar_prefetch=6,
        grid=(N_UP_CHUNKS, MAX_TILES),
        in_specs=[pl.BlockSpec((TM_MOE, D_PACK), lambda n, t, *s: (last(t, s[5]), 0)),
                  pl.BlockSpec(memory_space=pl.ANY),
                  pl.BlockSpec(memory_space=pl.ANY)],
        out_specs=pl.BlockSpec((TM_MOE, TN_FF), lambda n, t, *s: (t, n)),
        scratch_shapes=[pltpu.VMEM((2, 2, D, TN_FF), F32), pltpu.SemaphoreType.DMA((2, 2))],
    )
    return pl.pallas_call(
        _expert_up_kernel,
        grid_spec=grid_spec,
        out_shape=jax.ShapeDtypeStruct((A_PAD, D_FF), BF16),
        compiler_params=_params(("arbitrary", "arbitrary")),
        name="expert_up",
    )(*tiles, xs, w_gate, w_up)


def _expert_down_kernel(te_ref, rows_ref, first_ref, nxt_ref, run_ref, meta_ref, h_ref, wd_hbm,
                        o_ref, wbuf, sem):
    t = pl.program_id(0)
    slot = lax.rem(run_ref[t], 2)

    def fetch(e, s):
        halves = [pl.ds(k * (D_FF // 2), D_FF // 2) for k in range(2)]
        return [pltpu.make_async_copy(wd_hbm.at[e, rows], wbuf.at[s, rows], sem.at[s, k])
                for k, rows in enumerate(halves)]

    def start(copies):
        for priority, c in enumerate(copies):
            c.start(priority=priority)

    @pl.when(first_ref[t] == 1)
    def _():
        @pl.when(t == 0)
        def _():
            start(fetch(te_ref[0], 0))

        for c in fetch(te_ref[t], slot):
            c.wait()

        @pl.when(nxt_ref[t] >= 0)
        def _():
            start(fetch(nxt_ref[t], 1 - slot))

    def compute(cap):
        o_ref[0:cap, :] = _pack_bf16_pairs(_dot(h_ref[0:cap, :], wbuf[slot].astype(BF16)))
        if cap < TM_MOE:
            o_ref[cap:, :] = jnp.zeros((TM_MOE - cap, D_PACK), jnp.uint32)

    _for_row_cap(rows_ref[t], compute)

    @pl.when(rows_ref[t] == 0)
    def _():
        o_ref[...] = jnp.zeros_like(o_ref)


def _expert_down(tiles, h1, w_down):
    last = lambda t, meta: jnp.minimum(t, meta[0] - 1)
    grid_spec = pltpu.PrefetchScalarGridSpec(
        num_scalar_prefetch=6,
        grid=(MAX_TILES,),
        in_specs=[pl.BlockSpec((TM_MOE, D_FF), lambda t, *s: (last(t, s[5]), 0)),
                  pl.BlockSpec(memory_space=pl.ANY)],
        out_specs=pl.BlockSpec((TM_MOE, D_PACK), lambda t, *s: (t, 0)),
        scratch_shapes=[pltpu.VMEM((2, D_FF, D), F32), pltpu.SemaphoreType.DMA((2, 2))],
    )
    return pl.pallas_call(
        _expert_down_kernel,
        grid_spec=grid_spec,
        out_shape=jax.ShapeDtypeStruct((A_PAD, D_PACK), jnp.uint32),
        compiler_params=_params(("arbitrary",)),
        name="expert_down",
    )(*tiles, h1, w_down)


def _combine_kernel(d1_ref, d2_ref, x_ref, route_ref, gt_ref, gf_ref, o_hbm,
                    y_ref, buf, sem, *, tok_offset):
    step = pl.program_id(0)
    slot = lax.rem(step, 2)

    def gather(s, into):
        base = tok_offset + s * TM_TOK

        def issue(r, carry):
            _row_copy(o_hbm.at[pl.ds(d1_ref[base + r], 1)], buf.at[into, 0, pl.ds(r, 1)], sem.at[into]).start()
            _row_copy(o_hbm.at[pl.ds(d2_ref[base + r], 1)], buf.at[into, 1, pl.ds(r, 1)], sem.at[into]).start()
            return carry

        lax.fori_loop(0, TM_TOK, issue, 0, unroll=ROW_UNROLL)

    @pl.when(step == 0)
    def _():
        gather(0, 0)

    @pl.when(step + 1 < pl.num_programs(0))
    def _():
        gather(step + 1, 1 - slot)

    for k in range(TOP_K):
        _row_copy(o_hbm.at[pl.ds(0, TM_TOK)], buf.at[slot, k], sem.at[slot]).wait()

    route = route_ref[...]
    lo1, hi1 = _unpack_bf16_pairs(buf[slot, 0])
    lo2, hi2 = _unpack_bf16_pairs(buf[slot, 1])
    g1, g2 = route[:, 2:3], route[:, 3:4]
    moe = jnp.concatenate([g1 * lo1 + g2 * lo2, g1 * hi1 + g2 * hi2], axis=1)
    x2 = x_ref[...] + _mod_rows(gt_ref) * moe
    y_ref[...] = x2 * lax.rsqrt(jnp.mean(x2 * x2, axis=-1, keepdims=True) + EPS) * gf_ref[...]


def _combine(dests, x1, route, mod, per_row, g_final, o_rows, tok_offset):
    n = x1.shape[0]
    off = tok_offset // TM_TOK
    grid_spec = pltpu.PrefetchScalarGridSpec(
        num_scalar_prefetch=2,
        grid=(n // TM_TOK,),
        in_specs=[pl.BlockSpec((TM_TOK, D), lambda i, *_: (i, 0)),
                  pl.BlockSpec((TM_TOK, LANES), lambda i, *_: (i + off, 0)),
                  _mod_spec(per_row, TM_TOK, D, lambda *_: 5),
                  pl.BlockSpec((1, D), lambda i, *_: (0, 0)),
                  pl.BlockSpec(memory_space=pl.ANY)],
        out_specs=pl.BlockSpec((TM_TOK, D), lambda i, *_: (i, 0)),
        scratch_shapes=[pltpu.VMEM((2, TOP_K, TM_TOK, D_PACK), jnp.uint32), pltpu.SemaphoreType.DMA((2,))],
    )
    return pl.pallas_call(
        functools.partial(_combine_kernel, tok_offset=tok_offset),
        grid_spec=grid_spec,
        out_shape=jax.ShapeDtypeStruct((n, D), F32),
        compiler_params=_params(("arbitrary",)),
        name="combine",
    )(*dests, x1, route, mod, g_final, o_rows)


def kernel(x_prompt, x_sample, cache_k, cache_v, state_C, state_n, state_m, c_prompt, c_sample, rel_bias, w_ada, b_ada, g_mix, g_ffn, w_in, sinks, b_igate, b_fgate, w_out, w_router_grp, b_router_grp, w_router_exp, b_router_exp, w_gate, w_up, w_down, g_final):
    xp = x_prompt.reshape(N_PROMPT, D)
    xs = x_sample.reshape(N_SAMPLE, D)

    c_all = jnp.concatenate([c_sample, c_prompt, jnp.zeros((C_ROWS - 1 - N_BATCH, D), F32)], axis=0)
    mod_s = _ada(c_all, w_ada[0], b_ada)
    mod_p = mod_s[N_BATCH]

    w_in_t = jnp.swapaxes(w_in[0], 0, 1)
    w_gates_t = jnp.pad(w_in_t[Z_WIDTH:].astype(BF16), ((0, LANES - 2 * M_HEADS), (0, 0)))
    h_all, zg = _norm(xp, xs, mod_p, mod_s, g_mix, w_gates_t)
    z = _inproj(h_all, w_in_t)

    rb_flat = rel_bias.reshape(NUM_BUCKETS * ATT_HEADS)
    sink_v = sinks[0]
    ya_p = _swa_prompt(z, rb_flat, sink_v)
    ya_s, nk_s, nv_s = _swa_sample(z,cache_k.reshape(N_BATCH, CACHE_ROWS, HEAD_DIM),
                                   cache_v.reshape(N_BATCH, CACHE_ROWS, HEAD_DIM), rb_flat, sink_v)

    gate_bias = jnp.concatenate([b_igate[0], b_fgate[0], jnp.zeros((LANES - 2 * M_HEADS,), F32)]).reshape(1, LANES)
    ym_p, c_p, n_p, m_p = _mlstm_prompt(z, zg, gate_bias)
    m0_tok = jnp.pad(jnp.repeat(state_m[0], T_DEC, axis=0), ((0, 0), (0, LANES - M_HEADS)))
    n0_flat = state_n[0].reshape(N_BATCH, M_HEADS * DK)
    n0_tok = jnp.repeat(n0_flat, T_DEC, axis=0)
    ym_s, c_s, n_s, m_s = _mlstm_sample(z, zg, gate_bias, m0_tok, n0_tok, state_C[0], n0_flat)

    x1_p = _outproj(ya_p, ym_p, w_out[0], xp, mod_p, False)
    x1_s = _outproj(ya_s, ym_s, w_out[0], xs, mod_s, True)

    w_route = jnp.pad(jnp.concatenate([w_router_grp[0], w_router_exp[0]], axis=1),
                      ((0, 0), (0, LANES - N_GROUPS - N_EXPERTS)))
    b_route = jnp.pad(jnp.concatenate([b_router_grp[0], b_router_exp[0]]),
                      (0, LANES - N_GROUPS - N_EXPERTS)).reshape(1, LANES)
    w_route_hi = w_route.astype(BF16)
    w_route_split = jnp.stack([w_route_hi, (w_route - w_route_hi.astype(F32)).astype(BF16)])
    h2, route = _router(x1_p, x1_s, mod_p, mod_s, g_ffn, w_route_split, b_route)

    dest, counts = _rank(route)
    dests = (dest[:, 0], dest[:, 1])

    i32 = lambda a: a.astype(jnp.int32)
    cnt = i32(counts[0, :N_EXPERTS])
    tiles_per = (cnt + TM_MOE - 1) // TM_MOE
    tile_end = jnp.cumsum(tiles_per)
    n_tiles = tile_end[-1]
    tile_ids = jnp.arange(MAX_TILES, dtype=jnp.int32)
    tile_expert = i32(jnp.minimum(jnp.searchsorted(tile_end, tile_ids, side="right"), N_EXPERTS - 1))
    last_expert = tile_expert[jnp.maximum(n_tiles - 1, 0)]
    tile_valid = tile_ids < n_tiles
    tile_expert = jnp.where(tile_valid, tile_expert, last_expert)
    prev_expert = jnp.concatenate([jnp.full((1,), -1, jnp.int32), tile_expert[:-1]])
    run_first = jnp.logical_and(tile_valid, tile_expert != prev_expert)
    run_id = jnp.maximum(jnp.cumsum(i32(run_first)) - 1, 0)
    expert_ids = jnp.arange(N_EXPERTS, dtype=jnp.int32)
    used = jnp.where(tiles_per > 0, expert_ids, N_EXPERTS)
    next_used = jnp.concatenate([lax.cummin(used[::-1])[::-1][1:], jnp.full((1,), N_EXPERTS, jnp.int32)])
    next_used = jnp.where(next_used >= N_EXPERTS, -1, next_used)
    first_tile = tile_end - tiles_per
    tile_rows = jnp.clip(cnt[tile_expert] - (tile_ids - first_tile[tile_expert]) * TM_MOE, 0, TM_MOE)
    tile_rows = jnp.where(tile_valid, tile_rows, 0)
    tiles = (tile_expert, i32(tile_rows), i32(run_first), next_used[tile_expert], i32(run_id),
             jnp.stack([n_tiles, jnp.sum(i32(run_first))]).astype(jnp.int32))
    next_expert = jnp.concatenate([tile_expert[1:], jnp.full((1,), -1, jnp.int32)])
    run_last = jnp.logical_or(tile_expert != next_expert, tile_ids == n_tiles - 1)
    zero_class = jnp.where(tile_valid, jnp.where(run_last, ZERO_FIRST, 0), ZERO_ANYTIME)
    xs_rows = _dispatch(dests, i32(zero_class), h2)
    h1 = _expert_up(tiles, xs_rows, w_gate[0], w_up[0])
    o_rows = _expert_down(tiles, h1, w_down[0])

    gf = g_final.reshape(1, D)
    y_p = _combine(dests, x1_p, route, mod_p, False, gf, o_rows, 0)
    y_s = _combine(dests, x1_s, route, mod_s, True, gf, o_rows, N_PROMPT)

    kv5 = lambda a: a.reshape(1, -1, WINDOW, KV_HEADS, HEAD_DIM)
    kcol = ATT_WIDTH
    nk_p = z[N_PROMPT - WINDOW:N_PROMPT, kcol:kcol + KV_WIDTH]
    nv_p = z[N_PROMPT - WINDOW:N_PROMPT, kcol + KV_WIDTH:kcol + 2 * KV_WIDTH]
    return (y_p.reshape(1, N_PROMPT, D), y_s.reshape(N_BATCH, T_DEC, D),
            kv5(nk_p), kv5(nv_p),
            c_p.reshape(1, 1, M_HEADS, DV, DK), n_p.reshape(1, 1, M_HEADS, DK), m_p[:, 0].reshape(1, 1, M_HEADS),
            kv5(nk_s), kv5(nv_s),
            c_s.reshape(1, N_BATCH, M_HEADS, DV, DK), n_s.reshape(1, N_BATCH, M_HEADS, DK),
            m_s.reshape(N_BATCH, T_DEC, LANES)[:, T_DEC - 1, :M_HEADS].reshape(1, N_BATCH, M_HEADS))
```

```python
import functools
import math

import numpy as np
import jax
import jax.numpy as jnp
from jax import lax
from jax.experimental import pallas as pl
from jax.experimental.pallas import tpu as pltpu

F32 = jnp.float32
BF16 = jnp.bfloat16
NEG_INF = float("-inf")

D = 4096
N_PROMPT = 8192
N_BATCH = 128
T_DEC = 8
N_SAMPLE = N_BATCH * T_DEC
N_TOK = N_PROMPT + N_SAMPLE
HEAD_DIM = 128
ATT_HEADS = 16
KV_HEADS = 4
GROUP = ATT_HEADS // KV_HEADS
WINDOW = 128
ATT_WIDTH = ATT_HEADS * HEAD_DIM
KV_WIDTH = KV_HEADS * HEAD_DIM
NUM_BUCKETS = 32
MAX_EXACT = 16
MAX_DISTANCE = 128
M_HEADS = 8
DK = 128
DV = 256
M_WIDTH = M_HEADS * DV
Z_WIDTH = ATT_WIDTH + 2 * KV_WIDTH + 2 * M_HEADS * DK + 2 * M_WIDTH
N_GROUPS = 4
EXP_PER_GROUP = 8
N_EXPERTS = N_GROUPS * EXP_PER_GROUP
TOP_K = 2
D_FF = 1024
EPS = 1e-6
ATT_SCALE = HEAD_DIM ** -0.5
Q_SCALE = DK ** -0.5

LANES = 128
SUBLANES = 8
VMEM_LIMIT = 56 * 1024 * 1024

TM_PROJ = 512
TN_IN = 1024
TN_OUT = 1024
TN_ADA = 512
C_ROWS = 136
ATT_BLOCK = 128
SAMPLE_BT = 8
ML_CHUNK = 256
TM_MOE = 256
TN_FF = 512
N_ASSIGN = N_TOK * TOP_K
MAX_TILES = N_ASSIGN // TM_MOE + N_EXPERTS
A_PAD = MAX_TILES * TM_MOE
TM_TOK = 256
TM_RANK = 1024


def _params(sem):
    return pltpu.CompilerParams(dimension_semantics=sem, vmem_limit_bytes=VMEM_LIMIT)


def _iota(shape, dim):
    return lax.broadcasted_iota(jnp.int32, shape, dim)


def _dot(a, b):
    return jnp.dot(a, b, preferred_element_type=F32)


def _dot_nt(a, b):
    return lax.dot_general(a, b, (((1,), (1,)), ((), ())), preferred_element_type=F32)


def _dot_tn(a, b):
    return lax.dot_general(a, b, (((0,), (0,)), ((), ())), preferred_element_type=F32)


def _split3(x):
    x1 = x.astype(BF16)
    r1 = x - x1.astype(F32)
    x2 = r1.astype(BF16)
    r2 = r1 - x2.astype(F32)
    return x1, x2, r2.astype(BF16)


def _dot_exact_lhs01(a01, x):
    x1, x2, x3 = _split3(x)
    return _dot(a01, x1) + _dot(a01, x2) + _dot(a01, x3)


def _pack_bf16_pairs(x):
    w = x.shape[1] // 2
    bits = lax.bitcast_convert_type(x.astype(BF16).astype(F32), jnp.uint32)
    return (bits[:, :w] >> 16) | (bits[:, w:] & jnp.uint32(0xFFFF0000))


def _unpack_bf16_pairs(words):
    return (lax.bitcast_convert_type(words << 16, F32),
            lax.bitcast_convert_type(words & jnp.uint32(0xFFFF0000), F32))


def _sigmoid(x):
    return 1.0 / (1.0 + jnp.exp(-x))


def _log_sigmoid(x):
    return jnp.minimum(x, 0.0) - jnp.log(1.0 + jnp.exp(-jnp.abs(x)))


def _stream_in(hbm_ref, buf, sem, rows_per_step):
    i = pl.program_id(0)
    slot = lax.rem(i, 2)

    def copy(step, s):
        rows = pl.ds(step * rows_per_step, rows_per_step)
        return pltpu.make_async_copy(hbm_ref.at[rows], buf.at[s], sem.at[s])

    @pl.when(i == 0)
    def _():
        copy(0, 0).start(priority=1)

    @pl.when(i + 1 < pl.num_programs(0))
    def _():
        copy(i + 1, 1 - slot).start(priority=1)

    copy(i, slot).wait()
    return slot


def _mod_spec(per_batch, rows, width, col):
    if per_batch:
        return pl.BlockSpec((rows // T_DEC, 1, width), lambda i, *rest: (i, 0, col(*rest)))
    return pl.BlockSpec((1, width), lambda i, *rest: (0, col(*rest)))


def _mod_rows(ref):
    v = ref[...]
    if v.ndim == 2:
        return v
    nb, _, width = v.shape
    return jnp.broadcast_to(v, (nb, T_DEC, width)).reshape(nb * T_DEC, width)


def _ada_kernel(c_ref, w_ref, b_ref, o_ref):
    c = c_ref[...]
    s = (c * _sigmoid(c)).astype(BF16)
    mod = _dot(s, w_ref[...].astype(BF16)) + b_ref[...]
    o_ref[...] = mod.reshape(C_ROWS, 1, TN_ADA)


def _ada(c_all, w_ada, b_ada):
    n = w_ada.shape[1]
    return pl.pallas_call(
        _ada_kernel,
        grid=(n // TN_ADA,),
        in_specs=[pl.BlockSpec((C_ROWS, D), lambda j: (0, 0)),
                  pl.BlockSpec((D, TN_ADA), lambda j: (0, j)),
                  pl.BlockSpec((1, TN_ADA), lambda j: (0, j))],
        out_specs=pl.BlockSpec((C_ROWS, 1, TN_ADA), lambda j: (0, 0, j)),
        out_shape=jax.ShapeDtypeStruct((C_ROWS, 1, n), F32),
        compiler_params=_params(("arbitrary",)),
        name="ada",
    )(c_all, w_ada, b_ada)


PROMPT_TOK_BLOCKS = N_PROMPT // TM_TOK


def _prompt_block(i):
    return jnp.minimum(i, PROMPT_TOK_BLOCKS - 1)


def _sample_block(i):
    return jnp.maximum(i - PROMPT_TOK_BLOCKS, 0)


def _norm_body(x_ref, sh_ref, sc_ref, g_ref, wg_ref, h_ref, zg_ref):
    x = x_ref[...]
    y = x * lax.rsqrt(jnp.mean(x * x, axis=-1, keepdims=True) + EPS) * g_ref[...]
    hb = (y * (1.0 + _mod_rows(sc_ref)) + _mod_rows(sh_ref)).astype(BF16)
    h_ref[...] = hb
    zg_ref[...] = _dot_nt(hb, wg_ref[...])


def _norm_kernel(xp_ref, xs_ref, shp_ref, scp_ref, shs_ref, scs_ref, g_ref, wg_ref, h_ref, zg_ref):
    i = pl.program_id(0)

    @pl.when(i < PROMPT_TOK_BLOCKS)
    def _():
        _norm_body(xp_ref, shp_ref, scp_ref, g_ref, wg_ref, h_ref, zg_ref)

    @pl.when(i >= PROMPT_TOK_BLOCKS)
    def _():
        _norm_body(xs_ref, shs_ref, scs_ref, g_ref, wg_ref, h_ref, zg_ref)


def _norm(xp, xs, mod_p, mod_s, g_mix, w_gate):
    per_batch = lambda col: pl.BlockSpec((TM_TOK // T_DEC, 1, D), lambda i: (_sample_block(i), 0, col))
    return pl.pallas_call(
        _norm_kernel,
        grid=(N_TOK // TM_TOK,),
        in_specs=[pl.BlockSpec((TM_TOK, D), lambda i: (_prompt_block(i), 0)),
                  pl.BlockSpec((TM_TOK, D), lambda i: (_sample_block(i), 0)),
                  pl.BlockSpec((1, D), lambda i: (0, 0)),
                  pl.BlockSpec((1, D), lambda i: (0, 1)),
                  per_batch(0), per_batch(1),
                  pl.BlockSpec((1, D), lambda i: (0, 0)),
                  pl.BlockSpec((LANES, D), lambda i: (0, 0))],
        out_specs=[pl.BlockSpec((TM_TOK, D), lambda i: (i, 0)),
                   pl.BlockSpec((TM_TOK, LANES), lambda i: (i, 0))],
        out_shape=[jax.ShapeDtypeStruct((N_TOK, D), BF16), jax.ShapeDtypeStruct((N_TOK, LANES), F32)],
        compiler_params=_params(("arbitrary",)),
        name="norm",
    )(xp, xs, mod_p, mod_p, mod_s, mod_s, g_mix, w_gate)


N_IN_CHUNKS = Z_WIDTH // TN_IN


def _inproj_kernel(h_ref, wt_hbm, z_ref, stage, w_bf, sem):
    j = pl.program_id(0)

    def fetch(chunk):
        rows = pl.ds(pl.multiple_of(chunk * TN_IN, TN_IN), TN_IN)
        return pltpu.make_async_copy(wt_hbm.at[rows], stage, sem.at[0])

    @pl.when(pl.program_id(1) == 0)
    def _():
        @pl.when(j == 0)
        def _():
            fetch(0).start()

        fetch(j).wait()
        w_bf[...] = stage[...].astype(BF16)

        @pl.when(j + 1 < N_IN_CHUNKS)
        def _():
            fetch(j + 1).start()

    z_ref[...] = _dot_nt(h_ref[...], w_bf[...])


def _inproj(h, w_in_t):
    return pl.pallas_call(
        _inproj_kernel,
        grid=(N_IN_CHUNKS, N_TOK // TM_PROJ),
        in_specs=[pl.BlockSpec((TM_PROJ, D), lambda j, i: (i, 0)),
                  pl.BlockSpec(memory_space=pl.ANY)],
        out_specs=pl.BlockSpec((TM_PROJ, TN_IN), lambda j, i: (i, j)),
        out_shape=jax.ShapeDtypeStruct((N_TOK, Z_WIDTH), F32),
        scratch_shapes=[pltpu.VMEM((TN_IN, D), F32), pltpu.VMEM((TN_IN, D), BF16), pltpu.SemaphoreType.DMA((1,))],
        compiler_params=_params(("arbitrary", "arbitrary")),
        name="inproj",
    )(h, w_in_t)


def _t5_bucket_np(dist):
    n = np.maximum(dist, 0)
    nf = np.maximum(n, 1).astype(np.float32)
    large = MAX_EXACT + (np.log(nf / MAX_EXACT) / math.log(MAX_DISTANCE / MAX_EXACT)
                         * (NUM_BUCKETS - MAX_EXACT)).astype(np.int32)
    large = np.minimum(large, NUM_BUCKETS - 1)
    return np.where(n < MAX_EXACT, n, large).astype(np.int32)


def _bucket_table(n_q, n_keys_valid, n_keys_padded):
    t = np.arange(n_q)[:, None]
    j = np.arange(n_keys_padded)[None, :]
    dist = t + WINDOW - j
    valid = (dist >= 0) & (dist < WINDOW) & (j < n_keys_valid)
    return np.where(valid, _t5_bucket_np(dist), -1).astype(np.int32)


def _fill_bias(bucket_ref, rb_ref, bias_scr, rows, per_head_table=False):
    for h in range(ATT_HEADS):
        bk = bucket_ref[h * rows:(h + 1) * rows, :] if per_head_table else bucket_ref[...]
        acc = jnp.full(bk.shape, NEG_INF, F32)
        for b in range(NUM_BUCKETS):
            acc = jnp.where(bk == b, rb_ref[b * ATT_HEADS + h], acc)
        bias_scr[h * rows:(h + 1) * rows, :] = acc


def _with_ones(v2):
    return jnp.concatenate([v2, jnp.ones_like(v2)], axis=1)


def _sink_softmax_av(lg, sink, v2_ones):
    m = jnp.maximum(jnp.max(lg, axis=-1, keepdims=True), sink)
    p = jnp.exp(lg - m).astype(BF16)
    pv = _dot(p, v2_ones)
    return pv[:, :HEAD_DIM] / (pv[:, HEAD_DIM:] + jnp.exp(sink - m))


def _swa_prompt_kernel(rb_ref, sink_ref, bucket_ref, q_ref, kp_ref, kc_ref, vp_ref, vc_ref,
                       o_ref, bias_scr):
    i = pl.program_id(0)

    @pl.when(i == 0)
    def _():
        _fill_bias(bucket_ref, rb_ref, bias_scr, ATT_BLOCK)

    first_prev = jnp.logical_and(i == 0, _iota((ATT_BLOCK, 2 * ATT_BLOCK), 1) < ATT_BLOCK)
    for g in range(KV_HEADS):
        ks = slice(g * HEAD_DIM, (g + 1) * HEAD_DIM)
        k2 = jnp.concatenate([kp_ref[:, ks], kc_ref[:, ks]], axis=0).astype(BF16)
        v2 = _with_ones(jnp.concatenate([vp_ref[:, ks], vc_ref[:, ks]], axis=0).astype(BF16))
        for r in range(GROUP):
            h = g * GROUP + r
            hs = slice(h * HEAD_DIM, (h + 1) * HEAD_DIM)
            lg = _dot_nt(q_ref[:, hs].astype(BF16), k2) * ATT_SCALE + bias_scr[h * ATT_BLOCK:(h + 1) * ATT_BLOCK, :]
            lg = jnp.where(first_prev, NEG_INF, lg)
            o_ref[:, hs] = _sink_softmax_av(lg, sink_ref[h], v2).astype(BF16)


def _swa_prompt(z, rb_flat, sinks):
    nb = N_PROMPT // ATT_BLOCK
    bucket = jnp.asarray(_bucket_table(ATT_BLOCK, 2 * ATT_BLOCK, 2 * ATT_BLOCK))
    kcol = ATT_WIDTH // KV_WIDTH
    prev = lambda i: jnp.maximum(i - 1, 0)
    smem = pl.BlockSpec(memory_space=pltpu.SMEM)
    return pl.pallas_call(
        _swa_prompt_kernel,
        grid=(nb,),
        in_specs=[smem, smem,
                  pl.BlockSpec((ATT_BLOCK, 2 * ATT_BLOCK), lambda i: (0, 0)),
                  pl.BlockSpec((ATT_BLOCK, ATT_WIDTH), lambda i: (i, 0)),
                  pl.BlockSpec((ATT_BLOCK, KV_WIDTH), lambda i: (prev(i), kcol)),
                  pl.BlockSpec((ATT_BLOCK, KV_WIDTH), lambda i: (i, kcol)),
                  pl.BlockSpec((ATT_BLOCK, KV_WIDTH), lambda i: (prev(i), kcol + 1)),
                  pl.BlockSpec((ATT_BLOCK, KV_WIDTH), lambda i: (i, kcol + 1))],
        out_specs=pl.BlockSpec((ATT_BLOCK, ATT_WIDTH), lambda i: (i, 0)),
        out_shape=jax.ShapeDtypeStruct((N_PROMPT, ATT_WIDTH), BF16),
        scratch_shapes=[pltpu.VMEM((ATT_HEADS * ATT_BLOCK, 2 * ATT_BLOCK), F32)],
        compiler_params=_params(("arbitrary",)),
        name="swa_prompt",
    )(rb_flat, sinks, bucket, z, z, z, z, z)


CACHE_ROWS = WINDOW * KV_HEADS
NEW_ROWS = T_DEC * KV_HEADS
S_KEYS = 5 * LANES


def _sample_bucket_table():
    t = np.arange(T_DEC)[:, None]
    col = np.arange(S_KEYS)[None, :]
    in_cache = col < CACHE_ROWS
    in_new = (col >= CACHE_ROWS) & (col < CACHE_ROWS + NEW_ROWS)
    key_head = np.where(in_cache, col % KV_HEADS, (col - CACHE_ROWS) // T_DEC)
    key_pos = np.where(in_cache, col // KV_HEADS, WINDOW + (col - CACHE_ROWS) % T_DEC)
    dist = t + WINDOW - key_pos
    valid = (dist >= 0) & (dist < WINDOW) & (in_cache | in_new)
    per_query = np.where(valid, _t5_bucket_np(dist), -1)
    heads = np.arange(ATT_HEADS)[:, None, None] // GROUP
    table = np.where(heads == key_head[None], per_query[None], -1)
    return table.reshape(ATT_HEADS * T_DEC, S_KEYS).astype(np.int32)


def _swa_sample_kernel(rb_ref, sink_ref, bucket_ref, q_ref, kn_ref, vn_ref, ck_hbm, cv_hbm,
                       o_ref, nk_ref, nv_ref, bias_scr, k_buf, v_buf, kv_sem):
    ck_ref = k_buf.at[_stream_in(ck_hbm, k_buf, kv_sem.at[0], SAMPLE_BT)]
    cv_ref = v_buf.at[_stream_in(cv_hbm, v_buf, kv_sem.at[1], SAMPLE_BT)]

    @pl.when(pl.program_id(0) == 0)
    def _():
        _fill_bias(bucket_ref, rb_ref, bias_scr, T_DEC, per_head_table=True)

    nk_ref[:, 0:CACHE_ROWS - NEW_ROWS, :] = ck_ref[:, NEW_ROWS:CACHE_ROWS, :]
    nv_ref[:, 0:CACHE_ROWS - NEW_ROWS, :] = cv_ref[:, NEW_ROWS:CACHE_ROWS, :]

    pad = jnp.zeros((S_KEYS - CACHE_ROWS - NEW_ROWS, HEAD_DIM), F32)
    sink_col = jnp.concatenate([jnp.full((T_DEC, 1), sink_ref[h], F32) for h in range(ATT_HEADS)], axis=0)
    bias = bias_scr[...]
    for b in range(SAMPLE_BT):
        ts = slice(b * T_DEC, (b + 1) * T_DEC)
        head_cols = lambda ref, n: [ref[ts, h * HEAD_DIM:(h + 1) * HEAD_DIM] for h in range(n)]
        k_new, v_new = head_cols(kn_ref, KV_HEADS), head_cols(vn_ref, KV_HEADS)
        for g in range(KV_HEADS):
            new_rows = pl.ds(CACHE_ROWS - NEW_ROWS + g, T_DEC, stride=KV_HEADS)
            nk_ref[b, new_rows, :] = k_new[g]
            nv_ref[b, new_rows, :] = v_new[g]
        qa = jnp.concatenate(head_cols(q_ref, ATT_HEADS), axis=0).astype(BF16)
        k2 = jnp.concatenate([ck_ref[b]] + k_new + [pad], axis=0).astype(BF16)
        v2 = _with_ones(jnp.concatenate([cv_ref[b]] + v_new + [pad], axis=0).astype(BF16))
        o = _sink_softmax_av(_dot_nt(qa, k2) * ATT_SCALE + bias, sink_col, v2)
        for h in range(ATT_HEADS):
            o_ref[ts, h * HEAD_DIM:(h + 1) * HEAD_DIM] = o[h * T_DEC:(h + 1) * T_DEC].astype(BF16)


def _swa_sample(z, cache_k, cache_v, rb_flat, sinks):
    rows = SAMPLE_BT * T_DEC
    z_off = N_PROMPT // rows
    bucket = jnp.asarray(_sample_bucket_table())
    kcol = ATT_WIDTH // KV_WIDTH
    smem = pl.BlockSpec(memory_space=pltpu.SMEM)
    cache_spec = pl.BlockSpec((SAMPLE_BT, CACHE_ROWS, HEAD_DIM), lambda i: (i, 0, 0))
    cache_shape = jax.ShapeDtypeStruct((N_BATCH, CACHE_ROWS, HEAD_DIM), F32)
    return pl.pallas_call(
        _swa_sample_kernel,
        grid=(N_BATCH // SAMPLE_BT,),
        in_specs=[smem, smem,
                  pl.BlockSpec((ATT_HEADS * T_DEC, S_KEYS), lambda i: (0, 0)),
                  pl.BlockSpec((rows, ATT_WIDTH), lambda i: (i + z_off, 0)),
                  pl.BlockSpec((rows, KV_WIDTH), lambda i: (i + z_off, kcol)),
                  pl.BlockSpec((rows, KV_WIDTH), lambda i: (i + z_off, kcol + 1)),
                  pl.BlockSpec(memory_space=pl.ANY), pl.BlockSpec(memory_space=pl.ANY)],
        out_specs=[pl.BlockSpec((rows, ATT_WIDTH), lambda i: (i, 0)), cache_spec, cache_spec],
        out_shape=[jax.ShapeDtypeStruct((N_SAMPLE, ATT_WIDTH), BF16), cache_shape, cache_shape],
        scratch_shapes=[pltpu.VMEM((ATT_HEADS * T_DEC, S_KEYS), F32),
                        pltpu.VMEM((2, SAMPLE_BT, CACHE_ROWS, HEAD_DIM), F32),
                        pltpu.VMEM((2, SAMPLE_BT, CACHE_ROWS, HEAD_DIM), F32),
                        pltpu.SemaphoreType.DMA((2, 2))],
        compiler_params=_params(("arbitrary",)),
        name="swa_sample",
    )(rb_flat, sinks, bucket, z, z, z, cache_k, cache_v)


LANE_IG, LANE_LF, LANE_B = 0, M_HEADS, 2 * M_HEADS


def _gate_table(zg, gate_bias, seg_len):
    L = zg.shape[0]
    g = zg + gate_bias
    lane = _iota((L, LANES), 1)
    lf = _log_sigmoid(g)
    lf_only = jnp.where(jnp.logical_and(lane >= LANE_LF, lane < LANE_B), lf, 0.0)
    row = _iota((L, L), 0)
    col = _iota((L, L), 1)
    same_seg = (row // seg_len) == (col // seg_len)
    tril = jnp.where(jnp.logical_and(col <= row, same_seg), 1.0, 0.0).astype(BF16)
    cum = pltpu.roll(_dot_exact_lhs01(tril, lf_only), M_HEADS, axis=1)
    table = jnp.where(lane < LANE_LF, g, jnp.where(lane < LANE_B, lf, jnp.where(lane < LANE_B + M_HEADS, cum, 0.0)))
    return table, jnp.logical_and(col <= row, same_seg)


def _mlstm_decay(table, table_t, mask, m0_col, h):
    b_c = table[:, LANE_B + h:LANE_B + h + 1]
    b_r = table_t[LANE_B + h:LANE_B + h + 1, :]
    ig_r = table_t[LANE_IG + h:LANE_IG + h + 1, :]
    log_d = jnp.where(mask, b_c - b_r + ig_r, NEG_INF)
    log_inter = b_c + m0_col
    m_t = jnp.maximum(log_inter, jnp.max(log_d, axis=1, keepdims=True))
    return b_c, m_t, jnp.exp(log_inter - m_t), jnp.exp(log_d - m_t)


def _mlstm_intra(table, table_t, mask, m0_col, h, q, k, v):
    b_c, m_t, w_inter, d = _mlstm_decay(table, table_t, mask, m0_col, h)
    s = _dot_nt(q.astype(BF16), k.astype(BF16)) * d
    num_intra = _dot(s.astype(BF16), v.astype(BF16))
    den_intra = jnp.sum(s, axis=1, keepdims=True)
    return b_c, m_t, w_inter, num_intra, den_intra


ST_WIDTH = DV + LANES


def _mlstm_prompt_kernel(gb_ref, zg_ref, q_ref, k_ref, va_ref, vb_ref, oa_ref, ob_ref,
                         y_ref, c_out, n_out, m_out, st_scr, m_scr):
    step = pl.program_id(0)
    L = ML_CHUNK

    @pl.when(step == 0)
    def _():
        st_scr[...] = jnp.zeros_like(st_scr)
        m_scr[...] = jnp.zeros_like(m_scr)

    table, mask = _gate_table(zg_ref[...], gb_ref[...], L)
    table_t = table.T
    ones = jnp.ones((L, LANES), F32)
    half = M_HEADS // 2
    for h in range(M_HEADS):
        v_ref, o_ref = (va_ref, oa_ref) if h < half else (vb_ref, ob_ref)
        vs = slice((h % half) * DV, (h % half + 1) * DV)
        qb = (q_ref[:, h * DK:(h + 1) * DK] * Q_SCALE).astype(BF16)
        kb = k_ref[:, h * DK:(h + 1) * DK].astype(BF16)
        v1 = jnp.concatenate([v_ref[:, vs], ones], axis=1)
        m0 = m_scr[h:h + 1, 0:1]
        b_c, m_t, w_inter, d = _mlstm_decay(table, table_t, mask, m0, h)
        s = _dot_nt(qb, kb) * d
        st = st_scr[h]
        tot = _dot(s.astype(BF16), v1.astype(BF16)) + w_inter * _dot(qb, st.astype(BF16))
        den = tot[:, DV:]
        hh = tot[:, :DV] / jnp.maximum(jnp.abs(jnp.concatenate([den, den], axis=1)), jnp.exp(-m_t))
        y_ref[:, h * DV:(h + 1) * DV] = (_sigmoid(o_ref[:, vs]) * hh).astype(BF16)

        ig_c = table[:, LANE_IG + h:LANE_IG + h + 1]
        m_new = m_t[L - 1:L, :]
        b_last = b_c[L - 1:L, :]
        w_s = jnp.exp(b_last - b_c + ig_c - m_new)
        decay = jnp.exp(b_last + m0 - m_new)
        st_scr[h] = decay * st + _dot_tn(kb, (w_s * v1).astype(BF16))
        m_scr[h:h + 1, :] = jnp.broadcast_to(m_new, (1, LANES))

    @pl.when(step == pl.num_programs(0) - 1)
    def _():
        for h in range(M_HEADS):
            st_t = st_scr[h].T
            c_out[h] = st_t[:DV, :]
            n_out[h:h + 1, :] = st_t[DV:DV + 1, :]
        m_out[...] = m_scr[...]


def _mlstm_prompt(z, zg, gate_bias):
    L = ML_CHUNK
    blk = M_HEADS * DK
    col = lambda c: pl.BlockSpec((L, blk), lambda i: (i, c))
    const = lambda shape: pl.BlockSpec(shape, lambda i: tuple(0 for _ in shape))
    return pl.pallas_call(
        _mlstm_prompt_kernel,
        grid=(N_PROMPT // L,),
        in_specs=[const((1, LANES)),
                  pl.BlockSpec((L, LANES), lambda i: (i, 0)),
                  col(3), col(4), col(5), col(6), col(7), col(8)],
        out_specs=[pl.BlockSpec((L, M_WIDTH), lambda i: (i, 0)),
                   const((M_HEADS, DV, DK)), const((M_HEADS, DK)), const((M_HEADS, LANES))],
        out_shape=[jax.ShapeDtypeStruct((N_PROMPT, M_WIDTH), BF16),
                   jax.ShapeDtypeStruct((M_HEADS, DV, DK), F32),
                   jax.ShapeDtypeStruct((M_HEADS, DK), F32),
                   jax.ShapeDtypeStruct((M_HEADS, LANES), F32)],
        scratch_shapes=[pltpu.VMEM((M_HEADS, DK, ST_WIDTH), F32),
                        pltpu.VMEM((M_HEADS, LANES), F32)],
        compiler_params=_params(("arbitrary",)),
        name="mlstm_prompt",
    )(gate_bias, zg, z, z, z, z, z, z)


S_ROWS = SAMPLE_BT * T_DEC


def _mlstm_sample_kernel(gb_ref, zg_ref, m0_ref, n0t_ref, q_ref, k_ref, va_ref, vb_ref, oa_ref, ob_ref,
                         c0_hbm, n0_ref, y_ref, c_out, n_out, m_out, c_buf, c_sem):
    L = S_ROWS
    c0_ref = c_buf.at[_stream_in(c0_hbm, c_buf, c_sem, SAMPLE_BT)]
    table, mask = _gate_table(zg_ref[...], gb_ref[...], T_DEC)
    table_t = jnp.concatenate([table, jnp.zeros((LANES - L, LANES), F32)], axis=0).T[:, 0:L]
    m0_all = m0_ref[...]
    lane = _iota((L, LANES), 1)
    row_b = _iota((SAMPLE_BT, L), 1) // T_DEC
    seg_sum = jnp.where(row_b == _iota((SAMPLE_BT, L), 0), 1.0, 0.0)
    m_tok = jnp.zeros((L, LANES), F32)
    half = M_HEADS // 2
    for h in range(M_HEADS):
        v_ref, o_ref = (va_ref, oa_ref) if h < half else (vb_ref, ob_ref)
        vs = slice((h % half) * DV, (h % half + 1) * DV)
        q = q_ref[:, h * DK:(h + 1) * DK] * Q_SCALE
        k = k_ref[:, h * DK:(h + 1) * DK]
        v = v_ref[:, vs]
        m0 = m0_all[:, h:h + 1]
        b_c, m_t, w_inter, num_intra, den_intra = _mlstm_intra(table, table_t, mask, m0, h, q, k, v)
        qb = q.astype(BF16)
        num_inter = jnp.concatenate(
            [_dot_nt(qb[b * T_DEC:(b + 1) * T_DEC], c0_ref[b, h].astype(BF16)) for b in range(SAMPLE_BT)], axis=0)
        num = num_intra + w_inter * num_inter
        den = den_intra + w_inter * jnp.sum(q * n0t_ref[:, h * DK:(h + 1) * DK], axis=1, keepdims=True)
        hh = num / jnp.maximum(jnp.abs(den), jnp.exp(-m_t))
        y_ref[:, h * DV:(h + 1) * DV] = (_sigmoid(o_ref[:, vs]) * hh).astype(BF16)
        m_tok = jnp.where(lane == h, m_t, m_tok)

        def last_tok(x):
            x3 = x.reshape(SAMPLE_BT, T_DEC, 1)
            return jnp.broadcast_to(x3[:, T_DEC - 1:T_DEC, :], x3.shape).reshape(L, 1)

        ig_c = table[:, LANE_IG + h:LANE_IG + h + 1]
        m_new = last_tok(m_t)
        b_last = last_tok(b_c)
        w_s = jnp.exp(b_last - b_c + ig_c - m_new)
        decay = jnp.exp(b_last + m0 - m_new)
        wv = (w_s * v).astype(BF16)
        kb = k.astype(BF16)
        rowsel = _iota((L, 1), 0) // T_DEC
        for b in range(SAMPLE_BT):
            dec_b = decay[b * T_DEC + T_DEC - 1:(b + 1) * T_DEC, :]
            wv_b = jnp.where(rowsel == b, wv, jnp.zeros_like(wv))
            c_out[b, h] = dec_b * c0_ref[b, h] + _dot_tn(wv_b, kb)
        dec_rows = decay.reshape(SAMPLE_BT, T_DEC, 1)[:, T_DEC - 1, :]
        n_out[:, h * DK:(h + 1) * DK] = dec_rows * n0_ref[:, h * DK:(h + 1) * DK] + jnp.dot(
            seg_sum, w_s * k, preferred_element_type=F32, precision=lax.Precision.HIGHEST)
    m_out[...] = m_tok


def _mlstm_sample(z, zg, gate_bias, m0_tok, n0_tok, state_c, state_n):
    L = S_ROWS
    blk = M_HEADS * DK
    z_off = N_PROMPT // L
    col = lambda c: pl.BlockSpec((L, blk), lambda i: (i + z_off, c))
    c_spec = pl.BlockSpec((SAMPLE_BT, M_HEADS, DV, DK), lambda i: (i, 0, 0, 0))
    n_spec = pl.BlockSpec((SAMPLE_BT, blk), lambda i: (i, 0))
    return pl.pallas_call(
        _mlstm_sample_kernel,
        grid=(N_BATCH // SAMPLE_BT,),
        in_specs=[pl.BlockSpec((1, LANES), lambda i: (0, 0)),
                  pl.BlockSpec((L, LANES), lambda i: (i + z_off, 0)),
                  pl.BlockSpec((L, LANES), lambda i: (i, 0)),
                  pl.BlockSpec((L, blk), lambda i: (i, 0)),
                  col(3), col(4), col(5), col(6), col(7), col(8),
                  pl.BlockSpec(memory_space=pl.ANY), n_spec],
        out_specs=[pl.BlockSpec((L, M_WIDTH), lambda i: (i, 0)), c_spec, n_spec,
                   pl.BlockSpec((L, LANES), lambda i: (i, 0))],
        out_shape=[jax.ShapeDtypeStruct((N_SAMPLE, M_WIDTH), BF16),
                   jax.ShapeDtypeStruct((N_BATCH, M_HEADS, DV, DK), F32),
                   jax.ShapeDtypeStruct((N_BATCH, blk), F32),
                   jax.ShapeDtypeStruct((N_SAMPLE, LANES), F32)],
        scratch_shapes=[pltpu.VMEM((2, SAMPLE_BT, M_HEADS, DV, DK), F32), pltpu.SemaphoreType.DMA((2,))],
        compiler_params=_params(("arbitrary",)),
        name="mlstm_sample",
    )(gate_bias, zg, m0_tok, n0_tok, z, z, z, z, z, z, state_c, state_n)


N_OUT_CHUNKS = D // TN_OUT


def _outproj_kernel(ya_ref, ym_ref, w_hbm, x_ref, gt_ref, o_ref, stage, w_bf, sem):
    j = pl.program_id(0)

    def fetch(chunk):
        cols = pl.ds(pl.multiple_of(chunk * TN_OUT, TN_OUT), TN_OUT)
        return pltpu.make_async_copy(w_hbm.at[:, cols], stage, sem.at[0])

    @pl.when(pl.program_id(1) == 0)
    def _():
        @pl.when(j == 0)
        def _():
            fetch(0).start()

        fetch(j).wait()
        w_bf[...] = stage[...].astype(BF16)

        @pl.when(j + 1 < N_OUT_CHUNKS)
        def _():
            fetch(j + 1).start()

    mix = _dot(ya_ref[...], w_bf[0:ATT_WIDTH, :]) + _dot(ym_ref[...], w_bf[ATT_WIDTH:, :])
    o_ref[...] = x_ref[...] + _mod_rows(gt_ref) * mix


def _outproj(ya, ym, w_out, x, mod, per_row):
    n = x.shape[0]
    TM = TM_PROJ
    gate1_col = 2 * N_OUT_CHUNKS
    if per_row:
        gate_spec = pl.BlockSpec((TM // T_DEC, 1, TN_OUT), lambda j, i: (i, 0, gate1_col + j))
    else:
        gate_spec = pl.BlockSpec((1, TN_OUT), lambda j, i: (0, gate1_col + j))
    return pl.pallas_call(
        _outproj_kernel,
        grid=(N_OUT_CHUNKS, n // TM),
        in_specs=[pl.BlockSpec((TM, ATT_WIDTH), lambda j, i: (i, 0)),
                  pl.BlockSpec((TM, M_WIDTH), lambda j, i: (i, 0)),
                  pl.BlockSpec(memory_space=pl.ANY),
                  pl.BlockSpec((TM, TN_OUT), lambda j, i: (i, j)),
                  gate_spec],
        out_specs=pl.BlockSpec((TM, TN_OUT), lambda j, i: (i, j)),
        out_shape=jax.ShapeDtypeStruct((n, D), F32),
        scratch_shapes=[pltpu.VMEM((D, TN_OUT), F32), pltpu.VMEM((D, TN_OUT), BF16), pltpu.SemaphoreType.DMA((1,))],
        compiler_params=_params(("arbitrary", "arbitrary")),
        name="outproj",
    )(ya, ym, w_out, x, mod)


def _router_kernel(x_ref, sh_ref, sc_ref, g_ref, wr_ref, br_ref, h_ref, route_ref):
    x = x_ref[...]
    y = x * lax.rsqrt(jnp.mean(x * x, axis=-1, keepdims=True) + EPS) * g_ref[...]
    h2 = y * (1.0 + _mod_rows(sc_ref)) + _mod_rows(sh_ref)
    h_hi = h2.astype(BF16)
    h_ref[...] = _pack_bf16_pairs(h2)
    h_lo = (h2 - h_hi.astype(F32)).astype(BF16)
    logits = _dot(h_hi, wr_ref[0]) + (_dot(h_hi, wr_ref[1]) + _dot(h_lo, wr_ref[0])) + br_ref[...]
    lane = _iota(logits.shape, 1)

    def first_max(vals):
        vmax = jnp.max(vals, axis=1, keepdims=True)
        idx = jnp.min(jnp.where(vals == vmax, lane, LANES), axis=1, keepdims=True)
        return vmax, idx

    gl = jnp.where(lane < N_GROUPS, logits, NEG_INF)
    gmax, grp = first_max(gl)
    p_grp = 1.0 / jnp.sum(jnp.exp(gl - gmax), axis=1, keepdims=True)
    e_lane = lane - N_GROUPS
    in_grp = jnp.logical_and(e_lane >= 0, jnp.logical_and(e_lane < N_EXPERTS, e_lane // EXP_PER_GROUP == grp))
    el = jnp.where(in_grp, logits, NEG_INF)
    v1, i1 = first_max(el)
    v2, i2 = first_max(jnp.where(lane == i1, NEG_INF, el))
    e2w = jnp.exp(v2 - v1)
    w1 = 1.0 / (1.0 + e2w)
    w2 = e2w / (1.0 + e2w)
    route = jnp.where(lane == 0, (i1 - N_GROUPS).astype(F32),
                      jnp.where(lane == 1, (i2 - N_GROUPS).astype(F32),
                                jnp.where(lane == 2, p_grp * w1, jnp.where(lane == 3, p_grp * w2, 0.0))))
    route_ref[...] = route


def _router_merged_kernel(xp_ref, xs_ref, shp_ref, scp_ref, shs_ref, scs_ref, g_ref, wr_ref, br_ref,
                          h_ref, route_ref):
    i = pl.program_id(0)

    @pl.when(i < PROMPT_TOK_BLOCKS)
    def _():
        _router_kernel(xp_ref, shp_ref, scp_ref, g_ref, wr_ref, br_ref, h_ref, route_ref)

    @pl.when(i >= PROMPT_TOK_BLOCKS)
    def _():
        _router_kernel(xs_ref, shs_ref, scs_ref, g_ref, wr_ref, br_ref, h_ref, route_ref)


def _router(x1_p, x1_s, mod_p, mod_s, g_ffn, w_route, b_route):
    pi, si = _prompt_block, _sample_block
    return pl.pallas_call(
        _router_merged_kernel,
        grid=(N_TOK // TM_TOK,),
        in_specs=[pl.BlockSpec((TM_TOK, D), lambda i: (pi(i), 0)),
                  pl.BlockSpec((TM_TOK, D), lambda i: (si(i), 0)),
                  pl.BlockSpec((1, D), lambda i: (0, 3)),
                  pl.BlockSpec((1, D), lambda i: (0, 4)),
                  pl.BlockSpec((TM_TOK // T_DEC, 1, D), lambda i: (si(i), 0, 3)),
                  pl.BlockSpec((TM_TOK // T_DEC, 1, D), lambda i: (si(i), 0, 4)),
                  pl.BlockSpec((1, D), lambda i: (0, 0)),
                  pl.BlockSpec((2, D, LANES), lambda i: (0, 0, 0)),
                  pl.BlockSpec((1, LANES), lambda i: (0, 0))],
        out_specs=[pl.BlockSpec((TM_TOK, D_PACK), lambda i: (i, 0)),
                   pl.BlockSpec((TM_TOK, LANES), lambda i: (i, 0))],
        out_shape=[jax.ShapeDtypeStruct((N_TOK, D_PACK), jnp.uint32),
                   jax.ShapeDtypeStruct((N_TOK, LANES), F32)],
        compiler_params=_params(("arbitrary",)),
        name="router",
    )(x1_p, x1_s, mod_p, mod_p, mod_s, mod_s, g_ffn, w_route, b_route)


def _rank_kernel(route_ref, dest_ref, cnt_ref, carry, rank_scr):
    pas = pl.program_id(0)
    step = pl.program_id(1)

    @pl.when(jnp.logical_and(pas == 0, step == 0))
    def _():
        carry[...] = jnp.zeros_like(carry)

    route = route_ref[...]
    n = route.shape[0]
    lane = _iota((n, LANES), 1).astype(F32)
    o1 = jnp.where(lane == route[:, 0:1], 1.0, 0.0)
    o2 = jnp.where(lane == route[:, 1:2], 1.0, 0.0)
    lane_i = _iota((n, LANES), 1)
    rows = pl.ds(pl.multiple_of(step * TM_RANK, TM_RANK), TM_RANK)

    @pl.when(pas == 0)
    def _():
        both = o1 + o2
        strict = jnp.where(_iota((n, n), 1) < _iota((n, n), 0), 1.0, 0.0).astype(BF16)
        prior = _dot(strict, both.astype(BF16)) + carry[0:1, :]
        r1 = jnp.sum(o1 * prior, axis=1, keepdims=True)
        r2 = jnp.sum(o2 * prior, axis=1, keepdims=True)
        rank_scr[rows, :] = jnp.where(lane_i == 0, r1, jnp.where(lane_i == 1, r2, 0.0))
        carry[...] = carry[...] + jnp.sum(both, axis=0, keepdims=True)

    @pl.when(pas == 1)
    def _():
        counts = carry[...]
        tiles_per = jnp.floor((counts + (TM_MOE - 1)) * (1.0 / TM_MOE))
        before = jnp.where(_iota((LANES, LANES), 0) < _iota((LANES, LANES), 1), 1.0, 0.0).astype(BF16)
        pad_start = _dot(tiles_per.astype(BF16), before)[0:1, :] * TM_MOE
        rank = rank_scr[rows, :]
        d1 = jnp.sum(o1 * pad_start, axis=1, keepdims=True) + rank[:, 0:1]
        d2 = jnp.sum(o2 * pad_start, axis=1, keepdims=True) + rank[:, 1:2]
        dest_ref[...] = jnp.where(lane_i == 0, d1, jnp.where(lane_i == 1, d2, 0.0)).astype(jnp.int32)
        cnt_ref[...] = counts


def _rank(route):
    n = route.shape[0]
    return pl.pallas_call(
        _rank_kernel,
        grid=(2, n // TM_RANK),
        in_specs=[pl.BlockSpec((TM_RANK, LANES), lambda p, i: (i, 0))],
        out_specs=[pl.BlockSpec((TM_RANK, LANES), lambda p, i: (i * p, 0)),
                   pl.BlockSpec((SUBLANES, LANES), lambda p, i: (0, 0))],
        out_shape=[jax.ShapeDtypeStruct((n, LANES), jnp.int32), jax.ShapeDtypeStruct((SUBLANES, LANES), F32)],
        scratch_shapes=[pltpu.VMEM((SUBLANES, LANES), F32), pltpu.VMEM((n, LANES), F32)],
        compiler_params=_params(("arbitrary", "arbitrary")),
        name="rank",
    )(route)


def _row_copy(src, dst, sem):
    return pltpu.make_async_copy(src, dst, sem)


ROW_UNROLL = 8
assert T_DEC == SUBLANES
D_PACK = D // 2

ZERO_FIRST, ZERO_ANYTIME = 1, 2


def _dispatch_kernel(d1_ref, d2_ref, zc_ref, h_ref, xs_out, zbuf, sem):
    step = pl.program_id(0)
    base = step * TM_TOK

    def zero_tile(t, s):
        rows = pl.ds(pl.multiple_of(t * TM_MOE, TM_MOE), TM_MOE)
        return pltpu.make_async_copy(zbuf, xs_out.at[rows], sem.at[s])

    def for_tiles(cls, s, act):
        def body(t, carry):
            @pl.when(zc_ref[t] == cls)
            def _():
                act(zero_tile(t, s))
            return carry
        lax.fori_loop(0, MAX_TILES, body, 0)

    @pl.when(step == 0)
    def _():
        zbuf[...] = jnp.zeros_like(zbuf)
        for_tiles(ZERO_FIRST, 1, lambda c: c.start())
        for_tiles(ZERO_ANYTIME, 2, lambda c: c.start())
        for_tiles(ZERO_FIRST, 1, lambda c: c.wait())

    @pl.when(step == pl.num_programs(0) - 1)
    def _():
        for_tiles(ZERO_ANYTIME, 2, lambda c: c.wait())

    def issue(r, carry):
        _row_copy(h_ref.at[pl.ds(r, 1)], xs_out.at[pl.ds(d1_ref[base + r], 1)], sem.at[0]).start()
        _row_copy(h_ref.at[pl.ds(r, 1)], xs_out.at[pl.ds(d2_ref[base + r], 1)], sem.at[0]).start()
        return carry

    lax.fori_loop(0, TM_TOK, issue, 0, unroll=ROW_UNROLL)
    for _ in range(TOP_K):
        _row_copy(h_ref, xs_out.at[pl.ds(0, TM_TOK)], sem.at[0]).wait()


def _dispatch(dests, zero_class, h2):
    grid_spec = pltpu.PrefetchScalarGridSpec(
        num_scalar_prefetch=3,
        grid=(N_TOK // TM_TOK,),
        in_specs=[pl.BlockSpec((TM_TOK, D_PACK), lambda i, *_: (i, 0))],
        out_specs=pl.BlockSpec(memory_space=pl.ANY),
        scratch_shapes=[pltpu.VMEM((TM_MOE, D_PACK), jnp.uint32), pltpu.SemaphoreType.DMA((3,))],
    )
    return pl.pallas_call(
        _dispatch_kernel,
        grid_spec=grid_spec,
        out_shape=jax.ShapeDtypeStruct((A_PAD, D_PACK), jnp.uint32),
        compiler_params=_params(("arbitrary",)),
        name="dispatch",
    )(*dests, zero_class, h2)


N_UP_CHUNKS = D_FF // TN_FF
ROW_CAPS = tuple(range(TM_MOE // 4, TM_MOE + 1, TM_MOE // 4))


def _for_row_cap(rows, body):
    lo = 0
    for cap in ROW_CAPS:
        pl.when(jnp.logical_and(rows > lo, rows <= cap))(functools.partial(body, cap))
        lo = cap


def _expert_up_kernel(te_ref, rows_ref, first_ref, nxt_ref, run_ref, meta_ref, x_ref, wg_hbm, wu_hbm,
                      o_ref, wbuf, sem):
    n = pl.program_id(0)
    t = pl.program_id(1)
    slot = lax.rem(n * meta_ref[1] + run_ref[t], 2)

    def fetch(e, chunk, s):
        cols = pl.ds(pl.multiple_of(chunk * TN_FF, TN_FF), TN_FF)
        return (pltpu.make_async_copy(wg_hbm.at[e, :, cols], wbuf.at[s, 0], sem.at[s, 0]),
                pltpu.make_async_copy(wu_hbm.at[e, :, cols], wbuf.at[s, 1], sem.at[s, 1]))

    def start(copies):
        for priority, c in enumerate(copies):
            c.start(priority=priority)

    @pl.when(first_ref[t] == 1)
    def _():
        @pl.when(jnp.logical_and(n == 0, t == 0))
        def _():
            start(fetch(te_ref[0], 0, 0))

        for c in fetch(te_ref[t], n, slot):
            c.wait()

        @pl.when(nxt_ref[t] >= 0)
        def _():
            start(fetch(nxt_ref[t], n, 1 - slot))

        @pl.when(jnp.logical_and(nxt_ref[t] < 0, n + 1 < N_UP_CHUNKS))
        def _():
            start(fetch(te_ref[0], n + 1, 1 - slot))

    def compute(cap):
        x = jnp.concatenate([half.astype(BF16) for half in _unpack_bf16_pairs(x_ref[0:cap, :])], axis=1)
        g = _dot(x, wbuf[slot, 0].astype(BF16))
        u = _dot(x, wbuf[slot, 1].astype(BF16))
        o_ref[0:cap, :] = (g * _sigmoid(g) * u).astype(BF16)
        if cap < TM_MOE:
            o_ref[cap:, :] = jnp.zeros((TM_MOE - cap, TN_FF), BF16)

    _for_row_cap(rows_ref[t], compute)

    @pl.when(rows_ref[t] == 0)
    def _():
        o_ref[...] = jnp.zeros_like(o_ref)


def _expert_up(tiles, xs, w_gate, w_up):
    last = lambda t, meta: jnp.minimum(t, meta[0] - 1)
    grid_spec = pltpu.PrefetchScalarGridSpec(
        num_scalar_prefetch=6,
        grid=(N_UP_CHUNKS, MAX_TILES),
        in_specs=[pl.BlockSpec((TM_MOE, D_PACK), lambda n, t, *s: (last(t, s[5]), 0)),
                  pl.BlockSpec(memory_space=pl.ANY),
                  pl.BlockSpec(memory_space=pl.ANY)],
        out_specs=pl.BlockSpec((TM_MOE, TN_FF), lambda n, t, *s: (t, n)),
        scratch_shapes=[pltpu.VMEM((2, 2, D, TN_FF), F32), pltpu.SemaphoreType.DMA((2, 2))],
    )
    return pl.pallas_call(
        _expert_up_kernel,
        grid_spec=grid_spec,
        out_shape=jax.ShapeDtypeStruct((A_PAD, D_FF), BF16),
        compiler_params=_params(("arbitrary", "arbitrary")),
        name="expert_up",
    )(*tiles, xs, w_gate, w_up)


def _expert_down_kernel(te_ref, rows_ref, first_ref, nxt_ref, run_ref, meta_ref, h_ref, wd_hbm,
                        o_ref, wbuf, sem):
    t = pl.program_id(0)
    slot = lax.rem(run_ref[t], 2)

    def fetch(e, s):
        halves = [pl.ds(k * (D_FF // 2), D_FF // 2) for k in range(2)]
        return [pltpu.make_async_copy(wd_hbm.at[e, rows], wbuf.at[s, rows], sem.at[s, k])
                for k, rows in enumerate(halves)]

    def start(copies):
        for priority, c in enumerate(copies):
            c.start(priority=priority)

    @pl.when(first_ref[t] == 1)
    def _():
        @pl.when(t == 0)
        def _():
            start(fetch(te_ref[0], 0))

        for c in fetch(te_ref[t], slot):
            c.wait()

        @pl.when(nxt_ref[t] >= 0)
        def _():
            start(fetch(nxt_ref[t], 1 - slot))

    def compute(cap):
        o_ref[0:cap, :] = _pack_bf16_pairs(_dot(h_ref[0:cap, :], wbuf[slot].astype(BF16)))
        if cap < TM_MOE:
            o_ref[cap:, :] = jnp.zeros((TM_MOE - cap, D_PACK), jnp.uint32)

    _for_row_cap(rows_ref[t], compute)

    @pl.when(rows_ref[t] == 0)
    def _():
        o_ref[...] = jnp.zeros_like(o_ref)


def _expert_down(tiles, h1, w_down):
    last = lambda t, meta: jnp.minimum(t, meta[0] - 1)
    grid_spec = pltpu.PrefetchScalarGridSpec(
        num_scalar_prefetch=6,
        grid=(MAX_TILES,),
        in_specs=[pl.BlockSpec((TM_MOE, D_FF), lambda t, *s: (last(t, s[5]), 0)),
                  pl.BlockSpec(memory_space=pl.ANY)],
        out_specs=pl.BlockSpec((TM_MOE, D_PACK), lambda t, *s: (t, 0)),
        scratch_shapes=[pltpu.VMEM((2, D_FF, D), F32), pltpu.SemaphoreType.DMA((2, 2))],
    )
    return pl.pallas_call(
        _expert_down_kernel,
        grid_spec=grid_spec,
        out_shape=jax.ShapeDtypeStruct((A_PAD, D_PACK), jnp.uint32),
        compiler_params=_params(("arbitrary",)),
        name="expert_down",
    )(*tiles, h1, w_down)


def _combine_kernel(d1_ref, d2_ref, x_ref, route_ref, gt_ref, gf_ref, o_hbm,
                    y_ref, buf, sem, *, tok_offset):
    step = pl.program_id(0)
    slot = lax.rem(step, 2)

    def gather(s, into):
        base = tok_offset + s * TM_TOK

        def issue(r, carry):
            _row_copy(o_hbm.at[pl.ds(d1_ref[base + r], 1)], buf.at[into, 0, pl.ds(r, 1)], sem.at[into]).start()
            _row_copy(o_hbm.at[pl.ds(d2_ref[base + r], 1)], buf.at[into, 1, pl.ds(r, 1)], sem.at[into]).start()
            return carry

        lax.fori_loop(0, TM_TOK, issue, 0, unroll=ROW_UNROLL)

    @pl.when(step == 0)
    def _():
        gather(0, 0)

    @pl.when(step + 1 < pl.num_programs(0))
    def _():
        gather(step + 1, 1 - slot)

    for k in range(TOP_K):
        _row_copy(o_hbm.at[pl.ds(0, TM_TOK)], buf.at[slot, k], sem.at[slot]).wait()

    route = route_ref[...]
    lo1, hi1 = _unpack_bf16_pairs(buf[slot, 0])
    lo2, hi2 = _unpack_bf16_pairs(buf[slot, 1])
    g1, g2 = route[:, 2:3], route[:, 3:4]
    moe = jnp.concatenate([g1 * lo1 + g2 * lo2, g1 * hi1 + g2 * hi2], axis=1)
    x2 = x_ref[...] + _mod_rows(gt_ref) * moe
    y_ref[...] = x2 * lax.rsqrt(jnp.mean(x2 * x2, axis=-1, keepdims=True) + EPS) * gf_ref[...]


def _combine(dests, x1, route, mod, per_row, g_final, o_rows, tok_offset):
    n = x1.shape[0]
    off = tok_offset // TM_TOK
    grid_spec = pltpu.PrefetchScalarGridSpec(
        num_scalar_prefetch=2,
        grid=(n // TM_TOK,),
        in_specs=[pl.BlockSpec((TM_TOK, D), lambda i, *_: (i, 0)),
                  pl.BlockSpec((TM_TOK, LANES), lambda i, *_: (i + off, 0)),
                  _mod_spec(per_row, TM_TOK, D, lambda *_: 5),
                  pl.BlockSpec((1, D), lambda i, *_: (0, 0)),
                  pl.BlockSpec(memory_space=pl.ANY)],
        out_specs=pl.BlockSpec((TM_TOK, D), lambda i, *_: (i, 0)),
        scratch_shapes=[pltpu.VMEM((2, TOP_K, TM_TOK, D_PACK), jnp.uint32), pltpu.SemaphoreType.DMA((2,))],
    )
    return pl.pallas_call(
        functools.partial(_combine_kernel, tok_offset=tok_offset),
        grid_spec=grid_spec,
        out_shape=jax.ShapeDtypeStruct((n, D), F32),
        compiler_params=_params(("arbitrary",)),
        name="combine",
    )(*dests, x1, route, mod, g_final, o_rows)


def kernel(x_prompt, x_sample, cache_k, cache_v, state_C, state_n, state_m, c_prompt, c_sample, rel_bias, w_ada, b_ada, g_mix, g_ffn, w_in, sinks, b_igate, b_fgate, w_out, w_router_grp, b_router_grp, w_router_exp, b_router_exp, w_gate, w_up, w_down, g_final):
    xp = x_prompt.reshape(N_PROMPT, D)
    xs = x_sample.reshape(N_SAMPLE, D)

    c_all = jnp.concatenate([c_sample, c_prompt, jnp.zeros((C_ROWS - 1 - N_BATCH, D), F32)], axis=0)
    mod_s = _ada(c_all, w_ada[0], b_ada)
    mod_p = mod_s[N_BATCH]

    w_in_t = jnp.swapaxes(w_in[0], 0, 1)
    w_gates_t = jnp.pad(w_in_t[Z_WIDTH:].astype(BF16), ((0, LANES - 2 * M_HEADS), (0, 0)))
    h_all, zg = _norm(xp, xs, mod_p, mod_s, g_mix, w_gates_t)
    z = _inproj(h_all, w_in_t)

    rb_flat = rel_bias.reshape(NUM_BUCKETS * ATT_HEADS)
    sink_v = sinks[0]
    ya_p = _swa_prompt(z, rb_flat, sink_v)
    ya_s, nk_s, nv_s = _swa_sample(z,cache_k.reshape(N_BATCH, CACHE_ROWS, HEAD_DIM),
                                   cache_v.reshape(N_BATCH, CACHE_ROWS, HEAD_DIM), rb_flat, sink_v)

    gate_bias = jnp.concatenate([b_igate[0], b_fgate[0], jnp.zeros((LANES - 2 * M_HEADS,), F32)]).reshape(1, LANES)
    ym_p, c_p, n_p, m_p = _mlstm_prompt(z, zg, gate_bias)
    m0_tok = jnp.pad(jnp.repeat(state_m[0], T_DEC, axis=0), ((0, 0), (0, LANES - M_HEADS)))
    n0_flat = state_n[0].reshape(N_BATCH, M_HEADS * DK)
    n0_tok = jnp.repeat(n0_flat, T_DEC, axis=0)
    ym_s, c_s, n_s, m_s = _mlstm_sample(z, zg, gate_bias, m0_tok, n0_tok, state_C[0], n0_flat)

    x1_p = _outproj(ya_p, ym_p, w_out[0], xp, mod_p, False)
    x1_s = _outproj(ya_s, ym_s, w_out[0], xs, mod_s, True)

    w_route = jnp.pad(jnp.concatenate([w_router_grp[0], w_router_exp[0]], axis=1),
                      ((0, 0), (0, LANES - N_GROUPS - N_EXPERTS)))
    b_route = jnp.pad(jnp.concatenate([b_router_grp[0], b_router_exp[0]]),
                      (0, LANES - N_GROUPS - N_EXPERTS)).reshape(1, LANES)
    w_route_hi = w_route.astype(BF16)
    w_route_split = jnp.stack([w_route_hi, (w_route - w_route_hi.astype(F32)).astype(BF16)])
    h2, route = _router(x1_p, x1_s, mod_p, mod_s, g_ffn, w_route_split, b_route)

    dest, counts = _rank(route)
    dests = (dest[:, 0], dest[:, 1])

    i32 = lambda a: a.astype(jnp.int32)
    cnt = i32(counts[0, :N_EXPERTS])
    tiles_per = (cnt + TM_MOE - 1) // TM_MOE
    tile_end = jnp.cumsum(tiles_per)
    n_tiles = tile_end[-1]
    tile_ids = jnp.arange(MAX_TILES, dtype=jnp.int32)
    tile_expert = i32(jnp.minimum(jnp.searchsorted(tile_end, tile_ids, side="right"), N_EXPERTS - 1))
    last_expert = tile_expert[jnp.maximum(n_tiles - 1, 0)]
    tile_valid = tile_ids < n_tiles
    tile_expert = jnp.where(tile_valid, tile_expert, last_expert)
    prev_expert = jnp.concatenate([jnp.full((1,), -1, jnp.int32), tile_expert[:-1]])
    run_first = jnp.logical_and(tile_valid, tile_expert != prev_expert)
    run_id = jnp.maximum(jnp.cumsum(i32(run_first)) - 1, 0)
    expert_ids = jnp.arange(N_EXPERTS, dtype=jnp.int32)
    used = jnp.where(tiles_per > 0, expert_ids, N_EXPERTS)
    next_used = jnp.concatenate([lax.cummin(used[::-1])[::-1][1:], jnp.full((1,), N_EXPERTS, jnp.int32)])
    next_used = jnp.where(next_used >= N_EXPERTS, -1, next_used)
    first_tile = tile_end - tiles_per
    tile_rows = jnp.clip(cnt[tile_expert] - (tile_ids - first_tile[tile_expert]) * TM_MOE, 0, TM_MOE)
    tile_rows = jnp.where(tile_valid, tile_rows, 0)
    tiles = (tile_expert, i32(tile_rows), i32(run_first), next_used[tile_expert], i32(run_id),
             jnp.stack([n_tiles, jnp.sum(i32(run_first))]).astype(jnp.int32))
    next_expert = jnp.concatenate([tile_expert[1:], jnp.full((1,), -1, jnp.int32)])
    run_last = jnp.logical_or(tile_expert != next_expert, tile_ids == n_tiles - 1)
    zero_class = jnp.where(tile_valid, jnp.where(run_last, ZERO_FIRST, 0), ZERO_ANYTIME)
    xs_rows = _dispatch(dests, i32(zero_class), h2)
    h1 = _expert_up(tiles, xs_rows, w_gate[0], w_up[0])
    o_rows = _expert_down(tiles, h1, w_down[0])

    gf = g_final.reshape(1, D)
    y_p = _combine(dests, x1_p, route, mod_p, False, gf, o_rows, 0)
    y_s = _combine(dests, x1_s, route, mod_s, True, gf, o_rows, N_PROMPT)

    kv5 = lambda a: a.reshape(1, -1, WINDOW, KV_HEADS, HEAD_DIM)
    kcol = ATT_WIDTH
    nk_p = z[N_PROMPT - WINDOW:N_PROMPT, kcol:kcol + KV_WIDTH]
    nv_p = z[N_PROMPT - WINDOW:N_PROMPT, kcol + KV_WIDTH:kcol + 2 * KV_WIDTH]
    return (y_p.reshape(1, N_PROMPT, D), y_s.reshape(N_BATCH, T_DEC, D),
            kv5(nk_p), kv5(nv_p),
            c_p.reshape(1, 1, M_HEADS, DV, DK), n_p.reshape(1, 1, M_HEADS, DK), m_p[:, 0].reshape(1, 1, M_HEADS),
            kv5(nk_s), kv5(nv_s),
            c_s.reshape(1, N_BATCH, M_HEADS, DV, DK), n_s.reshape(1, N_BATCH, M_HEADS, DK),
            m_s.reshape(N_BATCH, T_DEC, LANES)[:, T_DEC - 1, :M_HEADS].reshape(1, N_BATCH, M_HEADS))
```

```python
import functools
import math

import numpy as np
import jax
import jax.numpy as jnp
from jax import lax
from jax.experimental import pallas as pl
from jax.experimental.pallas import tpu as pltpu

F32 = jnp.float32
BF16 = jnp.bfloat16
NEG_INF = float("-inf")

D = 4096
N_PROMPT = 8192
N_BATCH = 128
T_DEC = 8
N_SAMPLE = N_BATCH * T_DEC
N_TOK = N_PROMPT + N_SAMPLE
HEAD_DIM = 128
ATT_HEADS = 16
KV_HEADS = 4
GROUP = ATT_HEADS // KV_HEADS
WINDOW = 128
ATT_WIDTH = ATT_HEADS * HEAD_DIM
KV_WIDTH = KV_HEADS * HEAD_DIM
NUM_BUCKETS = 32
MAX_EXACT = 16
MAX_DISTANCE = 128
M_HEADS = 8
DK = 128
DV = 256
M_WIDTH = M_HEADS * DV
Z_WIDTH = ATT_WIDTH + 2 * KV_WIDTH + 2 * M_HEADS * DK + 2 * M_WIDTH
N_GROUPS = 4
EXP_PER_GROUP = 8
N_EXPERTS = N_GROUPS * EXP_PER_GROUP
TOP_K = 2
D_FF = 1024
EPS = 1e-6
ATT_SCALE = HEAD_DIM ** -0.5
Q_SCALE = DK ** -0.5

LANES = 128
SUBLANES = 8
VMEM_LIMIT = 56 * 1024 * 1024

TM_PROJ = 512
TM_IN = 1024
TN_IN = 1024
TN_OUT = 1024
TN_ADA = 512
C_ROWS = 136
ATT_BLOCK = 128
SAMPLE_BT = 8
ML_CHUNK = 256
TM_MOE = 256
TN_FF = 512
N_ASSIGN = N_TOK * TOP_K
MAX_TILES = N_ASSIGN // TM_MOE + N_EXPERTS
A_PAD = MAX_TILES * TM_MOE
TM_TOK = 256
TM_RANK = 1024


def _params(sem):
    return pltpu.CompilerParams(dimension_semantics=sem, vmem_limit_bytes=VMEM_LIMIT)


def _iota(shape, dim):
    return lax.broadcasted_iota(jnp.int32, shape, dim)


def _dot(a, b):
    return jnp.dot(a, b, preferred_element_type=F32)


def _dot_nt(a, b):
    return lax.dot_general(a, b, (((1,), (1,)), ((), ())), preferred_element_type=F32)


def _dot_tn(a, b):
    return lax.dot_general(a, b, (((0,), (0,)), ((), ())), preferred_element_type=F32)


def _split3(x):
    x1 = x.astype(BF16)
    r1 = x - x1.astype(F32)
    x2 = r1.astype(BF16)
    r2 = r1 - x2.astype(F32)
    return x1, x2, r2.astype(BF16)


def _dot_exact_lhs01(a01, x):
    x1, x2, x3 = _split3(x)
    return _dot(a01, x1) + _dot(a01, x2) + _dot(a01, x3)


def _pack_bf16_pairs(x):
    w = x.shape[1] // 2
    bits = lax.bitcast_convert_type(x.astype(BF16).astype(F32), jnp.uint32)
    return (bits[:, :w] >> 16) | (bits[:, w:] & jnp.uint32(0xFFFF0000))


def _unpack_bf16_pairs(words):
    return (lax.bitcast_convert_type(words << 16, F32),
            lax.bitcast_convert_type(words & jnp.uint32(0xFFFF0000), F32))


def _sigmoid(x):
    return 1.0 / (1.0 + jnp.exp(-x))


def _log_sigmoid(x):
    return jnp.minimum(x, 0.0) - jnp.log(1.0 + jnp.exp(-jnp.abs(x)))


def _mod_spec(per_batch, rows, width, col):
    if per_batch:
        return pl.BlockSpec((rows // T_DEC, 1, width), lambda i, *rest: (i, 0, col(*rest)))
    return pl.BlockSpec((1, width), lambda i, *rest: (0, col(*rest)))


def _mod_rows(ref):
    v = ref[...]
    if v.ndim == 2:
        return v
    nb, _, width = v.shape
    return jnp.broadcast_to(v, (nb, T_DEC, width)).reshape(nb * T_DEC, width)


def _ada_kernel(c_ref, w_ref, b_ref, o_ref):
    c = c_ref[...]
    s = (c * _sigmoid(c)).astype(BF16)
    mod = _dot(s, w_ref[...].astype(BF16)) + b_ref[...]
    o_ref[...] = mod.reshape(C_ROWS, 1, TN_ADA)


def _ada(c_all, w_ada, b_ada):
    n = w_ada.shape[1]
    return pl.pallas_call(
        _ada_kernel,
        grid=(n // TN_ADA,),
        in_specs=[pl.BlockSpec((C_ROWS, D), lambda j: (0, 0)),
                  pl.BlockSpec((D, TN_ADA), lambda j: (0, j)),
                  pl.BlockSpec((1, TN_ADA), lambda j: (0, j))],
        out_specs=pl.BlockSpec((C_ROWS, 1, TN_ADA), lambda j: (0, 0, j)),
        out_shape=jax.ShapeDtypeStruct((C_ROWS, 1, n), F32),
        compiler_params=_params(("arbitrary",)),
        name="ada",
    )(c_all, w_ada, b_ada)


PROMPT_TOK_BLOCKS = N_PROMPT // TM_TOK


def _prompt_block(i):
    return jnp.minimum(i, PROMPT_TOK_BLOCKS - 1)


def _sample_block(i):
    return jnp.maximum(i - PROMPT_TOK_BLOCKS, 0)


def _norm_body(x_ref, sh_ref, sc_ref, g_ref, wg_ref, h_ref, zg_ref):
    x = x_ref[...]
    y = x * lax.rsqrt(jnp.mean(x * x, axis=-1, keepdims=True) + EPS) * g_ref[...]
    hb = (y * (1.0 + _mod_rows(sc_ref)) + _mod_rows(sh_ref)).astype(BF16)
    h_ref[...] = hb
    zg_ref[...] = _dot_nt(hb, wg_ref[...])


def _norm_kernel(xp_ref, xs_ref, shp_ref, scp_ref, shs_ref, scs_ref, g_ref, wg_ref, h_ref, zg_ref):
    i = pl.program_id(0)

    @pl.when(i < PROMPT_TOK_BLOCKS)
    def _():
        _norm_body(xp_ref, shp_ref, scp_ref, g_ref, wg_ref, h_ref, zg_ref)

    @pl.when(i >= PROMPT_TOK_BLOCKS)
    def _():
        _norm_body(xs_ref, shs_ref, scs_ref, g_ref, wg_ref, h_ref, zg_ref)


def _norm(xp, xs, mod_p, mod_s, g_mix, w_gate):
    per_batch = lambda col: pl.BlockSpec((TM_TOK // T_DEC, 1, D), lambda i: (_sample_block(i), 0, col))
    return pl.pallas_call(
        _norm_kernel,
        grid=(N_TOK // TM_TOK,),
        in_specs=[pl.BlockSpec((TM_TOK, D), lambda i: (_prompt_block(i), 0)),
                  pl.BlockSpec((TM_TOK, D), lambda i: (_sample_block(i), 0)),
                  pl.BlockSpec((1, D), lambda i: (0, 0)),
                  pl.BlockSpec((1, D), lambda i: (0, 1)),
                  per_batch(0), per_batch(1),
                  pl.BlockSpec((1, D), lambda i: (0, 0)),
                  pl.BlockSpec((LANES, D), lambda i: (0, 0))],
        out_specs=[pl.BlockSpec((TM_TOK, D), lambda i: (i, 0)),
                   pl.BlockSpec((TM_TOK, LANES), lambda i: (i, 0))],
        out_shape=[jax.ShapeDtypeStruct((N_TOK, D), BF16), jax.ShapeDtypeStruct((N_TOK, LANES), F32)],
        compiler_params=_params(("arbitrary",)),
        name="norm",
    )(xp, xs, mod_p, mod_p, mod_s, mod_s, g_mix, w_gate)


N_IN_CHUNKS = Z_WIDTH // TN_IN


def _inproj_kernel(h_ref, wt_hbm, z_ref, stage, w_bf, sem):
    j = pl.program_id(0)

    def fetch(chunk):
        rows = pl.ds(pl.multiple_of(chunk * TN_IN, TN_IN), TN_IN)
        return pltpu.make_async_copy(wt_hbm.at[rows], stage, sem.at[0])

    @pl.when(pl.program_id(1) == 0)
    def _():
        @pl.when(j == 0)
        def _():
            fetch(0).start()

        fetch(j).wait()
        w_bf[...] = stage[...].astype(BF16)

        @pl.when(j + 1 < N_IN_CHUNKS)
        def _():
            fetch(j + 1).start()

    z_ref[...] = _dot_nt(h_ref[...], w_bf[...])


def _inproj(h, w_in_t):
    return pl.pallas_call(
        _inproj_kernel,
        grid=(N_IN_CHUNKS, N_TOK // TM_IN),
        in_specs=[pl.BlockSpec((TM_IN, D), lambda j, i: (i, 0)),
                  pl.BlockSpec(memory_space=pl.ANY)],
        out_specs=pl.BlockSpec((TM_IN, TN_IN), lambda j, i: (i, j)),
        out_shape=jax.ShapeDtypeStruct((N_TOK, Z_WIDTH), F32),
        scratch_shapes=[pltpu.VMEM((TN_IN, D), F32), pltpu.VMEM((TN_IN, D), BF16), pltpu.SemaphoreType.DMA((1,))],
        compiler_params=_params(("arbitrary", "arbitrary")),
        name="inproj",
    )(h, w_in_t)


def _t5_bucket_np(dist):
    n = np.maximum(dist, 0)
    nf = np.maximum(n, 1).astype(np.float32)
    large = MAX_EXACT + (np.log(nf / MAX_EXACT) / math.log(MAX_DISTANCE / MAX_EXACT)
                         * (NUM_BUCKETS - MAX_EXACT)).astype(np.int32)
    large = np.minimum(large, NUM_BUCKETS - 1)
    return np.where(n < MAX_EXACT, n, large).astype(np.int32)


def _bucket_table(n_q, n_keys_valid, n_keys_padded):
    t = np.arange(n_q)[:, None]
    j = np.arange(n_keys_padded)[None, :]
    dist = t + WINDOW - j
    valid = (dist >= 0) & (dist < WINDOW) & (j < n_keys_valid)
    return np.where(valid, _t5_bucket_np(dist), -1).astype(np.int32)


def _fill_bias(bucket_ref, rb_ref, bias_scr, rows, per_head_table=False):
    for h in range(ATT_HEADS):
        bk = bucket_ref[h * rows:(h + 1) * rows, :] if per_head_table else bucket_ref[...]
        acc = jnp.full(bk.shape, NEG_INF, F32)
        for b in range(NUM_BUCKETS):
            acc = jnp.where(bk == b, rb_ref[b * ATT_HEADS + h], acc)
        bias_scr[h * rows:(h + 1) * rows, :] = acc


def _with_ones(v2):
    return jnp.concatenate([v2, jnp.ones_like(v2)], axis=1)


def _sink_softmax_av(lg, sink, v2_ones):
    m = jnp.maximum(jnp.max(lg, axis=-1, keepdims=True), sink)
    p = jnp.exp(lg - m).astype(BF16)
    pv = _dot(p, v2_ones)
    return pv[:, :HEAD_DIM] / (pv[:, HEAD_DIM:] + jnp.exp(sink - m))


def _swa_prompt_kernel(rb_ref, sink_ref, bucket_ref, q_ref, kp_ref, kc_ref, vp_ref, vc_ref,
                       o_ref, bias_scr):
    i = pl.program_id(0)

    @pl.when(i == 0)
    def _():
        _fill_bias(bucket_ref, rb_ref, bias_scr, ATT_BLOCK)

    first_prev = jnp.logical_and(i == 0, _iota((ATT_BLOCK, 2 * ATT_BLOCK), 1) < ATT_BLOCK)
    for g in range(KV_HEADS):
        ks = slice(g * HEAD_DIM, (g + 1) * HEAD_DIM)
        k2 = jnp.concatenate([kp_ref[:, ks], kc_ref[:, ks]], axis=0).astype(BF16)
        v2 = _with_ones(jnp.concatenate([vp_ref[:, ks], vc_ref[:, ks]], axis=0).astype(BF16))
        for r in range(GROUP):
            h = g * GROUP + r
            hs = slice(h * HEAD_DIM, (h + 1) * HEAD_DIM)
            lg = _dot_nt(q_ref[:, hs].astype(BF16), k2) * ATT_SCALE + bias_scr[h * ATT_BLOCK:(h + 1) * ATT_BLOCK, :]
            lg = jnp.where(first_prev, NEG_INF, lg)
            o_ref[:, hs] = _sink_softmax_av(lg, sink_ref[h], v2).astype(BF16)


def _swa_prompt(z, rb_flat, sinks):
    nb = N_PROMPT // ATT_BLOCK
    bucket = jnp.asarray(_bucket_table(ATT_BLOCK, 2 * ATT_BLOCK, 2 * ATT_BLOCK))
    kcol = ATT_WIDTH // KV_WIDTH
    prev = lambda i: jnp.maximum(i - 1, 0)
    smem = pl.BlockSpec(memory_space=pltpu.SMEM)
    return pl.pallas_call(
        _swa_prompt_kernel,
        grid=(nb,),
        in_specs=[smem, smem,
                  pl.BlockSpec((ATT_BLOCK, 2 * ATT_BLOCK), lambda i: (0, 0)),
                  pl.BlockSpec((ATT_BLOCK, ATT_WIDTH), lambda i: (i, 0)),
                  pl.BlockSpec((ATT_BLOCK, KV_WIDTH), lambda i: (prev(i), kcol)),
                  pl.BlockSpec((ATT_BLOCK, KV_WIDTH), lambda i: (i, kcol)),
                  pl.BlockSpec((ATT_BLOCK, KV_WIDTH), lambda i: (prev(i), kcol + 1)),
                  pl.BlockSpec((ATT_BLOCK, KV_WIDTH), lambda i: (i, kcol + 1))],
        out_specs=pl.BlockSpec((ATT_BLOCK, ATT_WIDTH), lambda i: (i, 0)),
        out_shape=jax.ShapeDtypeStruct((N_PROMPT, ATT_WIDTH), BF16),
        scratch_shapes=[pltpu.VMEM((ATT_HEADS * ATT_BLOCK, 2 * ATT_BLOCK), F32)],
        compiler_params=_params(("arbitrary",)),
        name="swa_prompt",
    )(rb_flat, sinks, bucket, z, z, z, z, z)


CACHE_ROWS = WINDOW * KV_HEADS
NEW_ROWS = T_DEC * KV_HEADS
S_KEYS = 5 * LANES


def _sample_bucket_table():
    t = np.arange(T_DEC)[:, None]
    col = np.arange(S_KEYS)[None, :]
    in_cache = col < CACHE_ROWS
    in_new = (col >= CACHE_ROWS) & (col < CACHE_ROWS + NEW_ROWS)
    key_head = np.where(in_cache, col % KV_HEADS, (col - CACHE_ROWS) // T_DEC)
    key_pos = np.where(in_cache, col // KV_HEADS, WINDOW + (col - CACHE_ROWS) % T_DEC)
    dist = t + WINDOW - key_pos
    valid = (dist >= 0) & (dist < WINDOW) & (in_cache | in_new)
    per_query = np.where(valid, _t5_bucket_np(dist), -1)
    heads = np.arange(ATT_HEADS)[:, None, None] // GROUP
    table = np.where(heads == key_head[None], per_query[None], -1)
    return table.reshape(ATT_HEADS * T_DEC, S_KEYS).astype(np.int32)


def _swa_sample_kernel(rb_ref, sink_ref, bucket_ref, q_ref, kn_ref, vn_ref, ck_ref, cv_ref,
                       o_ref, nk_ref, nv_ref, bias_scr):
    @pl.when(pl.program_id(0) == 0)
    def _():
        _fill_bias(bucket_ref, rb_ref, bias_scr, T_DEC, per_head_table=True)

    nk_ref[:, 0:CACHE_ROWS - NEW_ROWS, :] = ck_ref[:, NEW_ROWS:CACHE_ROWS, :]
    nv_ref[:, 0:CACHE_ROWS - NEW_ROWS, :] = cv_ref[:, NEW_ROWS:CACHE_ROWS, :]

    pad = jnp.zeros((S_KEYS - CACHE_ROWS - NEW_ROWS, HEAD_DIM), F32)
    sink_col = jnp.concatenate([jnp.full((T_DEC, 1), sink_ref[h], F32) for h in range(ATT_HEADS)], axis=0)
    bias = bias_scr[...]
    for b in range(SAMPLE_BT):
        ts = slice(b * T_DEC, (b + 1) * T_DEC)
        head_cols = lambda ref, n: [ref[ts, h * HEAD_DIM:(h + 1) * HEAD_DIM] for h in range(n)]
        k_new, v_new = head_cols(kn_ref, KV_HEADS), head_cols(vn_ref, KV_HEADS)
        for g in range(KV_HEADS):
            new_rows = pl.ds(CACHE_ROWS - NEW_ROWS + g, T_DEC, stride=KV_HEADS)
            nk_ref[b, new_rows, :] = k_new[g]
            nv_ref[b, new_rows, :] = v_new[g]
        qa = jnp.concatenate(head_cols(q_ref, ATT_HEADS), axis=0).astype(BF16)
        k2 = jnp.concatenate([ck_ref[b]] + k_new + [pad], axis=0).astype(BF16)
        v2 = _with_ones(jnp.concatenate([cv_ref[b]] + v_new + [pad], axis=0).astype(BF16))
        o = _sink_softmax_av(_dot_nt(qa, k2) * ATT_SCALE + bias, sink_col, v2)
        for h in range(ATT_HEADS):
            o_ref[ts, h * HEAD_DIM:(h + 1) * HEAD_DIM] = o[h * T_DEC:(h + 1) * T_DEC].astype(BF16)


def _swa_sample(z, cache_k, cache_v, rb_flat, sinks):
    rows = SAMPLE_BT * T_DEC
    z_off = N_PROMPT // rows
    bucket = jnp.asarray(_sample_bucket_table())
    kcol = ATT_WIDTH // KV_WIDTH
    smem = pl.BlockSpec(memory_space=pltpu.SMEM)
    cache_spec = pl.BlockSpec((SAMPLE_BT, CACHE_ROWS, HEAD_DIM), lambda i: (i, 0, 0))
    cache_shape = jax.ShapeDtypeStruct((N_BATCH, CACHE_ROWS, HEAD_DIM), F32)
    return pl.pallas_call(
        _swa_sample_kernel,
        grid=(N_BATCH // SAMPLE_BT,),
        in_specs=[smem, smem,
                  pl.BlockSpec((ATT_HEADS * T_DEC, S_KEYS), lambda i: (0, 0)),
                  pl.BlockSpec((rows, ATT_WIDTH), lambda i: (i + z_off, 0)),
                  pl.BlockSpec((rows, KV_WIDTH), lambda i: (i + z_off, kcol)),
                  pl.BlockSpec((rows, KV_WIDTH), lambda i: (i + z_off, kcol + 1)),
                  cache_spec, cache_spec],
        out_specs=[pl.BlockSpec((rows, ATT_WIDTH), lambda i: (i, 0)), cache_spec, cache_spec],
        out_shape=[jax.ShapeDtypeStruct((N_SAMPLE, ATT_WIDTH), BF16), cache_shape, cache_shape],
        scratch_shapes=[pltpu.VMEM((ATT_HEADS * T_DEC, S_KEYS), F32)],
        compiler_params=_params(("arbitrary",)),
        name="swa_sample",
    )(rb_flat, sinks, bucket, z, z, z, cache_k, cache_v)


LANE_IG, LANE_LF, LANE_B = 0, M_HEADS, 2 * M_HEADS


def _gate_table(zg, gate_bias, seg_len):
    L = zg.shape[0]
    g = zg + gate_bias
    lane = _iota((L, LANES), 1)
    lf = _log_sigmoid(g)
    lf_only = jnp.where(jnp.logical_and(lane >= LANE_LF, lane < LANE_B), lf, 0.0)
    row = _iota((L, L), 0)
    col = _iota((L, L), 1)
    same_seg = (row // seg_len) == (col // seg_len)
    tril = jnp.where(jnp.logical_and(col <= row, same_seg), 1.0, 0.0).astype(BF16)
    cum = pltpu.roll(_dot_exact_lhs01(tril, lf_only), M_HEADS, axis=1)
    table = jnp.where(lane < LANE_LF, g, jnp.where(lane < LANE_B, lf, jnp.where(lane < LANE_B + M_HEADS, cum, 0.0)))
    return table, jnp.logical_and(col <= row, same_seg)


def _mlstm_decay(table, table_t, mask, m0_col, h):
    b_c = table[:, LANE_B + h:LANE_B + h + 1]
    b_r = table_t[LANE_B + h:LANE_B + h + 1, :]
    ig_r = table_t[LANE_IG + h:LANE_IG + h + 1, :]
    log_d = jnp.where(mask, b_c - b_r + ig_r, NEG_INF)
    log_inter = b_c + m0_col
    m_t = jnp.maximum(log_inter, jnp.max(log_d, axis=1, keepdims=True))
    return b_c, m_t, jnp.exp(log_inter - m_t), jnp.exp(log_d - m_t)


def _mlstm_intra(table, table_t, mask, m0_col, h, q, k, v):
    b_c, m_t, w_inter, d = _mlstm_decay(table, table_t, mask, m0_col, h)
    s = _dot_nt(q.astype(BF16), k.astype(BF16)) * d
    num_intra = _dot(s.astype(BF16), v.astype(BF16))
    den_intra = jnp.sum(s, axis=1, keepdims=True)
    return b_c, m_t, w_inter, num_intra, den_intra


ST_WIDTH = DV + LANES


def _mlstm_prompt_kernel(gb_ref, zg_ref, q_ref, k_ref, va_ref, vb_ref, oa_ref, ob_ref,
                         y_ref, c_out, n_out, m_out, st_scr, m_scr):
    step = pl.program_id(0)
    L = ML_CHUNK

    @pl.when(step == 0)
    def _():
        st_scr[...] = jnp.zeros_like(st_scr)
        m_scr[...] = jnp.zeros_like(m_scr)

    table, mask = _gate_table(zg_ref[...], gb_ref[...], L)
    table_t = table.T
    ones = jnp.ones((L, LANES), F32)
    half = M_HEADS // 2
    for h in range(M_HEADS):
        v_ref, o_ref = (va_ref, oa_ref) if h < half else (vb_ref, ob_ref)
        vs = slice((h % half) * DV, (h % half + 1) * DV)
        qb = (q_ref[:, h * DK:(h + 1) * DK] * Q_SCALE).astype(BF16)
        kb = k_ref[:, h * DK:(h + 1) * DK].astype(BF16)
        v1 = jnp.concatenate([v_ref[:, vs], ones], axis=1)
        m0 = m_scr[h:h + 1, 0:1]
        b_c, m_t, w_inter, d = _mlstm_decay(table, table_t, mask, m0, h)
        s = _dot_nt(qb, kb) * d
        st = st_scr[h]
        tot = _dot(s.astype(BF16), v1.astype(BF16)) + w_inter * _dot(qb, st.astype(BF16))
        den = tot[:, DV:]
        hh = tot[:, :DV] / jnp.maximum(jnp.abs(jnp.concatenate([den, den], axis=1)), jnp.exp(-m_t))
        y_ref[:, h * DV:(h + 1) * DV] = (_sigmoid(o_ref[:, vs]) * hh).astype(BF16)

        ig_c = table[:, LANE_IG + h:LANE_IG + h + 1]
        m_new = m_t[L - 1:L, :]
        b_last = b_c[L - 1:L, :]
        w_s = jnp.exp(b_last - b_c + ig_c - m_new)
        decay = jnp.exp(b_last + m0 - m_new)
        st_scr[h] = decay * st + _dot_tn(kb, (w_s * v1).astype(BF16))
        m_scr[h:h + 1, :] = jnp.broadcast_to(m_new, (1, LANES))

    @pl.when(step == pl.num_programs(0) - 1)
    def _():
        for h in range(M_HEADS):
            st_t = st_scr[h].T
            c_out[h] = st_t[:DV, :]
            n_out[h:h + 1, :] = st_t[DV:DV + 1, :]
        m_out[...] = m_scr[...]


def _mlstm_prompt(z, zg, gate_bias):
    L = ML_CHUNK
    blk = M_HEADS * DK
    col = lambda c: pl.BlockSpec((L, blk), lambda i: (i, c))
    const = lambda shape: pl.BlockSpec(shape, lambda i: tuple(0 for _ in shape))
    return pl.pallas_call(
        _mlstm_prompt_kernel,
        grid=(N_PROMPT // L,),
        in_specs=[const((1, LANES)),
                  pl.BlockSpec((L, LANES), lambda i: (i, 0)),
                  col(3), col(4), col(5), col(6), col(7), col(8)],
        out_specs=[pl.BlockSpec((L, M_WIDTH), lambda i: (i, 0)),
                   const((M_HEADS, DV, DK)), const((M_HEADS, DK)), const((M_HEADS, LANES))],
        out_shape=[jax.ShapeDtypeStruct((N_PROMPT, M_WIDTH), BF16),
                   jax.ShapeDtypeStruct((M_HEADS, DV, DK), F32),
                   jax.ShapeDtypeStruct((M_HEADS, DK), F32),
                   jax.ShapeDtypeStruct((M_HEADS, LANES), F32)],
        scratch_shapes=[pltpu.VMEM((M_HEADS, DK, ST_WIDTH), F32),
                        pltpu.VMEM((M_HEADS, LANES), F32)],
        compiler_params=_params(("arbitrary",)),
        name="mlstm_prompt",
    )(gate_bias, zg, z, z, z, z, z, z)


S_ROWS = SAMPLE_BT * T_DEC


def _mlstm_sample_kernel(gb_ref, zg_ref, m0_ref, n0t_ref, q_ref, k_ref, va_ref, vb_ref, oa_ref, ob_ref,
                         c0_ref, n0_ref, y_ref, c_out, n_out, m_out):
    L = S_ROWS
    table, mask = _gate_table(zg_ref[...], gb_ref[...], T_DEC)
    table_t = jnp.concatenate([table, jnp.zeros((LANES - L, LANES), F32)], axis=0).T[:, 0:L]
    m0_all = m0_ref[...]
    lane = _iota((L, LANES), 1)
    row_b = _iota((SAMPLE_BT, L), 1) // T_DEC
    seg_sum = jnp.where(row_b == _iota((SAMPLE_BT, L), 0), 1.0, 0.0)
    m_tok = jnp.zeros((L, LANES), F32)
    half = M_HEADS // 2
    for h in range(M_HEADS):
        v_ref, o_ref = (va_ref, oa_ref) if h < half else (vb_ref, ob_ref)
        vs = slice((h % half) * DV, (h % half + 1) * DV)
        q = q_ref[:, h * DK:(h + 1) * DK] * Q_SCALE
        k = k_ref[:, h * DK:(h + 1) * DK]
        v = v_ref[:, vs]
        m0 = m0_all[:, h:h + 1]
        b_c, m_t, w_inter, num_intra, den_intra = _mlstm_intra(table, table_t, mask, m0, h, q, k, v)
        qb = q.astype(BF16)
        num_inter = jnp.concatenate(
            [_dot_nt(qb[b * T_DEC:(b + 1) * T_DEC], c0_ref[b, h].astype(BF16)) for b in range(SAMPLE_BT)], axis=0)
        num = num_intra + w_inter * num_inter
        den = den_intra + w_inter * jnp.sum(q * n0t_ref[:, h * DK:(h + 1) * DK], axis=1, keepdims=True)
        hh = num / jnp.maximum(jnp.abs(den), jnp.exp(-m_t))
        y_ref[:, h * DV:(h + 1) * DV] = (_sigmoid(o_ref[:, vs]) * hh).astype(BF16)
        m_tok = jnp.where(lane == h, m_t, m_tok)

        def last_tok(x):
            x3 = x.reshape(SAMPLE_BT, T_DEC, 1)
            return jnp.broadcast_to(x3[:, T_DEC - 1:T_DEC, :], x3.shape).reshape(L, 1)

        ig_c = table[:, LANE_IG + h:LANE_IG + h + 1]
        m_new = last_tok(m_t)
        b_last = last_tok(b_c)
        w_s = jnp.exp(b_last - b_c + ig_c - m_new)
        decay = jnp.exp(b_last + m0 - m_new)
        wv = (w_s * v).astype(BF16)
        kb = k.astype(BF16)
        rowsel = _iota((L, 1), 0) // T_DEC
        for b in range(SAMPLE_BT):
            dec_b = decay[b * T_DEC + T_DEC - 1:(b + 1) * T_DEC, :]
            wv_b = jnp.where(rowsel == b, wv, jnp.zeros_like(wv))
            c_out[b, h] = dec_b * c0_ref[b, h] + _dot_tn(wv_b, kb)
        dec_rows = decay.reshape(SAMPLE_BT, T_DEC, 1)[:, T_DEC - 1, :]
        n_out[:, h * DK:(h + 1) * DK] = dec_rows * n0_ref[:, h * DK:(h + 1) * DK] + jnp.dot(
            seg_sum, w_s * k, preferred_element_type=F32, precision=lax.Precision.HIGHEST)
    m_out[...] = m_tok


def _mlstm_sample(z, zg, gate_bias, m0_tok, n0_tok, state_c, state_n):
    L = S_ROWS
    blk = M_HEADS * DK
    z_off = N_PROMPT // L
    col = lambda c: pl.BlockSpec((L, blk), lambda i: (i + z_off, c))
    c_spec = pl.BlockSpec((SAMPLE_BT, M_HEADS, DV, DK), lambda i: (i, 0, 0, 0))
    n_spec = pl.BlockSpec((SAMPLE_BT, blk), lambda i: (i, 0))
    return pl.pallas_call(
        _mlstm_sample_kernel,
        grid=(N_BATCH // SAMPLE_BT,),
        in_specs=[pl.BlockSpec((1, LANES), lambda i: (0, 0)),
                  pl.BlockSpec((L, LANES), lambda i: (i + z_off, 0)),
                  pl.BlockSpec((L, LANES), lambda i: (i, 0)),
                  pl.BlockSpec((L, blk), lambda i: (i, 0)),
                  col(3), col(4), col(5), col(6), col(7), col(8),
                  c_spec, n_spec],
        out_specs=[pl.BlockSpec((L, M_WIDTH), lambda i: (i, 0)), c_spec, n_spec,
                   pl.BlockSpec((L, LANES), lambda i: (i, 0))],
        out_shape=[jax.ShapeDtypeStruct((N_SAMPLE, M_WIDTH), BF16),
                   jax.ShapeDtypeStruct((N_BATCH, M_HEADS, DV, DK), F32),
                   jax.ShapeDtypeStruct((N_BATCH, blk), F32),
                   jax.ShapeDtypeStruct((N_SAMPLE, LANES), F32)],
        compiler_params=_params(("arbitrary",)),
        name="mlstm_sample",
    )(gate_bias, zg, m0_tok, n0_tok, z, z, z, z, z, z, state_c, state_n)


N_OUT_CHUNKS = D // TN_OUT


def _outproj_kernel(ya_ref, ym_ref, w_hbm, x_ref, gt_ref, o_ref, stage, w_bf, sem):
    j = pl.program_id(0)

    def fetch(chunk):
        cols = pl.ds(pl.multiple_of(chunk * TN_OUT, TN_OUT), TN_OUT)
        return pltpu.make_async_copy(w_hbm.at[:, cols], stage, sem.at[0])

    @pl.when(pl.program_id(1) == 0)
    def _():
        @pl.when(j == 0)
        def _():
            fetch(0).start()

        fetch(j).wait()
        w_bf[...] = stage[...].astype(BF16)

        @pl.when(j + 1 < N_OUT_CHUNKS)
        def _():
            fetch(j + 1).start()

    mix = _dot(ya_ref[...], w_bf[0:ATT_WIDTH, :]) + _dot(ym_ref[...], w_bf[ATT_WIDTH:, :])
    o_ref[...] = x_ref[...] + _mod_rows(gt_ref) * mix


def _outproj(ya, ym, w_out, x, mod, per_row):
    n = x.shape[0]
    TM = TM_PROJ
    gate1_col = 2 * N_OUT_CHUNKS
    if per_row:
        gate_spec = pl.BlockSpec((TM // T_DEC, 1, TN_OUT), lambda j, i: (i, 0, gate1_col + j))
    else:
        gate_spec = pl.BlockSpec((1, TN_OUT), lambda j, i: (0, gate1_col + j))
    return pl.pallas_call(
        _outproj_kernel,
        grid=(N_OUT_CHUNKS, n // TM),
        in_specs=[pl.BlockSpec((TM, ATT_WIDTH), lambda j, i: (i, 0)),
                  pl.BlockSpec((TM, M_WIDTH), lambda j, i: (i, 0)),
                  pl.BlockSpec(memory_space=pl.ANY),
                  pl.BlockSpec((TM, TN_OUT), lambda j, i: (i, j)),
                  gate_spec],
        out_specs=pl.BlockSpec((TM, TN_OUT), lambda j, i: (i, j)),
        out_shape=jax.ShapeDtypeStruct((n, D), F32),
        scratch_shapes=[pltpu.VMEM((D, TN_OUT), F32), pltpu.VMEM((D, TN_OUT), BF16), pltpu.SemaphoreType.DMA((1,))],
        compiler_params=_params(("arbitrary", "arbitrary")),
        name="outproj",
    )(ya, ym, w_out, x, mod)


def _router_kernel(x_ref, sh_ref, sc_ref, g_ref, wr_ref, br_ref, h_ref, route_ref):
    x = x_ref[...]
    y = x * lax.rsqrt(jnp.mean(x * x, axis=-1, keepdims=True) + EPS) * g_ref[...]
    h2 = y * (1.0 + _mod_rows(sc_ref)) + _mod_rows(sh_ref)
    h_hi = h2.astype(BF16)
    h_ref[...] = _pack_bf16_pairs(h2)
    h_lo = (h2 - h_hi.astype(F32)).astype(BF16)
    logits = _dot(h_hi, wr_ref[0]) + (_dot(h_hi, wr_ref[1]) + _dot(h_lo, wr_ref[0])) + br_ref[...]
    lane = _iota(logits.shape, 1)

    def first_max(vals):
        vmax = jnp.max(vals, axis=1, keepdims=True)
        idx = jnp.min(jnp.where(vals == vmax, lane, LANES), axis=1, keepdims=True)
        return vmax, idx

    gl = jnp.where(lane < N_GROUPS, logits, NEG_INF)
    gmax, grp = first_max(gl)
    p_grp = 1.0 / jnp.sum(jnp.exp(gl - gmax), axis=1, keepdims=True)
    e_lane = lane - N_GROUPS
    in_grp = jnp.logical_and(e_lane >= 0, jnp.logical_and(e_lane < N_EXPERTS, e_lane // EXP_PER_GROUP == grp))
    el = jnp.where(in_grp, logits, NEG_INF)
    v1, i1 = first_max(el)
    v2, i2 = first_max(jnp.where(lane == i1, NEG_INF, el))
    e2w = jnp.exp(v2 - v1)
    w1 = 1.0 / (1.0 + e2w)
    w2 = e2w / (1.0 + e2w)
    route = jnp.where(lane == 0, (i1 - N_GROUPS).astype(F32),
                      jnp.where(lane == 1, (i2 - N_GROUPS).astype(F32),
                                jnp.where(lane == 2, p_grp * w1, jnp.where(lane == 3, p_grp * w2, 0.0))))
    route_ref[...] = route


def _router_merged_kernel(xp_ref, xs_ref, shp_ref, scp_ref, shs_ref, scs_ref, g_ref, wr_ref, br_ref,
                          h_ref, route_ref):
    i = pl.program_id(0)

    @pl.when(i < PROMPT_TOK_BLOCKS)
    def _():
        _router_kernel(xp_ref, shp_ref, scp_ref, g_ref, wr_ref, br_ref, h_ref, route_ref)

    @pl.when(i >= PROMPT_TOK_BLOCKS)
    def _():
        _router_kernel(xs_ref, shs_ref, scs_ref, g_ref, wr_ref, br_ref, h_ref, route_ref)


def _router(x1_p, x1_s, mod_p, mod_s, g_ffn, w_route, b_route):
    pi, si = _prompt_block, _sample_block
    return pl.pallas_call(
        _router_merged_kernel,
        grid=(N_TOK // TM_TOK,),
        in_specs=[pl.BlockSpec((TM_TOK, D), lambda i: (pi(i), 0)),
                  pl.BlockSpec((TM_TOK, D), lambda i: (si(i), 0)),
                  pl.BlockSpec((1, D), lambda i: (0, 3)),
                  pl.BlockSpec((1, D), lambda i: (0, 4)),
                  pl.BlockSpec((TM_TOK // T_DEC, 1, D), lambda i: (si(i), 0, 3)),
                  pl.BlockSpec((TM_TOK // T_DEC, 1, D), lambda i: (si(i), 0, 4)),
                  pl.BlockSpec((1, D), lambda i: (0, 0)),
                  pl.BlockSpec((2, D, LANES), lambda i: (0, 0, 0)),
                  pl.BlockSpec((1, LANES), lambda i: (0, 0))],
        out_specs=[pl.BlockSpec((TM_TOK, D_PACK), lambda i: (i, 0)),
                   pl.BlockSpec((TM_TOK, LANES), lambda i: (i, 0))],
        out_shape=[jax.ShapeDtypeStruct((N_TOK, D_PACK), jnp.uint32),
                   jax.ShapeDtypeStruct((N_TOK, LANES), F32)],
        compiler_params=_params(("arbitrary",)),
        name="router",
    )(x1_p, x1_s, mod_p, mod_p, mod_s, mod_s, g_ffn, w_route, b_route)


def _rank_kernel(route_ref, dest_ref, cnt_ref, carry, rank_scr):
    pas = pl.program_id(0)
    step = pl.program_id(1)

    @pl.when(jnp.logical_and(pas == 0, step == 0))
    def _():
        carry[...] = jnp.zeros_like(carry)

    route = route_ref[...]
    n = route.shape[0]
    lane = _iota((n, LANES), 1).astype(F32)
    o1 = jnp.where(lane == route[:, 0:1], 1.0, 0.0)
    o2 = jnp.where(lane == route[:, 1:2], 1.0, 0.0)
    lane_i = _iota((n, LANES), 1)
    rows = pl.ds(pl.multiple_of(step * TM_RANK, TM_RANK), TM_RANK)

    @pl.when(pas == 0)
    def _():
        both = o1 + o2
        strict = jnp.where(_iota((n, n), 1) < _iota((n, n), 0), 1.0, 0.0).astype(BF16)
        prior = _dot(strict, both.astype(BF16)) + carry[0:1, :]
        r1 = jnp.sum(o1 * prior, axis=1, keepdims=True)
        r2 = jnp.sum(o2 * prior, axis=1, keepdims=True)
        rank_scr[rows, :] = jnp.where(lane_i == 0, r1, jnp.where(lane_i == 1, r2, 0.0))
        carry[...] = carry[...] + jnp.sum(both, axis=0, keepdims=True)

    @pl.when(pas == 1)
    def _():
        counts = carry[...]
        tiles_per = jnp.floor((counts + (TM_MOE - 1)) * (1.0 / TM_MOE))
        before = jnp.where(_iota((LANES, LANES), 0) < _iota((LANES, LANES), 1), 1.0, 0.0).astype(BF16)
        pad_start = _dot(tiles_per.astype(BF16), before)[0:1, :] * TM_MOE
        rank = rank_scr[rows, :]
        d1 = jnp.sum(o1 * pad_start, axis=1, keepdims=True) + rank[:, 0:1]
        d2 = jnp.sum(o2 * pad_start, axis=1, keepdims=True) + rank[:, 1:2]
        dest_ref[...] = jnp.where(lane_i == 0, d1, jnp.where(lane_i == 1, d2, 0.0)).astype(jnp.int32)
        cnt_ref[...] = counts


def _rank(route):
    n = route.shape[0]
    return pl.pallas_call(
        _rank_kernel,
        grid=(2, n // TM_RANK),
        in_specs=[pl.BlockSpec((TM_RANK, LANES), lambda p, i: (i, 0))],
        out_specs=[pl.BlockSpec((TM_RANK, LANES), lambda p, i: (i * p, 0)),
                   pl.BlockSpec((SUBLANES, LANES), lambda p, i: (0, 0))],
        out_shape=[jax.ShapeDtypeStruct((n, LANES), jnp.int32), jax.ShapeDtypeStruct((SUBLANES, LANES), F32)],
        scratch_shapes=[pltpu.VMEM((SUBLANES, LANES), F32), pltpu.VMEM((n, LANES), F32)],
        compiler_params=_params(("arbitrary", "arbitrary")),
        name="rank",
    )(route)


def _row_copy(src, dst, sem):
    return pltpu.make_async_copy(src, dst, sem)


ROW_UNROLL = 8
assert T_DEC == SUBLANES
D_PACK = D // 2

ZERO_FIRST, ZERO_ANYTIME = 1, 2


def _dispatch_kernel(d1_ref, d2_ref, zc_ref, h_ref, xs_out, zbuf, sem):
    step = pl.program_id(0)
    base = step * TM_TOK

    def zero_tile(t, s):
        rows = pl.ds(pl.multiple_of(t * TM_MOE, TM_MOE), TM_MOE)
        return pltpu.make_async_copy(zbuf, xs_out.at[rows], sem.at[s])

    def for_tiles(cls, s, act):
        def body(t, carry):
            @pl.when(zc_ref[t] == cls)
            def _():
                act(zero_tile(t, s))
            return carry
        lax.fori_loop(0, MAX_TILES, body, 0)

    @pl.when(step == 0)
    def _():
        zbuf[...] = jnp.zeros_like(zbuf)
        for_tiles(ZERO_FIRST, 1, lambda c: c.start())
        for_tiles(ZERO_ANYTIME, 2, lambda c: c.start())
        for_tiles(ZERO_FIRST, 1, lambda c: c.wait())

    @pl.when(step == pl.num_programs(0) - 1)
    def _():
        for_tiles(ZERO_ANYTIME, 2, lambda c: c.wait())

    def issue(r, carry):
        _row_copy(h_ref.at[pl.ds(r, 1)], xs_out.at[pl.ds(d1_ref[base + r], 1)], sem.at[0]).start()
        _row_copy(h_ref.at[pl.ds(r, 1)], xs_out.at[pl.ds(d2_ref[base + r], 1)], sem.at[0]).start()
        return carry

    lax.fori_loop(0, TM_TOK, issue, 0, unroll=ROW_UNROLL)
    for _ in range(TOP_K):
        _row_copy(h_ref, xs_out.at[pl.ds(0, TM_TOK)], sem.at[0]).wait()


def _dispatch(dests, zero_class, h2):
    grid_spec = pltpu.PrefetchScalarGridSpec(
        num_scalar_prefetch=3,
        grid=(N_TOK // TM_TOK,),
        in_specs=[pl.BlockSpec((TM_TOK, D_PACK), lambda i, *_: (i, 0))],
        out_specs=pl.BlockSpec(memory_space=pl.ANY),
        scratch_shapes=[pltpu.VMEM((TM_MOE, D_PACK), jnp.uint32), pltpu.SemaphoreType.DMA((3,))],
    )
    return pl.pallas_call(
        _dispatch_kernel,
        grid_spec=grid_spec,
        out_shape=jax.ShapeDtypeStruct((A_PAD, D_PACK), jnp.uint32),
        compiler_params=_params(("arbitrary",)),
        name="dispatch",
    )(*dests, zero_class, h2)


N_UP_CHUNKS = D_FF // TN_FF
ROW_CAPS = tuple(range(TM_MOE // 4, TM_MOE + 1, TM_MOE // 4))


def _for_row_cap(rows, body):
    lo = 0
    for cap in ROW_CAPS:
        pl.when(jnp.logical_and(rows > lo, rows <= cap))(functools.partial(body, cap))
        lo = cap


def _expert_up_kernel(te_ref, rows_ref, first_ref, nxt_ref, run_ref, meta_ref, x_ref, wg_hbm, wu_hbm,
                      o_ref, wbuf, sem):
    n = pl.program_id(0)
    t = pl.program_id(1)
    slot = lax.rem(n * meta_ref[1] + run_ref[t], 2)

    def fetch(e, chunk, s):
        cols = pl.ds(pl.multiple_of(chunk * TN_FF, TN_FF), TN_FF)
        return (pltpu.make_async_copy(wg_hbm.at[e, :, cols], wbuf.at[s, 0], sem.at[s, 0]),
                pltpu.make_async_copy(wu_hbm.at[e, :, cols], wbuf.at[s, 1], sem.at[s, 1]))

    def start(copies):
        for priority, c in enumerate(copies):
            c.start(priority=priority)

    @pl.when(first_ref[t] == 1)
    def _():
        @pl.when(jnp.logical_and(n == 0, t == 0))
        def _():
            start(fetch(te_ref[0], 0, 0))

        for c in fetch(te_ref[t], n, slot):
            c.wait()

        @pl.when(nxt_ref[t] >= 0)
        def _():
            start(fetch(nxt_ref[t], n, 1 - slot))

        @pl.when(jnp.logical_and(nxt_ref[t] < 0, n + 1 < N_UP_CHUNKS))
        def _():
            start(fetch(te_ref[0], n + 1, 1 - slot))

    def compute(cap):
        x = jnp.concatenate([half.astype(BF16) for half in _unpack_bf16_pairs(x_ref[0:cap, :])], axis=1)
        g = _dot(x, wbuf[slot, 0].astype(BF16))
        u = _dot(x, wbuf[slot, 1].astype(BF16))
        o_ref[0:cap, :] = (g * _sigmoid(g) * u).astype(BF16)
        if cap < TM_MOE:
            o_ref[cap:, :] = jnp.zeros((TM_MOE - cap, TN_FF), BF16)

    _for_row_cap(rows_ref[t], compute)

    @pl.when(rows_ref[t] == 0)
    def _():
        o_ref[...] = jnp.zeros_like(o_ref)


def _expert_up(tiles, xs, w_gate, w_up):
    last = lambda t, meta: jnp.minimum(t, meta[0] - 1)
    grid_spec = pltpu.PrefetchScalarGridSpec(
        num_scalar_prefetch=6,
        grid=(N_UP_CHUNKS, MAX_TILES),
        in_specs=[pl.BlockSpec((TM_MOE, D_PACK), lambda n, t, *s: (last(t, s[5]), 0)),
                  pl.BlockSpec(memory_space=pl.ANY),
                  pl.BlockSpec(memory_space=pl.ANY)],
        out_specs=pl.BlockSpec((TM_MOE, TN_FF), lambda n, t, *s: (t, n)),
        scratch_shapes=[pltpu.VMEM((2, 2, D, TN_FF), F32), pltpu.SemaphoreType.DMA((2, 2))],
    )
    return pl.pallas_call(
        _expert_up_kernel,
        grid_spec=grid_spec,
        out_shape=jax.ShapeDtypeStruct((A_PAD, D_FF), BF16),
        compiler_params=_params(("arbitrary", "arbitrary")),
        name="expert_up",
    )(*tiles, xs, w_gate, w_up)


def _expert_down_kernel(te_ref, rows_ref, first_ref, nxt_ref, run_ref, meta_ref, h_ref, wd_hbm,
                        o_ref, wbuf, sem):
    t = pl.program_id(0)
    slot = lax.rem(run_ref[t], 2)

    def fetch(e, s):
        halves = [pl.ds(k * (D_FF // 2), D_FF // 2) for k in range(2)]
        return [pltpu.make_async_copy(wd_hbm.at[e, rows], wbuf.at[s, rows], sem.at[s, k])
                for k, rows in enumerate(halves)]

    def start(copies):
        for priority, c in enumerate(copies):
            c.start(priority=priority)

    @pl.when(first_ref[t] == 1)
    def _():
        @pl.when(t == 0)
        def _():
            start(fetch(te_ref[0], 0))

        for c in fetch(te_ref[t], slot):
            c.wait()

        @pl.when(nxt_ref[t] >= 0)
        def _():
            start(fetch(nxt_ref[t], 1 - slot))

    def compute(cap):
        o_ref[0:cap, :] = _pack_bf16_pairs(_dot(h_ref[0:cap, :], wbuf[slot].astype(BF16)))
        if cap < TM_MOE:
            o_ref[cap:, :] = jnp.zeros((TM_MOE - cap, D_PACK), jnp.uint32)

    _for_row_cap(rows_ref[t], compute)

    @pl.when(rows_ref[t] == 0)
    def _():
        o_ref[...] = jnp.zeros_like(o_ref)


def _expert_down(tiles, h1, w_down):
    last = lambda t, meta: jnp.minimum(t, meta[0] - 1)
    grid_spec = pltpu.PrefetchScalarGridSpec(
        num_scalar_prefetch=6,
        grid=(MAX_TILES,),
        in_specs=[pl.BlockSpec((TM_MOE, D_FF), lambda t, *s: (last(t, s[5]), 0)),
                  pl.BlockSpec(memory_space=pl.ANY)],
        out_specs=pl.BlockSpec((TM_MOE, D_PACK), lambda t, *s: (t, 0)),
        scratch_shapes=[pltpu.VMEM((2, D_FF, D), F32), pltpu.SemaphoreType.DMA((2, 2))],
    )
    return pl.pallas_call(
        _expert_down_kernel,
        grid_spec=grid_spec,
        out_shape=jax.ShapeDtypeStruct((A_PAD, D_PACK), jnp.uint32),
        compiler_params=_params(("arbitrary",)),
        name="expert_down",
    )(*tiles, h1, w_down)


def _combine_kernel(d1_ref, d2_ref, x_ref, route_ref, gt_ref, gf_ref, o_hbm,
                    y_ref, buf, sem, *, tok_offset):
    step = pl.program_id(0)
    slot = lax.rem(step, 2)

    def gather(s, into):
        base = tok_offset + s * TM_TOK

        def issue(r, carry):
            _row_copy(o_hbm.at[pl.ds(d1_ref[base + r], 1)], buf.at[into, 0, pl.ds(r, 1)], sem.at[into]).start()
            _row_copy(o_hbm.at[pl.ds(d2_ref[base + r], 1)], buf.at[into, 1, pl.ds(r, 1)], sem.at[into]).start()
            return carry

        lax.fori_loop(0, TM_TOK, issue, 0, unroll=ROW_UNROLL)

    @pl.when(step == 0)
    def _():
        gather(0, 0)

    @pl.when(step + 1 < pl.num_programs(0))
    def _():
        gather(step + 1, 1 - slot)

    for k in range(TOP_K):
        _row_copy(o_hbm.at[pl.ds(0, TM_TOK)], buf.at[slot, k], sem.at[slot]).wait()

    route = route_ref[...]
    lo1, hi1 = _unpack_bf16_pairs(buf[slot, 0])
    lo2, hi2 = _unpack_bf16_pairs(buf[slot, 1])
    g1, g2 = route[:, 2:3], route[:, 3:4]
    moe = jnp.concatenate([g1 * lo1 + g2 * lo2, g1 * hi1 + g2 * hi2], axis=1)
    x2 = x_ref[...] + _mod_rows(gt_ref) * moe
    y_ref[...] = x2 * lax.rsqrt(jnp.mean(x2 * x2, axis=-1, keepdims=True) + EPS) * gf_ref[...]


def _combine(dests, x1, route, mod, per_row, g_final, o_rows, tok_offset):
    n = x1.shape[0]
    off = tok_offset // TM_TOK
    grid_spec = pltpu.PrefetchScalarGridSpec(
        num_scalar_prefetch=2,
        grid=(n // TM_TOK,),
        in_specs=[pl.BlockSpec((TM_TOK, D), lambda i, *_: (i, 0)),
                  pl.BlockSpec((TM_TOK, LANES), lambda i, *_: (i + off, 0)),
                  _mod_spec(per_row, TM_TOK, D, lambda *_: 5),
                  pl.BlockSpec((1, D), lambda i, *_: (0, 0)),
                  pl.BlockSpec(memory_space=pl.ANY)],
        out_specs=pl.BlockSpec((TM_TOK, D), lambda i, *_: (i, 0)),
        scratch_shapes=[pltpu.VMEM((2, TOP_K, TM_TOK, D_PACK), jnp.uint32), pltpu.SemaphoreType.DMA((2,))],
    )
    return pl.pallas_call(
        functools.partial(_combine_kernel, tok_offset=tok_offset),
        grid_spec=grid_spec,
        out_shape=jax.ShapeDtypeStruct((n, D), F32),
        compiler_params=_params(("arbitrary",)),
        name="combine",
    )(*dests, x1, route, mod, g_final, o_rows)


def kernel(x_prompt, x_sample, cache_k, cache_v, state_C, state_n, state_m, c_prompt, c_sample, rel_bias, w_ada, b_ada, g_mix, g_ffn, w_in, sinks, b_igate, b_fgate, w_out, w_router_grp, b_router_grp, w_router_exp, b_router_exp, w_gate, w_up, w_down, g_final):
    xp = x_prompt.reshape(N_PROMPT, D)
    xs = x_sample.reshape(N_SAMPLE, D)

    c_all = jnp.concatenate([c_sample, c_prompt, jnp.zeros((C_ROWS - 1 - N_BATCH, D), F32)], axis=0)
    mod_s = _ada(c_all, w_ada[0], b_ada)
    mod_p = mod_s[N_BATCH]

    w_in_t = jnp.swapaxes(w_in[0], 0, 1)
    w_gates_t = jnp.pad(w_in_t[Z_WIDTH:].astype(BF16), ((0, LANES - 2 * M_HEADS), (0, 0)))
    h_all, zg = _norm(xp, xs, mod_p, mod_s, g_mix, w_gates_t)
    z = _inproj(h_all, w_in_t)

    rb_flat = rel_bias.reshape(NUM_BUCKETS * ATT_HEADS)
    sink_v = sinks[0]
    ya_p = _swa_prompt(z, rb_flat, sink_v)
    ya_s, nk_s, nv_s = _swa_sample(z,cache_k.reshape(N_BATCH, CACHE_ROWS, HEAD_DIM),
                                   cache_v.reshape(N_BATCH, CACHE_ROWS, HEAD_DIM), rb_flat, sink_v)

    gate_bias = jnp.concatenate([b_igate[0], b_fgate[0], jnp.zeros((LANES - 2 * M_HEADS,), F32)]).reshape(1, LANES)
    ym_p, c_p, n_p, m_p = _mlstm_prompt(z, zg, gate_bias)
    m0_tok = jnp.pad(jnp.repeat(state_m[0], T_DEC, axis=0), ((0, 0), (0, LANES - M_HEADS)))
    n0_flat = state_n[0].reshape(N_BATCH, M_HEADS * DK)
    n0_tok = jnp.repeat(n0_flat, T_DEC, axis=0)
    ym_s, c_s, n_s, m_s = _mlstm_sample(z, zg, gate_bias, m0_tok, n0_tok, state_C[0], n0_flat)

    x1_p = _outproj(ya_p, ym_p, w_out[0], xp, mod_p, False)
    x1_s = _outproj(ya_s, ym_s, w_out[0], xs, mod_s, True)

    w_route = jnp.pad(jnp.concatenate([w_router_grp[0], w_router_exp[0]], axis=1),
                      ((0, 0), (0, LANES - N_GROUPS - N_EXPERTS)))
    b_route = jnp.pad(jnp.concatenate([b_router_grp[0], b_router_exp[0]]),
                      (0, LANES - N_GROUPS - N_EXPERTS)).reshape(1, LANES)
    w_route_hi = w_route.astype(BF16)
    w_route_split = jnp.stack([w_route_hi, (w_route - w_route_hi.astype(F32)).astype(BF16)])
    h2, route = _router(x1_p, x1_s, mod_p, mod_s, g_ffn, w_route_split, b_route)

    dest, counts = _rank(route)
    dests = (dest[:, 0], dest[:, 1])

    i32 = lambda a: a.astype(jnp.int32)
    cnt = i32(counts[0, :N_EXPERTS])
    tiles_per = (cnt + TM_MOE - 1) // TM_MOE
    tile_end = jnp.cumsum(tiles_per)
    n_tiles = tile_end[-1]
    tile_ids = jnp.arange(MAX_TILES, dtype=jnp.int32)
    tile_expert = i32(jnp.minimum(jnp.searchsorted(tile_end, tile_ids, side="right"), N_EXPERTS - 1))
    last_expert = tile_expert[jnp.maximum(n_tiles - 1, 0)]
    tile_valid = tile_ids < n_tiles
    tile_expert = jnp.where(tile_valid, tile_expert, last_expert)
    prev_expert = jnp.concatenate([jnp.full((1,), -1, jnp.int32), tile_expert[:-1]])
    run_first = jnp.logical_and(tile_valid, tile_expert != prev_expert)
    run_id = jnp.maximum(jnp.cumsum(i32(run_first)) - 1, 0)
    expert_ids = jnp.arange(N_EXPERTS, dtype=jnp.int32)
    used = jnp.where(tiles_per > 0, expert_ids, N_EXPERTS)
    next_used = jnp.concatenate([lax.cummin(used[::-1])[::-1][1:], jnp.full((1,), N_EXPERTS, jnp.int32)])
    next_used = jnp.where(next_used >= N_EXPERTS, -1, next_used)
    first_tile = tile_end - tiles_per
    tile_rows = jnp.clip(cnt[tile_expert] - (tile_ids - first_tile[tile_expert]) * TM_MOE, 0, TM_MOE)
    tile_rows = jnp.where(tile_valid, tile_rows, 0)
    tiles = (tile_expert, i32(tile_rows), i32(run_first), next_used[tile_expert], i32(run_id),
             jnp.stack([n_tiles, jnp.sum(i32(run_first))]).astype(jnp.int32))
    next_expert = jnp.concatenate([tile_expert[1:], jnp.full((1,), -1, jnp.int32)])
    run_last = jnp.logical_or(tile_expert != next_expert, tile_ids == n_tiles - 1)
    zero_class = jnp.where(tile_valid, jnp.where(run_last, ZERO_FIRST, 0), ZERO_ANYTIME)
    xs_rows = _dispatch(dests, i32(zero_class), h2)
    h1 = _expert_up(tiles, xs_rows, w_gate[0], w_up[0])
    o_rows = _expert_down(tiles, h1, w_down[0])

    gf = g_final.reshape(1, D)
    y_p = _combine(dests, x1_p, route, mod_p, False, gf, o_rows, 0)
    y_s = _combine(dests, x1_s, route, mod_s, True, gf, o_rows, N_PROMPT)

    kv5 = lambda a: a.reshape(1, -1, WINDOW, KV_HEADS, HEAD_DIM)
    kcol = ATT_WIDTH
    nk_p = z[N_PROMPT - WINDOW:N_PROMPT, kcol:kcol + KV_WIDTH]
    nv_p = z[N_PROMPT - WINDOW:N_PROMPT, kcol + KV_WIDTH:kcol + 2 * KV_WIDTH]
    return (y_p.reshape(1, N_PROMPT, D), y_s.reshape(N_BATCH, T_DEC, D),
            kv5(nk_p), kv5(nv_p),
            c_p.reshape(1, 1, M_HEADS, DV, DK), n_p.reshape(1, 1, M_HEADS, DK), m_p[:, 0].reshape(1, 1, M_HEADS),
            kv5(nk_s), kv5(nv_s),
            c_s.reshape(1, N_BATCH, M_HEADS, DV, DK), n_s.reshape(1, N_BATCH, M_HEADS, DK),
            m_s.reshape(N_BATCH, T_DEC, LANES)[:, T_DEC - 1, :M_HEADS].reshape(1, N_BATCH, M_HEADS))
```

```python
import functools
import math

import numpy as np
import jax
import jax.numpy as jnp
from jax import lax
from jax.experimental import pallas as pl
from jax.experimental.pallas import tpu as pltpu

F32 = jnp.float32
BF16 = jnp.bfloat16
NEG_INF = float("-inf")

D = 4096
N_PROMPT = 8192
N_BATCH = 128
T_DEC = 8
N_SAMPLE = N_BATCH * T_DEC
N_TOK = N_PROMPT + N_SAMPLE
HEAD_DIM = 128
ATT_HEADS = 16
KV_HEADS = 4
GROUP = ATT_HEADS // KV_HEADS
WINDOW = 128
ATT_WIDTH = ATT_HEADS * HEAD_DIM
KV_WIDTH = KV_HEADS * HEAD_DIM
NUM_BUCKETS = 32
MAX_EXACT = 16
MAX_DISTANCE = 128
M_HEADS = 8
DK = 128
DV = 256
M_WIDTH = M_HEADS * DV
Z_WIDTH = ATT_WIDTH + 2 * KV_WIDTH + 2 * M_HEADS * DK + 2 * M_WIDTH
N_GROUPS = 4
EXP_PER_GROUP = 8
N_EXPERTS = N_GROUPS * EXP_PER_GROUP
TOP_K = 2
D_FF = 1024
EPS = 1e-6
ATT_SCALE = HEAD_DIM ** -0.5
Q_SCALE = DK ** -0.5

LANES = 128
SUBLANES = 8
VMEM_LIMIT = 56 * 1024 * 1024

TM_PROJ = 512
TM_IN = 1024
TN_IN = 1024
TN_OUT = 1024
TN_ADA = 512
C_ROWS = 136
ATT_BLOCK = 128
SAMPLE_BT = 8
ML_CHUNK = 256
TM_MOE = 256
TN_FF = 512
N_ASSIGN = N_TOK * TOP_K
MAX_TILES = N_ASSIGN // TM_MOE + N_EXPERTS
A_PAD = MAX_TILES * TM_MOE
TM_TOK = 256
TM_RANK = 1024


def _params(sem):
    return pltpu.CompilerParams(dimension_semantics=sem, vmem_limit_bytes=VMEM_LIMIT)


def _iota(shape, dim):
    return lax.broadcasted_iota(jnp.int32, shape, dim)


def _dot(a, b):
    return jnp.dot(a, b, preferred_element_type=F32)


def _dot_nt(a, b):
    return lax.dot_general(a, b, (((1,), (1,)), ((), ())), preferred_element_type=F32)


def _dot_tn(a, b):
    return lax.dot_general(a, b, (((0,), (0,)), ((), ())), preferred_element_type=F32)


def _split3(x):
    x1 = x.astype(BF16)
    r1 = x - x1.astype(F32)
    x2 = r1.astype(BF16)
    r2 = r1 - x2.astype(F32)
    return x1, x2, r2.astype(BF16)


def _dot_exact_lhs01(a01, x):
    x1, x2, x3 = _split3(x)
    return _dot(a01, x1) + _dot(a01, x2) + _dot(a01, x3)


def _pack_bf16_pairs(x):
    w = x.shape[1] // 2
    bits = lax.bitcast_convert_type(x.astype(BF16).astype(F32), jnp.uint32)
    return (bits[:, :w] >> 16) | (bits[:, w:] & jnp.uint32(0xFFFF0000))


def _unpack_bf16_pairs(words):
    return (lax.bitcast_convert_type(words << 16, F32),
            lax.bitcast_convert_type(words & jnp.uint32(0xFFFF0000), F32))


def _sigmoid(x):
    return 1.0 / (1.0 + jnp.exp(-x))


def _log_sigmoid(x):
    return jnp.minimum(x, 0.0) - jnp.log(1.0 + jnp.exp(-jnp.abs(x)))


def _mod_spec(per_batch, rows, width, col):
    if per_batch:
        return pl.BlockSpec((rows // T_DEC, 1, width), lambda i, *rest: (i, 0, col(*rest)))
    return pl.BlockSpec((1, width), lambda i, *rest: (0, col(*rest)))


def _mod_rows(ref):
    v = ref[...]
    if v.ndim == 2:
        return v
    nb, _, width = v.shape
    return jnp.broadcast_to(v, (nb, T_DEC, width)).reshape(nb * T_DEC, width)


def _ada_kernel(c_ref, w_ref, b_ref, o_ref):
    c = c_ref[...]
    s = (c * _sigmoid(c)).astype(BF16)
    mod = _dot(s, w_ref[...].astype(BF16)) + b_ref[...]
    o_ref[...] = mod.reshape(C_ROWS, 1, TN_ADA)


def _ada(c_all, w_ada, b_ada):
    n = w_ada.shape[1]
    return pl.pallas_call(
        _ada_kernel,
        grid=(n // TN_ADA,),
        in_specs=[pl.BlockSpec((C_ROWS, D), lambda j: (0, 0)),
                  pl.BlockSpec((D, TN_ADA), lambda j: (0, j)),
                  pl.BlockSpec((1, TN_ADA), lambda j: (0, j))],
        out_specs=pl.BlockSpec((C_ROWS, 1, TN_ADA), lambda j: (0, 0, j)),
        out_shape=jax.ShapeDtypeStruct((C_ROWS, 1, n), F32),
        compiler_params=_params(("arbitrary",)),
        name="ada",
    )(c_all, w_ada, b_ada)


PROMPT_TOK_BLOCKS = N_PROMPT // TM_TOK


def _prompt_block(i):
    return jnp.minimum(i, PROMPT_TOK_BLOCKS - 1)


def _sample_block(i):
    return jnp.maximum(i - PROMPT_TOK_BLOCKS, 0)


def _norm_body(x_ref, sh_ref, sc_ref, g_ref, wg_ref, h_ref, zg_ref):
    x = x_ref[...]
    y = x * lax.rsqrt(jnp.mean(x * x, axis=-1, keepdims=True) + EPS) * g_ref[...]
    hb = (y * (1.0 + _mod_rows(sc_ref)) + _mod_rows(sh_ref)).astype(BF16)
    h_ref[...] = hb
    zg_ref[...] = _dot_nt(hb, wg_ref[...])


def _norm_kernel(xp_ref, xs_ref, shp_ref, scp_ref, shs_ref, scs_ref, g_ref, wg_ref, h_ref, zg_ref):
    i = pl.program_id(0)

    @pl.when(i < PROMPT_TOK_BLOCKS)
    def _():
        _norm_body(xp_ref, shp_ref, scp_ref, g_ref, wg_ref, h_ref, zg_ref)

    @pl.when(i >= PROMPT_TOK_BLOCKS)
    def _():
        _norm_body(xs_ref, shs_ref, scs_ref, g_ref, wg_ref, h_ref, zg_ref)


def _norm(xp, xs, mod_p, mod_s, g_mix, w_gate):
    per_batch = lambda col: pl.BlockSpec((TM_TOK // T_DEC, 1, D), lambda i: (_sample_block(i), 0, col))
    return pl.pallas_call(
        _norm_kernel,
        grid=(N_TOK // TM_TOK,),
        in_specs=[pl.BlockSpec((TM_TOK, D), lambda i: (_prompt_block(i), 0)),
                  pl.BlockSpec((TM_TOK, D), lambda i: (_sample_block(i), 0)),
                  pl.BlockSpec((1, D), lambda i: (0, 0)),
                  pl.BlockSpec((1, D), lambda i: (0, 1)),
                  per_batch(0), per_batch(1),
                  pl.BlockSpec((1, D), lambda i: (0, 0)),
                  pl.BlockSpec((LANES, D), lambda i: (0, 0))],
        out_specs=[pl.BlockSpec((TM_TOK, D), lambda i: (i, 0)),
                   pl.BlockSpec((TM_TOK, LANES), lambda i: (i, 0))],
        out_shape=[jax.ShapeDtypeStruct((N_TOK, D), BF16), jax.ShapeDtypeStruct((N_TOK, LANES), F32)],
        compiler_params=_params(("arbitrary",)),
        name="norm",
    )(xp, xs, mod_p, mod_p, mod_s, mod_s, g_mix, w_gate)


N_IN_CHUNKS = Z_WIDTH // TN_IN


def _inproj_kernel(h_ref, wt_hbm, z_ref, stage, w_bf, sem):
    j = pl.program_id(0)

    def fetch(chunk):
        rows = pl.ds(pl.multiple_of(chunk * TN_IN, TN_IN), TN_IN)
        return pltpu.make_async_copy(wt_hbm.at[rows], stage, sem.at[0])

    @pl.when(pl.program_id(1) == 0)
    def _():
        @pl.when(j == 0)
        def _():
            fetch(0).start()

        fetch(j).wait()
        w_bf[...] = stage[...].astype(BF16)

        @pl.when(j + 1 < N_IN_CHUNKS)
        def _():
            fetch(j + 1).start()

    z_ref[...] = _dot_nt(h_ref[...], w_bf[...])


def _inproj(h, w_in_t):
    return pl.pallas_call(
        _inproj_kernel,
        grid=(N_IN_CHUNKS, N_TOK // TM_IN),
        in_specs=[pl.BlockSpec((TM_IN, D), lambda j, i: (i, 0)),
                  pl.BlockSpec(memory_space=pl.ANY)],
        out_specs=pl.BlockSpec((TM_IN, TN_IN), lambda j, i: (i, j)),
        out_shape=jax.ShapeDtypeStruct((N_TOK, Z_WIDTH), F32),
        scratch_shapes=[pltpu.VMEM((TN_IN, D), F32), pltpu.VMEM((TN_IN, D), BF16), pltpu.SemaphoreType.DMA((1,))],
        compiler_params=_params(("arbitrary", "arbitrary")),
        name="inproj",
    )(h, w_in_t)


def _t5_bucket_np(dist):
    n = np.maximum(dist, 0)
    nf = np.maximum(n, 1).astype(np.float32)
    large = MAX_EXACT + (np.log(nf / MAX_EXACT) / math.log(MAX_DISTANCE / MAX_EXACT)
                         * (NUM_BUCKETS - MAX_EXACT)).astype(np.int32)
    large = np.minimum(large, NUM_BUCKETS - 1)
    return np.where(n < MAX_EXACT, n, large).astype(np.int32)


def _bucket_table(n_q, n_keys_valid, n_keys_padded):
    t = np.arange(n_q)[:, None]
    j = np.arange(n_keys_padded)[None, :]
    dist = t + WINDOW - j
    valid = (dist >= 0) & (dist < WINDOW) & (j < n_keys_valid)
    return np.where(valid, _t5_bucket_np(dist), -1).astype(np.int32)


def _fill_bias(bucket_ref, rb_ref, bias_scr, rows, per_head_table=False):
    for h in range(ATT_HEADS):
        bk = bucket_ref[h * rows:(h + 1) * rows, :] if per_head_table else bucket_ref[...]
        acc = jnp.full(bk.shape, NEG_INF, F32)
        for b in range(NUM_BUCKETS):
            acc = jnp.where(bk == b, rb_ref[b * ATT_HEADS + h], acc)
        bias_scr[h * rows:(h + 1) * rows, :] = acc


def _with_ones(v2):
    return jnp.concatenate([v2, jnp.ones_like(v2)], axis=1)


def _sink_softmax_av(lg, sink, v2_ones):
    m = jnp.maximum(jnp.max(lg, axis=-1, keepdims=True), sink)
    p = jnp.exp(lg - m).astype(BF16)
    pv = _dot(p, v2_ones)
    return pv[:, :HEAD_DIM] / (pv[:, HEAD_DIM:] + jnp.exp(sink - m))


def _swa_prompt_kernel(rb_ref, sink_ref, bucket_ref, q_ref, kp_ref, kc_ref, vp_ref, vc_ref,
                       o_ref, bias_scr):
    i = pl.program_id(0)

    @pl.when(i == 0)
    def _():
        _fill_bias(bucket_ref, rb_ref, bias_scr, ATT_BLOCK)

    first_prev = jnp.logical_and(i == 0, _iota((ATT_BLOCK, 2 * ATT_BLOCK), 1) < ATT_BLOCK)
    for g in range(KV_HEADS):
        ks = slice(g * HEAD_DIM, (g + 1) * HEAD_DIM)
        k2 = jnp.concatenate([kp_ref[:, ks], kc_ref[:, ks]], axis=0).astype(BF16)
        v2 = _with_ones(jnp.concatenate([vp_ref[:, ks], vc_ref[:, ks]], axis=0).astype(BF16))
        for r in range(GROUP):
            h = g * GROUP + r
            hs = slice(h * HEAD_DIM, (h + 1) * HEAD_DIM)
            lg = _dot_nt(q_ref[:, hs].astype(BF16), k2) * ATT_SCALE + bias_scr[h * ATT_BLOCK:(h + 1) * ATT_BLOCK, :]
            lg = jnp.where(first_prev, NEG_INF, lg)
            o_ref[:, hs] = _sink_softmax_av(lg, sink_ref[h], v2).astype(BF16)


def _swa_prompt(z, rb_flat, sinks):
    nb = N_PROMPT // ATT_BLOCK
    bucket = jnp.asarray(_bucket_table(ATT_BLOCK, 2 * ATT_BLOCK, 2 * ATT_BLOCK))
    kcol = ATT_WIDTH // KV_WIDTH
    prev = lambda i: jnp.maximum(i - 1, 0)
    smem = pl.BlockSpec(memory_space=pltpu.SMEM)
    return pl.pallas_call(
        _swa_prompt_kernel,
        grid=(nb,),
        in_specs=[smem, smem,
                  pl.BlockSpec((ATT_BLOCK, 2 * ATT_BLOCK), lambda i: (0, 0)),
                  pl.BlockSpec((ATT_BLOCK, ATT_WIDTH), lambda i: (i, 0)),
                  pl.BlockSpec((ATT_BLOCK, KV_WIDTH), lambda i: (prev(i), kcol)),
                  pl.BlockSpec((ATT_BLOCK, KV_WIDTH), lambda i: (i, kcol)),
                  pl.BlockSpec((ATT_BLOCK, KV_WIDTH), lambda i: (prev(i), kcol + 1)),
                  pl.BlockSpec((ATT_BLOCK, KV_WIDTH), lambda i: (i, kcol + 1))],
        out_specs=pl.BlockSpec((ATT_BLOCK, ATT_WIDTH), lambda i: (i, 0)),
        out_shape=jax.ShapeDtypeStruct((N_PROMPT, ATT_WIDTH), BF16),
        scratch_shapes=[pltpu.VMEM((ATT_HEADS * ATT_BLOCK, 2 * ATT_BLOCK), F32)],
        compiler_params=_params(("arbitrary",)),
        name="swa_prompt",
    )(rb_flat, sinks, bucket, z, z, z, z, z)


CACHE_ROWS = WINDOW * KV_HEADS
NEW_ROWS = T_DEC * KV_HEADS
S_KEYS = 5 * LANES


def _sample_bucket_table():
    t = np.arange(T_DEC)[:, None]
    col = np.arange(S_KEYS)[None, :]
    in_cache = col < CACHE_ROWS
    in_new = (col >= CACHE_ROWS) & (col < CACHE_ROWS + NEW_ROWS)
    key_head = np.where(in_cache, col % KV_HEADS, (col - CACHE_ROWS) // T_DEC)
    key_pos = np.where(in_cache, col // KV_HEADS, WINDOW + (col - CACHE_ROWS) % T_DEC)
    dist = t + WINDOW - key_pos
    valid = (dist >= 0) & (dist < WINDOW) & (in_cache | in_new)
    per_query = np.where(valid, _t5_bucket_np(dist), -1)
    heads = np.arange(ATT_HEADS)[:, None, None] // GROUP
    table = np.where(heads == key_head[None], per_query[None], -1)
    return table.reshape(ATT_HEADS * T_DEC, S_KEYS).astype(np.int32)


def _swa_sample_kernel(rb_ref, sink_ref, bucket_ref, q_ref, kn_ref, vn_ref, ck_ref, cv_ref,
                       o_ref, nk_ref, nv_ref, bias_scr):
    @pl.when(pl.program_id(0) == 0)
    def _():
        _fill_bias(bucket_ref, rb_ref, bias_scr, T_DEC, per_head_table=True)

    nk_ref[:, 0:CACHE_ROWS - NEW_ROWS, :] = ck_ref[:, NEW_ROWS:CACHE_ROWS, :]
    nv_ref[:, 0:CACHE_ROWS - NEW_ROWS, :] = cv_ref[:, NEW_ROWS:CACHE_ROWS, :]

    pad = jnp.zeros((S_KEYS - CACHE_ROWS - NEW_ROWS, HEAD_DIM), F32)
    sink_col = jnp.concatenate([jnp.full((T_DEC, 1), sink_ref[h], F32) for h in range(ATT_HEADS)], axis=0)
    bias = bias_scr[...]
    for b in range(SAMPLE_BT):
        ts = slice(b * T_DEC, (b + 1) * T_DEC)
        head_cols = lambda ref, n: [ref[ts, h * HEAD_DIM:(h + 1) * HEAD_DIM] for h in range(n)]
        k_new, v_new = head_cols(kn_ref, KV_HEADS), head_cols(vn_ref, KV_HEADS)
        for g in range(KV_HEADS):
            new_rows = pl.ds(CACHE_ROWS - NEW_ROWS + g, T_DEC, stride=KV_HEADS)
            nk_ref[b, new_rows, :] = k_new[g]
            nv_ref[b, new_rows, :] = v_new[g]
        qa = jnp.concatenate(head_cols(q_ref, ATT_HEADS), axis=0).astype(BF16)
        k2 = jnp.concatenate([ck_ref[b]] + k_new + [pad], axis=0).astype(BF16)
        v2 = _with_ones(jnp.concatenate([cv_ref[b]] + v_new + [pad], axis=0).astype(BF16))
        o = _sink_softmax_av(_dot_nt(qa, k2) * ATT_SCALE + bias, sink_col, v2)
        for h in range(ATT_HEADS):
            o_ref[ts, h * HEAD_DIM:(h + 1) * HEAD_DIM] = o[h * T_DEC:(h + 1) * T_DEC].astype(BF16)


def _swa_sample(z, cache_k, cache_v, rb_flat, sinks):
    rows = SAMPLE_BT * T_DEC
    z_off = N_PROMPT // rows
    bucket = jnp.asarray(_sample_bucket_table())
    kcol = ATT_WIDTH // KV_WIDTH
    smem = pl.BlockSpec(memory_space=pltpu.SMEM)
    cache_spec = pl.BlockSpec((SAMPLE_BT, CACHE_ROWS, HEAD_DIM), lambda i: (i, 0, 0))
    cache_shape = jax.ShapeDtypeStruct((N_BATCH, CACHE_ROWS, HEAD_DIM), F32)
    return pl.pallas_call(
        _swa_sample_kernel,
        grid=(N_BATCH // SAMPLE_BT,),
        in_specs=[smem, smem,
                  pl.BlockSpec((ATT_HEADS * T_DEC, S_KEYS), lambda i: (0, 0)),
                  pl.BlockSpec((rows, ATT_WIDTH), lambda i: (i + z_off, 0)),
                  pl.BlockSpec((rows, KV_WIDTH), lambda i: (i + z_off, kcol)),
                  pl.BlockSpec((rows, KV_WIDTH), lambda i: (i + z_off, kcol + 1)),
                  cache_spec, cache_spec],
        out_specs=[pl.BlockSpec((rows, ATT_WIDTH), lambda i: (i, 0)), cache_spec, cache_spec],
        out_shape=[jax.ShapeDtypeStruct((N_SAMPLE, ATT_WIDTH), BF16), cache_shape, cache_shape],
        scratch_shapes=[pltpu.VMEM((ATT_HEADS * T_DEC, S_KEYS), F32)],
        compiler_params=_params(("arbitrary",)),
        name="swa_sample",
    )(rb_flat, sinks, bucket, z, z, z, cache_k, cache_v)


LANE_IG, LANE_LF, LANE_B = 0, M_HEADS, 2 * M_HEADS


def _gate_table(zg, gate_bias, seg_len):
    L = zg.shape[0]
    g = zg + gate_bias
    lane = _iota((L, LANES), 1)
    lf = _log_sigmoid(g)
    lf_only = jnp.where(jnp.logical_and(lane >= LANE_LF, lane < LANE_B), lf, 0.0)
    row = _iota((L, L), 0)
    col = _iota((L, L), 1)
    same_seg = (row // seg_len) == (col // seg_len)
    tril = jnp.where(jnp.logical_and(col <= row, same_seg), 1.0, 0.0).astype(BF16)
    cum = pltpu.roll(_dot_exact_lhs01(tril, lf_only), M_HEADS, axis=1)
    table = jnp.where(lane < LANE_LF, g, jnp.where(lane < LANE_B, lf, jnp.where(lane < LANE_B + M_HEADS, cum, 0.0)))
    return table, jnp.logical_and(col <= row, same_seg)


def _mlstm_decay(table, table_t, mask, m0_col, h):
    b_c = table[:, LANE_B + h:LANE_B + h + 1]
    b_r = table_t[LANE_B + h:LANE_B + h + 1, :]
    ig_r = table_t[LANE_IG + h:LANE_IG + h + 1, :]
    log_d = jnp.where(mask, b_c - b_r + ig_r, NEG_INF)
    log_inter = b_c + m0_col
    m_t = jnp.maximum(log_inter, jnp.max(log_d, axis=1, keepdims=True))
    return b_c, m_t, jnp.exp(log_inter - m_t), jnp.exp(log_d - m_t)


def _mlstm_intra(table, table_t, mask, m0_col, h, q, k, v):
    b_c, m_t, w_inter, d = _mlstm_decay(table, table_t, mask, m0_col, h)
    s = _dot_nt(q.astype(BF16), k.astype(BF16)) * d
    num_intra = _dot(s.astype(BF16), v.astype(BF16))
    den_intra = jnp.sum(s, axis=1, keepdims=True)
    return b_c, m_t, w_inter, num_intra, den_intra


ST_WIDTH = DV + LANES


def _mlstm_prompt_kernel(gb_ref, zg_ref, q_ref, k_ref, va_ref, vb_ref, oa_ref, ob_ref,
                         y_ref, c_out, n_out, m_out, st_scr, m_scr):
    step = pl.program_id(0)
    L = ML_CHUNK

    @pl.when(step == 0)
    def _():
        st_scr[...] = jnp.zeros_like(st_scr)
        m_scr[...] = jnp.zeros_like(m_scr)

    table, mask = _gate_table(zg_ref[...], gb_ref[...], L)
    table_t = table.T
    ones = jnp.ones((L, LANES), F32)
    half = M_HEADS // 2
    for h in range(M_HEADS):
        v_ref, o_ref = (va_ref, oa_ref) if h < half else (vb_ref, ob_ref)
        vs = slice((h % half) * DV, (h % half + 1) * DV)
        qb = (q_ref[:, h * DK:(h + 1) * DK] * Q_SCALE).astype(BF16)
        kb = k_ref[:, h * DK:(h + 1) * DK].astype(BF16)
        v1 = jnp.concatenate([v_ref[:, vs], ones], axis=1)
        m0 = m_scr[h:h + 1, 0:1]
        b_c, m_t, w_inter, d = _mlstm_decay(table, table_t, mask, m0, h)
        s = _dot_nt(qb, kb) * d
        st = st_scr[h]
        tot = _dot(s.astype(BF16), v1.astype(BF16)) + w_inter * _dot(qb, st.astype(BF16))
        den = tot[:, DV:]
        hh = tot[:, :DV] / jnp.maximum(jnp.abs(jnp.concatenate([den, den], axis=1)), jnp.exp(-m_t))
        y_ref[:, h * DV:(h + 1) * DV] = (_sigmoid(o_ref[:, vs]) * hh).astype(BF16)

        ig_c = table[:, LANE_IG + h:LANE_IG + h + 1]
        m_new = m_t[L - 1:L, :]
        b_last = b_c[L - 1:L, :]
        w_s = jnp.exp(b_last - b_c + ig_c - m_new)
        decay = jnp.exp(b_last + m0 - m_new)
        st_scr[h] = decay * st + _dot_tn(kb, (w_s * v1).astype(BF16))
        m_scr[h:h + 1, :] = jnp.broadcast_to(m_new, (1, LANES))

    @pl.when(step == pl.num_programs(0) - 1)
    def _():
        for h in range(M_HEADS):
            st_t = st_scr[h].T
            c_out[h] = st_t[:DV, :]
            n_out[h:h + 1, :] = st_t[DV:DV + 1, :]
        m_out[...] = m_scr[...]


def _mlstm_prompt(z, zg, gate_bias):
    L = ML_CHUNK
    blk = M_HEADS * DK
    col = lambda c: pl.BlockSpec((L, blk), lambda i: (i, c))
    const = lambda shape: pl.BlockSpec(shape, lambda i: tuple(0 for _ in shape))
    return pl.pallas_call(
        _mlstm_prompt_kernel,
        grid=(N_PROMPT // L,),
        in_specs=[const((1, LANES)),
                  pl.BlockSpec((L, LANES), lambda i: (i, 0)),
                  col(3), col(4), col(5), col(6), col(7), col(8)],
        out_specs=[pl.BlockSpec((L, M_WIDTH), lambda i: (i, 0)),
                   const((M_HEADS, DV, DK)), const((M_HEADS, DK)), const((M_HEADS, LANES))],
        out_shape=[jax.ShapeDtypeStruct((N_PROMPT, M_WIDTH), BF16),
                   jax.ShapeDtypeStruct((M_HEADS, DV, DK), F32),
                   jax.ShapeDtypeStruct((M_HEADS, DK), F32),
                   jax.ShapeDtypeStruct((M_HEADS, LANES), F32)],
        scratch_shapes=[pltpu.VMEM((M_HEADS, DK, ST_WIDTH), F32),
                        pltpu.VMEM((M_HEADS, LANES), F32)],
        compiler_params=_params(("arbitrary",)),
        name="mlstm_prompt",
    )(gate_bias, zg, z, z, z, z, z, z)


S_ROWS = SAMPLE_BT * T_DEC


def _mlstm_sample_kernel(gb_ref, zg_ref, m0_ref, n0t_ref, q_ref, k_ref, va_ref, vb_ref, oa_ref, ob_ref,
                         c0_ref, n0_ref, y_ref, c_out, n_out, m_out):
    L = S_ROWS
    table, mask = _gate_table(zg_ref[...], gb_ref[...], T_DEC)
    table_t = jnp.concatenate([table, jnp.zeros((LANES - L, LANES), F32)], axis=0).T[:, 0:L]
    m0_all = m0_ref[...]
    lane = _iota((L, LANES), 1)
    row_b = _iota((SAMPLE_BT, L), 1) // T_DEC
    seg_sum = jnp.where(row_b == _iota((SAMPLE_BT, L), 0), 1.0, 0.0)
    m_tok = jnp.zeros((L, LANES), F32)
    half = M_HEADS // 2
    for h in range(M_HEADS):
        v_ref, o_ref = (va_ref, oa_ref) if h < half else (vb_ref, ob_ref)
        vs = slice((h % half) * DV, (h % half + 1) * DV)
        q = q_ref[:, h * DK:(h + 1) * DK] * Q_SCALE
        k = k_ref[:, h * DK:(h + 1) * DK]
        v = v_ref[:, vs]
        m0 = m0_all[:, h:h + 1]
        b_c, m_t, w_inter, num_intra, den_intra = _mlstm_intra(table, table_t, mask, m0, h, q, k, v)
        qb = q.astype(BF16)
        num_inter = jnp.concatenate(
            [_dot_nt(qb[b * T_DEC:(b + 1) * T_DEC], c0_ref[b, h].astype(BF16)) for b in range(SAMPLE_BT)], axis=0)
        num = num_intra + w_inter * num_inter
        den = den_intra + w_inter * jnp.sum(q * n0t_ref[:, h * DK:(h + 1) * DK], axis=1, keepdims=True)
        hh = num / jnp.maximum(jnp.abs(den), jnp.exp(-m_t))
        y_ref[:, h * DV:(h + 1) * DV] = (_sigmoid(o_ref[:, vs]) * hh).astype(BF16)
        m_tok = jnp.where(lane == h, m_t, m_tok)

        def last_tok(x):
            x3 = x.reshape(SAMPLE_BT, T_DEC, 1)
            return jnp.broadcast_to(x3[:, T_DEC - 1:T_DEC, :], x3.shape).reshape(L, 1)

        ig_c = table[:, LANE_IG + h:LANE_IG + h + 1]
        m_new = last_tok(m_t)
        b_last = last_tok(b_c)
        w_s = jnp.exp(b_last - b_c + ig_c - m_new)
        decay = jnp.exp(b_last + m0 - m_new)
        wv = (w_s * v).astype(BF16)
        kb = k.astype(BF16)
        rowsel = _iota((L, 1), 0) // T_DEC
        for b in range(SAMPLE_BT):
            dec_b = decay[b * T_DEC + T_DEC - 1:(b + 1) * T_DEC, :]
            wv_b = jnp.where(rowsel == b, wv, jnp.zeros_like(wv))
            c_out[b, h] = dec_b * c0_ref[b, h] + _dot_tn(wv_b, kb)
        dec_rows = decay.reshape(SAMPLE_BT, T_DEC, 1)[:, T_DEC - 1, :]
        n_out[:, h * DK:(h + 1) * DK] = dec_rows * n0_ref[:, h * DK:(h + 1) * DK] + jnp.dot(
            seg_sum, w_s * k, preferred_element_type=F32, precision=lax.Precision.HIGHEST)
    m_out[...] = m_tok


def _mlstm_sample(z, zg, gate_bias, m0_tok, n0_tok, state_c, state_n):
    L = S_ROWS
    blk = M_HEADS * DK
    z_off = N_PROMPT // L
    col = lambda c: pl.BlockSpec((L, blk), lambda i: (i + z_off, c))
    c_spec = pl.BlockSpec((SAMPLE_BT, M_HEADS, DV, DK), lambda i: (i, 0, 0, 0))
    n_spec = pl.BlockSpec((SAMPLE_BT, blk), lambda i: (i, 0))
    return pl.pallas_call(
        _mlstm_sample_kernel,
        grid=(N_BATCH // SAMPLE_BT,),
        in_specs=[pl.BlockSpec((1, LANES), lambda i: (0, 0)),
                  pl.BlockSpec((L, LANES), lambda i: (i + z_off, 0)),
                  pl.BlockSpec((L, LANES), lambda i: (i, 0)),
                  pl.BlockSpec((L, blk), lambda i: (i, 0)),
                  col(3), col(4), col(5), col(6), col(7), col(8),
                  c_spec, n_spec],
        out_specs=[pl.BlockSpec((L, M_WIDTH), lambda i: (i, 0)), c_spec, n_spec,
                   pl.BlockSpec((L, LANES), lambda i: (i, 0))],
        out_shape=[jax.ShapeDtypeStruct((N_SAMPLE, M_WIDTH), BF16),
                   jax.ShapeDtypeStruct((N_BATCH, M_HEADS, DV, DK), F32),
                   jax.ShapeDtypeStruct((N_BATCH, blk), F32),
                   jax.ShapeDtypeStruct((N_SAMPLE, LANES), F32)],
        compiler_params=_params(("arbitrary",)),
        name="mlstm_sample",
    )(gate_bias, zg, m0_tok, n0_tok, z, z, z, z, z, z, state_c, state_n)


N_OUT_CHUNKS = D // TN_OUT


def _outproj_kernel(ya_ref, ym_ref, w_hbm, x_ref, gt_ref, o_ref, stage, w_bf, sem):
    j = pl.program_id(0)

    def fetch(chunk):
        cols = pl.ds(pl.multiple_of(chunk * TN_OUT, TN_OUT), TN_OUT)
        return pltpu.make_async_copy(w_hbm.at[:, cols], stage, sem.at[0])

    @pl.when(pl.program_id(1) == 0)
    def _():
        @pl.when(j == 0)
        def _():
            fetch(0).start()

        fetch(j).wait()
        w_bf[...] = stage[...].astype(BF16)

        @pl.when(j + 1 < N_OUT_CHUNKS)
        def _():
            fetch(j + 1).start()

    mix = _dot(ya_ref[...], w_bf[0:ATT_WIDTH, :]) + _dot(ym_ref[...], w_bf[ATT_WIDTH:, :])
    o_ref[...] = x_ref[...] + _mod_rows(gt_ref) * mix


def _outproj(ya, ym, w_out, x, mod, per_row):
    n = x.shape[0]
    TM = TM_PROJ
    gate1_col = 2 * N_OUT_CHUNKS
    if per_row:
        gate_spec = pl.BlockSpec((TM // T_DEC, 1, TN_OUT), lambda j, i: (i, 0, gate1_col + j))
    else:
        gate_spec = pl.BlockSpec((1, TN_OUT), lambda j, i: (0, gate1_col + j))
    return pl.pallas_call(
        _outproj_kernel,
        grid=(N_OUT_CHUNKS, n // TM),
        in_specs=[pl.BlockSpec((TM, ATT_WIDTH), lambda j, i: (i, 0)),
                  pl.BlockSpec((TM, M_WIDTH), lambda j, i: (i, 0)),
                  pl.BlockSpec(memory_space=pl.ANY),
                  pl.BlockSpec((TM, TN_OUT), lambda j, i: (i, j)),
                  gate_spec],
        out_specs=pl.BlockSpec((TM, TN_OUT), lambda j, i: (i, j)),
        out_shape=jax.ShapeDtypeStruct((n, D), F32),
        scratch_shapes=[pltpu.VMEM((D, TN_OUT), F32), pltpu.VMEM((D, TN_OUT), BF16), pltpu.SemaphoreType.DMA((1,))],
        compiler_params=_params(("arbitrary", "arbitrary")),
        name="outproj",
    )(ya, ym, w_out, x, mod)


def _router_kernel(x_ref, sh_ref, sc_ref, g_ref, wr_ref, br_ref, h_ref, route_ref):
    x = x_ref[...]
    y = x * lax.rsqrt(jnp.mean(x * x, axis=-1, keepdims=True) + EPS) * g_ref[...]
    h2 = y * (1.0 + _mod_rows(sc_ref)) + _mod_rows(sh_ref)
    h_hi = h2.astype(BF16)
    h_ref[...] = _pack_bf16_pairs(h2)
    h_lo = (h2 - h_hi.astype(F32)).astype(BF16)
    logits = _dot(h_hi, wr_ref[0]) + (_dot(h_hi, wr_ref[1]) + _dot(h_lo, wr_ref[0])) + br_ref[...]
    lane = _iota(logits.shape, 1)

    def first_max(vals):
        vmax = jnp.max(vals, axis=1, keepdims=True)
        idx = jnp.min(jnp.where(vals == vmax, lane, LANES), axis=1, keepdims=True)
        return vmax, idx

    gl = jnp.where(lane < N_GROUPS, logits, NEG_INF)
    gmax, grp = first_max(gl)
    p_grp = 1.0 / jnp.sum(jnp.exp(gl - gmax), axis=1, keepdims=True)
    e_lane = lane - N_GROUPS
    in_grp = jnp.logical_and(e_lane >= 0, jnp.logical_and(e_lane < N_EXPERTS, e_lane // EXP_PER_GROUP == grp))
    el = jnp.where(in_grp, logits, NEG_INF)
    v1, i1 = first_max(el)
    v2, i2 = first_max(jnp.where(lane == i1, NEG_INF, el))
    e2w = jnp.exp(v2 - v1)
    w1 = 1.0 / (1.0 + e2w)
    w2 = e2w / (1.0 + e2w)
    route = jnp.where(lane == 0, (i1 - N_GROUPS).astype(F32),
                      jnp.where(lane == 1, (i2 - N_GROUPS).astype(F32),
                                jnp.where(lane == 2, p_grp * w1, jnp.where(lane == 3, p_grp * w2, 0.0))))
    route_ref[...] = route


def _router_merged_kernel(xp_ref, xs_ref, shp_ref, scp_ref, shs_ref, scs_ref, g_ref, wr_ref, br_ref,
                          h_ref, route_ref):
    i = pl.program_id(0)

    @pl.when(i < PROMPT_TOK_BLOCKS)
    def _():
        _router_kernel(xp_ref, shp_ref, scp_ref, g_ref, wr_ref, br_ref, h_ref, route_ref)

    @pl.when(i >= PROMPT_TOK_BLOCKS)
    def _():
        _router_kernel(xs_ref, shs_ref, scs_ref, g_ref, wr_ref, br_ref, h_ref, route_ref)


def _router(x1_p, x1_s, mod_p, mod_s, g_ffn, w_route, b_route):
    pi, si = _prompt_block, _sample_block
    return pl.pallas_call(
        _router_merged_kernel,
        grid=(N_TOK // TM_TOK,),
        in_specs=[pl.BlockSpec((TM_TOK, D), lambda i: (pi(i), 0)),
                  pl.BlockSpec((TM_TOK, D), lambda i: (si(i), 0)),
                  pl.BlockSpec((1, D), lambda i: (0, 3)),
                  pl.BlockSpec((1, D), lambda i: (0, 4)),
                  pl.BlockSpec((TM_TOK // T_DEC, 1, D), lambda i: (si(i), 0, 3)),
                  pl.BlockSpec((TM_TOK // T_DEC, 1, D), lambda i: (si(i), 0, 4)),
                  pl.BlockSpec((1, D), lambda i: (0, 0)),
                  pl.BlockSpec((2, D, LANES), lambda i: (0, 0, 0)),
                  pl.BlockSpec((1, LANES), lambda i: (0, 0))],
        out_specs=[pl.BlockSpec((TM_TOK, D_PACK), lambda i: (i, 0)),
                   pl.BlockSpec((TM_TOK, LANES), lambda i: (i, 0))],
        out_shape=[jax.ShapeDtypeStruct((N_TOK, D_PACK), jnp.uint32),
                   jax.ShapeDtypeStruct((N_TOK, LANES), F32)],
        compiler_params=_params(("arbitrary",)),
        name="router",
    )(x1_p, x1_s, mod_p, mod_p, mod_s, mod_s, g_ffn, w_route, b_route)


def _rank_kernel(route_ref, dest_ref, cnt_ref, carry, rank_scr):
    pas = pl.program_id(0)
    step = pl.program_id(1)

    @pl.when(jnp.logical_and(pas == 0, step == 0))
    def _():
        carry[...] = jnp.zeros_like(carry)

    route = route_ref[...]
    n = route.shape[0]
    lane = _iota((n, LANES), 1).astype(F32)
    o1 = jnp.where(lane == route[:, 0:1], 1.0, 0.0)
    o2 = jnp.where(lane == route[:, 1:2], 1.0, 0.0)
    lane_i = _iota((n, LANES), 1)
    rows = pl.ds(pl.multiple_of(step * TM_RANK, TM_RANK), TM_RANK)

    @pl.when(pas == 0)
    def _():
        both = o1 + o2
        strict = jnp.where(_iota((n, n), 1) < _iota((n, n), 0), 1.0, 0.0).astype(BF16)
        prior = _dot(strict, both.astype(BF16)) + carry[0:1, :]
        r1 = jnp.sum(o1 * prior, axis=1, keepdims=True)
        r2 = jnp.sum(o2 * prior, axis=1, keepdims=True)
        rank_scr[rows, :] = jnp.where(lane_i == 0, r1, jnp.where(lane_i == 1, r2, 0.0))
        carry[...] = carry[...] + jnp.sum(both, axis=0, keepdims=True)

    @pl.when(pas == 1)
    def _():
        counts = carry[...]
        tiles_per = jnp.floor((counts + (TM_MOE - 1)) * (1.0 / TM_MOE))
        before = jnp.where(_iota((LANES, LANES), 0) < _iota((LANES, LANES), 1), 1.0, 0.0).astype(BF16)
        pad_start = _dot(tiles_per.astype(BF16), before)[0:1, :] * TM_MOE
        rank = rank_scr[rows, :]
        d1 = jnp.sum(o1 * pad_start, axis=1, keepdims=True) + rank[:, 0:1]
        d2 = jnp.sum(o2 * pad_start, axis=1, keepdims=True) + rank[:, 1:2]
        dest_ref[...] = jnp.where(lane_i == 0, d1, jnp.where(lane_i == 1, d2, 0.0)).astype(jnp.int32)
        cnt_ref[...] = counts


def _rank(route):
    n = route.shape[0]
    return pl.pallas_call(
        _rank_kernel,
        grid=(2, n // TM_RANK),
        in_specs=[pl.BlockSpec((TM_RANK, LANES), lambda p, i: (i, 0))],
        out_specs=[pl.BlockSpec((TM_RANK, LANES), lambda p, i: (i * p, 0)),
                   pl.BlockSpec((SUBLANES, LANES), lambda p, i: (0, 0))],
        out_shape=[jax.ShapeDtypeStruct((n, LANES), jnp.int32), jax.ShapeDtypeStruct((SUBLANES, LANES), F32)],
        scratch_shapes=[pltpu.VMEM((SUBLANES, LANES), F32), pltpu.VMEM((n, LANES), F32)],
        compiler_params=_params(("arbitrary", "arbitrary")),
        name="rank",
    )(route)


def _row_copy(src, dst, sem):
    return pltpu.make_async_copy(src, dst, sem)


ROW_UNROLL = 8
assert T_DEC == SUBLANES
D_PACK = D // 2

ZERO_FIRST, ZERO_ANYTIME = 1, 2


def _dispatch_kernel(d1_ref, d2_ref, zc_ref, h_ref, xs_out, zbuf, sem):
    step = pl.program_id(0)
    base = step * TM_TOK

    def zero_tile(t, s):
        rows = pl.ds(pl.multiple_of(t * TM_MOE, TM_MOE), TM_MOE)
        return pltpu.make_async_copy(zbuf, xs_out.at[rows], sem.at[s])

    def for_tiles(cls, s, act):
        def body(t, carry):
            @pl.when(zc_ref[t] == cls)
            def _():
                act(zero_tile(t, s))
            return carry
        lax.fori_loop(0, MAX_TILES, body, 0)

    @pl.when(step == 0)
    def _():
        zbuf[...] = jnp.zeros_like(zbuf)
        for_tiles(ZERO_FIRST, 1, lambda c: c.start())
        for_tiles(ZERO_ANYTIME, 2, lambda c: c.start())
        for_tiles(ZERO_FIRST, 1, lambda c: c.wait())

    @pl.when(step == pl.num_programs(0) - 1)
    def _():
        for_tiles(ZERO_ANYTIME, 2, lambda c: c.wait())

    def issue(r, carry):
        _row_copy(h_ref.at[pl.ds(r, 1)], xs_out.at[pl.ds(d1_ref[base + r], 1)], sem.at[0]).start()
        _row_copy(h_ref.at[pl.ds(r, 1)], xs_out.at[pl.ds(d2_ref[base + r], 1)], sem.at[0]).start()
        return carry

    lax.fori_loop(0, TM_TOK, issue, 0, unroll=ROW_UNROLL)
    for _ in range(TOP_K):
        _row_copy(h_ref, xs_out.at[pl.ds(0, TM_TOK)], sem.at[0]).wait()


def _dispatch(dests, zero_class, h2):
    grid_spec = pltpu.PrefetchScalarGridSpec(
        num_scalar_prefetch=3,
        grid=(N_TOK // TM_TOK,),
        in_specs=[pl.BlockSpec((TM_TOK, D_PACK), lambda i, *_: (i, 0))],
        out_specs=pl.BlockSpec(memory_space=pl.ANY),
        scratch_shapes=[pltpu.VMEM((TM_MOE, D_PACK), jnp.uint32), pltpu.SemaphoreType.DMA((3,))],
    )
    return pl.pallas_call(
        _dispatch_kernel,
        grid_spec=grid_spec,
        out_shape=jax.ShapeDtypeStruct((A_PAD, D_PACK), jnp.uint32),
        compiler_params=_params(("arbitrary",)),
        name="dispatch",
    )(*dests, zero_class, h2)


N_UP_CHUNKS = D_FF // TN_FF
ROW_CAPS = tuple(range(TM_MOE // 4, TM_MOE + 1, TM_MOE // 4))


def _for_row_cap(rows, body):
    lo = 0
    for cap in ROW_CAPS:
        pl.when(jnp.logical_and(rows > lo, rows <= cap))(functools.partial(body, cap))
        lo = cap


def _expert_up_kernel(te_ref, rows_ref, first_ref, nxt_ref, run_ref, meta_ref, x_ref, wg_hbm, wu_hbm,
                      o_ref, wbuf, sem):
    n = pl.program_id(0)
    t = pl.program_id(1)
    slot = lax.rem(n * meta_ref[1] + run_ref[t], 2)

    def fetch(e, chunk, s):
        cols = pl.ds(pl.multiple_of(chunk * TN_FF, TN_FF), TN_FF)
        return (pltpu.make_async_copy(wg_hbm.at[e, :, cols], wbuf.at[s, 0], sem.at[s, 0]),
                pltpu.make_async_copy(wu_hbm.at[e, :, cols], wbuf.at[s, 1], sem.at[s, 1]))

    def start(copies):
        for priority, c in enumerate(copies):
            c.start(priority=priority)

    @pl.when(first_ref[t] == 1)
    def _():
        @pl.when(jnp.logical_and(n == 0, t == 0))
        def _():
            start(fetch(te_ref[0], 0, 0))

        @pl.when(nxt_ref[t] >= 0)
        def _():
            start(fetch(nxt_ref[t], n, 1 - slot))

        @pl.when(jnp.logical_and(nxt_ref[t] < 0, n + 1 < N_UP_CHUNKS))
        def _():
            start(fetch(te_ref[0], n + 1, 1 - slot))

        for c in fetch(te_ref[t], n, slot):
            c.wait()

    def compute(cap):
        x = jnp.concatenate([half.astype(BF16) for half in _unpack_bf16_pairs(x_ref[0:cap, :])], axis=1)
        g = _dot(x, wbuf[slot, 0].astype(BF16))
        u = _dot(x, wbuf[slot, 1].astype(BF16))
        o_ref[0:cap, :] = (g * _sigmoid(g) * u).astype(BF16)
        if cap < TM_MOE:
            o_ref[cap:, :] = jnp.zeros((TM_MOE - cap, TN_FF), BF16)

    _for_row_cap(rows_ref[t], compute)

    @pl.when(rows_ref[t] == 0)
    def _():
        o_ref[...] = jnp.zeros_like(o_ref)


def _expert_up(tiles, xs, w_gate, w_up):
    last = lambda t, meta: jnp.minimum(t, meta[0] - 1)
    grid_spec = pltpu.PrefetchScalarGridSpec(
        num_scalar_prefetch=6,
        grid=(N_UP_CHUNKS, MAX_TILES),
        in_specs=[pl.BlockSpec((TM_MOE, D_PACK), lambda n, t, *s: (last(t, s[5]), 0)),
                  pl.BlockSpec(memory_space=pl.ANY),
                  pl.BlockSpec(memory_space=pl.ANY)],
        out_specs=pl.BlockSpec((TM_MOE, TN_FF), lambda n, t, *s: (t, n)),
        scratch_shapes=[pltpu.VMEM((2, 2, D, TN_FF), F32), pltpu.SemaphoreType.DMA((2, 2))],
    )
    return pl.pallas_call(
        _expert_up_kernel,
        grid_spec=grid_spec,
        out_shape=jax.ShapeDtypeStruct((A_PAD, D_FF), BF16),
        compiler_params=_params(("arbitrary", "arbitrary")),
        name="expert_up",
    )(*tiles, xs, w_gate, w_up)


def _expert_down_kernel(te_ref, rows_ref, first_ref, nxt_ref, run_ref, meta_ref, h_ref, wd_hbm,
                        o_ref, wbuf, sem):
    t = pl.program_id(0)
    slot = lax.rem(run_ref[t], 2)

    def fetch(e, s):
        halves = [pl.ds(k * (D_FF // 2), D_FF // 2) for k in range(2)]
        return [pltpu.make_async_copy(wd_hbm.at[e, rows], wbuf.at[s, rows], sem.at[s, k])
                for k, rows in enumerate(halves)]

    def start(copies):
        for priority, c in enumerate(copies):
            c.start(priority=priority)

    @pl.when(first_ref[t] == 1)
    def _():
        @pl.when(t == 0)
        def _():
            start(fetch(te_ref[0], 0))

        @pl.when(nxt_ref[t] >= 0)
        def _():
            start(fetch(nxt_ref[t], 1 - slot))

        for c in fetch(te_ref[t], slot):
            c.wait()

    def compute(cap):
        o_ref[0:cap, :] = _pack_bf16_pairs(_dot(h_ref[0:cap, :], wbuf[slot].astype(BF16)))
        if cap < TM_MOE:
            o_ref[cap:, :] = jnp.zeros((TM_MOE - cap, D_PACK), jnp.uint32)

    _for_row_cap(rows_ref[t], compute)

    @pl.when(rows_ref[t] == 0)
    def _():
        o_ref[...] = jnp.zeros_like(o_ref)


def _expert_down(tiles, h1, w_down):
    last = lambda t, meta: jnp.minimum(t, meta[0] - 1)
    grid_spec = pltpu.PrefetchScalarGridSpec(
        num_scalar_prefetch=6,
        grid=(MAX_TILES,),
        in_specs=[pl.BlockSpec((TM_MOE, D_FF), lambda t, *s: (last(t, s[5]), 0)),
                  pl.BlockSpec(memory_space=pl.ANY)],
        out_specs=pl.BlockSpec((TM_MOE, D_PACK), lambda t, *s: (t, 0)),
        scratch_shapes=[pltpu.VMEM((2, D_FF, D), F32), pltpu.SemaphoreType.DMA((2, 2))],
    )
    return pl.pallas_call(
        _expert_down_kernel,
        grid_spec=grid_spec,
        out_shape=jax.ShapeDtypeStruct((A_PAD, D_PACK), jnp.uint32),
        compiler_params=_params(("arbitrary",)),
        name="expert_down",
    )(*tiles, h1, w_down)


def _combine_kernel(d1_ref, d2_ref, x_ref, route_ref, gt_ref, gf_ref, o_hbm,
                    y_ref, buf, sem, *, tok_offset):
    step = pl.program_id(0)
    slot = lax.rem(step, 2)

    def gather(s, into):
        base = tok_offset + s * TM_TOK

        def issue(r, carry):
            _row_copy(o_hbm.at[pl.ds(d1_ref[base + r], 1)], buf.at[into, 0, pl.ds(r, 1)], sem.at[into]).start()
            _row_copy(o_hbm.at[pl.ds(d2_ref[base + r], 1)], buf.at[into, 1, pl.ds(r, 1)], sem.at[into]).start()
            return carry

        lax.fori_loop(0, TM_TOK, issue, 0, unroll=ROW_UNROLL)

    @pl.when(step == 0)
    def _():
        gather(0, 0)

    @pl.when(step + 1 < pl.num_programs(0))
    def _():
        gather(step + 1, 1 - slot)

    for k in range(TOP_K):
        _row_copy(o_hbm.at[pl.ds(0, TM_TOK)], buf.at[slot, k], sem.at[slot]).wait()

    route = route_ref[...]
    lo1, hi1 = _unpack_bf16_pairs(buf[slot, 0])
    lo2, hi2 = _unpack_bf16_pairs(buf[slot, 1])
    g1, g2 = route[:, 2:3], route[:, 3:4]
    moe = jnp.concatenate([g1 * lo1 + g2 * lo2, g1 * hi1 + g2 * hi2], axis=1)
    x2 = x_ref[...] + _mod_rows(gt_ref) * moe
    y_ref[...] = x2 * lax.rsqrt(jnp.mean(x2 * x2, axis=-1, keepdims=True) + EPS) * gf_ref[...]


def _combine(dests, x1, route, mod, per_row, g_final, o_rows, tok_offset):
    n = x1.shape[0]
    off = tok_offset // TM_TOK
    grid_spec = pltpu.PrefetchScalarGridSpec(
        num_scalar_prefetch=2,
        grid=(n // TM_TOK,),
        in_specs=[pl.BlockSpec((TM_TOK, D), lambda i, *_: (i, 0)),
                  pl.BlockSpec((TM_TOK, LANES), lambda i, *_: (i + off, 0)),
                  _mod_spec(per_row, TM_TOK, D, lambda *_: 5),
                  pl.BlockSpec((1, D), lambda i, *_: (0, 0)),
                  pl.BlockSpec(memory_space=pl.ANY)],
        out_specs=pl.BlockSpec((TM_TOK, D), lambda i, *_: (i, 0)),
        scratch_shapes=[pltpu.VMEM((2, TOP_K, TM_TOK, D_PACK), jnp.uint32), pltpu.SemaphoreType.DMA((2,))],
    )
    return pl.pallas_call(
        functools.partial(_combine_kernel, tok_offset=tok_offset),
        grid_spec=grid_spec,
        out_shape=jax.ShapeDtypeStruct((n, D), F32),
        compiler_params=_params(("arbitrary",)),
        name="combine",
    )(*dests, x1, route, mod, g_final, o_rows)


def kernel(x_prompt, x_sample, cache_k, cache_v, state_C, state_n, state_m, c_prompt, c_sample, rel_bias, w_ada, b_ada, g_mix, g_ffn, w_in, sinks, b_igate, b_fgate, w_out, w_router_grp, b_router_grp, w_router_exp, b_router_exp, w_gate, w_up, w_down, g_final):
    xp = x_prompt.reshape(N_PROMPT, D)
    xs = x_sample.reshape(N_SAMPLE, D)

    c_all = jnp.concatenate([c_sample, c_prompt, jnp.zeros((C_ROWS - 1 - N_BATCH, D), F32)], axis=0)
    mod_s = _ada(c_all, w_ada[0], b_ada)
    mod_p = mod_s[N_BATCH]

    w_in_t = jnp.swapaxes(w_in[0], 0, 1)
    w_gates_t = jnp.pad(w_in_t[Z_WIDTH:].astype(BF16), ((0, LANES - 2 * M_HEADS), (0, 0)))
    h_all, zg = _norm(xp, xs, mod_p, mod_s, g_mix, w_gates_t)
    z = _inproj(h_all, w_in_t)

    rb_flat = rel_bias.reshape(NUM_BUCKETS * ATT_HEADS)
    sink_v = sinks[0]
    ya_p = _swa_prompt(z, rb_flat, sink_v)
    ya_s, nk_s, nv_s = _swa_sample(z,cache_k.reshape(N_BATCH, CACHE_ROWS, HEAD_DIM),
                                   cache_v.reshape(N_BATCH, CACHE_ROWS, HEAD_DIM), rb_flat, sink_v)

    gate_bias = jnp.concatenate([b_igate[0], b_fgate[0], jnp.zeros((LANES - 2 * M_HEADS,), F32)]).reshape(1, LANES)
    ym_p, c_p, n_p, m_p = _mlstm_prompt(z, zg, gate_bias)
    m0_tok = jnp.pad(jnp.repeat(state_m[0], T_DEC, axis=0), ((0, 0), (0, LANES - M_HEADS)))
    n0_flat = state_n[0].reshape(N_BATCH, M_HEADS * DK)
    n0_tok = jnp.repeat(n0_flat, T_DEC, axis=0)
    ym_s, c_s, n_s, m_s = _mlstm_sample(z, zg, gate_bias, m0_tok, n0_tok, state_C[0], n0_flat)

    x1_p = _outproj(ya_p, ym_p, w_out[0], xp, mod_p, False)
    x1_s = _outproj(ya_s, ym_s, w_out[0], xs, mod_s, True)

    w_route = jnp.pad(jnp.concatenate([w_router_grp[0], w_router_exp[0]], axis=1),
                      ((0, 0), (0, LANES - N_GROUPS - N_EXPERTS)))
    b_route = jnp.pad(jnp.concatenate([b_router_grp[0], b_router_exp[0]]),
                      (0, LANES - N_GROUPS - N_EXPERTS)).reshape(1, LANES)
    w_route_hi = w_route.astype(BF16)
    w_route_split = jnp.stack([w_route_hi, (w_route - w_route_hi.astype(F32)).astype(BF16)])
    h2, route = _router(x1_p, x1_s, mod_p, mod_s, g_ffn, w_route_split, b_route)

    dest, counts = _rank(route)
    dests = (dest[:, 0], dest[:, 1])

    i32 = lambda a: a.astype(jnp.int32)
    cnt = i32(counts[0, :N_EXPERTS])
    tiles_per = (cnt + TM_MOE - 1) // TM_MOE
    tile_end = jnp.cumsum(tiles_per)
    n_tiles = tile_end[-1]
    tile_ids = jnp.arange(MAX_TILES, dtype=jnp.int32)
    tile_expert = i32(jnp.minimum(jnp.searchsorted(tile_end, tile_ids, side="right"), N_EXPERTS - 1))
    last_expert = tile_expert[jnp.maximum(n_tiles - 1, 0)]
    tile_valid = tile_ids < n_tiles
    tile_expert = jnp.where(tile_valid, tile_expert, last_expert)
    prev_expert = jnp.concatenate([jnp.full((1,), -1, jnp.int32), tile_expert[:-1]])
    run_first = jnp.logical_and(tile_valid, tile_expert != prev_expert)
    run_id = jnp.maximum(jnp.cumsum(i32(run_first)) - 1, 0)
    expert_ids = jnp.arange(N_EXPERTS, dtype=jnp.int32)
    used = jnp.where(tiles_per > 0, expert_ids, N_EXPERTS)
    next_used = jnp.concatenate([lax.cummin(used[::-1])[::-1][1:], jnp.full((1,), N_EXPERTS, jnp.int32)])
    next_used = jnp.where(next_used >= N_EXPERTS, -1, next_used)
    first_tile = tile_end - tiles_per
    tile_rows = jnp.clip(cnt[tile_expert] - (tile_ids - first_tile[tile_expert]) * TM_MOE, 0, TM_MOE)
    tile_rows = jnp.where(tile_valid, tile_rows, 0)
    tiles = (tile_expert, i32(tile_rows), i32(run_first), next_used[tile_expert], i32(run_id),
             jnp.stack([n_tiles, jnp.sum(i32(run_first))]).astype(jnp.int32))
    next_expert = jnp.concatenate([tile_expert[1:], jnp.full((1,), -1, jnp.int32)])
    run_last = jnp.logical_or(tile_expert != next_expert, tile_ids == n_tiles - 1)
    zero_class = jnp.where(tile_valid, jnp.where(run_last, ZERO_FIRST, 0), ZERO_ANYTIME)
    xs_rows = _dispatch(dests, i32(zero_class), h2)
    h1 = _expert_up(tiles, xs_rows, w_gate[0], w_up[0])
    o_rows = _expert_down(tiles, h1, w_down[0])

    gf = g_final.reshape(1, D)
    y_p = _combine(dests, x1_p, route, mod_p, False, gf, o_rows, 0)
    y_s = _combine(dests, x1_s, route, mod_s, True, gf, o_rows, N_PROMPT)

    kv5 = lambda a: a.reshape(1, -1, WINDOW, KV_HEADS, HEAD_DIM)
    kcol = ATT_WIDTH
    nk_p = z[N_PROMPT - WINDOW:N_PROMPT, kcol:kcol + KV_WIDTH]
    nv_p = z[N_PROMPT - WINDOW:N_PROMPT, kcol + KV_WIDTH:kcol + 2 * KV_WIDTH]
    return (y_p.reshape(1, N_PROMPT, D), y_s.reshape(N_BATCH, T_DEC, D),
            kv5(nk_p), kv5(nv_p),
            c_p.reshape(1, 1, M_HEADS, DV, DK), n_p.reshape(1, 1, M_HEADS, DK), m_p[:, 0].reshape(1, 1, M_HEADS),
            kv5(nk_s), kv5(nv_s),
            c_s.reshape(1, N_BATCH, M_HEADS, DV, DK), n_s.reshape(1, N_BATCH, M_HEADS, DK),
            m_s.reshape(N_BATCH, T_DEC, LANES)[:, T_DEC - 1, :M_HEADS].reshape(1, N_BATCH, M_HEADS))
```

```python
import functools
import math

import numpy as np
import jax
import jax.numpy as jnp
from jax import lax
from jax.experimental import pallas as pl
from jax.experimental.pallas import tpu as pltpu

F32 = jnp.float32
BF16 = jnp.bfloat16
NEG_INF = float("-inf")

D = 4096
N_PROMPT = 8192
N_BATCH = 128
T_DEC = 8
N_SAMPLE = N_BATCH * T_DEC
N_TOK = N_PROMPT + N_SAMPLE
HEAD_DIM = 128
ATT_HEADS = 16
KV_HEADS = 4
GROUP = ATT_HEADS // KV_HEADS
WINDOW = 128
ATT_WIDTH = ATT_HEADS * HEAD_DIM
KV_WIDTH = KV_HEADS * HEAD_DIM
NUM_BUCKETS = 32
MAX_EXACT = 16
MAX_DISTANCE = 128
M_HEADS = 8
DK = 128
DV = 256
M_WIDTH = M_HEADS * DV
Z_WIDTH = ATT_WIDTH + 2 * KV_WIDTH + 2 * M_HEADS * DK + 2 * M_WIDTH
N_GROUPS = 4
EXP_PER_GROUP = 8
N_EXPERTS = N_GROUPS * EXP_PER_GROUP
TOP_K = 2
D_FF = 1024
EPS = 1e-6
ATT_SCALE = HEAD_DIM ** -0.5
Q_SCALE = DK ** -0.5

LANES = 128
SUBLANES = 8
VMEM_LIMIT = 56 * 1024 * 1024

TM_PROJ = 512
TM_IN = 1024
TN_IN = 1024
TN_OUT = 1024
TN_ADA = 512
C_ROWS = 136
ATT_BLOCK = 128
SAMPLE_BT = 8
ML_CHUNK = 256
TM_MOE = 256
TN_FF = 512
N_ASSIGN = N_TOK * TOP_K
MAX_TILES = N_ASSIGN // TM_MOE + N_EXPERTS
A_PAD = MAX_TILES * TM_MOE
TM_TOK = 256
TM_RANK = 1024


def _params(sem):
    return pltpu.CompilerParams(dimension_semantics=sem, vmem_limit_bytes=VMEM_LIMIT)


def _iota(shape, dim):
    return lax.broadcasted_iota(jnp.int32, shape, dim)


def _dot(a, b):
    return jnp.dot(a, b, preferred_element_type=F32)


def _dot_nt(a, b):
    return lax.dot_general(a, b, (((1,), (1,)), ((), ())), preferred_element_type=F32)


def _dot_tn(a, b):
    return lax.dot_general(a, b, (((0,), (0,)), ((), ())), preferred_element_type=F32)


def _split3(x):
    x1 = x.astype(BF16)
    r1 = x - x1.astype(F32)
    x2 = r1.astype(BF16)
    r2 = r1 - x2.astype(F32)
    return x1, x2, r2.astype(BF16)


def _dot_exact_lhs01(a01, x):
    x1, x2, x3 = _split3(x)
    return _dot(a01, x1) + _dot(a01, x2) + _dot(a01, x3)


def _pack_bf16_pairs(x):
    w = x.shape[1] // 2
    bits = lax.bitcast_convert_type(x.astype(BF16).astype(F32), jnp.uint32)
    return (bits[:, :w] >> 16) | (bits[:, w:] & jnp.uint32(0xFFFF0000))


def _unpack_bf16_pairs(words):
    return (lax.bitcast_convert_type(words << 16, F32),
            lax.bitcast_convert_type(words & jnp.uint32(0xFFFF0000), F32))


def _sigmoid(x):
    return 1.0 / (1.0 + jnp.exp(-x))


def _log_sigmoid(x):
    return jnp.minimum(x, 0.0) - jnp.log(1.0 + jnp.exp(-jnp.abs(x)))


def _mod_spec(per_batch, rows, width, col):
    if per_batch:
        return pl.BlockSpec((rows // T_DEC, 1, width), lambda i, *rest: (i, 0, col(*rest)))
    return pl.BlockSpec((1, width), lambda i, *rest: (0, col(*rest)))


def _mod_rows(ref):
    v = ref[...]
    if v.ndim == 2:
        return v
    nb, _, width = v.shape
    return jnp.broadcast_to(v, (nb, T_DEC, width)).reshape(nb * T_DEC, width)


def _ada_kernel(c_ref, w_ref, b_ref, o_ref):
    c = c_ref[...]
    s = (c * _sigmoid(c)).astype(BF16)
    mod = _dot(s, w_ref[...].astype(BF16)) + b_ref[...]
    o_ref[...] = mod.reshape(C_ROWS, 1, TN_ADA)


def _ada(c_all, w_ada, b_ada):
    n = w_ada.shape[1]
    return pl.pallas_call(
        _ada_kernel,
        grid=(n // TN_ADA,),
        in_specs=[pl.BlockSpec((C_ROWS, D), lambda j: (0, 0)),
                  pl.BlockSpec((D, TN_ADA), lambda j: (0, j)),
                  pl.BlockSpec((1, TN_ADA), lambda j: (0, j))],
        out_specs=pl.BlockSpec((C_ROWS, 1, TN_ADA), lambda j: (0, 0, j)),
        out_shape=jax.ShapeDtypeStruct((C_ROWS, 1, n), F32),
        compiler_params=_params(("arbitrary",)),
        name="ada",
    )(c_all, w_ada, b_ada)


PROMPT_TOK_BLOCKS = N_PROMPT // TM_TOK


def _prompt_block(i):
    return jnp.minimum(i, PROMPT_TOK_BLOCKS - 1)


def _sample_block(i):
    return jnp.maximum(i - PROMPT_TOK_BLOCKS, 0)


def _norm_body(x_ref, sh_ref, sc_ref, g_ref, wg_ref, h_ref, zg_ref):
    x = x_ref[...]
    y = x * lax.rsqrt(jnp.mean(x * x, axis=-1, keepdims=True) + EPS) * g_ref[...]
    hb = (y * (1.0 + _mod_rows(sc_ref)) + _mod_rows(sh_ref)).astype(BF16)
    h_ref[...] = hb
    zg_ref[...] = _dot_nt(hb, wg_ref[...])


def _norm_kernel(xp_ref, xs_ref, shp_ref, scp_ref, shs_ref, scs_ref, g_ref, wg_ref, h_ref, zg_ref):
    i = pl.program_id(0)

    @pl.when(i < PROMPT_TOK_BLOCKS)
    def _():
        _norm_body(xp_ref, shp_ref, scp_ref, g_ref, wg_ref, h_ref, zg_ref)

    @pl.when(i >= PROMPT_TOK_BLOCKS)
    def _():
        _norm_body(xs_ref, shs_ref, scs_ref, g_ref, wg_ref, h_ref, zg_ref)


def _norm(xp, xs, mod_p, mod_s, g_mix, w_gate):
    per_batch = lambda col: pl.BlockSpec((TM_TOK // T_DEC, 1, D), lambda i: (_sample_block(i), 0, col))
    return pl.pallas_call(
        _norm_kernel,
        grid=(N_TOK // TM_TOK,),
        in_specs=[pl.BlockSpec((TM_TOK, D), lambda i: (_prompt_block(i), 0)),
                  pl.BlockSpec((TM_TOK, D), lambda i: (_sample_block(i), 0)),
                  pl.BlockSpec((1, D), lambda i: (0, 0)),
                  pl.BlockSpec((1, D), lambda i: (0, 1)),
                  per_batch(0), per_batch(1),
                  pl.BlockSpec((1, D), lambda i: (0, 0)),
                  pl.BlockSpec((LANES, D), lambda i: (0, 0))],
        out_specs=[pl.BlockSpec((TM_TOK, D), lambda i: (i, 0)),
                   pl.BlockSpec((TM_TOK, LANES), lambda i: (i, 0))],
        out_shape=[jax.ShapeDtypeStruct((N_TOK, D), BF16), jax.ShapeDtypeStruct((N_TOK, LANES), F32)],
        compiler_params=_params(("arbitrary",)),
        name="norm",
    )(xp, xs, mod_p, mod_p, mod_s, mod_s, g_mix, w_gate)


N_IN_CHUNKS = Z_WIDTH // TN_IN


def _inproj_kernel(h_ref, wt_hbm, z_ref, stage, w_bf, sem):
    j = pl.program_id(0)

    def fetch(chunk):
        rows = pl.ds(pl.multiple_of(chunk * TN_IN, TN_IN), TN_IN)
        return pltpu.make_async_copy(wt_hbm.at[rows], stage, sem.at[0])

    @pl.when(pl.program_id(1) == 0)
    def _():
        @pl.when(j == 0)
        def _():
            fetch(0).start()

        fetch(j).wait()
        w_bf[...] = stage[...].astype(BF16)

        @pl.when(j + 1 < N_IN_CHUNKS)
        def _():
            fetch(j + 1).start()

    z_ref[...] = _dot_nt(h_ref[...], w_bf[...])


def _inproj(h, w_in_t):
    return pl.pallas_call(
        _inproj_kernel,
        grid=(N_IN_CHUNKS, N_TOK // TM_IN),
        in_specs=[pl.BlockSpec((TM_IN, D), lambda j, i: (i, 0)),
                  pl.BlockSpec(memory_space=pl.ANY)],
        out_specs=pl.BlockSpec((TM_IN, TN_IN), lambda j, i: (i, j)),
        out_shape=jax.ShapeDtypeStruct((N_TOK, Z_WIDTH), F32),
        scratch_shapes=[pltpu.VMEM((TN_IN, D), F32), pltpu.VMEM((TN_IN, D), BF16), pltpu.SemaphoreType.DMA((1,))],
        compiler_params=_params(("arbitrary", "arbitrary")),
        name="inproj",
    )(h, w_in_t)


def _t5_bucket_np(dist):
    n = np.maximum(dist, 0)
    nf = np.maximum(n, 1).astype(np.float32)
    large = MAX_EXACT + (np.log(nf / MAX_EXACT) / math.log(MAX_DISTANCE / MAX_EXACT)
                         * (NUM_BUCKETS - MAX_EXACT)).astype(np.int32)
    large = np.minimum(large, NUM_BUCKETS - 1)
    return np.where(n < MAX_EXACT, n, large).astype(np.int32)


def _bucket_table(n_q, n_keys_valid, n_keys_padded):
    t = np.arange(n_q)[:, None]
    j = np.arange(n_keys_padded)[None, :]
    dist = t + WINDOW - j
    valid = (dist >= 0) & (dist < WINDOW) & (j < n_keys_valid)
    return np.where(valid, _t5_bucket_np(dist), -1).astype(np.int32)


def _fill_bias(bucket_ref, rb_ref, bias_scr, rows, per_head_table=False):
    for h in range(ATT_HEADS):
        bk = bucket_ref[h * rows:(h + 1) * rows, :] if per_head_table else bucket_ref[...]
        acc = jnp.full(bk.shape, NEG_INF, F32)
        for b in range(NUM_BUCKETS):
            acc = jnp.where(bk == b, rb_ref[b * ATT_HEADS + h], acc)
        bias_scr[h * rows:(h + 1) * rows, :] = acc


def _with_ones(v2):
    return jnp.concatenate([v2, jnp.ones_like(v2)], axis=1)


def _sink_softmax_av(lg, sink, v2_ones):
    m = jnp.maximum(jnp.max(lg, axis=-1, keepdims=True), sink)
    p = jnp.exp(lg - m).astype(BF16)
    pv = _dot(p, v2_ones)
    return pv[:, :HEAD_DIM] / (pv[:, HEAD_DIM:] + jnp.exp(sink - m))


def _swa_prompt_kernel(rb_ref, sink_ref, bucket_ref, q_ref, kp_ref, kc_ref, vp_ref, vc_ref,
                       o_ref, bias_scr):
    i = pl.program_id(0)

    @pl.when(i == 0)
    def _():
        _fill_bias(bucket_ref, rb_ref, bias_scr, ATT_BLOCK)

    first_prev = jnp.logical_and(i == 0, _iota((ATT_BLOCK, 2 * ATT_BLOCK), 1) < ATT_BLOCK)
    for g in range(KV_HEADS):
        ks = slice(g * HEAD_DIM, (g + 1) * HEAD_DIM)
        k2 = jnp.concatenate([kp_ref[:, ks], kc_ref[:, ks]], axis=0).astype(BF16)
        v2 = _with_ones(jnp.concatenate([vp_ref[:, ks], vc_ref[:, ks]], axis=0).astype(BF16))
        for r in range(GROUP):
            h = g * GROUP + r
            hs = slice(h * HEAD_DIM, (h + 1) * HEAD_DIM)
            lg = _dot_nt(q_ref[:, hs].astype(BF16), k2) * ATT_SCALE + bias_scr[h * ATT_BLOCK:(h + 1) * ATT_BLOCK, :]
            lg = jnp.where(first_prev, NEG_INF, lg)
            o_ref[:, hs] = _sink_softmax_av(lg, sink_ref[h], v2).astype(BF16)


def _swa_prompt(z, rb_flat, sinks):
    nb = N_PROMPT // ATT_BLOCK
    bucket = jnp.asarray(_bucket_table(ATT_BLOCK, 2 * ATT_BLOCK, 2 * ATT_BLOCK))
    kcol = ATT_WIDTH // KV_WIDTH
    prev = lambda i: jnp.maximum(i - 1, 0)
    smem = pl.BlockSpec(memory_space=pltpu.SMEM)
    return pl.pallas_call(
        _swa_prompt_kernel,
        grid=(nb,),
        in_specs=[smem, smem,
                  pl.BlockSpec((ATT_BLOCK, 2 * ATT_BLOCK), lambda i: (0, 0)),
                  pl.BlockSpec((ATT_BLOCK, ATT_WIDTH), lambda i: (i, 0)),
                  pl.BlockSpec((ATT_BLOCK, KV_WIDTH), lambda i: (prev(i), kcol)),
                  pl.BlockSpec((ATT_BLOCK, KV_WIDTH), lambda i: (i, kcol)),
                  pl.BlockSpec((ATT_BLOCK, KV_WIDTH), lambda i: (prev(i), kcol + 1)),
                  pl.BlockSpec((ATT_BLOCK, KV_WIDTH), lambda i: (i, kcol + 1))],
        out_specs=pl.BlockSpec((ATT_BLOCK, ATT_WIDTH), lambda i: (i, 0)),
        out_shape=jax.ShapeDtypeStruct((N_PROMPT, ATT_WIDTH), BF16),
        scratch_shapes=[pltpu.VMEM((ATT_HEADS * ATT_BLOCK, 2 * ATT_BLOCK), F32)],
        compiler_params=_params(("arbitrary",)),
        name="swa_prompt",
    )(rb_flat, sinks, bucket, z, z, z, z, z)


CACHE_ROWS = WINDOW * KV_HEADS
NEW_ROWS = T_DEC * KV_HEADS
S_KEYS = 5 * LANES


def _sample_bucket_table():
    t = np.arange(T_DEC)[:, None]
    col = np.arange(S_KEYS)[None, :]
    in_cache = col < CACHE_ROWS
    in_new = (col >= CACHE_ROWS) & (col < CACHE_ROWS + NEW_ROWS)
    key_head = np.where(in_cache, col % KV_HEADS, (col - CACHE_ROWS) // T_DEC)
    key_pos = np.where(in_cache, col // KV_HEADS, WINDOW + (col - CACHE_ROWS) % T_DEC)
    dist = t + WINDOW - key_pos
    valid = (dist >= 0) & (dist < WINDOW) & (in_cache | in_new)
    per_query = np.where(valid, _t5_bucket_np(dist), -1)
    heads = np.arange(ATT_HEADS)[:, None, None] // GROUP
    table = np.where(heads == key_head[None], per_query[None], -1)
    return table.reshape(ATT_HEADS * T_DEC, S_KEYS).astype(np.int32)


def _swa_sample_kernel(rb_ref, sink_ref, bucket_ref, q_ref, kn_ref, vn_ref, ck_ref, cv_ref,
                       o_ref, nk_ref, nv_ref, bias_scr):
    @pl.when(pl.program_id(0) == 0)
    def _():
        _fill_bias(bucket_ref, rb_ref, bias_scr, T_DEC, per_head_table=True)

    nk_ref[:, 0:CACHE_ROWS - NEW_ROWS, :] = ck_ref[:, NEW_ROWS:CACHE_ROWS, :]
    nv_ref[:, 0:CACHE_ROWS - NEW_ROWS, :] = cv_ref[:, NEW_ROWS:CACHE_ROWS, :]

    pad = jnp.zeros((S_KEYS - CACHE_ROWS - NEW_ROWS, HEAD_DIM), F32)
    sink_col = jnp.concatenate([jnp.full((T_DEC, 1), sink_ref[h], F32) for h in range(ATT_HEADS)], axis=0)
    bias = bias_scr[...]
    for b in range(SAMPLE_BT):
        ts = slice(b * T_DEC, (b + 1) * T_DEC)
        head_cols = lambda ref, n: [ref[ts, h * HEAD_DIM:(h + 1) * HEAD_DIM] for h in range(n)]
        k_new, v_new = head_cols(kn_ref, KV_HEADS), head_cols(vn_ref, KV_HEADS)
        for g in range(KV_HEADS):
            new_rows = pl.ds(CACHE_ROWS - NEW_ROWS + g, T_DEC, stride=KV_HEADS)
            nk_ref[b, new_rows, :] = k_new[g]
            nv_ref[b, new_rows, :] = v_new[g]
        qa = jnp.concatenate(head_cols(q_ref, ATT_HEADS), axis=0).astype(BF16)
        k2 = jnp.concatenate([ck_ref[b]] + k_new + [pad], axis=0).astype(BF16)
        v2 = _with_ones(jnp.concatenate([cv_ref[b]] + v_new + [pad], axis=0).astype(BF16))
        o = _sink_softmax_av(_dot_nt(qa, k2) * ATT_SCALE + bias, sink_col, v2)
        for h in range(ATT_HEADS):
            o_ref[ts, h * HEAD_DIM:(h + 1) * HEAD_DIM] = o[h * T_DEC:(h + 1) * T_DEC].astype(BF16)


def _swa_sample(z, cache_k, cache_v, rb_flat, sinks):
    rows = SAMPLE_BT * T_DEC
    z_off = N_PROMPT // rows
    bucket = jnp.asarray(_sample_bucket_table())
    kcol = ATT_WIDTH // KV_WIDTH
    smem = pl.BlockSpec(memory_space=pltpu.SMEM)
    cache_spec = pl.BlockSpec((SAMPLE_BT, CACHE_ROWS, HEAD_DIM), lambda i: (i, 0, 0))
    cache_shape = jax.ShapeDtypeStruct((N_BATCH, CACHE_ROWS, HEAD_DIM), F32)
    return pl.pallas_call(
        _swa_sample_kernel,
        grid=(N_BATCH // SAMPLE_BT,),
        in_specs=[smem, smem,
                  pl.BlockSpec((ATT_HEADS * T_DEC, S_KEYS), lambda i: (0, 0)),
                  pl.BlockSpec((rows, ATT_WIDTH), lambda i: (i + z_off, 0)),
                  pl.BlockSpec((rows, KV_WIDTH), lambda i: (i + z_off, kcol)),
                  pl.BlockSpec((rows, KV_WIDTH), lambda i: (i + z_off, kcol + 1)),
                  cache_spec, cache_spec],
        out_specs=[pl.BlockSpec((rows, ATT_WIDTH), lambda i: (i, 0)), cache_spec, cache_spec],
        out_shape=[jax.ShapeDtypeStruct((N_SAMPLE, ATT_WIDTH), BF16), cache_shape, cache_shape],
        scratch_shapes=[pltpu.VMEM((ATT_HEADS * T_DEC, S_KEYS), F32)],
        compiler_params=_params(("arbitrary",)),
        name="swa_sample",
    )(rb_flat, sinks, bucket, z, z, z, cache_k, cache_v)


LANE_IG, LANE_LF, LANE_B = 0, M_HEADS, 2 * M_HEADS


def _gate_table(zg, gate_bias, seg_len):
    L = zg.shape[0]
    g = zg + gate_bias
    lane = _iota((L, LANES), 1)
    lf = _log_sigmoid(g)
    lf_only = jnp.where(jnp.logical_and(lane >= LANE_LF, lane < LANE_B), lf, 0.0)
    row = _iota((L, L), 0)
    col = _iota((L, L), 1)
    same_seg = (row // seg_len) == (col // seg_len)
    tril = jnp.where(jnp.logical_and(col <= row, same_seg), 1.0, 0.0).astype(BF16)
    cum = pltpu.roll(_dot_exact_lhs01(tril, lf_only), M_HEADS, axis=1)
    table = jnp.where(lane < LANE_LF, g, jnp.where(lane < LANE_B, lf, jnp.where(lane < LANE_B + M_HEADS, cum, 0.0)))
    return table, jnp.logical_and(col <= row, same_seg)


def _mlstm_decay(table, table_t, mask, m0_col, h):
    b_c = table[:, LANE_B + h:LANE_B + h + 1]
    b_r = table_t[LANE_B + h:LANE_B + h + 1, :]
    ig_r = table_t[LANE_IG + h:LANE_IG + h + 1, :]
    log_d = jnp.where(mask, b_c - b_r + ig_r, NEG_INF)
    log_inter = b_c + m0_col
    m_t = jnp.maximum(log_inter, jnp.max(log_d, axis=1, keepdims=True))
    return b_c, m_t, jnp.exp(log_inter - m_t), jnp.exp(log_d - m_t)


def _mlstm_intra(table, table_t, mask, m0_col, h, q, k, v):
    b_c, m_t, w_inter, d = _mlstm_decay(table, table_t, mask, m0_col, h)
    s = _dot_nt(q.astype(BF16), k.astype(BF16)) * d
    num_intra = _dot(s.astype(BF16), v.astype(BF16))
    den_intra = jnp.sum(s, axis=1, keepdims=True)
    return b_c, m_t, w_inter, num_intra, den_intra


ST_WIDTH = DV + LANES


def _mlstm_prompt_kernel(gb_ref, zg_ref, q_ref, k_ref, va_ref, vb_ref, oa_ref, ob_ref,
                         y_ref, c_out, n_out, m_out, st_scr, m_scr):
    step = pl.program_id(0)
    L = ML_CHUNK

    @pl.when(step == 0)
    def _():
        st_scr[...] = jnp.zeros_like(st_scr)
        m_scr[...] = jnp.zeros_like(m_scr)

    table, mask = _gate_table(zg_ref[...], gb_ref[...], L)
    table_t = table.T
    ones = jnp.ones((L, LANES), F32)
    half = M_HEADS // 2
    for h in range(M_HEADS):
        v_ref, o_ref = (va_ref, oa_ref) if h < half else (vb_ref, ob_ref)
        vs = slice((h % half) * DV, (h % half + 1) * DV)
        qb = (q_ref[:, h * DK:(h + 1) * DK] * Q_SCALE).astype(BF16)
        kb = k_ref[:, h * DK:(h + 1) * DK].astype(BF16)
        v1 = jnp.concatenate([v_ref[:, vs], ones], axis=1)
        m0 = m_scr[h:h + 1, 0:1]
        b_c, m_t, w_inter, d = _mlstm_decay(table, table_t, mask, m0, h)
        s = _dot_nt(qb, kb) * d
        st = st_scr[h]
        tot = _dot(s.astype(BF16), v1.astype(BF16)) + w_inter * _dot(qb, st.astype(BF16))
        den = tot[:, DV:]
        hh = tot[:, :DV] / jnp.maximum(jnp.abs(jnp.concatenate([den, den], axis=1)), jnp.exp(-m_t))
        y_ref[:, h * DV:(h + 1) * DV] = (_sigmoid(o_ref[:, vs]) * hh).astype(BF16)

        ig_c = table[:, LANE_IG + h:LANE_IG + h + 1]
        m_new = m_t[L - 1:L, :]
        b_last = b_c[L - 1:L, :]
        w_s = jnp.exp(b_last - b_c + ig_c - m_new)
        decay = jnp.exp(b_last + m0 - m_new)
        st_scr[h] = decay * st + _dot_tn(kb, (w_s * v1).astype(BF16))
        m_scr[h:h + 1, :] = jnp.broadcast_to(m_new, (1, LANES))

    @pl.when(step == pl.num_programs(0) - 1)
    def _():
        for h in range(M_HEADS):
            st_t = st_scr[h].T
            c_out[h] = st_t[:DV, :]
            n_out[h:h + 1, :] = st_t[DV:DV + 1, :]
        m_out[...] = m_scr[...]


def _mlstm_prompt(z, zg, gate_bias):
    L = ML_CHUNK
    blk = M_HEADS * DK
    col = lambda c: pl.BlockSpec((L, blk), lambda i: (i, c))
    const = lambda shape: pl.BlockSpec(shape, lambda i: tuple(0 for _ in shape))
    return pl.pallas_call(
        _mlstm_prompt_kernel,
        grid=(N_PROMPT // L,),
        in_specs=[const((1, LANES)),
                  pl.BlockSpec((L, LANES), lambda i: (i, 0)),
                  col(3), col(4), col(5), col(6), col(7), col(8)],
        out_specs=[pl.BlockSpec((L, M_WIDTH), lambda i: (i, 0)),
                   const((M_HEADS, DV, DK)), const((M_HEADS, DK)), const((M_HEADS, LANES))],
        out_shape=[jax.ShapeDtypeStruct((N_PROMPT, M_WIDTH), BF16),
                   jax.ShapeDtypeStruct((M_HEADS, DV, DK), F32),
                   jax.ShapeDtypeStruct((M_HEADS, DK), F32),
                   jax.ShapeDtypeStruct((M_HEADS, LANES), F32)],
        scratch_shapes=[pltpu.VMEM((M_HEADS, DK, ST_WIDTH), F32),
                        pltpu.VMEM((M_HEADS, LANES), F32)],
        compiler_params=_params(("arbitrary",)),
        name="mlstm_prompt",
    )(gate_bias, zg, z, z, z, z, z, z)


S_ROWS = SAMPLE_BT * T_DEC


def _mlstm_sample_kernel(gb_ref, zg_ref, m0_ref, n0t_ref, q_ref, k_ref, va_ref, vb_ref, oa_ref, ob_ref,
                         c0_ref, n0_ref, y_ref, c_out, n_out, m_out):
    L = S_ROWS
    table, mask = _gate_table(zg_ref[...], gb_ref[...], T_DEC)
    table_t = jnp.concatenate([table, jnp.zeros((LANES - L, LANES), F32)], axis=0).T[:, 0:L]
    m0_all = m0_ref[...]
    lane = _iota((L, LANES), 1)
    row_b = _iota((SAMPLE_BT, L), 1) // T_DEC
    seg_sum = jnp.where(row_b == _iota((SAMPLE_BT, L), 0), 1.0, 0.0)
    m_tok = jnp.zeros((L, LANES), F32)
    half = M_HEADS // 2
    for h in range(M_HEADS):
        v_ref, o_ref = (va_ref, oa_ref) if h < half else (vb_ref, ob_ref)
        vs = slice((h % half) * DV, (h % half + 1) * DV)
        q = q_ref[:, h * DK:(h + 1) * DK] * Q_SCALE
        k = k_ref[:, h * DK:(h + 1) * DK]
        v = v_ref[:, vs]
        m0 = m0_all[:, h:h + 1]
        b_c, m_t, w_inter, num_intra, den_intra = _mlstm_intra(table, table_t, mask, m0, h, q, k, v)
        qb = q.astype(BF16)
        num_inter = jnp.concatenate(
            [_dot_nt(qb[b * T_DEC:(b + 1) * T_DEC], c0_ref[b, h].astype(BF16)) for b in range(SAMPLE_BT)], axis=0)
        num = num_intra + w_inter * num_inter
        den = den_intra + w_inter * jnp.sum(q * n0t_ref[:, h * DK:(h + 1) * DK], axis=1, keepdims=True)
        hh = num / jnp.maximum(jnp.abs(den), jnp.exp(-m_t))
        y_ref[:, h * DV:(h + 1) * DV] = (_sigmoid(o_ref[:, vs]) * hh).astype(BF16)
        m_tok = jnp.where(lane == h, m_t, m_tok)

        def last_tok(x):
            x3 = x.reshape(SAMPLE_BT, T_DEC, 1)
            return jnp.broadcast_to(x3[:, T_DEC - 1:T_DEC, :], x3.shape).reshape(L, 1)

        ig_c = table[:, LANE_IG + h:LANE_IG + h + 1]
        m_new = last_tok(m_t)
        b_last = last_tok(b_c)
        w_s = jnp.exp(b_last - b_c + ig_c - m_new)
        decay = jnp.exp(b_last + m0 - m_new)
        wv = (w_s * v).astype(BF16)
        kb = k.astype(BF16)
        rowsel = _iota((L, 1), 0) // T_DEC
        for b in range(SAMPLE_BT):
            dec_b = decay[b * T_DEC + T_DEC - 1:(b + 1) * T_DEC, :]
            wv_b = jnp.where(rowsel == b, wv, jnp.zeros_like(wv))
            c_out[b, h] = dec_b * c0_ref[b, h] + _dot_tn(wv_b, kb)
        dec_rows = decay.reshape(SAMPLE_BT, T_DEC, 1)[:, T_DEC - 1, :]
        n_out[:, h * DK:(h + 1) * DK] = dec_rows * n0_ref[:, h * DK:(h + 1) * DK] + jnp.dot(
            seg_sum, w_s * k, preferred_element_type=F32, precision=lax.Precision.HIGHEST)
    m_out[...] = m_tok


def _mlstm_sample(z, zg, gate_bias, m0_tok, n0_tok, state_c, state_n):
    L = S_ROWS
    blk = M_HEADS * DK
    z_off = N_PROMPT // L
    col = lambda c: pl.BlockSpec((L, blk), lambda i: (i + z_off, c))
    c_spec = pl.BlockSpec((SAMPLE_BT, M_HEADS, DV, DK), lambda i: (i, 0, 0, 0))
    n_spec = pl.BlockSpec((SAMPLE_BT, blk), lambda i: (i, 0))
    return pl.pallas_call(
        _mlstm_sample_kernel,
        grid=(N_BATCH // SAMPLE_BT,),
        in_specs=[pl.BlockSpec((1, LANES), lambda i: (0, 0)),
                  pl.BlockSpec((L, LANES), lambda i: (i + z_off, 0)),
                  pl.BlockSpec((L, LANES), lambda i: (i, 0)),
                  pl.BlockSpec((L, blk), lambda i: (i, 0)),
                  col(3), col(4), col(5), col(6), col(7), col(8),
                  c_spec, n_spec],
        out_specs=[pl.BlockSpec((L, M_WIDTH), lambda i: (i, 0)), c_spec, n_spec,
                   pl.BlockSpec((L, LANES), lambda i: (i, 0))],
        out_shape=[jax.ShapeDtypeStruct((N_SAMPLE, M_WIDTH), BF16),
                   jax.ShapeDtypeStruct((N_BATCH, M_HEADS, DV, DK), F32),
                   jax.ShapeDtypeStruct((N_BATCH, blk), F32),
                   jax.ShapeDtypeStruct((N_SAMPLE, LANES), F32)],
        compiler_params=_params(("arbitrary",)),
        name="mlstm_sample",
    )(gate_bias, zg, m0_tok, n0_tok, z, z, z, z, z, z, state_c, state_n)


N_OUT_CHUNKS = D // TN_OUT


def _outproj_kernel(ya_ref, ym_ref, w_hbm, x_ref, gt_ref, o_ref, stage, w_bf, sem):
    j = pl.program_id(0)

    def fetch(chunk):
        cols = pl.ds(pl.multiple_of(chunk * TN_OUT, TN_OUT), TN_OUT)
        return pltpu.make_async_copy(w_hbm.at[:, cols], stage, sem.at[0])

    @pl.when(pl.program_id(1) == 0)
    def _():
        @pl.when(j == 0)
        def _():
            fetch(0).start()

        fetch(j).wait()
        w_bf[...] = stage[...].astype(BF16)

        @pl.when(j + 1 < N_OUT_CHUNKS)
        def _():
            fetch(j + 1).start()

    mix = _dot(ya_ref[...], w_bf[0:ATT_WIDTH, :]) + _dot(ym_ref[...], w_bf[ATT_WIDTH:, :])
    o_ref[...] = x_ref[...] + _mod_rows(gt_ref) * mix


def _outproj(ya, ym, w_out, x, mod, per_row):
    n = x.shape[0]
    TM = TM_PROJ
    gate1_col = 2 * N_OUT_CHUNKS
    if per_row:
        gate_spec = pl.BlockSpec((TM // T_DEC, 1, TN_OUT), lambda j, i: (i, 0, gate1_col + j))
    else:
        gate_spec = pl.BlockSpec((1, TN_OUT), lambda j, i: (0, gate1_col + j))
    return pl.pallas_call(
        _outproj_kernel,
        grid=(N_OUT_CHUNKS, n // TM),
        in_specs=[pl.BlockSpec((TM, ATT_WIDTH), lambda j, i: (i, 0)),
                  pl.BlockSpec((TM, M_WIDTH), lambda j, i: (i, 0)),
                  pl.BlockSpec(memory_space=pl.ANY),
                  pl.BlockSpec((TM, TN_OUT), lambda j, i: (i, j)),
                  gate_spec],
        out_specs=pl.BlockSpec((TM, TN_OUT), lambda j, i: (i, j)),
        out_shape=jax.ShapeDtypeStruct((n, D), F32),
        scratch_shapes=[pltpu.VMEM((D, TN_OUT), F32), pltpu.VMEM((D, TN_OUT), BF16), pltpu.SemaphoreType.DMA((1,))],
        compiler_params=_params(("arbitrary", "arbitrary")),
        name="outproj",
    )(ya, ym, w_out, x, mod)


def _router_kernel(x_ref, sh_ref, sc_ref, g_ref, wr_ref, br_ref, h_ref, route_ref):
    x = x_ref[...]
    y = x * lax.rsqrt(jnp.mean(x * x, axis=-1, keepdims=True) + EPS) * g_ref[...]
    h2 = y * (1.0 + _mod_rows(sc_ref)) + _mod_rows(sh_ref)
    h_hi = h2.astype(BF16)
    h_ref[...] = _pack_bf16_pairs(h2)
    h_lo = (h2 - h_hi.astype(F32)).astype(BF16)
    logits = _dot(h_hi, wr_ref[0]) + (_dot(h_hi, wr_ref[1]) + _dot(h_lo, wr_ref[0])) + br_ref[...]
    lane = _iota(logits.shape, 1)

    def first_max(vals):
        vmax = jnp.max(vals, axis=1, keepdims=True)
        idx = jnp.min(jnp.where(vals == vmax, lane, LANES), axis=1, keepdims=True)
        return vmax, idx

    gl = jnp.where(lane < N_GROUPS, logits, NEG_INF)
    gmax, grp = first_max(gl)
    p_grp = 1.0 / jnp.sum(jnp.exp(gl - gmax), axis=1, keepdims=True)
    e_lane = lane - N_GROUPS
    in_grp = jnp.logical_and(e_lane >= 0, jnp.logical_and(e_lane < N_EXPERTS, e_lane // EXP_PER_GROUP == grp))
    el = jnp.where(in_grp, logits, NEG_INF)
    v1, i1 = first_max(el)
    v2, i2 = first_max(jnp.where(lane == i1, NEG_INF, el))
    e2w = jnp.exp(v2 - v1)
    w1 = 1.0 / (1.0 + e2w)
    w2 = e2w / (1.0 + e2w)
    route = jnp.where(lane == 0, (i1 - N_GROUPS).astype(F32),
                      jnp.where(lane == 1, (i2 - N_GROUPS).astype(F32),
                                jnp.where(lane == 2, p_grp * w1, jnp.where(lane == 3, p_grp * w2, 0.0))))
    route_ref[...] = route


def _router_merged_kernel(xp_ref, xs_ref, shp_ref, scp_ref, shs_ref, scs_ref, g_ref, wr_ref, br_ref,
                          h_ref, route_ref):
    i = pl.program_id(0)

    @pl.when(i < PROMPT_TOK_BLOCKS)
    def _():
        _router_kernel(xp_ref, shp_ref, scp_ref, g_ref, wr_ref, br_ref, h_ref, route_ref)

    @pl.when(i >= PROMPT_TOK_BLOCKS)
    def _():
        _router_kernel(xs_ref, shs_ref, scs_ref, g_ref, wr_ref, br_ref, h_ref, route_ref)


def _router(x1_p, x1_s, mod_p, mod_s, g_ffn, w_route, b_route):
    pi, si = _prompt_block, _sample_block
    return pl.pallas_call(
        _router_merged_kernel,
        grid=(N_TOK // TM_TOK,),
        in_specs=[pl.BlockSpec((TM_TOK, D), lambda i: (pi(i), 0)),
                  pl.BlockSpec((TM_TOK, D), lambda i: (si(i), 0)),
                  pl.BlockSpec((1, D), lambda i: (0, 3)),
                  pl.BlockSpec((1, D), lambda i: (0, 4)),
                  pl.BlockSpec((TM_TOK // T_DEC, 1, D), lambda i: (si(i), 0, 3)),
                  pl.BlockSpec((TM_TOK // T_DEC, 1, D), lambda i: (si(i), 0, 4)),
                  pl.BlockSpec((1, D), lambda i: (0, 0)),
                  pl.BlockSpec((2, D, LANES), lambda i: (0, 0, 0)),
                  pl.BlockSpec((1, LANES), lambda i: (0, 0))],
        out_specs=[pl.BlockSpec((TM_TOK, D_PACK), lambda i: (i, 0)),
                   pl.BlockSpec((TM_TOK, LANES), lambda i: (i, 0))],
        out_shape=[jax.ShapeDtypeStruct((N_TOK, D_PACK), jnp.uint32),
                   jax.ShapeDtypeStruct((N_TOK, LANES), F32)],
        compiler_params=_params(("arbitrary",)),
        name="router",
    )(x1_p, x1_s, mod_p, mod_p, mod_s, mod_s, g_ffn, w_route, b_route)


def _rank_kernel(route_ref, dest_ref, cnt_ref, carry, rank_scr):
    pas = pl.program_id(0)
    step = pl.program_id(1)

    @pl.when(jnp.logical_and(pas == 0, step == 0))
    def _():
        carry[...] = jnp.zeros_like(carry)

    route = route_ref[...]
    n = route.shape[0]
    lane = _iota((n, LANES), 1).astype(F32)
    o1 = jnp.where(lane == route[:, 0:1], 1.0, 0.0)
    o2 = jnp.where(lane == route[:, 1:2], 1.0, 0.0)
    lane_i = _iota((n, LANES), 1)
    rows = pl.ds(pl.multiple_of(step * TM_RANK, TM_RANK), TM_RANK)

    @pl.when(pas == 0)
    def _():
        both = o1 + o2
        strict = jnp.where(_iota((n, n), 1) < _iota((n, n), 0), 1.0, 0.0).astype(BF16)
        prior = _dot(strict, both.astype(BF16)) + carry[0:1, :]
        r1 = jnp.sum(o1 * prior, axis=1, keepdims=True)
        r2 = jnp.sum(o2 * prior, axis=1, keepdims=True)
        rank_scr[rows, :] = jnp.where(lane_i == 0, r1, jnp.where(lane_i == 1, r2, 0.0))
        carry[...] = carry[...] + jnp.sum(both, axis=0, keepdims=True)

    @pl.when(pas == 1)
    def _():
        counts = carry[...]
        tiles_per = jnp.floor((counts + (TM_MOE - 1)) * (1.0 / TM_MOE))
        before = jnp.where(_iota((LANES, LANES), 0) < _iota((LANES, LANES), 1), 1.0, 0.0).astype(BF16)
        pad_start = _dot(tiles_per.astype(BF16), before)[0:1, :] * TM_MOE
        rank = rank_scr[rows, :]
        d1 = jnp.sum(o1 * pad_start, axis=1, keepdims=True) + rank[:, 0:1]
        d2 = jnp.sum(o2 * pad_start, axis=1, keepdims=True) + rank[:, 1:2]
        dest_ref[...] = jnp.where(lane_i == 0, d1, jnp.where(lane_i == 1, d2, 0.0)).astype(jnp.int32)
        cnt_ref[...] = counts


def _rank(route):
    n = route.shape[0]
    return pl.pallas_call(
        _rank_kernel,
        grid=(2, n // TM_RANK),
        in_specs=[pl.BlockSpec((TM_RANK, LANES), lambda p, i: (i, 0))],
        out_specs=[pl.BlockSpec((TM_RANK, LANES), lambda p, i: (i * p, 0)),
                   pl.BlockSpec((SUBLANES, LANES), lambda p, i: (0, 0))],
        out_shape=[jax.ShapeDtypeStruct((n, LANES), jnp.int32), jax.ShapeDtypeStruct((SUBLANES, LANES), F32)],
        scratch_shapes=[pltpu.VMEM((SUBLANES, LANES), F32), pltpu.VMEM((n, LANES), F32)],
        compiler_params=_params(("arbitrary", "arbitrary")),
        name="rank",
    )(route)


def _row_copy(src, dst, sem):
    return pltpu.make_async_copy(src, dst, sem)


ROW_UNROLL = 8
assert T_DEC == SUBLANES
D_PACK = D // 2

ZERO_FIRST, ZERO_ANYTIME = 1, 2


def _dispatch_kernel(d1_ref, d2_ref, zc_ref, h_ref, xs_out, zbuf, sem):
    step = pl.program_id(0)
    base = step * TM_TOK

    def zero_tile(t, s):
        rows = pl.ds(pl.multiple_of(t * TM_MOE, TM_MOE), TM_MOE)
        return pltpu.make_async_copy(zbuf, xs_out.at[rows], sem.at[s])

    def for_tiles(cls, s, act):
        def body(t, carry):
            @pl.when(zc_ref[t] == cls)
            def _():
                act(zero_tile(t, s))
            return carry
        lax.fori_loop(0, MAX_TILES, body, 0)

    @pl.when(step == 0)
    def _():
        zbuf[...] = jnp.zeros_like(zbuf)
        for_tiles(ZERO_FIRST, 1, lambda c: c.start())
        for_tiles(ZERO_ANYTIME, 2, lambda c: c.start())
        for_tiles(ZERO_FIRST, 1, lambda c: c.wait())

    @pl.when(step == pl.num_programs(0) - 1)
    def _():
        for_tiles(ZERO_ANYTIME, 2, lambda c: c.wait())

    def issue(r, carry):
        _row_copy(h_ref.at[pl.ds(r, 1)], xs_out.at[pl.ds(d1_ref[base + r], 1)], sem.at[0]).start()
        _row_copy(h_ref.at[pl.ds(r, 1)], xs_out.at[pl.ds(d2_ref[base + r], 1)], sem.at[0]).start()
        return carry

    lax.fori_loop(0, TM_TOK, issue, 0, unroll=ROW_UNROLL)
    for _ in range(TOP_K):
        _row_copy(h_ref, xs_out.at[pl.ds(0, TM_TOK)], sem.at[0]).wait()


def _dispatch(dests, zero_class, h2):
    grid_spec = pltpu.PrefetchScalarGridSpec(
        num_scalar_prefetch=3,
        grid=(N_TOK // TM_TOK,),
        in_specs=[pl.BlockSpec((TM_TOK, D_PACK), lambda i, *_: (i, 0))],
        out_specs=pl.BlockSpec(memory_space=pl.ANY),
        scratch_shapes=[pltpu.VMEM((TM_MOE, D_PACK), jnp.uint32), pltpu.SemaphoreType.DMA((3,))],
    )
    return pl.pallas_call(
        _dispatch_kernel,
        grid_spec=grid_spec,
        out_shape=jax.ShapeDtypeStruct((A_PAD, D_PACK), jnp.uint32),
        compiler_params=_params(("arbitrary",)),
        name="dispatch",
    )(*dests, zero_class, h2)


N_UP_CHUNKS = D_FF // TN_FF
ROW_CAPS = tuple(range(TM_MOE // 4, TM_MOE + 1, TM_MOE // 4))


def _for_row_cap(rows, body):
    lo = 0
    for cap in ROW_CAPS:
        pl.when(jnp.logical_and(rows > lo, rows <= cap))(functools.partial(body, cap))
        lo = cap


def _expert_up_kernel(te_ref, rows_ref, first_ref, nxt_ref, run_ref, meta_ref, x_ref, wg_hbm, wu_hbm,
                      o_ref, wbuf, sem):
    n = pl.program_id(0)
    t = pl.program_id(1)
    slot = lax.rem(n * meta_ref[1] + run_ref[t], 2)

    def fetch(e, chunk, s):
        cols = pl.ds(pl.multiple_of(chunk * TN_FF, TN_FF), TN_FF)
        top, bottom = pl.ds(0, D // 2), pl.ds(D // 2, D // 2)
        return (pltpu.make_async_copy(wg_hbm.at[e, top, cols], wbuf.at[s, 0, top], sem.at[s, 0]),
                pltpu.make_async_copy(wg_hbm.at[e, bottom, cols], wbuf.at[s, 0, bottom], sem.at[s, 1]),
                pltpu.make_async_copy(wu_hbm.at[e, :, cols], wbuf.at[s, 1], sem.at[s, 2]))

    def start(copies):
        for priority, c in zip((0, 1, 1), copies):
            c.start(priority=priority)

    @pl.when(first_ref[t] == 1)
    def _():
        @pl.when(jnp.logical_and(n == 0, t == 0))
        def _():
            start(fetch(te_ref[0], 0, 0))

        for c in fetch(te_ref[t], n, slot):
            c.wait()

        @pl.when(nxt_ref[t] >= 0)
        def _():
            start(fetch(nxt_ref[t], n, 1 - slot))

        @pl.when(jnp.logical_and(nxt_ref[t] < 0, n + 1 < N_UP_CHUNKS))
        def _():
            start(fetch(te_ref[0], n + 1, 1 - slot))

    def compute(cap):
        x = jnp.concatenate([half.astype(BF16) for half in _unpack_bf16_pairs(x_ref[0:cap, :])], axis=1)
        g = _dot(x, wbuf[slot, 0].astype(BF16))
        u = _dot(x, wbuf[slot, 1].astype(BF16))
        o_ref[0:cap, :] = (g * _sigmoid(g) * u).astype(BF16)
        if cap < TM_MOE:
            o_ref[cap:, :] = jnp.zeros((TM_MOE - cap, TN_FF), BF16)

    _for_row_cap(rows_ref[t], compute)

    @pl.when(rows_ref[t] == 0)
    def _():
        o_ref[...] = jnp.zeros_like(o_ref)


def _expert_up(tiles, xs, w_gate, w_up):
    last = lambda t, meta: jnp.minimum(t, meta[0] - 1)
    grid_spec = pltpu.PrefetchScalarGridSpec(
        num_scalar_prefetch=6,
        grid=(N_UP_CHUNKS, MAX_TILES),
        in_specs=[pl.BlockSpec((TM_MOE, D_PACK), lambda n, t, *s: (last(t, s[5]), 0)),
                  pl.BlockSpec(memory_space=pl.ANY),
                  pl.BlockSpec(memory_space=pl.ANY)],
        out_specs=pl.BlockSpec((TM_MOE, TN_FF), lambda n, t, *s: (t, n)),
        scratch_shapes=[pltpu.VMEM((2, 2, D, TN_FF), F32), pltpu.SemaphoreType.DMA((2, 3))],
    )
    return pl.pallas_call(
        _expert_up_kernel,
        grid_spec=grid_spec,
        out_shape=jax.ShapeDtypeStruct((A_PAD, D_FF), BF16),
        compiler_params=_params(("arbitrary", "arbitrary")),
        name="expert_up",
    )(*tiles, xs, w_gate, w_up)


def _expert_down_kernel(te_ref, rows_ref, first_ref, nxt_ref, run_ref, meta_ref, h_ref, wd_hbm,
                        o_ref, wbuf, sem):
    t = pl.program_id(0)
    slot = lax.rem(run_ref[t], 2)

    def fetch(e, s):
        parts = [pl.ds(0, D_FF // 4), pl.ds(D_FF // 4, 3 * D_FF // 4)]
        return [pltpu.make_async_copy(wd_hbm.at[e, rows], wbuf.at[s, rows], sem.at[s, k])
                for k, rows in enumerate(parts)]

    def start(copies):
        for priority, c in enumerate(copies):
            c.start(priority=priority)

    @pl.when(first_ref[t] == 1)
    def _():
        @pl.when(t == 0)
        def _():
            start(fetch(te_ref[0], 0))

        for c in fetch(te_ref[t], slot):
            c.wait()

        @pl.when(nxt_ref[t] >= 0)
        def _():
            start(fetch(nxt_ref[t], 1 - slot))

    def compute(cap):
        o_ref[0:cap, :] = _pack_bf16_pairs(_dot(h_ref[0:cap, :], wbuf[slot].astype(BF16)))
        if cap < TM_MOE:
            o_ref[cap:, :] = jnp.zeros((TM_MOE - cap, D_PACK), jnp.uint32)

    _for_row_cap(rows_ref[t], compute)

    @pl.when(rows_ref[t] == 0)
    def _():
        o_ref[...] = jnp.zeros_like(o_ref)


def _expert_down(tiles, h1, w_down):
    last = lambda t, meta: jnp.minimum(t, meta[0] - 1)
    grid_spec = pltpu.PrefetchScalarGridSpec(
        num_scalar_prefetch=6,
        grid=(MAX_TILES,),
        in_specs=[pl.BlockSpec((TM_MOE, D_FF), lambda t, *s: (last(t, s[5]), 0)),
                  pl.BlockSpec(memory_space=pl.ANY)],
        out_specs=pl.BlockSpec((TM_MOE, D_PACK), lambda t, *s: (t, 0)),
        scratch_shapes=[pltpu.VMEM((2, D_FF, D), F32), pltpu.SemaphoreType.DMA((2, 2))],
    )
    return pl.pallas_call(
        _expert_down_kernel,
        grid_spec=grid_spec,
        out_shape=jax.ShapeDtypeStruct((A_PAD, D_PACK), jnp.uint32),
        compiler_params=_params(("arbitrary",)),
        name="expert_down",
    )(*tiles, h1, w_down)


def _combine_kernel(d1_ref, d2_ref, x_ref, route_ref, gt_ref, gf_ref, o_hbm,
                    y_ref, buf, sem, *, tok_offset):
    step = pl.program_id(0)
    slot = lax.rem(step, 2)

    def gather(s, into):
        base = tok_offset + s * TM_TOK

        def issue(r, carry):
            _row_copy(o_hbm.at[pl.ds(d1_ref[base + r], 1)], buf.at[into, 0, pl.ds(r, 1)], sem.at[into]).start()
            _row_copy(o_hbm.at[pl.ds(d2_ref[base + r], 1)], buf.at[into, 1, pl.ds(r, 1)], sem.at[into]).start()
            return carry

        lax.fori_loop(0, TM_TOK, issue, 0, unroll=ROW_UNROLL)

    @pl.when(step == 0)
    def _():
        gather(0, 0)

    @pl.when(step + 1 < pl.num_programs(0))
    def _():
        gather(step + 1, 1 - slot)

    for k in range(TOP_K):
        _row_copy(o_hbm.at[pl.ds(0, TM_TOK)], buf.at[slot, k], sem.at[slot]).wait()

    route = route_ref[...]
    lo1, hi1 = _unpack_bf16_pairs(buf[slot, 0])
    lo2, hi2 = _unpack_bf16_pairs(buf[slot, 1])
    g1, g2 = route[:, 2:3], route[:, 3:4]
    moe = jnp.concatenate([g1 * lo1 + g2 * lo2, g1 * hi1 + g2 * hi2], axis=1)
    x2 = x_ref[...] + _mod_rows(gt_ref) * moe
    y_ref[...] = x2 * lax.rsqrt(jnp.mean(x2 * x2, axis=-1, keepdims=True) + EPS) * gf_ref[...]


def _combine(dests, x1, route, mod, per_row, g_final, o_rows, tok_offset):
    n = x1.shape[0]
    off = tok_offset // TM_TOK
    grid_spec = pltpu.PrefetchScalarGridSpec(
        num_scalar_prefetch=2,
        grid=(n // TM_TOK,),
        in_specs=[pl.BlockSpec((TM_TOK, D), lambda i, *_: (i, 0)),
                  pl.BlockSpec((TM_TOK, LANES), lambda i, *_: (i + off, 0)),
                  _mod_spec(per_row, TM_TOK, D, lambda *_: 5),
                  pl.BlockSpec((1, D), lambda i, *_: (0, 0)),
                  pl.BlockSpec(memory_space=pl.ANY)],
        out_specs=pl.BlockSpec((TM_TOK, D), lambda i, *_: (i, 0)),
        scratch_shapes=[pltpu.VMEM((2, TOP_K, TM_TOK, D_PACK), jnp.uint32), pltpu.SemaphoreType.DMA((2,))],
    )
    return pl.pallas_call(
        functools.partial(_combine_kernel, tok_offset=tok_offset),
        grid_spec=grid_spec,
        out_shape=jax.ShapeDtypeStruct((n, D), F32),
        compiler_params=_params(("arbitrary",)),
        name="combine",
    )(*dests, x1, route, mod, g_final, o_rows)


def kernel(x_prompt, x_sample, cache_k, cache_v, state_C, state_n, state_m, c_prompt, c_sample, rel_bias, w_ada, b_ada, g_mix, g_ffn, w_in, sinks, b_igate, b_fgate, w_out, w_router_grp, b_router_grp, w_router_exp, b_router_exp, w_gate, w_up, w_down, g_final):
    xp = x_prompt.reshape(N_PROMPT, D)
    xs = x_sample.reshape(N_SAMPLE, D)

    c_all = jnp.concatenate([c_sample, c_prompt, jnp.zeros((C_ROWS - 1 - N_BATCH, D), F32)], axis=0)
    mod_s = _ada(c_all, w_ada[0], b_ada)
    mod_p = mod_s[N_BATCH]

    w_in_t = jnp.swapaxes(w_in[0], 0, 1)
    w_gates_t = jnp.pad(w_in_t[Z_WIDTH:].astype(BF16), ((0, LANES - 2 * M_HEADS), (0, 0)))
    h_all, zg = _norm(xp, xs, mod_p, mod_s, g_mix, w_gates_t)
    z = _inproj(h_all, w_in_t)

    rb_flat = rel_bias.reshape(NUM_BUCKETS * ATT_HEADS)
    sink_v = sinks[0]
    ya_p = _swa_prompt(z, rb_flat, sink_v)
    ya_s, nk_s, nv_s = _swa_sample(z,cache_k.reshape(N_BATCH, CACHE_ROWS, HEAD_DIM),
                                   cache_v.reshape(N_BATCH, CACHE_ROWS, HEAD_DIM), rb_flat, sink_v)

    gate_bias = jnp.concatenate([b_igate[0], b_fgate[0], jnp.zeros((LANES - 2 * M_HEADS,), F32)]).reshape(1, LANES)
    ym_p, c_p, n_p, m_p = _mlstm_prompt(z, zg, gate_bias)
    m0_tok = jnp.pad(jnp.repeat(state_m[0], T_DEC, axis=0), ((0, 0), (0, LANES - M_HEADS)))
    n0_flat = state_n[0].reshape(N_BATCH, M_HEADS * DK)
    n0_tok = jnp.repeat(n0_flat, T_DEC, axis=0)
    ym_s, c_s, n_s, m_s = _mlstm_sample(z, zg, gate_bias, m0_tok, n0_tok, state_C[0], n0_flat)

    x1_p = _outproj(ya_p, ym_p, w_out[0], xp, mod_p, False)
    x1_s = _outproj(ya_s, ym_s, w_out[0], xs, mod_s, True)

    w_route = jnp.pad(jnp.concatenate([w_router_grp[0], w_router_exp[0]], axis=1),
                      ((0, 0), (0, LANES - N_GROUPS - N_EXPERTS)))
    b_route = jnp.pad(jnp.concatenate([b_router_grp[0], b_router_exp[0]]),
                      (0, LANES - N_GROUPS - N_EXPERTS)).reshape(1, LANES)
    w_route_hi = w_route.astype(BF16)
    w_route_split = jnp.stack([w_route_hi, (w_route - w_route_hi.astype(F32)).astype(BF16)])
    h2, route = _router(x1_p, x1_s, mod_p, mod_s, g_ffn, w_route_split, b_route)

    dest, counts = _rank(route)
    dests = (dest[:, 0], dest[:, 1])

    i32 = lambda a: a.astype(jnp.int32)
    cnt = i32(counts[0, :N_EXPERTS])
    tiles_per = (cnt + TM_MOE - 1) // TM_MOE
    tile_end = jnp.cumsum(tiles_per)
    n_tiles = tile_end[-1]
    tile_ids = jnp.arange(MAX_TILES, dtype=jnp.int32)
    tile_expert = i32(jnp.minimum(jnp.searchsorted(tile_end, tile_ids, side="right"), N_EXPERTS - 1))
    last_expert = tile_expert[jnp.maximum(n_tiles - 1, 0)]
    tile_valid = tile_ids < n_tiles
    tile_expert = jnp.where(tile_valid, tile_expert, last_expert)
    prev_expert = jnp.concatenate([jnp.full((1,), -1, jnp.int32), tile_expert[:-1]])
    run_first = jnp.logical_and(tile_valid, tile_expert != prev_expert)
    run_id = jnp.maximum(jnp.cumsum(i32(run_first)) - 1, 0)
    expert_ids = jnp.arange(N_EXPERTS, dtype=jnp.int32)
    used = jnp.where(tiles_per > 0, expert_ids, N_EXPERTS)
    next_used = jnp.concatenate([lax.cummin(used[::-1])[::-1][1:], jnp.full((1,), N_EXPERTS, jnp.int32)])
    next_used = jnp.where(next_used >= N_EXPERTS, -1, next_used)
    first_tile = tile_end - tiles_per
    tile_rows = jnp.clip(cnt[tile_expert] - (tile_ids - first_tile[tile_expert]) * TM_MOE, 0, TM_MOE)
    tile_rows = jnp.where(tile_valid, tile_rows, 0)
    tiles = (tile_expert, i32(tile_rows), i32(run_first), next_used[tile_expert], i32(run_id),
             jnp.stack([n_tiles, jnp.sum(i32(run_first))]).astype(jnp.int32))
    next_expert = jnp.concatenate([tile_expert[1:], jnp.full((1,), -1, jnp.int32)])
    run_last = jnp.logical_or(tile_expert != next_expert, tile_ids == n_tiles - 1)
    zero_class = jnp.where(tile_valid, jnp.where(run_last, ZERO_FIRST, 0), ZERO_ANYTIME)
    xs_rows = _dispatch(dests, i32(zero_class), h2)
    h1 = _expert_up(tiles, xs_rows, w_gate[0], w_up[0])
    o_rows = _expert_down(tiles, h1, w_down[0])

    gf = g_final.reshape(1, D)
    y_p = _combine(dests, x1_p, route, mod_p, False, gf, o_rows, 0)
    y_s = _combine(dests, x1_s, route, mod_s, True, gf, o_rows, N_PROMPT)

    kv5 = lambda a: a.reshape(1, -1, WINDOW, KV_HEADS, HEAD_DIM)
    kcol = ATT_WIDTH
    nk_p = z[N_PROMPT - WINDOW:N_PROMPT, kcol:kcol + KV_WIDTH]
    nv_p = z[N_PROMPT - WINDOW:N_PROMPT, kcol + KV_WIDTH:kcol + 2 * KV_WIDTH]
    return (y_p.reshape(1, N_PROMPT, D), y_s.reshape(N_BATCH, T_DEC, D),
            kv5(nk_p), kv5(nv_p),
            c_p.reshape(1, 1, M_HEADS, DV, DK), n_p.reshape(1, 1, M_HEADS, DK), m_p[:, 0].reshape(1, 1, M_HEADS),
            kv5(nk_s), kv5(nv_s),
            c_s.reshape(1, N_BATCH, M_HEADS, DV, DK), n_s.reshape(1, N_BATCH, M_HEADS, DK),
            m_s.reshape(N_BATCH, T_DEC, LANES)[:, T_DEC - 1, :M_HEADS].reshape(1, N_BATCH, M_HEADS))
```

```python
import functools
import math

import numpy as np
import jax
import jax.numpy as jnp
from jax import lax
from jax.experimental import pallas as pl
from jax.experimental.pallas import tpu as pltpu

F32 = jnp.float32
BF16 = jnp.bfloat16
NEG_INF = float("-inf")

D = 4096
N_PROMPT = 8192
N_BATCH = 128
T_DEC = 8
N_SAMPLE = N_BATCH * T_DEC
N_TOK = N_PROMPT + N_SAMPLE
HEAD_DIM = 128
ATT_HEADS = 16
KV_HEADS = 4
GROUP = ATT_HEADS // KV_HEADS
WINDOW = 128
ATT_WIDTH = ATT_HEADS * HEAD_DIM
KV_WIDTH = KV_HEADS * HEAD_DIM
NUM_BUCKETS = 32
MAX_EXACT = 16
MAX_DISTANCE = 128
M_HEADS = 8
DK = 128
DV = 256
M_WIDTH = M_HEADS * DV
Z_WIDTH = ATT_WIDTH + 2 * KV_WIDTH + 2 * M_HEADS * DK + 2 * M_WIDTH
N_GROUPS = 4
EXP_PER_GROUP = 8
N_EXPERTS = N_GROUPS * EXP_PER_GROUP
TOP_K = 2
D_FF = 1024
EPS = 1e-6
ATT_SCALE = HEAD_DIM ** -0.5
Q_SCALE = DK ** -0.5

LANES = 128
SUBLANES = 8
VMEM_LIMIT = 56 * 1024 * 1024

TM_PROJ = 512
TM_IN = 1024
TN_IN = 1024
TN_OUT = 1024
TN_ADA = 512
C_ROWS = 136
ATT_BLOCK = 128
SAMPLE_BT = 8
ML_CHUNK = 256
TM_MOE = 256
TN_FF = 512
N_ASSIGN = N_TOK * TOP_K
MAX_TILES = N_ASSIGN // TM_MOE + N_EXPERTS
A_PAD = MAX_TILES * TM_MOE
TM_TOK = 256
TM_RANK = 1024


def _params(sem):
    return pltpu.CompilerParams(dimension_semantics=sem, vmem_limit_bytes=VMEM_LIMIT)


def _iota(shape, dim):
    return lax.broadcasted_iota(jnp.int32, shape, dim)


def _dot(a, b):
    return jnp.dot(a, b, preferred_element_type=F32)


def _dot_nt(a, b):
    return lax.dot_general(a, b, (((1,), (1,)), ((), ())), preferred_element_type=F32)


def _dot_tn(a, b):
    return lax.dot_general(a, b, (((0,), (0,)), ((), ())), preferred_element_type=F32)


def _split3(x):
    x1 = x.astype(BF16)
    r1 = x - x1.astype(F32)
    x2 = r1.astype(BF16)
    r2 = r1 - x2.astype(F32)
    return x1, x2, r2.astype(BF16)


def _dot_exact_lhs01(a01, x):
    x1, x2, x3 = _split3(x)
    return _dot(a01, x1) + _dot(a01, x2) + _dot(a01, x3)


def _pack_bf16_pairs(x):
    w = x.shape[1] // 2
    bits = lax.bitcast_convert_type(x.astype(BF16).astype(F32), jnp.uint32)
    return (bits[:, :w] >> 16) | (bits[:, w:] & jnp.uint32(0xFFFF0000))


def _unpack_bf16_pairs(words):
    return (lax.bitcast_convert_type(words << 16, F32),
            lax.bitcast_convert_type(words & jnp.uint32(0xFFFF0000), F32))


def _sigmoid(x):
    return 1.0 / (1.0 + jnp.exp(-x))


def _log_sigmoid(x):
    return jnp.minimum(x, 0.0) - jnp.log(1.0 + jnp.exp(-jnp.abs(x)))


def _mod_spec(per_batch, rows, width, col):
    if per_batch:
        return pl.BlockSpec((rows // T_DEC, 1, width), lambda i, *rest: (i, 0, col(*rest)))
    return pl.BlockSpec((1, width), lambda i, *rest: (0, col(*rest)))


def _mod_rows(ref):
    v = ref[...]
    if v.ndim == 2:
        return v
    nb, _, width = v.shape
    return jnp.broadcast_to(v, (nb, T_DEC, width)).reshape(nb * T_DEC, width)


def _ada_kernel(c_ref, w_ref, b_ref, o_ref):
    c = c_ref[...]
    s = (c * _sigmoid(c)).astype(BF16)
    mod = _dot(s, w_ref[...].astype(BF16)) + b_ref[...]
    o_ref[...] = mod.reshape(C_ROWS, 1, TN_ADA)


def _ada(c_all, w_ada, b_ada):
    n = w_ada.shape[1]
    return pl.pallas_call(
        _ada_kernel,
        grid=(n // TN_ADA,),
        in_specs=[pl.BlockSpec((C_ROWS, D), lambda j: (0, 0)),
                  pl.BlockSpec((D, TN_ADA), lambda j: (0, j)),
                  pl.BlockSpec((1, TN_ADA), lambda j: (0, j))],
        out_specs=pl.BlockSpec((C_ROWS, 1, TN_ADA), lambda j: (0, 0, j)),
        out_shape=jax.ShapeDtypeStruct((C_ROWS, 1, n), F32),
        compiler_params=_params(("arbitrary",)),
        name="ada",
    )(c_all, w_ada, b_ada)


PROMPT_TOK_BLOCKS = N_PROMPT // TM_TOK


def _prompt_block(i):
    return jnp.minimum(i, PROMPT_TOK_BLOCKS - 1)


def _sample_block(i):
    return jnp.maximum(i - PROMPT_TOK_BLOCKS, 0)


def _norm_body(x_ref, sh_ref, sc_ref, g_ref, wg_ref, h_ref, zg_ref):
    x = x_ref[...]
    y = x * lax.rsqrt(jnp.mean(x * x, axis=-1, keepdims=True) + EPS) * g_ref[...]
    hb = (y * (1.0 + _mod_rows(sc_ref)) + _mod_rows(sh_ref)).astype(BF16)
    h_ref[...] = hb
    zg_ref[...] = _dot_nt(hb, wg_ref[...])


def _norm_kernel(xp_ref, xs_ref, shp_ref, scp_ref, shs_ref, scs_ref, g_ref, wg_ref, h_ref, zg_ref):
    i = pl.program_id(0)

    @pl.when(i < PROMPT_TOK_BLOCKS)
    def _():
        _norm_body(xp_ref, shp_ref, scp_ref, g_ref, wg_ref, h_ref, zg_ref)

    @pl.when(i >= PROMPT_TOK_BLOCKS)
    def _():
        _norm_body(xs_ref, shs_ref, scs_ref, g_ref, wg_ref, h_ref, zg_ref)


def _norm(xp, xs, mod_p, mod_s, g_mix, w_gate):
    per_batch = lambda col: pl.BlockSpec((TM_TOK // T_DEC, 1, D), lambda i: (_sample_block(i), 0, col))
    return pl.pallas_call(
        _norm_kernel,
        grid=(N_TOK // TM_TOK,),
        in_specs=[pl.BlockSpec((TM_TOK, D), lambda i: (_prompt_block(i), 0)),
                  pl.BlockSpec((TM_TOK, D), lambda i: (_sample_block(i), 0)),
                  pl.BlockSpec((1, D), lambda i: (0, 0)),
                  pl.BlockSpec((1, D), lambda i: (0, 1)),
                  per_batch(0), per_batch(1),
                  pl.BlockSpec((1, D), lambda i: (0, 0)),
                  pl.BlockSpec((LANES, D), lambda i: (0, 0))],
        out_specs=[pl.BlockSpec((TM_TOK, D), lambda i: (i, 0)),
                   pl.BlockSpec((TM_TOK, LANES), lambda i: (i, 0))],
        out_shape=[jax.ShapeDtypeStruct((N_TOK, D), BF16), jax.ShapeDtypeStruct((N_TOK, LANES), F32)],
        compiler_params=_params(("arbitrary",)),
        name="norm",
    )(xp, xs, mod_p, mod_p, mod_s, mod_s, g_mix, w_gate)


N_IN_CHUNKS = Z_WIDTH // TN_IN


def _inproj_kernel(h_ref, wt_hbm, z_ref, stage, w_bf, sem):
    j = pl.program_id(0)

    def fetch(chunk):
        rows = pl.ds(pl.multiple_of(chunk * TN_IN, TN_IN), TN_IN)
        return pltpu.make_async_copy(wt_hbm.at[rows], stage, sem.at[0])

    @pl.when(pl.program_id(1) == 0)
    def _():
        @pl.when(j == 0)
        def _():
            fetch(0).start()

        fetch(j).wait()
        w_bf[...] = stage[...].astype(BF16)

        @pl.when(j + 1 < N_IN_CHUNKS)
        def _():
            fetch(j + 1).start()

    z_ref[...] = _dot_nt(h_ref[...], w_bf[...])


def _inproj(h, w_in_t):
    return pl.pallas_call(
        _inproj_kernel,
        grid=(N_IN_CHUNKS, N_TOK // TM_IN),
        in_specs=[pl.BlockSpec((TM_IN, D), lambda j, i: (i, 0)),
                  pl.BlockSpec(memory_space=pl.ANY)],
        out_specs=pl.BlockSpec((TM_IN, TN_IN), lambda j, i: (i, j)),
        out_shape=jax.ShapeDtypeStruct((N_TOK, Z_WIDTH), F32),
        scratch_shapes=[pltpu.VMEM((TN_IN, D), F32), pltpu.VMEM((TN_IN, D), BF16), pltpu.SemaphoreType.DMA((1,))],
        compiler_params=_params(("arbitrary", "arbitrary")),
        name="inproj",
    )(h, w_in_t)


def _t5_bucket_np(dist):
    n = np.maximum(dist, 0)
    nf = np.maximum(n, 1).astype(np.float32)
    large = MAX_EXACT + (np.log(nf / MAX_EXACT) / math.log(MAX_DISTANCE / MAX_EXACT)
                         * (NUM_BUCKETS - MAX_EXACT)).astype(np.int32)
    large = np.minimum(large, NUM_BUCKETS - 1)
    return np.where(n < MAX_EXACT, n, large).astype(np.int32)


def _bucket_table(n_q, n_keys_valid, n_keys_padded):
    t = np.arange(n_q)[:, None]
    j = np.arange(n_keys_padded)[None, :]
    dist = t + WINDOW - j
    valid = (dist >= 0) & (dist < WINDOW) & (j < n_keys_valid)
    return np.where(valid, _t5_bucket_np(dist), -1).astype(np.int32)


def _fill_bias(bucket_ref, rb_ref, bias_scr, rows, per_head_table=False):
    for h in range(ATT_HEADS):
        bk = bucket_ref[h * rows:(h + 1) * rows, :] if per_head_table else bucket_ref[...]
        acc = jnp.full(bk.shape, NEG_INF, F32)
        for b in range(NUM_BUCKETS):
            acc = jnp.where(bk == b, rb_ref[b * ATT_HEADS + h], acc)
        bias_scr[h * rows:(h + 1) * rows, :] = acc


def _with_ones(v2):
    return jnp.concatenate([v2, jnp.ones_like(v2)], axis=1)


def _sink_softmax_av(lg, sink, v2_ones):
    m = jnp.maximum(jnp.max(lg, axis=-1, keepdims=True), sink)
    p = jnp.exp(lg - m).astype(BF16)
    pv = _dot(p, v2_ones)
    return pv[:, :HEAD_DIM] / (pv[:, HEAD_DIM:] + jnp.exp(sink - m))


def _swa_prompt_kernel(rb_ref, sink_ref, bucket_ref, q_ref, kp_ref, kc_ref, vp_ref, vc_ref,
                       o_ref, bias_scr):
    i = pl.program_id(0)

    @pl.when(i == 0)
    def _():
        _fill_bias(bucket_ref, rb_ref, bias_scr, ATT_BLOCK)

    first_prev = jnp.logical_and(i == 0, _iota((ATT_BLOCK, 2 * ATT_BLOCK), 1) < ATT_BLOCK)
    for g in range(KV_HEADS):
        ks = slice(g * HEAD_DIM, (g + 1) * HEAD_DIM)
        k2 = jnp.concatenate([kp_ref[:, ks], kc_ref[:, ks]], axis=0).astype(BF16)
        v2 = _with_ones(jnp.concatenate([vp_ref[:, ks], vc_ref[:, ks]], axis=0).astype(BF16))
        for r in range(GROUP):
            h = g * GROUP + r
            hs = slice(h * HEAD_DIM, (h + 1) * HEAD_DIM)
            lg = _dot_nt(q_ref[:, hs].astype(BF16), k2) * ATT_SCALE + bias_scr[h * ATT_BLOCK:(h + 1) * ATT_BLOCK, :]
            lg = jnp.where(first_prev, NEG_INF, lg)
            o_ref[:, hs] = _sink_softmax_av(lg, sink_ref[h], v2).astype(BF16)


def _swa_prompt(z, rb_flat, sinks):
    nb = N_PROMPT // ATT_BLOCK
    bucket = jnp.asarray(_bucket_table(ATT_BLOCK, 2 * ATT_BLOCK, 2 * ATT_BLOCK))
    kcol = ATT_WIDTH // KV_WIDTH
    prev = lambda i: jnp.maximum(i - 1, 0)
    smem = pl.BlockSpec(memory_space=pltpu.SMEM)
    return pl.pallas_call(
        _swa_prompt_kernel,
        grid=(nb,),
        in_specs=[smem, smem,
                  pl.BlockSpec((ATT_BLOCK, 2 * ATT_BLOCK), lambda i: (0, 0)),
                  pl.BlockSpec((ATT_BLOCK, ATT_WIDTH), lambda i: (i, 0)),
                  pl.BlockSpec((ATT_BLOCK, KV_WIDTH), lambda i: (prev(i), kcol)),
                  pl.BlockSpec((ATT_BLOCK, KV_WIDTH), lambda i: (i, kcol)),
                  pl.BlockSpec((ATT_BLOCK, KV_WIDTH), lambda i: (prev(i), kcol + 1)),
                  pl.BlockSpec((ATT_BLOCK, KV_WIDTH), lambda i: (i, kcol + 1))],
        out_specs=pl.BlockSpec((ATT_BLOCK, ATT_WIDTH), lambda i: (i, 0)),
        out_shape=jax.ShapeDtypeStruct((N_PROMPT, ATT_WIDTH), BF16),
        scratch_shapes=[pltpu.VMEM((ATT_HEADS * ATT_BLOCK, 2 * ATT_BLOCK), F32)],
        compiler_params=_params(("arbitrary",)),
        name="swa_prompt",
    )(rb_flat, sinks, bucket, z, z, z, z, z)


CACHE_ROWS = WINDOW * KV_HEADS
NEW_ROWS = T_DEC * KV_HEADS
S_KEYS = 5 * LANES


def _sample_bucket_table():
    t = np.arange(T_DEC)[:, None]
    col = np.arange(S_KEYS)[None, :]
    in_cache = col < CACHE_ROWS
    in_new = (col >= CACHE_ROWS) & (col < CACHE_ROWS + NEW_ROWS)
    key_head = np.where(in_cache, col % KV_HEADS, (col - CACHE_ROWS) // T_DEC)
    key_pos = np.where(in_cache, col // KV_HEADS, WINDOW + (col - CACHE_ROWS) % T_DEC)
    dist = t + WINDOW - key_pos
    valid = (dist >= 0) & (dist < WINDOW) & (in_cache | in_new)
    per_query = np.where(valid, _t5_bucket_np(dist), -1)
    heads = np.arange(ATT_HEADS)[:, None, None] // GROUP
    table = np.where(heads == key_head[None], per_query[None], -1)
    return table.reshape(ATT_HEADS * T_DEC, S_KEYS).astype(np.int32)


def _swa_sample_kernel(rb_ref, sink_ref, bucket_ref, q_ref, kn_ref, vn_ref, ck_ref, cv_ref,
                       o_ref, nk_ref, nv_ref, bias_scr):
    @pl.when(pl.program_id(0) == 0)
    def _():
        _fill_bias(bucket_ref, rb_ref, bias_scr, T_DEC, per_head_table=True)

    nk_ref[:, 0:CACHE_ROWS - NEW_ROWS, :] = ck_ref[:, NEW_ROWS:CACHE_ROWS, :]
    nv_ref[:, 0:CACHE_ROWS - NEW_ROWS, :] = cv_ref[:, NEW_ROWS:CACHE_ROWS, :]

    pad = jnp.zeros((S_KEYS - CACHE_ROWS - NEW_ROWS, HEAD_DIM), F32)
    sink_col = jnp.concatenate([jnp.full((T_DEC, 1), sink_ref[h], F32) for h in range(ATT_HEADS)], axis=0)
    bias = bias_scr[...]
    for b in range(SAMPLE_BT):
        ts = slice(b * T_DEC, (b + 1) * T_DEC)
        head_cols = lambda ref, n: [ref[ts, h * HEAD_DIM:(h + 1) * HEAD_DIM] for h in range(n)]
        k_new, v_new = head_cols(kn_ref, KV_HEADS), head_cols(vn_ref, KV_HEADS)
        for g in range(KV_HEADS):
            new_rows = pl.ds(CACHE_ROWS - NEW_ROWS + g, T_DEC, stride=KV_HEADS)
            nk_ref[b, new_rows, :] = k_new[g]
            nv_ref[b, new_rows, :] = v_new[g]
        qa = jnp.concatenate(head_cols(q_ref, ATT_HEADS), axis=0).astype(BF16)
        k2 = jnp.concatenate([ck_ref[b]] + k_new + [pad], axis=0).astype(BF16)
        v2 = _with_ones(jnp.concatenate([cv_ref[b]] + v_new + [pad], axis=0).astype(BF16))
        o = _sink_softmax_av(_dot_nt(qa, k2) * ATT_SCALE + bias, sink_col, v2)
        for h in range(ATT_HEADS):
            o_ref[ts, h * HEAD_DIM:(h + 1) * HEAD_DIM] = o[h * T_DEC:(h + 1) * T_DEC].astype(BF16)


def _swa_sample(z, cache_k, cache_v, rb_flat, sinks):
    rows = SAMPLE_BT * T_DEC
    z_off = N_PROMPT // rows
    bucket = jnp.asarray(_sample_bucket_table())
    kcol = ATT_WIDTH // KV_WIDTH
    smem = pl.BlockSpec(memory_space=pltpu.SMEM)
    cache_spec = pl.BlockSpec((SAMPLE_BT, CACHE_ROWS, HEAD_DIM), lambda i: (i, 0, 0))
    cache_shape = jax.ShapeDtypeStruct((N_BATCH, CACHE_ROWS, HEAD_DIM), F32)
    return pl.pallas_call(
        _swa_sample_kernel,
        grid=(N_BATCH // SAMPLE_BT,),
        in_specs=[smem, smem,
                  pl.BlockSpec((ATT_HEADS * T_DEC, S_KEYS), lambda i: (0, 0)),
                  pl.BlockSpec((rows, ATT_WIDTH), lambda i: (i + z_off, 0)),
                  pl.BlockSpec((rows, KV_WIDTH), lambda i: (i + z_off, kcol)),
                  pl.BlockSpec((rows, KV_WIDTH), lambda i: (i + z_off, kcol + 1)),
                  cache_spec, cache_spec],
        out_specs=[pl.BlockSpec((rows, ATT_WIDTH), lambda i: (i, 0)), cache_spec, cache_spec],
        out_shape=[jax.ShapeDtypeStruct((N_SAMPLE, ATT_WIDTH), BF16), cache_shape, cache_shape],
        scratch_shapes=[pltpu.VMEM((ATT_HEADS * T_DEC, S_KEYS), F32)],
        compiler_params=_params(("arbitrary",)),
        name="swa_sample",
    )(rb_flat, sinks, bucket, z, z, z, cache_k, cache_v)


LANE_IG, LANE_LF, LANE_B = 0, M_HEADS, 2 * M_HEADS


def _gate_table(zg, gate_bias, seg_len):
    L = zg.shape[0]
    g = zg + gate_bias
    lane = _iota((L, LANES), 1)
    lf = _log_sigmoid(g)
    lf_only = jnp.where(jnp.logical_and(lane >= LANE_LF, lane < LANE_B), lf, 0.0)
    row = _iota((L, L), 0)
    col = _iota((L, L), 1)
    same_seg = (row // seg_len) == (col // seg_len)
    tril = jnp.where(jnp.logical_and(col <= row, same_seg), 1.0, 0.0).astype(BF16)
    cum = pltpu.roll(_dot_exact_lhs01(tril, lf_only), M_HEADS, axis=1)
    table = jnp.where(lane < LANE_LF, g, jnp.where(lane < LANE_B, lf, jnp.where(lane < LANE_B + M_HEADS, cum, 0.0)))
    return table, jnp.logical_and(col <= row, same_seg)


def _mlstm_decay(table, table_t, mask, m0_col, h):
    b_c = table[:, LANE_B + h:LANE_B + h + 1]
    b_r = table_t[LANE_B + h:LANE_B + h + 1, :]
    ig_r = table_t[LANE_IG + h:LANE_IG + h + 1, :]
    log_d = jnp.where(mask, b_c - b_r + ig_r, NEG_INF)
    log_inter = b_c + m0_col
    m_t = jnp.maximum(log_inter, jnp.max(log_d, axis=1, keepdims=True))
    return b_c, m_t, jnp.exp(log_inter - m_t), jnp.exp(log_d - m_t)


def _mlstm_intra(table, table_t, mask, m0_col, h, q, k, v):
    b_c, m_t, w_inter, d = _mlstm_decay(table, table_t, mask, m0_col, h)
    s = _dot_nt(q.astype(BF16), k.astype(BF16)) * d
    num_intra = _dot(s.astype(BF16), v.astype(BF16))
    den_intra = jnp.sum(s, axis=1, keepdims=True)
    return b_c, m_t, w_inter, num_intra, den_intra


ST_WIDTH = DV + LANES


def _mlstm_prompt_kernel(gb_ref, zg_ref, q_ref, k_ref, va_ref, vb_ref, oa_ref, ob_ref,
                         y_ref, c_out, n_out, m_out, st_scr, m_scr):
    step = pl.program_id(0)
    L = ML_CHUNK

    @pl.when(step == 0)
    def _():
        st_scr[...] = jnp.zeros_like(st_scr)
        m_scr[...] = jnp.zeros_like(m_scr)

    table, mask = _gate_table(zg_ref[...], gb_ref[...], L)
    table_t = table.T
    ones = jnp.ones((L, LANES), F32)
    half = M_HEADS // 2
    for h in range(M_HEADS):
        v_ref, o_ref = (va_ref, oa_ref) if h < half else (vb_ref, ob_ref)
        vs = slice((h % half) * DV, (h % half + 1) * DV)
        qb = (q_ref[:, h * DK:(h + 1) * DK] * Q_SCALE).astype(BF16)
        kb = k_ref[:, h * DK:(h + 1) * DK].astype(BF16)
        v1 = jnp.concatenate([v_ref[:, vs], ones], axis=1)
        m0 = m_scr[h:h + 1, 0:1]
        b_c, m_t, w_inter, d = _mlstm_decay(table, table_t, mask, m0, h)
        s = _dot_nt(qb, kb) * d
        st = st_scr[h]
        tot = _dot(s.astype(BF16), v1.astype(BF16)) + w_inter * _dot(qb, st.astype(BF16))
        den = tot[:, DV:]
        hh = tot[:, :DV] / jnp.maximum(jnp.abs(jnp.concatenate([den, den], axis=1)), jnp.exp(-m_t))
        y_ref[:, h * DV:(h + 1) * DV] = (_sigmoid(o_ref[:, vs]) * hh).astype(BF16)

        ig_c = table[:, LANE_IG + h:LANE_IG + h + 1]
        m_new = m_t[L - 1:L, :]
        b_last = b_c[L - 1:L, :]
        w_s = jnp.exp(b_last - b_c + ig_c - m_new)
        decay = jnp.exp(b_last + m0 - m_new)
        st_scr[h] = decay * st + _dot_tn(kb, (w_s * v1).astype(BF16))
        m_scr[h:h + 1, :] = jnp.broadcast_to(m_new, (1, LANES))

    @pl.when(step == pl.num_programs(0) - 1)
    def _():
        for h in range(M_HEADS):
            st_t = st_scr[h].T
            c_out[h] = st_t[:DV, :]
            n_out[h:h + 1, :] = st_t[DV:DV + 1, :]
        m_out[...] = m_scr[...]


def _mlstm_prompt(z, zg, gate_bias):
    L = ML_CHUNK
    blk = M_HEADS * DK
    col = lambda c: pl.BlockSpec((L, blk), lambda i: (i, c))
    const = lambda shape: pl.BlockSpec(shape, lambda i: tuple(0 for _ in shape))
    return pl.pallas_call(
        _mlstm_prompt_kernel,
        grid=(N_PROMPT // L,),
        in_specs=[const((1, LANES)),
                  pl.BlockSpec((L, LANES), lambda i: (i, 0)),
                  col(3), col(4), col(5), col(6), col(7), col(8)],
        out_specs=[pl.BlockSpec((L, M_WIDTH), lambda i: (i, 0)),
                   const((M_HEADS, DV, DK)), const((M_HEADS, DK)), const((M_HEADS, LANES))],
        out_shape=[jax.ShapeDtypeStruct((N_PROMPT, M_WIDTH), BF16),
                   jax.ShapeDtypeStruct((M_HEADS, DV, DK), F32),
                   jax.ShapeDtypeStruct((M_HEADS, DK), F32),
                   jax.ShapeDtypeStruct((M_HEADS, LANES), F32)],
        scratch_shapes=[pltpu.VMEM((M_HEADS, DK, ST_WIDTH), F32),
                        pltpu.VMEM((M_HEADS, LANES), F32)],
        compiler_params=_params(("arbitrary",)),
        name="mlstm_prompt",
    )(gate_bias, zg, z, z, z, z, z, z)


S_ROWS = SAMPLE_BT * T_DEC


def _mlstm_sample_kernel(gb_ref, zg_ref, m0_ref, n0t_ref, q_ref, k_ref, va_ref, vb_ref, oa_ref, ob_ref,
                         c0_ref, n0_ref, y_ref, c_out, n_out, m_out):
    L = S_ROWS
    table, mask = _gate_table(zg_ref[...], gb_ref[...], T_DEC)
    table_t = jnp.concatenate([table, jnp.zeros((LANES - L, LANES), F32)], axis=0).T[:, 0:L]
    m0_all = m0_ref[...]
    lane = _iota((L, LANES), 1)
    row_b = _iota((SAMPLE_BT, L), 1) // T_DEC
    seg_sum = jnp.where(row_b == _iota((SAMPLE_BT, L), 0), 1.0, 0.0)
    m_tok = jnp.zeros((L, LANES), F32)
    half = M_HEADS // 2
    for h in range(M_HEADS):
        v_ref, o_ref = (va_ref, oa_ref) if h < half else (vb_ref, ob_ref)
        vs = slice((h % half) * DV, (h % half + 1) * DV)
        q = q_ref[:, h * DK:(h + 1) * DK] * Q_SCALE
        k = k_ref[:, h * DK:(h + 1) * DK]
        v = v_ref[:, vs]
        m0 = m0_all[:, h:h + 1]
        b_c, m_t, w_inter, num_intra, den_intra = _mlstm_intra(table, table_t, mask, m0, h, q, k, v)
        qb = q.astype(BF16)
        num_inter = jnp.concatenate(
            [_dot_nt(qb[b * T_DEC:(b + 1) * T_DEC], c0_ref[b, h].astype(BF16)) for b in range(SAMPLE_BT)], axis=0)
        num = num_intra + w_inter * num_inter
        den = den_intra + w_inter * jnp.sum(q * n0t_ref[:, h * DK:(h + 1) * DK], axis=1, keepdims=True)
        hh = num / jnp.maximum(jnp.abs(den), jnp.exp(-m_t))
        y_ref[:, h * DV:(h + 1) * DV] = (_sigmoid(o_ref[:, vs]) * hh).astype(BF16)
        m_tok = jnp.where(lane == h, m_t, m_tok)

        def last_tok(x):
            x3 = x.reshape(SAMPLE_BT, T_DEC, 1)
            return jnp.broadcast_to(x3[:, T_DEC - 1:T_DEC, :], x3.shape).reshape(L, 1)

        ig_c = table[:, LANE_IG + h:LANE_IG + h + 1]
        m_new = last_tok(m_t)
        b_last = last_tok(b_c)
        w_s = jnp.exp(b_last - b_c + ig_c - m_new)
        decay = jnp.exp(b_last + m0 - m_new)
        wv = (w_s * v).astype(BF16)
        kb = k.astype(BF16)
        rowsel = _iota((L, 1), 0) // T_DEC
        for b in range(SAMPLE_BT):
            dec_b = decay[b * T_DEC + T_DEC - 1:(b + 1) * T_DEC, :]
            wv_b = jnp.where(rowsel == b, wv, jnp.zeros_like(wv))
            c_out[b, h] = dec_b * c0_ref[b, h] + _dot_tn(wv_b, kb)
        dec_rows = decay.reshape(SAMPLE_BT, T_DEC, 1)[:, T_DEC - 1, :]
        n_out[:, h * DK:(h + 1) * DK] = dec_rows * n0_ref[:, h * DK:(h + 1) * DK] + jnp.dot(
            seg_sum, w_s * k, preferred_element_type=F32, precision=lax.Precision.HIGHEST)
    m_out[...] = m_tok


def _mlstm_sample(z, zg, gate_bias, m0_tok, n0_tok, state_c, state_n):
    L = S_ROWS
    blk = M_HEADS * DK
    z_off = N_PROMPT // L
    col = lambda c: pl.BlockSpec((L, blk), lambda i: (i + z_off, c))
    c_spec = pl.BlockSpec((SAMPLE_BT, M_HEADS, DV, DK), lambda i: (i, 0, 0, 0))
    n_spec = pl.BlockSpec((SAMPLE_BT, blk), lambda i: (i, 0))
    return pl.pallas_call(
        _mlstm_sample_kernel,
        grid=(N_BATCH // SAMPLE_BT,),
        in_specs=[pl.BlockSpec((1, LANES), lambda i: (0, 0)),
                  pl.BlockSpec((L, LANES), lambda i: (i + z_off, 0)),
                  pl.BlockSpec((L, LANES), lambda i: (i, 0)),
                  pl.BlockSpec((L, blk), lambda i: (i, 0)),
                  col(3), col(4), col(5), col(6), col(7), col(8),
                  c_spec, n_spec],
        out_specs=[pl.BlockSpec((L, M_WIDTH), lambda i: (i, 0)), c_spec, n_spec,
                   pl.BlockSpec((L, LANES), lambda i: (i, 0))],
        out_shape=[jax.ShapeDtypeStruct((N_SAMPLE, M_WIDTH), BF16),
                   jax.ShapeDtypeStruct((N_BATCH, M_HEADS, DV, DK), F32),
                   jax.ShapeDtypeStruct((N_BATCH, blk), F32),
                   jax.ShapeDtypeStruct((N_SAMPLE, LANES), F32)],
        compiler_params=_params(("arbitrary",)),
        name="mlstm_sample",
    )(gate_bias, zg, m0_tok, n0_tok, z, z, z, z, z, z, state_c, state_n)


N_OUT_CHUNKS = D // TN_OUT


def _outproj_kernel(ya_ref, ym_ref, w_hbm, x_ref, gt_ref, o_ref, stage, w_bf, sem):
    j = pl.program_id(0)

    def fetch(chunk):
        cols = pl.ds(pl.multiple_of(chunk * TN_OUT, TN_OUT), TN_OUT)
        return pltpu.make_async_copy(w_hbm.at[:, cols], stage, sem.at[0])

    @pl.when(pl.program_id(1) == 0)
    def _():
        @pl.when(j == 0)
        def _():
            fetch(0).start()

        fetch(j).wait()
        w_bf[...] = stage[...].astype(BF16)

        @pl.when(j + 1 < N_OUT_CHUNKS)
        def _():
            fetch(j + 1).start()

    mix = _dot(ya_ref[...], w_bf[0:ATT_WIDTH, :]) + _dot(ym_ref[...], w_bf[ATT_WIDTH:, :])
    o_ref[...] = x_ref[...] + _mod_rows(gt_ref) * mix


def _outproj(ya, ym, w_out, x, mod, per_row):
    n = x.shape[0]
    TM = TM_PROJ
    gate1_col = 2 * N_OUT_CHUNKS
    if per_row:
        gate_spec = pl.BlockSpec((TM // T_DEC, 1, TN_OUT), lambda j, i: (i, 0, gate1_col + j))
    else:
        gate_spec = pl.BlockSpec((1, TN_OUT), lambda j, i: (0, gate1_col + j))
    return pl.pallas_call(
        _outproj_kernel,
        grid=(N_OUT_CHUNKS, n // TM),
        in_specs=[pl.BlockSpec((TM, ATT_WIDTH), lambda j, i: (i, 0)),
                  pl.BlockSpec((TM, M_WIDTH), lambda j, i: (i, 0)),
                  pl.BlockSpec(memory_space=pl.ANY),
                  pl.BlockSpec((TM, TN_OUT), lambda j, i: (i, j)),
                  gate_spec],
        out_specs=pl.BlockSpec((TM, TN_OUT), lambda j, i: (i, j)),
        out_shape=jax.ShapeDtypeStruct((n, D), F32),
        scratch_shapes=[pltpu.VMEM((D, TN_OUT), F32), pltpu.VMEM((D, TN_OUT), BF16), pltpu.SemaphoreType.DMA((1,))],
        compiler_params=_params(("arbitrary", "arbitrary")),
        name="outproj",
    )(ya, ym, w_out, x, mod)


def _router_kernel(x_ref, sh_ref, sc_ref, g_ref, wr_ref, br_ref, h_ref, route_ref):
    x = x_ref[...]
    y = x * lax.rsqrt(jnp.mean(x * x, axis=-1, keepdims=True) + EPS) * g_ref[...]
    h2 = y * (1.0 + _mod_rows(sc_ref)) + _mod_rows(sh_ref)
    h_hi = h2.astype(BF16)
    h_ref[...] = _pack_bf16_pairs(h2)
    h_lo = (h2 - h_hi.astype(F32)).astype(BF16)
    logits = _dot(h_hi, wr_ref[0]) + (_dot(h_hi, wr_ref[1]) + _dot(h_lo, wr_ref[0])) + br_ref[...]
    lane = _iota(logits.shape, 1)

    def first_max(vals):
        vmax = jnp.max(vals, axis=1, keepdims=True)
        idx = jnp.min(jnp.where(vals == vmax, lane, LANES), axis=1, keepdims=True)
        return vmax, idx

    gl = jnp.where(lane < N_GROUPS, logits, NEG_INF)
    gmax, grp = first_max(gl)
    p_grp = 1.0 / jnp.sum(jnp.exp(gl - gmax), axis=1, keepdims=True)
    e_lane = lane - N_GROUPS
    in_grp = jnp.logical_and(e_lane >= 0, jnp.logical_and(e_lane < N_EXPERTS, e_lane // EXP_PER_GROUP == grp))
    el = jnp.where(in_grp, logits, NEG_INF)
    v1, i1 = first_max(el)
    v2, i2 = first_max(jnp.where(lane == i1, NEG_INF, el))
    e2w = jnp.exp(v2 - v1)
    w1 = 1.0 / (1.0 + e2w)
    w2 = e2w / (1.0 + e2w)
    route = jnp.where(lane == 0, (i1 - N_GROUPS).astype(F32),
                      jnp.where(lane == 1, (i2 - N_GROUPS).astype(F32),
                                jnp.where(lane == 2, p_grp * w1, jnp.where(lane == 3, p_grp * w2, 0.0))))
    route_ref[...] = route


def _router_merged_kernel(xp_ref, xs_ref, shp_ref, scp_ref, shs_ref, scs_ref, g_ref, wr_ref, br_ref,
                          h_ref, route_ref):
    i = pl.program_id(0)

    @pl.when(i < PROMPT_TOK_BLOCKS)
    def _():
        _router_kernel(xp_ref, shp_ref, scp_ref, g_ref, wr_ref, br_ref, h_ref, route_ref)

    @pl.when(i >= PROMPT_TOK_BLOCKS)
    def _():
        _router_kernel(xs_ref, shs_ref, scs_ref, g_ref, wr_ref, br_ref, h_ref, route_ref)


def _router(x1_p, x1_s, mod_p, mod_s, g_ffn, w_route, b_route):
    pi, si = _prompt_block, _sample_block
    return pl.pallas_call(
        _router_merged_kernel,
        grid=(N_TOK // TM_TOK,),
        in_specs=[pl.BlockSpec((TM_TOK, D), lambda i: (pi(i), 0)),
                  pl.BlockSpec((TM_TOK, D), lambda i: (si(i), 0)),
                  pl.BlockSpec((1, D), lambda i: (0, 3)),
                  pl.BlockSpec((1, D), lambda i: (0, 4)),
                  pl.BlockSpec((TM_TOK // T_DEC, 1, D), lambda i: (si(i), 0, 3)),
                  pl.BlockSpec((TM_TOK // T_DEC, 1, D), lambda i: (si(i), 0, 4)),
                  pl.BlockSpec((1, D), lambda i: (0, 0)),
                  pl.BlockSpec((2, D, LANES), lambda i: (0, 0, 0)),
                  pl.BlockSpec((1, LANES), lambda i: (0, 0))],
        out_specs=[pl.BlockSpec((TM_TOK, D_PACK), lambda i: (i, 0)),
                   pl.BlockSpec((TM_TOK, LANES), lambda i: (i, 0))],
        out_shape=[jax.ShapeDtypeStruct((N_TOK, D_PACK), jnp.uint32),
                   jax.ShapeDtypeStruct((N_TOK, LANES), F32)],
        compiler_params=_params(("arbitrary",)),
        name="router",
    )(x1_p, x1_s, mod_p, mod_p, mod_s, mod_s, g_ffn, w_route, b_route)


def _rank_kernel(route_ref, dest_ref, cnt_ref, carry, rank_scr):
    pas = pl.program_id(0)
    step = pl.program_id(1)

    @pl.when(jnp.logical_and(pas == 0, step == 0))
    def _():
        carry[...] = jnp.zeros_like(carry)

    route = route_ref[...]
    n = route.shape[0]
    lane = _iota((n, LANES), 1).astype(F32)
    o1 = jnp.where(lane == route[:, 0:1], 1.0, 0.0)
    o2 = jnp.where(lane == route[:, 1:2], 1.0, 0.0)
    lane_i = _iota((n, LANES), 1)
    rows = pl.ds(pl.multiple_of(step * TM_RANK, TM_RANK), TM_RANK)

    @pl.when(pas == 0)
    def _():
        both = o1 + o2
        strict = jnp.where(_iota((n, n), 1) < _iota((n, n), 0), 1.0, 0.0).astype(BF16)
        prior = _dot(strict, both.astype(BF16)) + carry[0:1, :]
        r1 = jnp.sum(o1 * prior, axis=1, keepdims=True)
        r2 = jnp.sum(o2 * prior, axis=1, keepdims=True)
        rank_scr[rows, :] = jnp.where(lane_i == 0, r1, jnp.where(lane_i == 1, r2, 0.0))
        carry[...] = carry[...] + jnp.sum(both, axis=0, keepdims=True)

    @pl.when(pas == 1)
    def _():
        counts = carry[...]
        tiles_per = jnp.floor((counts + (TM_MOE - 1)) * (1.0 / TM_MOE))
        before = jnp.where(_iota((LANES, LANES), 0) < _iota((LANES, LANES), 1), 1.0, 0.0).astype(BF16)
        pad_start = _dot(tiles_per.astype(BF16), before)[0:1, :] * TM_MOE
        rank = rank_scr[rows, :]
        d1 = jnp.sum(o1 * pad_start, axis=1, keepdims=True) + rank[:, 0:1]
        d2 = jnp.sum(o2 * pad_start, axis=1, keepdims=True) + rank[:, 1:2]
        dest_ref[...] = jnp.where(lane_i == 0, d1, jnp.where(lane_i == 1, d2, 0.0)).astype(jnp.int32)
        cnt_ref[...] = counts


def _rank(route):
    n = route.shape[0]
    return pl.pallas_call(
        _rank_kernel,
        grid=(2, n // TM_RANK),
        in_specs=[pl.BlockSpec((TM_RANK, LANES), lambda p, i: (i, 0))],
        out_specs=[pl.BlockSpec((TM_RANK, LANES), lambda p, i: (i * p, 0)),
                   pl.BlockSpec((SUBLANES, LANES), lambda p, i: (0, 0))],
        out_shape=[jax.ShapeDtypeStruct((n, LANES), jnp.int32), jax.ShapeDtypeStruct((SUBLANES, LANES), F32)],
        scratch_shapes=[pltpu.VMEM((SUBLANES, LANES), F32), pltpu.VMEM((n, LANES), F32)],
        compiler_params=_params(("arbitrary", "arbitrary")),
        name="rank",
    )(route)


def _row_copy(src, dst, sem):
    return pltpu.make_async_copy(src, dst, sem)


ROW_UNROLL = 8
assert T_DEC == SUBLANES
D_PACK = D // 2

ZERO_FIRST, ZERO_ANYTIME = 1, 2


def _dispatch_kernel(d1_ref, d2_ref, zc_ref, h_ref, xs_out, zbuf, sem):
    step = pl.program_id(0)
    base = step * TM_TOK

    def zero_tile(t, s):
        rows = pl.ds(pl.multiple_of(t * TM_MOE, TM_MOE), TM_MOE)
        return pltpu.make_async_copy(zbuf, xs_out.at[rows], sem.at[s])

    def for_tiles(cls, s, act):
        def body(t, carry):
            @pl.when(zc_ref[t] == cls)
            def _():
                act(zero_tile(t, s))
            return carry
        lax.fori_loop(0, MAX_TILES, body, 0)

    @pl.when(step == 0)
    def _():
        zbuf[...] = jnp.zeros_like(zbuf)
        for_tiles(ZERO_FIRST, 1, lambda c: c.start())
        for_tiles(ZERO_ANYTIME, 2, lambda c: c.start())
        for_tiles(ZERO_FIRST, 1, lambda c: c.wait())

    @pl.when(step == pl.num_programs(0) - 1)
    def _():
        for_tiles(ZERO_ANYTIME, 2, lambda c: c.wait())

    def issue(r, carry):
        _row_copy(h_ref.at[pl.ds(r, 1)], xs_out.at[pl.ds(d1_ref[base + r], 1)], sem.at[0]).start(priority=0)
        _row_copy(h_ref.at[pl.ds(r, 1)], xs_out.at[pl.ds(d2_ref[base + r], 1)], sem.at[0]).start(priority=1)
        return carry

    lax.fori_loop(0, TM_TOK, issue, 0, unroll=ROW_UNROLL)
    for _ in range(TOP_K):
        _row_copy(h_ref, xs_out.at[pl.ds(0, TM_TOK)], sem.at[0]).wait()


def _dispatch(dests, zero_class, h2):
    grid_spec = pltpu.PrefetchScalarGridSpec(
        num_scalar_prefetch=3,
        grid=(N_TOK // TM_TOK,),
        in_specs=[pl.BlockSpec((TM_TOK, D_PACK), lambda i, *_: (i, 0))],
        out_specs=pl.BlockSpec(memory_space=pl.ANY),
        scratch_shapes=[pltpu.VMEM((TM_MOE, D_PACK), jnp.uint32), pltpu.SemaphoreType.DMA((3,))],
    )
    return pl.pallas_call(
        _dispatch_kernel,
        grid_spec=grid_spec,
        out_shape=jax.ShapeDtypeStruct((A_PAD, D_PACK), jnp.uint32),
        compiler_params=_params(("arbitrary",)),
        name="dispatch",
    )(*dests, zero_class, h2)


N_UP_CHUNKS = D_FF // TN_FF
ROW_CAPS = tuple(range(TM_MOE // 4, TM_MOE + 1, TM_MOE // 4))


def _for_row_cap(rows, body):
    lo = 0
    for cap in ROW_CAPS:
        pl.when(jnp.logical_and(rows > lo, rows <= cap))(functools.partial(body, cap))
        lo = cap


def _expert_up_kernel(te_ref, rows_ref, first_ref, nxt_ref, run_ref, meta_ref, x_ref, wg_hbm, wu_hbm,
                      o_ref, wbuf, sem):
    n = pl.program_id(0)
    t = pl.program_id(1)
    slot = lax.rem(n * meta_ref[1] + run_ref[t], 2)

    def fetch(e, chunk, s):
        cols = pl.ds(pl.multiple_of(chunk * TN_FF, TN_FF), TN_FF)
        return (pltpu.make_async_copy(wg_hbm.at[e, :, cols], wbuf.at[s, 0], sem.at[s, 0]),
                pltpu.make_async_copy(wu_hbm.at[e, :, cols], wbuf.at[s, 1], sem.at[s, 1]))

    def start(copies):
        for priority, c in enumerate(copies):
            c.start(priority=priority)

    @pl.when(first_ref[t] == 1)
    def _():
        @pl.when(jnp.logical_and(n == 0, t == 0))
        def _():
            start(fetch(te_ref[0], 0, 0))

        for c in fetch(te_ref[t], n, slot):
            c.wait()

        @pl.when(nxt_ref[t] >= 0)
        def _():
            start(fetch(nxt_ref[t], n, 1 - slot))

        @pl.when(jnp.logical_and(nxt_ref[t] < 0, n + 1 < N_UP_CHUNKS))
        def _():
            start(fetch(te_ref[0], n + 1, 1 - slot))

    def compute(cap):
        x = jnp.concatenate([half.astype(BF16) for half in _unpack_bf16_pairs(x_ref[0:cap, :])], axis=1)
        g = _dot(x, wbuf[slot, 0].astype(BF16))
        u = _dot(x, wbuf[slot, 1].astype(BF16))
        o_ref[0:cap, :] = (g * _sigmoid(g) * u).astype(BF16)
        if cap < TM_MOE:
            o_ref[cap:, :] = jnp.zeros((TM_MOE - cap, TN_FF), BF16)

    _for_row_cap(rows_ref[t], compute)

    @pl.when(rows_ref[t] == 0)
    def _():
        o_ref[...] = jnp.zeros_like(o_ref)


def _expert_up(tiles, xs, w_gate, w_up):
    last = lambda t, meta: jnp.minimum(t, meta[0] - 1)
    grid_spec = pltpu.PrefetchScalarGridSpec(
        num_scalar_prefetch=6,
        grid=(N_UP_CHUNKS, MAX_TILES),
        in_specs=[pl.BlockSpec((TM_MOE, D_PACK), lambda n, t, *s: (last(t, s[5]), 0)),
                  pl.BlockSpec(memory_space=pl.ANY),
                  pl.BlockSpec(memory_space=pl.ANY)],
        out_specs=pl.BlockSpec((TM_MOE, TN_FF), lambda n, t, *s: (t, n)),
        scratch_shapes=[pltpu.VMEM((2, 2, D, TN_FF), F32), pltpu.SemaphoreType.DMA((2, 2))],
    )
    return pl.pallas_call(
        _expert_up_kernel,
        grid_spec=grid_spec,
        out_shape=jax.ShapeDtypeStruct((A_PAD, D_FF), BF16),
        compiler_params=_params(("arbitrary", "arbitrary")),
        name="expert_up",
    )(*tiles, xs, w_gate, w_up)


def _expert_down_kernel(te_ref, rows_ref, first_ref, nxt_ref, run_ref, meta_ref, h_ref, wd_hbm,
                        o_ref, wbuf, sem):
    t = pl.program_id(0)
    slot = lax.rem(run_ref[t], 2)

    def fetch(e, s):
        halves = [pl.ds(k * (D_FF // 2), D_FF // 2) for k in range(2)]
        return [pltpu.make_async_copy(wd_hbm.at[e, rows], wbuf.at[s, rows], sem.at[s, k])
                for k, rows in enumerate(halves)]

    def start(copies):
        for priority, c in enumerate(copies):
            c.start(priority=priority)

    @pl.when(first_ref[t] == 1)
    def _():
        @pl.when(t == 0)
        def _():
            start(fetch(te_ref[0], 0))

        for c in fetch(te_ref[t], slot):
            c.wait()

        @pl.when(nxt_ref[t] >= 0)
        def _():
            start(fetch(nxt_ref[t], 1 - slot))

    def compute(cap):
        o_ref[0:cap, :] = _pack_bf16_pairs(_dot(h_ref[0:cap, :], wbuf[slot].astype(BF16)))
        if cap < TM_MOE:
            o_ref[cap:, :] = jnp.zeros((TM_MOE - cap, D_PACK), jnp.uint32)

    _for_row_cap(rows_ref[t], compute)

    @pl.when(rows_ref[t] == 0)
    def _():
        o_ref[...] = jnp.zeros_like(o_ref)


def _expert_down(tiles, h1, w_down):
    last = lambda t, meta: jnp.minimum(t, meta[0] - 1)
    grid_spec = pltpu.PrefetchScalarGridSpec(
        num_scalar_prefetch=6,
        grid=(MAX_TILES,),
        in_specs=[pl.BlockSpec((TM_MOE, D_FF), lambda t, *s: (last(t, s[5]), 0)),
                  pl.BlockSpec(memory_space=pl.ANY)],
        out_specs=pl.BlockSpec((TM_MOE, D_PACK), lambda t, *s: (t, 0)),
        scratch_shapes=[pltpu.VMEM((2, D_FF, D), F32), pltpu.SemaphoreType.DMA((2, 2))],
    )
    return pl.pallas_call(
        _expert_down_kernel,
        grid_spec=grid_spec,
        out_shape=jax.ShapeDtypeStruct((A_PAD, D_PACK), jnp.uint32),
        compiler_params=_params(("arbitrary",)),
        name="expert_down",
    )(*tiles, h1, w_down)


def _combine_kernel(d1_ref, d2_ref, x_ref, route_ref, gt_ref, gf_ref, o_hbm,
                    y_ref, buf, sem, *, tok_offset):
    step = pl.program_id(0)
    slot = lax.rem(step, 2)

    def gather(s, into):
        base = tok_offset + s * TM_TOK

        def issue(r, carry):
            _row_copy(o_hbm.at[pl.ds(d1_ref[base + r], 1)], buf.at[into, 0, pl.ds(r, 1)], sem.at[into]).start(priority=0)
            _row_copy(o_hbm.at[pl.ds(d2_ref[base + r], 1)], buf.at[into, 1, pl.ds(r, 1)], sem.at[into]).start(priority=1)
            return carry

        lax.fori_loop(0, TM_TOK, issue, 0, unroll=ROW_UNROLL)

    @pl.when(step == 0)
    def _():
        gather(0, 0)

    @pl.when(step + 1 < pl.num_programs(0))
    def _():
        gather(step + 1, 1 - slot)

    for k in range(TOP_K):
        _row_copy(o_hbm.at[pl.ds(0, TM_TOK)], buf.at[slot, k], sem.at[slot]).wait()

    route = route_ref[...]
    lo1, hi1 = _unpack_bf16_pairs(buf[slot, 0])
    lo2, hi2 = _unpack_bf16_pairs(buf[slot, 1])
    g1, g2 = route[:, 2:3], route[:, 3:4]
    moe = jnp.concatenate([g1 * lo1 + g2 * lo2, g1 * hi1 + g2 * hi2], axis=1)
    x2 = x_ref[...] + _mod_rows(gt_ref) * moe
    y_ref[...] = x2 * lax.rsqrt(jnp.mean(x2 * x2, axis=-1, keepdims=True) + EPS) * gf_ref[...]


def _combine(dests, x1, route, mod, per_row, g_final, o_rows, tok_offset):
    n = x1.shape[0]
    off = tok_offset // TM_TOK
    grid_spec = pltpu.PrefetchScalarGridSpec(
        num_scalar_prefetch=2,
        grid=(n // TM_TOK,),
        in_specs=[pl.BlockSpec((TM_TOK, D), lambda i, *_: (i, 0)),
                  pl.BlockSpec((TM_TOK, LANES), lambda i, *_: (i + off, 0)),
                  _mod_spec(per_row, TM_TOK, D, lambda *_: 5),
                  pl.BlockSpec((1, D), lambda i, *_: (0, 0)),
                  pl.BlockSpec(memory_space=pl.ANY)],
        out_specs=pl.BlockSpec((TM_TOK, D), lambda i, *_: (i, 0)),
        scratch_shapes=[pltpu.VMEM((2, TOP_K, TM_TOK, D_PACK), jnp.uint32), pltpu.SemaphoreType.DMA((2,))],
    )
    return pl.pallas_call(
        functools.partial(_combine_kernel, tok_offset=tok_offset),
        grid_spec=grid_spec,
        out_shape=jax.ShapeDtypeStruct((n, D), F32),
        compiler_params=_params(("arbitrary",)),
        name="combine",
    )(*dests, x1, route, mod, g_final, o_rows)


def kernel(x_prompt, x_sample, cache_k, cache_v, state_C, state_n, state_m, c_prompt, c_sample, rel_bias, w_ada, b_ada, g_mix, g_ffn, w_in, sinks, b_igate, b_fgate, w_out, w_router_grp, b_router_grp, w_router_exp, b_router_exp, w_gate, w_up, w_down, g_final):
    xp = x_prompt.reshape(N_PROMPT, D)
    xs = x_sample.reshape(N_SAMPLE, D)

    c_all = jnp.concatenate([c_sample, c_prompt, jnp.zeros((C_ROWS - 1 - N_BATCH, D), F32)], axis=0)
    mod_s = _ada(c_all, w_ada[0], b_ada)
    mod_p = mod_s[N_BATCH]

    w_in_t = jnp.swapaxes(w_in[0], 0, 1)
    w_gates_t = jnp.pad(w_in_t[Z_WIDTH:].astype(BF16), ((0, LANES - 2 * M_HEADS), (0, 0)))
    h_all, zg = _norm(xp, xs, mod_p, mod_s, g_mix, w_gates_t)
    z = _inproj(h_all, w_in_t)

    rb_flat = rel_bias.reshape(NUM_BUCKETS * ATT_HEADS)
    sink_v = sinks[0]
    ya_p = _swa_prompt(z, rb_flat, sink_v)
    ya_s, nk_s, nv_s = _swa_sample(z,cache_k.reshape(N_BATCH, CACHE_ROWS, HEAD_DIM),
                                   cache_v.reshape(N_BATCH, CACHE_ROWS, HEAD_DIM), rb_flat, sink_v)

    gate_bias = jnp.concatenate([b_igate[0], b_fgate[0], jnp.zeros((LANES - 2 * M_HEADS,), F32)]).reshape(1, LANES)
    ym_p, c_p, n_p, m_p = _mlstm_prompt(z, zg, gate_bias)
    m0_tok = jnp.pad(jnp.repeat(state_m[0], T_DEC, axis=0), ((0, 0), (0, LANES - M_HEADS)))
    n0_flat = state_n[0].reshape(N_BATCH, M_HEADS * DK)
    n0_tok = jnp.repeat(n0_flat, T_DEC, axis=0)
    ym_s, c_s, n_s, m_s = _mlstm_sample(z, zg, gate_bias, m0_tok, n0_tok, state_C[0], n0_flat)

    x1_p = _outproj(ya_p, ym_p, w_out[0], xp, mod_p, False)
    x1_s = _outproj(ya_s, ym_s, w_out[0], xs, mod_s, True)

    w_route = jnp.pad(jnp.concatenate([w_router_grp[0], w_router_exp[0]], axis=1),
                      ((0, 0), (0, LANES - N_GROUPS - N_EXPERTS)))
    b_route = jnp.pad(jnp.concatenate([b_router_grp[0], b_router_exp[0]]),
                      (0, LANES - N_GROUPS - N_EXPERTS)).reshape(1, LANES)
    w_route_hi = w_route.astype(BF16)
    w_route_split = jnp.stack([w_route_hi, (w_route - w_route_hi.astype(F32)).astype(BF16)])
    h2, route = _router(x1_p, x1_s, mod_p, mod_s, g_ffn, w_route_split, b_route)

    dest, counts = _rank(route)
    dests = (dest[:, 0], dest[:, 1])

    i32 = lambda a: a.astype(jnp.int32)
    cnt = i32(counts[0, :N_EXPERTS])
    tiles_per = (cnt + TM_MOE - 1) // TM_MOE
    tile_end = jnp.cumsum(tiles_per)
    n_tiles = tile_end[-1]
    tile_ids = jnp.arange(MAX_TILES, dtype=jnp.int32)
    tile_expert = i32(jnp.minimum(jnp.searchsorted(tile_end, tile_ids, side="right"), N_EXPERTS - 1))
    last_expert = tile_expert[jnp.maximum(n_tiles - 1, 0)]
    tile_valid = tile_ids < n_tiles
    tile_expert = jnp.where(tile_valid, tile_expert, last_expert)
    prev_expert = jnp.concatenate([jnp.full((1,), -1, jnp.int32), tile_expert[:-1]])
    run_first = jnp.logical_and(tile_valid, tile_expert != prev_expert)
    run_id = jnp.maximum(jnp.cumsum(i32(run_first)) - 1, 0)
    expert_ids = jnp.arange(N_EXPERTS, dtype=jnp.int32)
    used = jnp.where(tiles_per > 0, expert_ids, N_EXPERTS)
    next_used = jnp.concatenate([lax.cummin(used[::-1])[::-1][1:], jnp.full((1,), N_EXPERTS, jnp.int32)])
    next_used = jnp.where(next_used >= N_EXPERTS, -1, next_used)
    first_tile = tile_end - tiles_per
    tile_rows = jnp.clip(cnt[tile_expert] - (tile_ids - first_tile[tile_expert]) * TM_MOE, 0, TM_MOE)
    tile_rows = jnp.where(tile_valid, tile_rows, 0)
    tiles = (tile_expert, i32(tile_rows), i32(run_first), next_used[tile_expert], i32(run_id),
             jnp.stack([n_tiles, jnp.sum(i32(run_first))]).astype(jnp.int32))
    next_expert = jnp.concatenate([tile_expert[1:], jnp.full((1,), -1, jnp.int32)])
    run_last = jnp.logical_or(tile_expert != next_expert, tile_ids == n_tiles - 1)
    zero_class = jnp.where(tile_valid, jnp.where(run_last, ZERO_FIRST, 0), ZERO_ANYTIME)
    xs_rows = _dispatch(dests, i32(zero_class), h2)
    h1 = _expert_up(tiles, xs_rows, w_gate[0], w_up[0])
    o_rows = _expert_down(tiles, h1, w_down[0])

    gf = g_final.reshape(1, D)
    y_p = _combine(dests, x1_p, route, mod_p, False, gf, o_rows, 0)
    y_s = _combine(dests, x1_s, route, mod_s, True, gf, o_rows, N_PROMPT)

    kv5 = lambda a: a.reshape(1, -1, WINDOW, KV_HEADS, HEAD_DIM)
    kcol = ATT_WIDTH
    nk_p = z[N_PROMPT - WINDOW:N_PROMPT, kcol:kcol + KV_WIDTH]
    nv_p = z[N_PROMPT - WINDOW:N_PROMPT, kcol + KV_WIDTH:kcol + 2 * KV_WIDTH]
    return (y_p.reshape(1, N_PROMPT, D), y_s.reshape(N_BATCH, T_DEC, D),
            kv5(nk_p), kv5(nv_p),
            c_p.reshape(1, 1, M_HEADS, DV, DK), n_p.reshape(1, 1, M_HEADS, DK), m_p[:, 0].reshape(1, 1, M_HEADS),
            kv5(nk_s), kv5(nv_s),
            c_s.reshape(1, N_BATCH, M_HEADS, DV, DK), n_s.reshape(1, N_BATCH, M_HEADS, DK),
            m_s.reshape(N_BATCH, T_DEC, LANES)[:, T_DEC - 1, :M_HEADS].reshape(1, N_BATCH, M_HEADS))
```
